```python
import math
import jax, jax.numpy as jnp
from jax import lax
import numpy as np

D_MODEL = 2048
BATCH = 4
SEQ = 2048
DEPTH = 2
DEC_BATCH = 128
DEC_SEQ = 1
PAST_LEN = 16384
PAGE_SIZE = 128

D_A = 512
S5_GROUP = 16
G_A = D_A // S5_GROUP
P_A = 64
D_B = 512
H_B = 8
GM_CHUNK = 128
D_C = 1024
HD_C = 64
H_C = D_C // HD_C
N_C = 128
G_C = 2
K_C = 4
SSD_CHUNK = 128
D_XBC = D_C + 2 * G_C * N_C
D_FF = 5632
K_F = 3
N_IN = D_A + 2 * D_B + D_C + D_XBC + H_C + 3 * D_MODEL
EPS = 1e-6

kernel_name = "hybrid_s5_gmlp_ssd_convffn_step"


def rmsnorm(x, g):
    xf = x.astype(jnp.float32)
    y = xf * lax.rsqrt(jnp.mean(xf * xf, axis=-1, keepdims=True) + EPS)
    return (y * g.astype(jnp.float32)).astype(x.dtype)


def causal_dwconv(x, buf, w, b):
    K = w.shape[0]
    L = x.shape[1]
    xp = jnp.concatenate([buf.astype(x.dtype), x], axis=1)
    y = b
    for k in range(K):
        y = y + w[k] * xp[:, k:k + L]
    return y, xp[:, L:]


def s5_combine(e1, e2):
    a1r, a1i, b1r, b1i = e1
    a2r, a2i, b2r, b2i = e2
    return (a2r * a1r - a2i * a1i, a2r * a1i + a2i * a1r,
            a2r * b1r - a2i * b1i + b2r, a2r * b1i + a2i * b1r + b2i)


def s5_branch(u, h0_re, h0_im, lam_re, lam_im, log_dt, b_re, b_im, c_re, c_im, d_skip, w_glu):
    Bsz, L, _ = u.shape
    uf = u.astype(jnp.float32)
    ug = uf.reshape(Bsz, L, G_A, S5_GROUP)
    dt = jnp.exp(log_dt.astype(jnp.float32))[:, None]
    lr = lam_re.astype(jnp.float32)
    li = lam_im.astype(jnp.float32)
    mag = jnp.exp(lr * dt)
    ar = mag * jnp.cos(li * dt)
    ai = mag * jnp.sin(li * dt)
    den = lr * lr + li * li
    nr = ar - 1.0
    kr = (nr * lr + ai * li) / den
    ki = (ai * lr - nr * li) / den
    bre = b_re.astype(jnp.float32)
    bim = b_im.astype(jnp.float32)
    bbr = kr[..., None] * bre - ki[..., None] * bim
    bbi = kr[..., None] * bim + ki[..., None] * bre
    bu_r = jnp.einsum('gph,blgh->blgp', bbr, ug)
    bu_i = jnp.einsum('gph,blgh->blgp', bbi, ug)
    h0r = h0_re.astype(jnp.float32)
    h0i = h0_im.astype(jnp.float32)
    bu_r = bu_r.at[:, 0].add(ar * h0r - ai * h0i)
    bu_i = bu_i.at[:, 0].add(ar * h0i + ai * h0r)
    a_r = jnp.broadcast_to(ar, bu_r.shape)
    a_i = jnp.broadcast_to(ai, bu_i.shape)
    _, _, hr, hi = lax.associative_scan(s5_combine, (a_r, a_i, bu_r, bu_i), axis=1)
    y = (jnp.einsum('ghp,blgp->blgh', c_re.astype(jnp.float32), hr)
         - jnp.einsum('ghp,blgp->blgh', c_im.astype(jnp.float32), hi))
    y = y.reshape(Bsz, L, D_A) + d_skip.astype(jnp.float32) * uf
    y = jax.nn.gelu(y)
    y = y * jax.nn.sigmoid(y @ w_glu.astype(jnp.float32))
    return y.astype(u.dtype), hr[:, -1], hi[:, -1]


def gmlp_branch(u, v, g_v, w_s, b_s):
    Bsz, L, _ = u.shape
    vn = rmsnorm(v, g_v)
    nc = -(-L // GM_CHUNK)
    pad = nc * GM_CHUNK - L
    vp = jnp.pad(vn, ((0, 0), (0, pad), (0, 0))).reshape(Bsz, nc, GM_CHUNK, H_B, D_B // H_B)
    mask = jnp.tril(jnp.ones((GM_CHUNK, GM_CHUNK), dtype=bool))
    ws = jnp.where(mask, w_s, 0.0)
    s = jnp.einsum('hij,bcjhd->bcihd', ws, vp) + b_s.T[:, :, None]
    s = s.reshape(Bsz, nc * GM_CHUNK, D_B)[:, :L]
    return u * s, vn


def ssd_scan(x, dt, a, bm, cm, h0):
    Bsz, L = x.shape[:2]
    Q = SSD_CHUNK if L % SSD_CHUNK == 0 else L
    nc = L // Q

    def chunk(t):
        return t.reshape((Bsz, nc, Q) + t.shape[2:])

    xc, dtc, bc, cc = chunk(x), chunk(dt), chunk(bm), chunk(cm)
    da_cs = jnp.cumsum(dtc * a, axis=2)
    xdt = xc * dtc[..., None]
    seg = da_cs[:, :, :, None, :] - da_cs[:, :, None, :, :]
    causal = jnp.tril(jnp.ones((Q, Q), dtype=bool))[:, :, None]
    decay = jnp.exp(jnp.where(causal, seg, -jnp.inf))
    scores = jnp.einsum('bcihn,bcjhn->bcijh', cc, bc) * decay
    y_diag = jnp.einsum('bcijh,bcjhp->bcihp', scores, xdt)
    dec_end = jnp.exp(da_cs[:, :, -1:, :] - da_cs)
    states = jnp.einsum('bcjhn,bcjh,bcjhp->bchpn', bc, dec_end, xdt)
    chunk_decay = jnp.exp(da_cs[:, :, -1, :])

    def step(h, inp):
        s_c, d_c = inp
        return h * d_c[..., None, None] + s_c, h

    h_last, h_prev = lax.scan(step, h0, (jnp.moveaxis(states, 1, 0), jnp.moveaxis(chunk_decay, 1, 0)))
    h_prev = jnp.moveaxis(h_prev, 0, 1)
    y_off = jnp.einsum('bcihn,bchpn,bcih->bcihp', cc, h_prev, jnp.exp(da_cs))
    y = (y_diag + y_off).reshape(Bsz, L, x.shape[2], x.shape[3])
    return y, h_last


def ssd_branch(z, xbc, dt_raw, conv_buf, h0, conv_w, conv_b, dt_bias, a_log, d_skip, g_norm):
    xbc, new_buf = causal_dwconv(xbc, conv_buf, conv_w, conv_b)
    xbc = jax.nn.silu(xbc)
    Bsz, L, _ = xbc.shape
    xs = xbc[..., :D_C].reshape(Bsz, L, H_C, HD_C).astype(jnp.float32)
    bm = xbc[..., D_C:D_C + G_C * N_C].reshape(Bsz, L, G_C, N_C)
    cm = xbc[..., D_C + G_C * N_C:].reshape(Bsz, L, G_C, N_C)
    bm = jnp.repeat(bm, H_C // G_C, axis=2).astype(jnp.float32)
    cm = jnp.repeat(cm, H_C // G_C, axis=2).astype(jnp.float32)
    dt = jax.nn.softplus(dt_raw.astype(jnp.float32) + dt_bias.astype(jnp.float32))
    a = -jnp.exp(a_log.astype(jnp.float32))
    y, h_last = ssd_scan(xs, dt, a, bm, cm, h0.astype(jnp.float32))
    y = y + d_skip.astype(jnp.float32)[:, None] * xs
    y = y.reshape(Bsz, L, D_C)
    y = rmsnorm(y * jax.nn.silu(z.astype(jnp.float32)), g_norm)
    return y.astype(z.dtype), h_last, new_buf


def conv_ffn(h, buf, w_up, conv_w, conv_b, w_down):
    up = h @ w_up
    up, new_buf = causal_dwconv(up, buf, conv_w, conv_b)
    a, b = jnp.split(up, 2, axis=-1)
    return (jax.nn.silu(a) * b) @ w_down, new_buf


def layer(x, c, s5_re, s5_im, ssm, conv_c, conv_f, p):
    mod = jax.nn.silu(c) @ p['w_mod'] + p['b_mod']
    sh_m, sc_m, gt_m, sh_f, sc_f, gt_f = [m[:, None, :] for m in jnp.split(mod, 6, axis=-1)]
    h = rmsnorm(x, p['g_mix']) * (1.0 + sc_m) + sh_m
    proj = h @ p['w_in']
    i1 = D_A
    i2 = i1 + D_B
    i3 = i2 + D_B
    i4 = i3 + D_C
    i5 = i4 + D_XBC
    i6 = i5 + H_C
    u_a, u_b, v_b, z_c, xbc_c, dt_c, gates = jnp.split(proj, [i1, i2, i3, i4, i5, i6], axis=-1)
    o_a, s5r_new, s5i_new = s5_branch(u_a, s5_re, s5_im, p['lam_re'], p['lam_im'], p['log_dt'],
                                      p['b_re'], p['b_im'], p['c_re'], p['c_im'], p['s5_d'], p['w_glu'])
    o_b, v_rows = gmlp_branch(jax.nn.gelu(u_b), jax.nn.gelu(v_b), p['g_v'], p['w_s'], p['b_s'])
    o_c, ssm_new, convc_new = ssd_branch(z_c, xbc_c, dt_c, conv_c, ssm, p['ssd_conv_w'], p['ssd_conv_b'],
                                         p['dt_bias'], p['a_log'], p['ssd_d'], p['ssd_g'])
    g_a, g_b, g_c = jnp.split(jax.nn.sigmoid(gates), 3, axis=-1)
    merged = g_a * (o_a @ p['w_pa']) + g_b * (o_b @ p['w_pb']) + g_c * (o_c @ p['w_pc'])
    x = x + gt_m * (merged @ p['w_out'])
    h2 = rmsnorm(x, p['g_ffn']) * (1.0 + sc_f) + sh_f
    f, convf_new = conv_ffn(h2, conv_f, p['w_up'], p['ffn_conv_w'], p['ffn_conv_b'], p['w_down'])
    x = x + gt_f * f
    return x, s5r_new, s5i_new, ssm_new, convc_new, convf_new, v_rows


def setup_inputs(seed: int = 0) -> dict:
    key = jax.random.key(seed)
    ks = iter(jax.random.split(key, 64))

    def nrm(shape, scale):
        return scale * jax.random.normal(next(ks), shape, jnp.float32)

    def unif(shape, lo, hi):
        return jax.random.uniform(next(ks), shape, jnp.float32, lo, hi)

    L = DEPTH
    dt0 = jnp.exp(unif((L, H_C), math.log(1e-3), math.log(1e-1)))
    lam_im0 = jnp.broadcast_to(math.pi * jnp.arange(P_A, dtype=jnp.float32), (L, G_A, P_A))
    return {
        'x_prompt': nrm((BATCH, SEQ, D_MODEL), 1.0),
        'x_sample': nrm((DEC_BATCH, DEC_SEQ, D_MODEL), 1.0),
        'c_prompt': nrm((BATCH, D_MODEL), 1.0),
        'c_sample': nrm((DEC_BATCH, D_MODEL), 1.0),
        'state_s5_re': nrm((L, DEC_BATCH, G_A, P_A), 0.3),
        'state_s5_im': nrm((L, DEC_BATCH, G_A, P_A), 0.3),
        'state_ssm': nrm((L, DEC_BATCH, H_C, HD_C, N_C), 0.1),
        'state_ssd_conv': nrm((L, DEC_BATCH, K_C - 1, D_XBC), 1.0),
        'state_ffn_conv': nrm((L, DEC_BATCH, K_F - 1, 2 * D_FF), 1.0),
        'w_mod': nrm((L, D_MODEL, 6 * D_MODEL), 0.3 * D_MODEL ** -0.5),
        'b_mod': nrm((L, 6 * D_MODEL), 0.02),
        'g_mix': 1.0 + nrm((L, D_MODEL), 0.02),
        'w_in': nrm((L, D_MODEL, N_IN), D_MODEL ** -0.5),
        's5_lam_re': -0.5 + nrm((L, G_A, P_A), 0.01),
        's5_lam_im': lam_im0 + nrm((L, G_A, P_A), 0.01),
        's5_log_dt': unif((L, G_A), math.log(1e-3), math.log(1e-1)),
        's5_b_re': nrm((L, G_A, P_A, S5_GROUP), (2 * S5_GROUP) ** -0.5),
        's5_b_im': nrm((L, G_A, P_A, S5_GROUP), (2 * S5_GROUP) ** -0.5),
        's5_c_re': nrm((L, G_A, S5_GROUP, P_A), P_A ** -0.5),
        's5_c_im': nrm((L, G_A, S5_GROUP, P_A), P_A ** -0.5),
        's5_d': nrm((L, D_A), 1.0),
        's5_w_glu': nrm((L, D_A, D_A), D_A ** -0.5),
        'gm_g_v': 1.0 + nrm((L, D_B), 0.02),
        'gm_w_s': nrm((L, H_B, GM_CHUNK, GM_CHUNK), 0.5 * GM_CHUNK ** -0.5),
        'gm_b_s': 1.0 + nrm((L, H_B, GM_CHUNK), 0.02),
        'ssd_conv_w': nrm((L, K_C, D_XBC), K_C ** -0.5),
        'ssd_conv_b': nrm((L, D_XBC), 0.02),
        'ssd_dt_bias': dt0 + jnp.log(-jnp.expm1(-dt0)),
        'ssd_a_log': jnp.log(unif((L, H_C), 1.0, 16.0)),
        'ssd_d': 1.0 + nrm((L, H_C), 0.1),
        'ssd_g_norm': 1.0 + nrm((L, D_C), 0.02),
        'w_pa': nrm((L, D_A, D_MODEL), D_A ** -0.5),
        'w_pb': nrm((L, D_B, D_MODEL), D_B ** -0.5),
        'w_pc': nrm((L, D_C, D_MODEL), D_C ** -0.5),
        'w_out': nrm((L, D_MODEL, D_MODEL), D_MODEL ** -0.5),
        'g_ffn': 1.0 + nrm((L, D_MODEL), 0.02),
        'ffn_w_up': nrm((L, D_MODEL, 2 * D_FF), D_MODEL ** -0.5),
        'ffn_conv_w': nrm((L, K_F, 2 * D_FF), K_F ** -0.5),
        'ffn_conv_b': nrm((L, 2 * D_FF), 0.02),
        'ffn_w_down': nrm((L, D_FF, D_MODEL), D_FF ** -0.5),
        'g_final': 1.0 + nrm((D_MODEL,), 0.02),
    }


def reference(x_prompt, x_sample, c_prompt, c_sample, state_s5_re, state_s5_im, state_ssm,
              state_ssd_conv, state_ffn_conv, w_mod, b_mod, g_mix, w_in, s5_lam_re, s5_lam_im,
              s5_log_dt, s5_b_re, s5_b_im, s5_c_re, s5_c_im, s5_d, s5_w_glu, gm_g_v, gm_w_s, gm_b_s,
              ssd_conv_w, ssd_conv_b, ssd_dt_bias, ssd_a_log, ssd_d, ssd_g_norm, w_pa, w_pb, w_pc,
              w_out, g_ffn, ffn_w_up, ffn_conv_w, ffn_conv_b, ffn_w_down, g_final):
    bp = x_prompt.shape[0]
    dtp = x_prompt.dtype
    xp, xs = x_prompt, x_sample
    s5r_p, s5i_p, ssm_p, cc_p, cf_p = [], [], [], [], []
    s5r_s, s5i_s, ssm_s, cc_s, cf_s, gv_s = [], [], [], [], [], []
    for l in range(DEPTH):
        p = {
            'w_mod': w_mod[l], 'b_mod': b_mod[l], 'g_mix': g_mix[l], 'w_in': w_in[l],
            'lam_re': s5_lam_re[l], 'lam_im': s5_lam_im[l], 'log_dt': s5_log_dt[l],
            'b_re': s5_b_re[l], 'b_im': s5_b_im[l], 'c_re': s5_c_re[l], 'c_im': s5_c_im[l],
            's5_d': s5_d[l], 'w_glu': s5_w_glu[l],
            'g_v': gm_g_v[l], 'w_s': gm_w_s[l], 'b_s': gm_b_s[l],
            'ssd_conv_w': ssd_conv_w[l], 'ssd_conv_b': ssd_conv_b[l], 'dt_bias': ssd_dt_bias[l],
            'a_log': ssd_a_log[l], 'ssd_d': ssd_d[l], 'ssd_g': ssd_g_norm[l],
            'w_pa': w_pa[l], 'w_pb': w_pb[l], 'w_pc': w_pc[l], 'w_out': w_out[l],
            'g_ffn': g_ffn[l], 'w_up': ffn_w_up[l], 'ffn_conv_w': ffn_conv_w[l],
            'ffn_conv_b': ffn_conv_b[l], 'w_down': ffn_w_down[l],
        }
        xp, a1, a2, a3, a4, a5, _ = layer(
            xp, c_prompt,
            jnp.zeros((bp, G_A, P_A), dtp), jnp.zeros((bp, G_A, P_A), dtp),
            jnp.zeros((bp, H_C, HD_C, N_C), dtp), jnp.zeros((bp, K_C - 1, D_XBC), dtp),
            jnp.zeros((bp, K_F - 1, 2 * D_FF), dtp), p)
        s5r_p.append(a1); s5i_p.append(a2); ssm_p.append(a3); cc_p.append(a4); cf_p.append(a5)
        xs, b1, b2, b3, b4, b5, b6 = layer(
            xs, c_sample, state_s5_re[l], state_s5_im[l], state_ssm[l],
            state_ssd_conv[l], state_ffn_conv[l], p)
        s5r_s.append(b1); s5i_s.append(b2); ssm_s.append(b3); cc_s.append(b4); cf_s.append(b5); gv_s.append(b6)
    y_prompt = rmsnorm(xp, g_final)
    y_sample = rmsnorm(xs, g_final)
    return (y_prompt, y_sample,
            jnp.stack(s5r_p), jnp.stack(s5i_p), jnp.stack(ssm_p), jnp.stack(cc_p), jnp.stack(cf_p),
            jnp.stack(s5r_s), jnp.stack(s5i_s), jnp.stack(ssm_s), jnp.stack(cc_s), jnp.stack(cf_s),
            jnp.stack(gv_s))
```

```python
import functools

import jax
import jax.numpy as jnp
from jax import lax
from jax.experimental import pallas as pl
from jax.experimental.pallas import tpu as pltpu

F32 = jnp.float32
BF16 = jnp.bfloat16

D_MODEL = 2048
DEPTH = 2
D_A = 512
S5_GROUP = 16
G_A = D_A // S5_GROUP
P_A = 64
N_S5 = G_A * P_A
D_B = 512
H_B = 8
GM_CHUNK = 128
D_C = 1024
HD_C = 64
H_C = D_C // HD_C
N_C = 128
G_C = 2
K_C = 4
SSD_CHUNK = 128
D_XBC = D_C + 2 * G_C * N_C
D_FF = 5632
K_F = 3
EPS = 1e-6

OFF_UA = 0
OFF_Z = D_A + 2 * D_B
OFF_XBC = OFF_Z + D_C
OFF_DT = OFF_XBC + D_XBC
OFF_GATES = OFF_DT + H_C

LANE = 128
SUBLANE = 8
NEG_BIG = -1e30


def _cp(sem, vmem_mb=48):
    return pltpu.CompilerParams(dimension_semantics=sem, vmem_limit_bytes=vmem_mb * 1024 * 1024)


def _silu(x):
    return x * jax.nn.sigmoid(x)


def _softplus(x):
    return jnp.maximum(x, 0.0) + jnp.log1p(jnp.exp(-jnp.abs(x)))


def _rms(x):
    return x * lax.rsqrt(jnp.mean(x * x, axis=-1, keepdims=True) + EPS)


def _split_bf16(x):
    hi = x.astype(BF16)
    lo = (x - hi.astype(F32)).astype(BF16)
    return hi, lo


def _dot(a, b):
    return jnp.dot(a, b, preferred_element_type=F32)


def _dot_nt(a, b):
    return lax.dot_general(a, b, (((1,), (1,)), ((), ())), preferred_element_type=F32)


def _dot_tn(a, b):
    return lax.dot_general(a, b, (((0,), (0,)), ((), ())), preferred_element_type=F32)


def _mod_body(c_ref, w_ref, b_ref, o_ref):
    a = _silu(c_ref[...]).astype(BF16)
    o_ref[...] = _dot(a, w_ref[...].astype(BF16)) + b_ref[...]


def mod_all(c_all, w_mod, b_mod, *, tn=1024):
    R, D = c_all.shape
    N = w_mod.shape[-1]
    return pl.pallas_call(
        _mod_body,
        grid=(DEPTH, N // tn),
        in_specs=[pl.BlockSpec((R, D), lambda l, j: (0, 0)),
                  pl.BlockSpec((None, D, tn), lambda l, j: (l, 0, j)),
                  pl.BlockSpec((None, 1, tn), lambda l, j: (l, 0, j))],
        out_specs=pl.BlockSpec((None, R, tn), lambda l, j: (l, 0, j)),
        out_shape=jax.ShapeDtypeStruct((DEPTH, R, N), F32),
        compiler_params=_cp(("arbitrary", "arbitrary")),
        name="mod_all",
    )(c_all, w_mod, b_mod.reshape(DEPTH, 1, N))


def _norm_mod_body(x_ref, g_ref, sc_ref, sh_ref, o_ref):
    y = _rms(x_ref[...]) * g_ref[...]
    o_ref[...] = (y * (1.0 + sc_ref[...]) + sh_ref[...]).astype(o_ref.dtype)


def norm_mod(x3, g, sc3, sh3, *, tm):
    B, L, D = x3.shape
    per_row = sc3.shape[1] != 1
    ts = tm if per_row else 1
    sidx = (lambda b, i: (b, i, 0)) if per_row else (lambda b, i: (b, 0, 0))
    return pl.pallas_call(
        _norm_mod_body,
        grid=(B, L // tm),
        in_specs=[pl.BlockSpec((None, tm, D), lambda b, i: (b, i, 0)),
                  pl.BlockSpec((1, D), lambda b, i: (0, 0)),
                  pl.BlockSpec((None, ts, D), sidx),
                  pl.BlockSpec((None, ts, D), sidx)],
        out_specs=pl.BlockSpec((None, tm, D), lambda b, i: (b, i, 0)),
        out_shape=jax.ShapeDtypeStruct((B, L, D), BF16),
        compiler_params=_cp(("arbitrary", "arbitrary")),
        name="norm_mod",
    )(x3, g, sc3, sh3)


def _rms_body(x_ref, g_ref, o_ref):
    o_ref[...] = _rms(x_ref[...]) * g_ref[...]


def final_norm(x3, g, *, tm):
    B, L, D = x3.shape
    return pl.pallas_call(
        _rms_body,
        grid=(B, L // tm),
        in_specs=[pl.BlockSpec((None, tm, D), lambda b, i: (b, i, 0)),
                  pl.BlockSpec((1, D), lambda b, i: (0, 0))],
        out_specs=pl.BlockSpec((None, tm, D), lambda b, i: (b, i, 0)),
        out_shape=jax.ShapeDtypeStruct((B, L, D), F32),
        compiler_params=_cp(("arbitrary", "arbitrary")),
        name="final_norm",
    )(x3, g)


def _mm_body(*refs, x_of_w, n_x, n_e, n_o, epilogue):
    n_w = len(x_of_w)
    xs = refs[:n_x]
    ws = refs[n_x:n_x + n_w]
    es = refs[n_x + n_w:n_x + n_w + n_e]
    outs = refs[n_x + n_w + n_e:n_x + n_w + n_e + n_o]
    wsc = refs[n_x + n_w + n_e + n_o:]

    @pl.when(pl.program_id(1) == 0)
    def _():
        for w, s in zip(ws, wsc):
            s[...] = w[...].astype(BF16)

    accs = [_dot(xs[xi][...], s[...]) for xi, s in zip(x_of_w, wsc)]
    epilogue(accs, es, outs)


def matmul(xs, ws, extras, outs, epilogue, *, x_of_w, tm, tn, name, vmem_mb=48):
    M = xs[0].shape[0]
    N = outs[0][0]
    in_specs = [pl.BlockSpec((tm, x.shape[1]), lambda j, i: (i, 0)) for x in xs]
    args = list(xs)
    scratch = []
    for w3, l, col0 in ws:
        K = w3.shape[1]
        assert col0 % tn == 0
        in_specs.append(pl.BlockSpec((None, K, tn), functools.partial(
            lambda j, i, l, c: (l, 0, c + j), l=l, c=col0 // tn)))
        args.append(w3)
        scratch.append(pltpu.VMEM((K, tn), BF16))
    for arr, spec in extras:
        in_specs.append(spec)
        args.append(arr)
    body = functools.partial(_mm_body, x_of_w=tuple(x_of_w), n_x=len(xs), n_e=len(extras),
                             n_o=len(outs), epilogue=epilogue)
    res = pl.pallas_call(
        body,
        grid=(N // tn, M // tm),
        in_specs=in_specs,
        out_specs=[pl.BlockSpec((tm, tn), lambda j, i: (i, j)) for _ in outs],
        out_shape=[jax.ShapeDtypeStruct((M, n), dt) for n, dt in outs],
        scratch_shapes=scratch,
        compiler_params=_cp(("arbitrary", "arbitrary"), vmem_mb),
        name=name,
    )(*args)
    return res


def _epi_store(accs, es, outs):
    outs[0][...] = accs[0].astype(outs[0].dtype)


def _epi_merge(accs, es, outs):
    ga, gb, gc, pa, pb, pc = accs
    m = jax.nn.sigmoid(ga) * pa + jax.nn.sigmoid(gb) * pb + jax.nn.sigmoid(gc) * pc
    outs[0][...] = m.astype(outs[0].dtype)


def _epi_residual(accs, es, outs):
    res_ref, gt_ref = es
    outs[0][...] = res_ref[...] + gt_ref[...] * accs[0]


def _tile_spec(tm, tn):
    return pl.BlockSpec((tm, tn), lambda j, i: (i, j))


def _gate_spec(gt3, tm, tn, tiles_per_seq):
    if gt3.shape[1] == 1:
        return pl.BlockSpec((None, 1, tn), lambda j, i: (i // tiles_per_seq, 0, j))
    return pl.BlockSpec((None, tm, tn), lambda j, i: (0, i, j))


def _s5_prep_body(lrf_ref, lif_ref, ldf_ref, lrr_ref, lir_ref, ldr_ref, bre_ref, bim_ref,
                  tab_ref, bbr_ref, bbi_ref):
    dtf = jnp.exp(ldf_ref[...])
    n = (lax.broadcasted_iota(jnp.int32, (SUBLANE, N_S5), 0) + 1).astype(F32)
    mag = jnp.exp(n * (lrf_ref[...] * dtf))
    ang = n * (lif_ref[...] * dtf)
    pr = mag * jnp.cos(ang)
    pi = mag * jnp.sin(ang)
    row = lax.broadcasted_iota(jnp.int32, (SUBLANE, N_S5), 0)
    for k, d in enumerate((1, 2, 4)):
        keep = row >= d
        tab_ref[2 * k] = jnp.where(keep, pr[d - 1:d, :], 0.0)
        tab_ref[2 * k + 1] = jnp.where(keep, pi[d - 1:d, :], 0.0)
    tab_ref[6] = pr
    tab_ref[7] = pi
    dtr = jnp.exp(ldr_ref[...])
    lr = lrr_ref[...]
    li = lir_ref[...]
    m1 = jnp.exp(lr * dtr)
    ar = m1 * jnp.cos(li * dtr)
    ai = m1 * jnp.sin(li * dtr)
    den = lr * lr + li * li
    nr = ar - 1.0
    kr = (nr * lr + ai * li) / den
    ki = (ai * lr - nr * li) / den
    bre = bre_ref[...]
    bim = bim_ref[...]
    bbr_ref[...] = kr * bre - ki * bim
    bbi_ref[...] = kr * bim + ki * bre


def s5_prep(lam_re, lam_im, log_dt, b_re, b_im):
    lrf = lam_re.reshape(1, N_S5)
    lif = lam_im.reshape(1, N_S5)
    ldf = jnp.repeat(log_dt, P_A).reshape(1, N_S5)
    lrr = jnp.repeat(lam_re, S5_GROUP, axis=0)
    lir = jnp.repeat(lam_im, S5_GROUP, axis=0)
    ldr = jnp.repeat(log_dt, S5_GROUP).reshape(D_A, 1)
    bre = jnp.transpose(b_re, (0, 2, 1)).reshape(D_A, P_A)
    bim = jnp.transpose(b_im, (0, 2, 1)).reshape(D_A, P_A)
    tab, bbr, bbi = pl.pallas_call(
        _s5_prep_body,
        out_shape=[jax.ShapeDtypeStruct((8, SUBLANE, N_S5), F32),
                   jax.ShapeDtypeStruct((D_A, P_A), F32),
                   jax.ShapeDtypeStruct((D_A, P_A), F32)],
        name="s5_prep",
    )(lrf, lif, ldf, lrr, lir, ldr, bre, bim)
    eye = jnp.eye(G_A, dtype=F32)

    def blockdiag(m):
        return jnp.einsum('ghp,gk->ghkp', m.reshape(G_A, S5_GROUP, P_A), eye).reshape(D_A, N_S5)

    wb = jnp.concatenate([blockdiag(bbr), blockdiag(bbi)], axis=1).astype(BF16)
    return tab, wb


def s5_out_weights(c_re, c_im):
    eye = jnp.eye(G_A, dtype=F32)

    def blockdiag(c):
        return jnp.einsum('ghp,gk->gpkh', c, eye).reshape(N_S5, D_A).astype(BF16)

    return blockdiag(c_re), blockdiag(c_im)


def _s5_tail(hr, hi, u, wcr_ref, wci_ref, d_ref, wglu_ref):
    y = _dot(hr.astype(BF16), wcr_ref[...]) - _dot(hi.astype(BF16), wci_ref[...])
    y = y + d_ref[...] * u
    y = jax.nn.gelu(y)
    return y * jax.nn.sigmoid(_dot(y.astype(BF16), wglu_ref[...]))


def _s5_p_body(u_ref, wb_ref, wcr_ref, wci_ref, tab_ref, d_ref, wglu_ref,
               o_ref, hr_ref, hi_ref, bu_sc, car_sc, *, tb):
    @pl.when(pl.program_id(1) == 0)
    def _():
        car_sc[...] = jnp.zeros_like(car_sc)

    u = u_ref[...]
    bu_sc[...] = _dot(u.astype(BF16), wb_ref[...])

    for lg in range(N_S5 // LANE):
        cre = slice(lg * LANE, (lg + 1) * LANE)
        cim = slice(N_S5 + lg * LANE, N_S5 + (lg + 1) * LANE)
        tabs = [tab_ref[k, :, cre] for k in range(8)]

        def tile(rt, carry, cre=cre, cim=cim, tabs=tabs):
            cr, ci = carry
            r0 = pl.multiple_of(rt * SUBLANE, SUBLANE)
            xr = bu_sc[pl.ds(r0, SUBLANE), cre]
            xi = bu_sc[pl.ds(r0, SUBLANE), cim]
            for k, d in enumerate((1, 2, 4)):
                a_r, a_i = tabs[2 * k], tabs[2 * k + 1]
                sr = pltpu.roll(xr, d, 0)
                si = pltpu.roll(xi, d, 0)
                xr, xi = xr + a_r * sr - a_i * si, xi + a_r * si + a_i * sr
            p_r, p_i = tabs[6], tabs[7]
            xr, xi = xr + p_r * cr - p_i * ci, xi + p_r * ci + p_i * cr
            bu_sc[pl.ds(r0, SUBLANE), cre] = xr
            bu_sc[pl.ds(r0, SUBLANE), cim] = xi
            last = SUBLANE - 1
            return (jnp.broadcast_to(xr[last:last + 1, :], (SUBLANE, LANE)),
                    jnp.broadcast_to(xi[last:last + 1, :], (SUBLANE, LANE)))

        cr, ci = lax.fori_loop(0, tb // SUBLANE, tile, (car_sc[:, cre], car_sc[:, cim]), unroll=2)
        car_sc[:, cre] = cr
        car_sc[:, cim] = ci

    y = _s5_tail(bu_sc[:, :N_S5], bu_sc[:, N_S5:], u, wcr_ref, wci_ref, d_ref, wglu_ref)
    o_ref[...] = y.astype(o_ref.dtype)
    hr_ref[...] = car_sc[0:1, :N_S5]
    hi_ref[...] = car_sc[0:1, N_S5:]


def s5_prompt(proj3, wb, wcr, wci, tab, d_skip, wglu, *, tb=256):
    B, L = proj3.shape[:2]
    const2 = lambda b, t: (0, 0)
    return pl.pallas_call(
        functools.partial(_s5_p_body, tb=tb),
        grid=(B, L // tb),
        in_specs=[pl.BlockSpec((None, tb, D_A), lambda b, t: (b, t, 0)),
                  pl.BlockSpec((D_A, 2 * N_S5), const2),
                  pl.BlockSpec((N_S5, D_A), const2),
                  pl.BlockSpec((N_S5, D_A), const2),
                  pl.BlockSpec((8, SUBLANE, N_S5), lambda b, t: (0, 0, 0)),
                  pl.BlockSpec((1, D_A), const2),
                  pl.BlockSpec((D_A, D_A), const2)],
        out_specs=[pl.BlockSpec((None, tb, D_A), lambda b, t: (b, t, 0)),
                   pl.BlockSpec((None, 1, N_S5), lambda b, t: (b, 0, 0)),
                   pl.BlockSpec((None, 1, N_S5), lambda b, t: (b, 0, 0))],
        out_shape=[jax.ShapeDtypeStruct((B, L, D_A), BF16),
                   jax.ShapeDtypeStruct((B, 1, N_S5), F32),
                   jax.ShapeDtypeStruct((B, 1, N_S5), F32)],
        scratch_shapes=[pltpu.VMEM((tb, 2 * N_S5), F32), pltpu.VMEM((SUBLANE, 2 * N_S5), F32)],
        compiler_params=_cp(("arbitrary", "arbitrary")),
        name="s5_prompt",
    )(proj3, wb, wcr, wci, tab, d_skip, wglu)


def _s5_s_body(u_ref, h0r_ref, h0i_ref, wb_ref, wcr_ref, wci_ref, tab_ref, d_ref, wglu_ref,
               o_ref, hr_ref, hi_ref):
    u = u_ref[...]
    bu = _dot(u.astype(BF16), wb_ref[...])
    a_r = tab_ref[6, 0:1, :]
    a_i = tab_ref[7, 0:1, :]
    h0r = h0r_ref[...]
    h0i = h0i_ref[...]
    hr = a_r * h0r - a_i * h0i + bu[:, :N_S5]
    hi = a_r * h0i + a_i * h0r + bu[:, N_S5:]
    hr_ref[...] = hr
    hi_ref[...] = hi
    o_ref[...] = _s5_tail(hr, hi, u, wcr_ref, wci_ref, d_ref, wglu_ref).astype(o_ref.dtype)


def s5_sample(proj, h0r, h0i, wb, wcr, wci, tab, d_skip, wglu):
    R = proj.shape[0]
    z2 = lambda i: (0, 0)
    return pl.pallas_call(
        _s5_s_body,
        grid=(1,),
        in_specs=[pl.BlockSpec((R, D_A), z2),
                  pl.BlockSpec((R, N_S5), z2), pl.BlockSpec((R, N_S5), z2),
                  pl.BlockSpec((D_A, 2 * N_S5), z2),
                  pl.BlockSpec((N_S5, D_A), z2), pl.BlockSpec((N_S5, D_A), z2),
                  pl.BlockSpec((8, SUBLANE, N_S5), lambda i: (0, 0, 0)),
                  pl.BlockSpec((1, D_A), z2), pl.BlockSpec((D_A, D_A), z2)],
        out_specs=[pl.BlockSpec((R, D_A), z2), pl.BlockSpec((R, N_S5), z2), pl.BlockSpec((R, N_S5), z2)],
        out_shape=[jax.ShapeDtypeStruct((R, D_A), BF16),
                   jax.ShapeDtypeStruct((R, N_S5), F32),
                   jax.ShapeDtypeStruct((R, N_S5), F32)],
        compiler_params=_cp(("arbitrary",)),
        name="s5_sample",
    )(proj, h0r, h0i, wb, wcr, wci, tab, d_skip, wglu)


def _gmlp_p_body(u_ref, v_ref, gv_ref, wcat_ref, bias_ref, o_ref, *, tb):
    hd = D_B // H_B
    lane_head = lax.broadcasted_iota(jnp.int32, (GM_CHUNK, D_B), 1) // hd
    wi = lax.broadcasted_iota(jnp.int32, (GM_CHUNK, H_B * GM_CHUNK), 0)
    wj = lax.broadcasted_iota(jnp.int32, (GM_CHUNK, H_B * GM_CHUNK), 1) % GM_CHUNK
    wcat = jnp.where(wi >= wj, wcat_ref[...], 0.0).astype(BF16)
    for c in range(tb // GM_CHUNK):
        rows = slice(c * GM_CHUNK, (c + 1) * GM_CHUNK)
        vn = _rms(jax.nn.gelu(v_ref[rows, :])) * gv_ref[...]
        vb = vn.astype(BF16)
        stack = jnp.concatenate(
            [jnp.where(lane_head == h, vb, jnp.zeros_like(vb)) for h in range(H_B)], axis=0)
        s = _dot(wcat, stack) + bias_ref[...]
        o_ref[rows, :] = (jax.nn.gelu(u_ref[rows, :]) * s).astype(o_ref.dtype)


def gmlp_prompt(proj3, g_v, w_s, b_s, *, tb=512):
    B, L = proj3.shape[:2]
    hd = D_B // H_B
    wcat = jnp.transpose(w_s, (1, 0, 2)).reshape(GM_CHUNK, H_B * GM_CHUNK)
    bias = jnp.repeat(b_s.T, hd, axis=1)
    const2 = lambda b, t: (0, 0)
    ub = D_A // D_B
    return pl.pallas_call(
        functools.partial(_gmlp_p_body, tb=tb),
        grid=(B, L // tb),
        in_specs=[pl.BlockSpec((None, tb, D_B), lambda b, t: (b, t, ub)),
                  pl.BlockSpec((None, tb, D_B), lambda b, t: (b, t, ub + 1)),
                  pl.BlockSpec((1, D_B), const2),
                  pl.BlockSpec((GM_CHUNK, H_B * GM_CHUNK), const2),
                  pl.BlockSpec((GM_CHUNK, D_B), const2)],
        out_specs=pl.BlockSpec((None, tb, D_B), lambda b, t: (b, t, 0)),
        out_shape=jax.ShapeDtypeStruct((B, L, D_B), BF16),
        compiler_params=_cp(("arbitrary", "arbitrary")),
        name="gmlp_prompt",
    )(proj3, proj3, g_v, wcat, bias)


def _gmlp_s_body(u_ref, v_ref, gv_ref, w0_ref, b0_ref, o_ref, vn_ref):
    vn = _rms(jax.nn.gelu(v_ref[...])) * gv_ref[...]
    vn_ref[...] = vn
    s = w0_ref[...] * vn + b0_ref[...]
    o_ref[...] = (jax.nn.gelu(u_ref[...]) * s).astype(o_ref.dtype)


def gmlp_sample(proj, g_v, w_s, b_s):
    R = proj.shape[0]
    hd = D_B // H_B
    w0 = jnp.repeat(w_s[:, 0, 0], hd).reshape(1, D_B)
    b0 = jnp.repeat(b_s[:, 0], hd).reshape(1, D_B)
    z2 = lambda i: (0, 0)
    ub = D_A // D_B
    return pl.pallas_call(
        _gmlp_s_body,
        grid=(1,),
        in_specs=[pl.BlockSpec((R, D_B), lambda i: (0, ub)),
                  pl.BlockSpec((R, D_B), lambda i: (0, ub + 1)),
                  pl.BlockSpec((1, D_B), z2), pl.BlockSpec((1, D_B), z2), pl.BlockSpec((1, D_B), z2)],
        out_specs=[pl.BlockSpec((R, D_B), z2), pl.BlockSpec((R, D_B), z2)],
        out_shape=[jax.ShapeDtypeStruct((R, D_B), BF16), jax.ShapeDtypeStruct((R, D_B), F32)],
        compiler_params=_cp(("arbitrary",)),
        name="gmlp_sample",
    )(proj, proj, g_v, w0, b0)


def _head_expand_matrix():
    r = lax.broadcasted_iota(jnp.int32, (LANE, D_C), 0)
    c = lax.broadcasted_iota(jnp.int32, (LANE, D_C), 1) // HD_C
    return jnp.where(r == c, 1.0, 0.0).astype(BF16)


def _expand_heads(v, e):
    hi, lo = _split_bf16(v)
    return _dot(hi, e) + _dot(lo, e)


def _ssd_p_body(z_ref, xbc_ref, dt_ref, cw_ref, cb_ref, dtb_ref, alog_ref, dsk_ref, gn_ref,
                o_ref, st_ref, h_ref, h_sc, cv_sc, *, nt):
    Q = SSD_CHUNK
    t = pl.program_id(1)

    @pl.when(t == 0)
    def _():
        h_sc[...] = jnp.zeros_like(h_sc)
        cv_sc[0:SUBLANE, :] = jnp.zeros((SUBLANE, D_XBC), F32)

    xbc = xbc_ref[...]
    cv_sc[SUBLANE:SUBLANE + Q, :] = xbc
    acc = cb_ref[...]
    for k in range(K_C):
        off = SUBLANE - (K_C - 1) + k
        acc = acc + cw_ref[k:k + 1, :] * cv_sc[off:off + Q, :]
    cv_sc[0:SUBLANE, :] = xbc[Q - SUBLANE:, :]
    st_ref[...] = xbc[Q - (K_C - 1):, :]
    xc = _silu(acc)
    xs = xc[:, :D_C]

    dt = _softplus(dt_ref[...] + dtb_ref[...])
    a = -jnp.exp(alog_ref[...])
    da = dt * a
    ii = lax.broadcasted_iota(jnp.int32, (Q, Q), 0)
    jj = lax.broadcasted_iota(jnp.int32, (Q, Q), 1)
    causal = ii >= jj
    tril = jnp.where(causal, 1.0, 0.0).astype(BF16)
    d0, d1 = _split_bf16(da)
    d2 = (da - d0.astype(F32) - d1.astype(F32)).astype(BF16)
    cs = _dot(tril, d0) + _dot(tril, d1) + _dot(tril, d2)
    cst = cs.T
    cs_end = cs[Q - 1:Q, :]
    e = _head_expand_matrix()
    dt_f = _expand_heads(dt, e)
    ws_f = _expand_heads(dt * jnp.exp(cs_end - cs), e)
    ecs_f = _expand_heads(jnp.exp(cs), e)
    xdt = xs * dt_f
    xw = xs * ws_f

    lane = lax.broadcasted_iota(jnp.int32, (Q, LANE), 1)
    hpg = H_C // G_C
    gw = hpg * HD_C
    ys = []
    for g in range(G_C):
        bg = xc[:, D_C + g * N_C:D_C + (g + 1) * N_C].astype(BF16)
        cg = xc[:, D_C + G_C * N_C + g * N_C:D_C + G_C * N_C + (g + 1) * N_C].astype(BF16)
        gmat = _dot_nt(cg, bg)
        hprev = h_sc[g * gw:(g + 1) * gw, :]
        yoff = _dot_nt(cg, hprev.astype(BF16)) * ecs_f[:, g * gw:(g + 1) * gw]
        snew = _dot_tn(xw[:, g * gw:(g + 1) * gw].astype(BF16), bg)
        for hp in range(hpg // 2):
            h0 = g * hpg + 2 * hp
            xpair = xdt[:, h0 * HD_C:(h0 + 2) * HD_C]
            x_lo = jnp.where(lane < HD_C, xpair, 0.0).astype(BF16)
            x_hi = jnp.where(lane >= HD_C, xpair, 0.0).astype(BF16)
            yd = None
            for hh, xh in ((h0, x_lo), (h0 + 1, x_hi)):
                seg = cs[:, hh:hh + 1] - cst[hh:hh + 1, :]
                sc = (gmat * jnp.exp(jnp.where(causal, seg, NEG_BIG))).astype(BF16)
                part = _dot(sc, xh)
                yd = part if yd is None else yd + part
            ys.append(yd + yoff[:, 2 * hp * HD_C:(2 * hp + 2) * HD_C])
        for hh in range(hpg):
            h = g * hpg + hh
            cd = jnp.exp(cst[h:h + 1, Q - 1:Q])
            rows = slice(h * HD_C, (h + 1) * HD_C)
            h_sc[rows, :] = h_sc[rows, :] * cd + snew[hh * HD_C:(hh + 1) * HD_C, :]

    y = jnp.concatenate(ys, axis=1) + dsk_ref[...] * xs
    y = y * _silu(z_ref[...])
    o_ref[...] = (_rms(y) * gn_ref[...]).astype(o_ref.dtype)

    @pl.when(t == nt - 1)
    def _():
        h_ref[...] = h_sc[...]


def _ssd_params(dt_bias, a_log, d_skip):
    pad = LANE - H_C
    dtb = jnp.pad(dt_bias, (0, pad)).reshape(1, LANE)
    alog = jnp.pad(a_log, (0, pad)).reshape(1, LANE)
    dsk = jnp.repeat(d_skip, HD_C).reshape(1, D_C)
    return dtb, alog, dsk


def ssd_prompt(z3, xbc3, dt3, conv_w, conv_b, dt_bias, a_log, d_skip, g_norm):
    B, L = z3.shape[:2]
    Q = SSD_CHUNK
    nt = L // Q
    dtb, alog, dsk = _ssd_params(dt_bias, a_log, d_skip)
    const2 = lambda b, t: (0, 0)
    blk = lambda w: pl.BlockSpec((None, Q, w), lambda b, t: (b, t, 0))
    return pl.pallas_call(
        functools.partial(_ssd_p_body, nt=nt),
        grid=(B, nt),
        in_specs=[blk(D_C), blk(D_XBC), blk(LANE),
                  pl.BlockSpec((K_C, D_XBC), const2), pl.BlockSpec((1, D_XBC), const2),
                  pl.BlockSpec((1, LANE), const2), pl.BlockSpec((1, LANE), const2),
                  pl.BlockSpec((1, D_C), const2), pl.BlockSpec((1, D_C), const2)],
        out_specs=[blk(D_C),
                   pl.BlockSpec((None, K_C - 1, D_XBC), lambda b, t: (b, 0, 0)),
                   pl.BlockSpec((None, H_C * HD_C, N_C), lambda b, t: (b, 0, 0))],
        out_shape=[jax.ShapeDtypeStruct((B, L, D_C), BF16),
                   jax.ShapeDtypeStruct((B, K_C - 1, D_XBC), F32),
                   jax.ShapeDtypeStruct((B, H_C * HD_C, N_C), F32)],
        scratch_shapes=[pltpu.VMEM((H_C * HD_C, N_C), F32), pltpu.VMEM((SUBLANE + Q, D_XBC), F32)],
        compiler_params=_cp(("arbitrary", "arbitrary")),
        name="ssd_prompt",
    )(z3, xbc3, dt3, conv_w, conv_b.reshape(1, D_XBC), dtb, alog, dsk, g_norm.reshape(1, D_C))


def _ssd_s_body(z_ref, xbc_ref, dt_ref, buf_ref, h0_ref, cw_ref, cb_ref, dtb_ref, alog_ref,
                dsk_ref, gn_ref, o_ref, hn_ref,
                xs_sc, bc_sc, xt_hi, xt_lo, at_hi, at_lo, y_sc, *, tbatch, nsteps):
    s = pl.program_id(0)
    R = xs_sc.shape[0]

    @pl.when(s == 0)
    def _():
        acc = cb_ref[...] + cw_ref[K_C - 1:K_C, :] * xbc_ref[...]
        for k in range(K_C - 1):
            acc = acc + cw_ref[k:k + 1, :] * buf_ref[k]
        xc = _silu(acc)
        xs = xc[:, :D_C]
        xs_sc[...] = xs
        bc_sc[...] = xc[:, D_C:]
        dt = _softplus(dt_ref[...] + dtb_ref[...])
        e = _head_expand_matrix()
        dt_f = _expand_heads(dt, e)
        da_f = jnp.exp(_expand_heads(dt * (-jnp.exp(alog_ref[...])), e))
        hi, lo = _split_bf16((xs * dt_f).T)
        xt_hi[...] = hi
        xt_lo[...] = lo
        hi, lo = _split_bf16(da_f.T)
        at_hi[...] = hi
        at_lo[...] = lo

    gw = (H_C // G_C) * HD_C
    kk = lax.broadcasted_iota(jnp.int32, (R, LANE), 0)
    r0 = pl.multiple_of(s * tbatch, tbatch)
    bc8 = bc_sc[pl.ds(r0, tbatch), :]
    ti = lax.broadcasted_iota(jnp.int32, (tbatch, gw), 0)
    ytile = [jnp.zeros((tbatch, gw), F32) for _ in range(G_C)]
    for i in range(tbatch):
        b = s * tbatch + i
        onehot = jnp.where(kk == b, 1.0, 0.0).astype(BF16)
        xb = _dot(xt_hi[...], onehot) + _dot(xt_lo[...], onehot)
        ab = _dot(at_hi[...], onehot) + _dot(at_lo[...], onehot)
        for g in range(G_C):
            rows = slice(g * gw, (g + 1) * gw)
            brow = bc8[i:i + 1, g * N_C:(g + 1) * N_C]
            hn = h0_ref[i, rows, :] * ab[rows, :] + xb[rows, :] * brow
            hn_ref[i, rows, :] = hn
            c8 = bc8[:, G_C * N_C + g * N_C:G_C * N_C + (g + 1) * N_C]
            yg = _dot_nt(c8.astype(BF16), hn.astype(BF16))
            ytile[g] = jnp.where(ti == i, yg, ytile[g])
    for g in range(G_C):
        y_sc[pl.ds(r0, tbatch), g * gw:(g + 1) * gw] = ytile[g]

    @pl.when(s == nsteps - 1)
    def _():
        y = y_sc[...] + dsk_ref[...] * xs_sc[...]
        y = y * _silu(z_ref[...])
        o_ref[...] = (_rms(y) * gn_ref[...]).astype(o_ref.dtype)


def ssd_sample(z, xbc, dt, buf, h0, conv_w, conv_b, dt_bias, a_log, d_skip, g_norm, *, tbatch=8):
    R = z.shape[0]
    nsteps = R // tbatch
    dtb, alog, dsk = _ssd_params(dt_bias, a_log, d_skip)
    z2 = lambda s: (0, 0)
    hw = H_C * HD_C
    return pl.pallas_call(
        functools.partial(_ssd_s_body, tbatch=tbatch, nsteps=nsteps),
        grid=(nsteps,),
        in_specs=[pl.BlockSpec((R, D_C), z2), pl.BlockSpec((R, D_XBC), z2), pl.BlockSpec((R, LANE), z2),
                  pl.BlockSpec((K_C - 1, R, D_XBC), lambda s: (0, 0, 0)),
                  pl.BlockSpec((tbatch, hw, N_C), lambda s: (s, 0, 0)),
                  pl.BlockSpec((K_C, D_XBC), z2), pl.BlockSpec((1, D_XBC), z2),
                  pl.BlockSpec((1, LANE), z2), pl.BlockSpec((1, LANE), z2),
                  pl.BlockSpec((1, D_C), z2), pl.BlockSpec((1, D_C), z2)],
        out_specs=[pl.BlockSpec((R, D_C), z2),
                   pl.BlockSpec((tbatch, hw, N_C), lambda s: (s, 0, 0))],
        out_shape=[jax.ShapeDtypeStruct((R, D_C), BF16),
                   jax.ShapeDtypeStruct((R, hw, N_C), F32)],
        scratch_shapes=[pltpu.VMEM((R, D_C), F32), pltpu.VMEM((R, 2 * G_C * N_C), F32),
                        pltpu.VMEM((D_C, R), BF16), pltpu.VMEM((D_C, R), BF16),
                        pltpu.VMEM((D_C, R), BF16), pltpu.VMEM((D_C, R), BF16),
                        pltpu.VMEM((R, D_C), F32)],
        compiler_params=_cp(("arbitrary",)),
        name="ssd_sample",
    )(z, xbc, dt, buf, h0, conv_w, conv_b.reshape(1, D_XBC), dtb, alog, dsk, g_norm.reshape(1, D_C))


def _up_p_body(x_ref, wa_ref, wb_ref, cwa_ref, cwb_ref, cba_ref, cbb_ref,
               o_ref, sa_ref, sb_ref, wa_sc, wb_sc, ca_sc, cb_sc, *, tm, tiles_per_seq):
    i = pl.program_id(1)

    @pl.when(i == 0)
    def _():
        wa_sc[...] = wa_ref[...].astype(BF16)
        wb_sc[...] = wb_ref[...].astype(BF16)

    @pl.when(i % tiles_per_seq == 0)
    def _():
        ca_sc[0:SUBLANE, :] = jnp.zeros((SUBLANE, ca_sc.shape[1]), F32)
        cb_sc[0:SUBLANE, :] = jnp.zeros((SUBLANE, cb_sc.shape[1]), F32)

    x = x_ref[...]

    def half(w_sc, c_sc, cw_ref, cb_ref, st_ref):
        up = _dot(x, w_sc[...])
        c_sc[SUBLANE:SUBLANE + tm, :] = up
        acc = cb_ref[...]
        for k in range(K_F):
            off = SUBLANE - (K_F - 1) + k
            acc = acc + cw_ref[k:k + 1, :] * c_sc[off:off + tm, :]
        c_sc[0:SUBLANE, :] = up[tm - SUBLANE:, :]
        st_ref[...] = up[tm - (K_F - 1):, :]
        return acc

    a = half(wa_sc, ca_sc, cwa_ref, cba_ref, sa_ref)
    b = half(wb_sc, cb_sc, cwb_ref, cbb_ref, sb_ref)
    o_ref[...] = (_silu(a) * b).astype(o_ref.dtype)


def up_prompt(h2, w_up, l, conv_w, conv_b, *, seq_len, tm=1024, tn=512):
    M, D = h2.shape
    B = M // seq_len
    tiles_per_seq = seq_len // tm
    nj = D_FF // tn
    cw = conv_w
    cb = conv_b.reshape(1, 2 * D_FF)
    body = functools.partial(_up_p_body, tm=tm, tiles_per_seq=tiles_per_seq)
    st_spec = pl.BlockSpec((None, K_F - 1, tn), lambda j, i: (i // tiles_per_seq, 0, j))
    return pl.pallas_call(
        body,
        grid=(nj, M // tm),
        in_specs=[pl.BlockSpec((tm, D), lambda j, i: (i, 0)),
                  pl.BlockSpec((None, D, tn), lambda j, i: (l, 0, j)),
                  pl.BlockSpec((None, D, tn), lambda j, i: (l, 0, nj + j)),
                  pl.BlockSpec((K_F, tn), lambda j, i: (0, j)),
                  pl.BlockSpec((K_F, tn), lambda j, i: (0, nj + j)),
                  pl.BlockSpec((1, tn), lambda j, i: (0, j)),
                  pl.BlockSpec((1, tn), lambda j, i: (0, nj + j))],
        out_specs=[pl.BlockSpec((tm, tn), lambda j, i: (i, j)), st_spec, st_spec],
        out_shape=[jax.ShapeDtypeStruct((M, D_FF), BF16),
                   jax.ShapeDtypeStruct((B, K_F - 1, D_FF), F32),
                   jax.ShapeDtypeStruct((B, K_F - 1, D_FF), F32)],
        scratch_shapes=[pltpu.VMEM((D, tn), BF16), pltpu.VMEM((D, tn), BF16),
                        pltpu.VMEM((SUBLANE + tm, tn), F32), pltpu.VMEM((SUBLANE + tm, tn), F32)],
        compiler_params=_cp(("arbitrary", "arbitrary")),
        name="up_prompt",
    )(h2, w_up, w_up, cw, cw, cb, cb)


def _up_s_body(x_ref, wa_ref, wb_ref, bufa_ref, bufb_ref, cwa_ref, cwb_ref, cba_ref, cbb_ref,
               o_ref, ua_ref, ub_ref):
    x = x_ref[...]

    def half(w_ref, buf_ref, cw_ref, cb_ref, up_ref):
        up = _dot(x, w_ref[...].astype(BF16))
        up_ref[...] = up
        acc = cb_ref[...] + cw_ref[K_F - 1:K_F, :] * up
        for k in range(K_F - 1):
            acc = acc + cw_ref[k:k + 1, :] * buf_ref[k]
        return acc

    a = half(wa_ref, bufa_ref, cwa_ref, cba_ref, ua_ref)
    b = half(wb_ref, bufb_ref, cwb_ref, cbb_ref, ub_ref)
    o_ref[...] = (_silu(a) * b).astype(o_ref.dtype)


def up_sample(h2, w_up, l, buf, conv_w, conv_b, *, tn=512):
    R, D = h2.shape
    nj = D_FF // tn
    cb = conv_b.reshape(1, 2 * D_FF)
    out_spec = pl.BlockSpec((R, tn), lambda j: (0, j))
    return pl.pallas_call(
        _up_s_body,
        grid=(nj,),
        in_specs=[pl.BlockSpec((R, D), lambda j: (0, 0)),
                  pl.BlockSpec((None, D, tn), lambda j: (l, 0, j)),
                  pl.BlockSpec((None, D, tn), lambda j: (l, 0, nj + j)),
                  pl.BlockSpec((K_F - 1, R, tn), lambda j: (0, 0, j)),
                  pl.BlockSpec((K_F - 1, R, tn), lambda j: (0, 0, nj + j)),
                  pl.BlockSpec((K_F, tn), lambda j: (0, j)),
                  pl.BlockSpec((K_F, tn), lambda j: (0, nj + j)),
                  pl.BlockSpec((1, tn), lambda j: (0, j)),
                  pl.BlockSpec((1, tn), lambda j: (0, nj + j))],
        out_specs=[out_spec, out_spec, out_spec],
        out_shape=[jax.ShapeDtypeStruct((R, D_FF), BF16),
                   jax.ShapeDtypeStruct((R, D_FF), F32),
                   jax.ShapeDtypeStruct((R, D_FF), F32)],
        compiler_params=_cp(("arbitrary",)),
        name="up_sample",
    )(h2, w_up, w_up, buf, buf, conv_w, conv_w, cb, cb)


def _layer(x3, mods, l, p, state, *, seq_len, tm, tn_merge, tm_norm):
    B, L, D = x3.shape
    M = B * L
    sh_m, sc_m, gt_m, sh_f, sc_f, gt_f = mods
    tiles_per_seq = max(seq_len // tm, 1)
    prompt = state is None

    h = norm_mod(x3, p['g_mix'][l].reshape(1, D), sc_m, sh_m, tm=tm_norm).reshape(M, D)

    def proj(col0, n, tn, name):
        return matmul([h], [(p['w_in'], l, col0)], [], [(n, F32)], _epi_store,
                      x_of_w=(0,), tm=tm, tn=tn, name=name)[0]

    uav = proj(OFF_UA, D_A + 2 * D_B, 512, "proj_uav")
    zc = proj(OFF_Z, D_C, 512, "proj_z")
    xbc = proj(OFF_XBC, D_XBC, 512, "proj_xbc")
    dtp = proj(OFF_DT, LANE, LANE, "proj_dt")

    tab, wb = s5_prep(p['lam_re'][l], p['lam_im'][l], p['log_dt'][l], p['b_re'][l], p['b_im'][l])
    wcr, wci = s5_out_weights(p['c_re'][l], p['c_im'][l])
    d_a = p['s5_d'][l].reshape(1, D_A)
    wglu = p['w_glu'][l].astype(BF16)
    g_v = p['g_v'][l].reshape(1, D_B)

    if prompt:
        o_a, s5r, s5i = s5_prompt(uav.reshape(B, L, -1), wb, wcr, wci, tab, d_a, wglu)
        o_b = gmlp_prompt(uav.reshape(B, L, -1), g_v, p['w_s'][l], p['b_s'][l])
        v_rows = None
        o_c, convc, ssm = ssd_prompt(zc.reshape(B, L, -1), xbc.reshape(B, L, -1), dtp.reshape(B, L, -1),
                                     p['ssd_conv_w'][l], p['ssd_conv_b'][l], p['dt_bias'][l],
                                     p['a_log'][l], p['ssd_d'][l], p['ssd_g'][l])
        s5r = s5r.reshape(B, G_A, P_A)
        s5i = s5i.reshape(B, G_A, P_A)
    else:
        s5_re0, s5_im0, ssm0, convc0, convf0 = state
        o_a, s5r, s5i = s5_sample(uav, s5_re0.reshape(M, N_S5), s5_im0.reshape(M, N_S5),
                                  wb, wcr, wci, tab, d_a, wglu)
        o_b, v_rows = gmlp_sample(uav, g_v, p['w_s'][l], p['b_s'][l])
        o_c, ssm = ssd_sample(zc, xbc, dtp, jnp.transpose(convc0, (1, 0, 2)),
                              ssm0.reshape(M, H_C * HD_C, N_C),
                              p['ssd_conv_w'][l], p['ssd_conv_b'][l], p['dt_bias'][l],
                              p['a_log'][l], p['ssd_d'][l], p['ssd_g'][l])
        convc = jnp.concatenate([convc0[:, 1:], xbc[:, None, :]], axis=1)
        s5r = s5r.reshape(M, G_A, P_A)
        s5i = s5i.reshape(M, G_A, P_A)
    o_a = o_a.reshape(M, D_A)
    o_b = o_b.reshape(M, D_B)
    o_c = o_c.reshape(M, D_C)
    ssm = ssm.reshape(-1, H_C, HD_C, N_C)

    w_g = p['w_in'][l][:, OFF_GATES:][None]
    merged = matmul(
        [h, o_a, o_b, o_c],
        [(w_g, 0, 0), (w_g, 0, D), (w_g, 0, 2 * D),
         (p['w_pa'], l, 0), (p['w_pb'], l, 0), (p['w_pc'], l, 0)],
        [], [(D, BF16)], _epi_merge, x_of_w=(0, 0, 0, 1, 2, 3), tm=tm, tn=tn_merge, name="merge")[0]

    def residual(x_res, act_in, w3, gt, tm_r, name, vmem_mb):
        tn = 512
        return matmul([act_in], [(w3, l, 0)],
                      [(x_res, _tile_spec(tm_r, tn)),
                       (gt, _gate_spec(gt, tm_r, tn, max(seq_len // tm_r, 1)))],
                      [(D, F32)], _epi_residual, x_of_w=(0,), tm=tm_r, tn=tn, name=name,
                      vmem_mb=vmem_mb)[0]

    x2 = residual(x3.reshape(M, D), merged, p['w_out'], gt_m, tm, "out_proj", 48)

    h2 = norm_mod(x2.reshape(B, L, D), p['g_ffn'][l].reshape(1, D), sc_f, sh_f, tm=tm_norm).reshape(M, D)
    if prompt:
        act, st_a, st_b = up_prompt(h2, p['w_up'], l, p['ffn_conv_w'][l], p['ffn_conv_b'][l],
                                    seq_len=seq_len, tm=tm)
        convf = jnp.concatenate([st_a, st_b], axis=-1)
    else:
        act, up_a, up_b = up_sample(h2, p['w_up'], l, jnp.transpose(convf0, (1, 0, 2)),
                                    p['ffn_conv_w'][l], p['ffn_conv_b'][l])
        convf = jnp.concatenate([convf0[:, 1:], jnp.concatenate([up_a, up_b], axis=-1)[:, None, :]], axis=1)

    x2 = residual(x2, act, p['w_down'], gt_f, min(tm, 512), "down_proj", 56)
    return x2.reshape(B, L, D), s5r, s5i, ssm, convc, convf, v_rows


def kernel(x_prompt, x_sample, c_prompt, c_sample, state_s5_re, state_s5_im, state_ssm, state_ssd_conv, state_ffn_conv, w_mod, b_mod, g_mix, w_in, s5_lam_re, s5_lam_im, s5_log_dt, s5_b_re, s5_b_im, s5_c_re, s5_c_im, s5_d, s5_w_glu, gm_g_v, gm_w_s, gm_b_s, ssd_conv_w, ssd_conv_b, ssd_dt_bias, ssd_a_log, ssd_d, ssd_g_norm, w_pa, w_pb, w_pc, w_out, g_ffn, ffn_w_up, ffn_conv_w, ffn_conv_b, ffn_w_down, g_final):
    p = {
        'g_mix': g_mix, 'w_in': w_in,
        'lam_re': s5_lam_re, 'lam_im': s5_lam_im, 'log_dt': s5_log_dt,
        'b_re': s5_b_re, 'b_im': s5_b_im, 'c_re': s5_c_re, 'c_im': s5_c_im,
        's5_d': s5_d, 'w_glu': s5_w_glu,
        'g_v': gm_g_v, 'w_s': gm_w_s, 'b_s': gm_b_s,
        'ssd_conv_w': ssd_conv_w, 'ssd_conv_b': ssd_conv_b, 'dt_bias': ssd_dt_bias,
        'a_log': ssd_a_log, 'ssd_d': ssd_d, 'ssd_g': ssd_g_norm,
        'w_pa': w_pa, 'w_pb': w_pb, 'w_pc': w_pc, 'w_out': w_out,
        'g_ffn': g_ffn, 'w_up': ffn_w_up, 'ffn_conv_w': ffn_conv_w,
        'ffn_conv_b': ffn_conv_b, 'w_down': ffn_w_down,
    }
    bp, seq, D = x_prompt.shape
    bs = x_sample.shape[0]

    n_c = bs + bp
    pad = (-n_c) % SUBLANE
    c_all = jnp.concatenate([c_sample, c_prompt, jnp.zeros((pad, D), F32)], axis=0)
    mod = mod_all(c_all, w_mod, b_mod)

    xp = x_prompt
    xs = x_sample.reshape(1, bs, D)
    outs_p = [[] for _ in range(5)]
    outs_s = [[] for _ in range(6)]
    for l in range(DEPTH):
        mods_s = [m[None] for m in jnp.split(mod[l, :bs], 6, axis=-1)]
        mods_p = [m[:, None, :] for m in jnp.split(mod[l, bs:bs + bp], 6, axis=-1)]
        xp, *st_p = _layer(xp, mods_p, l, p, None, seq_len=seq, tm=1024, tn_merge=256, tm_norm=512)
        for acc, v in zip(outs_p, st_p[:5]):
            acc.append(v)
        state = (state_s5_re[l], state_s5_im[l], state_ssm[l], state_ssd_conv[l], state_ffn_conv[l])
        xs, *st_s = _layer(xs, mods_s, l, p, state, seq_len=1, tm=bs, tn_merge=256, tm_norm=bs)
        st_s[5] = st_s[5].reshape(bs, 1, D_B)
        for acc, v in zip(outs_s, st_s):
            acc.append(v)

    g_fin = g_final.reshape(1, D)
    y_prompt = final_norm(xp, g_fin, tm=512)
    y_sample = final_norm(xs, g_fin, tm=bs).reshape(bs, 1, D)
    return (y_prompt, y_sample,
            *[jnp.stack(v) for v in outs_p],
            *[jnp.stack(v) for v in outs_s])
```

```python
import functools

import jax
import jax.numpy as jnp
from jax import lax
from jax.experimental import pallas as pl
from jax.experimental.pallas import tpu as pltpu

F32 = jnp.float32
BF16 = jnp.bfloat16

D_MODEL = 2048
DEPTH = 2
D_A = 512
S5_GROUP = 16
G_A = D_A // S5_GROUP
P_A = 64
N_S5 = G_A * P_A
S5_GPB = 128 // S5_GROUP
S5_NB = D_A // 128
S5_SPB = S5_GPB * P_A
D_B = 512
H_B = 8
GM_CHUNK = 128
D_C = 1024
HD_C = 64
H_C = D_C // HD_C
N_C = 128
G_C = 2
K_C = 4
SSD_CHUNK = 128
D_XBC = D_C + 2 * G_C * N_C
D_FF = 5632
K_F = 3
EPS = 1e-6

OFF_UA = 0
OFF_Z = D_A + 2 * D_B
OFF_XBC = OFF_Z + D_C
OFF_DT = OFF_XBC + D_XBC
OFF_GATES = OFF_DT + H_C

LANE = 128
SUBLANE = 8
NEG_BIG = -1e30


def _cp(sem, vmem_mb=48):
    return pltpu.CompilerParams(dimension_semantics=sem, vmem_limit_bytes=vmem_mb * 1024 * 1024)


def _silu(x):
    return x * jax.nn.sigmoid(x)


def _softplus(x):
    return jnp.maximum(x, 0.0) + jnp.log1p(jnp.exp(-jnp.abs(x)))


def _rms(x):
    return x * lax.rsqrt(jnp.mean(x * x, axis=-1, keepdims=True) + EPS)


def _split_bf16(x):
    hi = x.astype(BF16)
    lo = (x - hi.astype(F32)).astype(BF16)
    return hi, lo


def _dot(a, b):
    return jnp.dot(a, b, preferred_element_type=F32)


def _dot_nt(a, b):
    return lax.dot_general(a, b, (((1,), (1,)), ((), ())), preferred_element_type=F32)


def _dot_tn(a, b):
    return lax.dot_general(a, b, (((0,), (0,)), ((), ())), preferred_element_type=F32)


def _mod_body(c_ref, w_ref, b_ref, o_ref):
    a = _silu(c_ref[...]).astype(BF16)
    o_ref[...] = _dot(a, w_ref[...].astype(BF16)) + b_ref[...]


def mod_all(c_all, w_mod, b_mod, *, tn=1024):
    R, D = c_all.shape
    N = w_mod.shape[-1]
    return pl.pallas_call(
        _mod_body,
        grid=(DEPTH, N // tn),
        in_specs=[pl.BlockSpec((R, D), lambda l, j: (0, 0)),
                  pl.BlockSpec((None, D, tn), lambda l, j: (l, 0, j)),
                  pl.BlockSpec((None, 1, tn), lambda l, j: (l, 0, j))],
        out_specs=pl.BlockSpec((None, R, tn), lambda l, j: (l, 0, j)),
        out_shape=jax.ShapeDtypeStruct((DEPTH, R, N), F32),
        compiler_params=_cp(("arbitrary", "arbitrary")),
        name="mod_all",
    )(c_all, w_mod, b_mod.reshape(DEPTH, 1, N))


def _norm_mod_body(x_ref, g_ref, sc_ref, sh_ref, o_ref):
    y = _rms(x_ref[...]) * g_ref[...]
    o_ref[...] = (y * (1.0 + sc_ref[...]) + sh_ref[...]).astype(o_ref.dtype)


def norm_mod(x3, g, sc3, sh3, *, tm):
    B, L, D = x3.shape
    per_row = sc3.shape[1] != 1
    ts = tm if per_row else 1
    sidx = (lambda b, i: (b, i, 0)) if per_row else (lambda b, i: (b, 0, 0))
    return pl.pallas_call(
        _norm_mod_body,
        grid=(B, L // tm),
        in_specs=[pl.BlockSpec((None, tm, D), lambda b, i: (b, i, 0)),
                  pl.BlockSpec((1, D), lambda b, i: (0, 0)),
                  pl.BlockSpec((None, ts, D), sidx),
                  pl.BlockSpec((None, ts, D), sidx)],
        out_specs=pl.BlockSpec((None, tm, D), lambda b, i: (b, i, 0)),
        out_shape=jax.ShapeDtypeStruct((B, L, D), BF16),
        compiler_params=_cp(("arbitrary", "arbitrary")),
        name="norm_mod",
    )(x3, g, sc3, sh3)


def _rms_body(x_ref, g_ref, o_ref):
    o_ref[...] = _rms(x_ref[...]) * g_ref[...]


def final_norm(x3, g, *, tm):
    B, L, D = x3.shape
    return pl.pallas_call(
        _rms_body,
        grid=(B, L // tm),
        in_specs=[pl.BlockSpec((None, tm, D), lambda b, i: (b, i, 0)),
                  pl.BlockSpec((1, D), lambda b, i: (0, 0))],
        out_specs=pl.BlockSpec((None, tm, D), lambda b, i: (b, i, 0)),
        out_shape=jax.ShapeDtypeStruct((B, L, D), F32),
        compiler_params=_cp(("arbitrary", "arbitrary")),
        name="final_norm",
    )(x3, g)


def _mm_body(*refs, x_of_w, w_is_t, n_x, n_e, n_o, epilogue):
    n_w = len(x_of_w)
    xs = refs[:n_x]
    ws = refs[n_x:n_x + n_w]
    es = refs[n_x + n_w:n_x + n_w + n_e]
    outs = refs[n_x + n_w + n_e:n_x + n_w + n_e + n_o]
    wsc = refs[n_x + n_w + n_e + n_o:]

    @pl.when(pl.program_id(1) == 0)
    def _():
        for w, s, is_t in zip(ws, wsc, w_is_t):
            s[...] = (w[...].T if is_t else w[...]).astype(BF16)

    accs = [_dot(xs[xi][...], s[...]) for xi, s in zip(x_of_w, wsc)]
    epilogue(accs, es, outs)


def matmul(xs, ws, extras, outs, epilogue, *, x_of_w, tm, tn, name, vmem_mb=48):
    M = xs[0].shape[0]
    N = outs[0][0]
    in_specs = [pl.BlockSpec((tm, x.shape[1]), lambda j, i: (i, 0)) for x in xs]
    args = list(xs)
    scratch = []
    for w3, l, col0, is_t in ws:
        if is_t:
            K = w3.shape[2]
            assert col0 % SUBLANE == 0
            in_specs.append(pl.BlockSpec((None, pl.Element(tn), pl.Element(K)), functools.partial(
                lambda j, i, l, c: (l, (c + j * (tn // SUBLANE)) * SUBLANE, 0),
                l=l, c=col0 // SUBLANE)))
        else:
            K = w3.shape[1]
            assert col0 % tn == 0
            in_specs.append(pl.BlockSpec((None, K, tn), functools.partial(
                lambda j, i, l, c: (l, 0, c + j), l=l, c=col0 // tn)))
        args.append(w3)
        scratch.append(pltpu.VMEM((K, tn), BF16))
    for arr, spec in extras:
        in_specs.append(spec)
        args.append(arr)
    body = functools.partial(_mm_body, x_of_w=tuple(x_of_w), w_is_t=tuple(w[3] for w in ws),
                             n_x=len(xs), n_e=len(extras), n_o=len(outs), epilogue=epilogue)
    res = pl.pallas_call(
        body,
        grid=(N // tn, M // tm),
        in_specs=in_specs,
        out_specs=[pl.BlockSpec((tm, tn), lambda j, i: (i, j)) for _ in outs],
        out_shape=[jax.ShapeDtypeStruct((M, n), dt) for n, dt in outs],
        scratch_shapes=scratch,
        compiler_params=_cp(("arbitrary", "arbitrary"), vmem_mb),
        name=name,
    )(*args)
    return res


def _epi_store(accs, es, outs):
    outs[0][...] = accs[0].astype(outs[0].dtype)


def _epi_merge(accs, es, outs):
    ga, gb, gc, pa, pb, pc = accs
    m = jax.nn.sigmoid(ga) * pa + jax.nn.sigmoid(gb) * pb + jax.nn.sigmoid(gc) * pc
    outs[0][...] = m.astype(outs[0].dtype)


def _epi_residual(accs, es, outs):
    res_ref, gt_ref = es
    outs[0][...] = res_ref[...] + gt_ref[...] * accs[0]


def _tile_spec(tm, tn):
    return pl.BlockSpec((tm, tn), lambda j, i: (i, j))


def _gate_spec(gt3, tm, tn, tiles_per_seq):
    if gt3.shape[1] == 1:
        return pl.BlockSpec((None, 1, tn), lambda j, i: (i // tiles_per_seq, 0, j))
    return pl.BlockSpec((None, tm, tn), lambda j, i: (0, i, j))


def _s5_prep_body(lrf_ref, lif_ref, ldf_ref, lrr_ref, lir_ref, ldr_ref, bre_ref, bim_ref,
                  tab_ref, bbr_ref, bbi_ref):
    dtf = jnp.exp(ldf_ref[...])
    n = (lax.broadcasted_iota(jnp.int32, (SUBLANE, N_S5), 0) + 1).astype(F32)
    mag = jnp.exp(n * (lrf_ref[...] * dtf))
    ang = n * (lif_ref[...] * dtf)
    pr = mag * jnp.cos(ang)
    pi = mag * jnp.sin(ang)
    row = lax.broadcasted_iota(jnp.int32, (SUBLANE, N_S5), 0)
    for k, d in enumerate((1, 2, 4)):
        keep = row >= d
        tab_ref[2 * k] = jnp.where(keep, pr[d - 1:d, :], 0.0)
        tab_ref[2 * k + 1] = jnp.where(keep, pi[d - 1:d, :], 0.0)
    tab_ref[6] = pr
    tab_ref[7] = pi
    dtr = jnp.exp(ldr_ref[...])
    lr = lrr_ref[...]
    li = lir_ref[...]
    m1 = jnp.exp(lr * dtr)
    ar = m1 * jnp.cos(li * dtr)
    ai = m1 * jnp.sin(li * dtr)
    den = lr * lr + li * li
    nr = ar - 1.0
    kr = (nr * lr + ai * li) / den
    ki = (ai * lr - nr * li) / den
    bre = bre_ref[...]
    bim = bim_ref[...]
    bbr_ref[...] = kr * bre - ki * bim
    bbi_ref[...] = kr * bim + ki * bre


def s5_prep(lam_re, lam_im, log_dt, b_re, b_im):
    lrf = lam_re.reshape(1, N_S5)
    lif = lam_im.reshape(1, N_S5)
    ldf = jnp.repeat(log_dt, P_A).reshape(1, N_S5)
    lrr = jnp.repeat(lam_re, S5_GROUP, axis=0)
    lir = jnp.repeat(lam_im, S5_GROUP, axis=0)
    ldr = jnp.repeat(log_dt, S5_GROUP).reshape(D_A, 1)
    bre = jnp.transpose(b_re, (0, 2, 1)).reshape(D_A, P_A)
    bim = jnp.transpose(b_im, (0, 2, 1)).reshape(D_A, P_A)
    tab, bbr, bbi = pl.pallas_call(
        _s5_prep_body,
        out_shape=[jax.ShapeDtypeStruct((8, SUBLANE, N_S5), F32),
                   jax.ShapeDtypeStruct((D_A, P_A), F32),
                   jax.ShapeDtypeStruct((D_A, P_A), F32)],
        name="s5_prep",
    )(lrf, lif, ldf, lrr, lir, ldr, bre, bim)
    eye = jnp.eye(S5_GPB, dtype=F32)

    def blockdiag(m):
        m = m.reshape(S5_NB, S5_GPB, S5_GROUP, P_A)
        return jnp.einsum('kghp,gj->kghjp', m, eye).reshape(S5_NB, LANE, S5_SPB)

    wb = jnp.concatenate([blockdiag(bbr), blockdiag(bbi)], axis=2).astype(BF16)
    return tab, wb


def s5_out_weights(c_re, c_im):
    eye = jnp.eye(S5_GPB, dtype=F32)

    def blockdiag(c):
        c = c.reshape(S5_NB, S5_GPB, S5_GROUP, P_A)
        return jnp.einsum('kghp,gj->kgpjh', c, eye).reshape(S5_NB, S5_SPB, LANE).astype(BF16)

    return blockdiag(c_re), blockdiag(c_im)


def _s5_in(ub, wb_ref, kb):
    return _dot(ub[:, kb * LANE:(kb + 1) * LANE], wb_ref[kb])


def _s5_tail(h_blocks, u, wcr_ref, wci_ref, d_ref, wglu_ref):
    ys = []
    for kb in range(S5_NB):
        hr, hi = h_blocks(kb)
        ys.append(_dot(hr.astype(BF16), wcr_ref[kb]) - _dot(hi.astype(BF16), wci_ref[kb]))
    y = jnp.concatenate(ys, axis=1) + d_ref[...] * u
    y = jax.nn.gelu(y)
    return y * jax.nn.sigmoid(_dot(y.astype(BF16), wglu_ref[...]))


def _s5_p_body(u_ref, wb_ref, wcr_ref, wci_ref, tab_ref, d_ref, wglu_ref,
               o_ref, hr_ref, hi_ref, bu_sc, car_sc, *, tb):
    @pl.when(pl.program_id(1) == 0)
    def _():
        car_sc[...] = jnp.zeros_like(car_sc)

    u = u_ref[...]
    ub = u.astype(BF16)
    for kb in range(S5_NB):
        bu = _s5_in(ub, wb_ref, kb)
        bu_sc[:, kb * S5_SPB:(kb + 1) * S5_SPB] = bu[:, :S5_SPB]
        bu_sc[:, N_S5 + kb * S5_SPB:N_S5 + (kb + 1) * S5_SPB] = bu[:, S5_SPB:]

    def tile(rt, _):
        r0 = pl.multiple_of(rt * SUBLANE, SUBLANE)
        for lg in range(N_S5 // LANE):
            cre = slice(lg * LANE, (lg + 1) * LANE)
            cim = slice(N_S5 + lg * LANE, N_S5 + (lg + 1) * LANE)
            xr = bu_sc[pl.ds(r0, SUBLANE), cre]
            xi = bu_sc[pl.ds(r0, SUBLANE), cim]
            for k, d in enumerate((1, 2, 4)):
                a_r = tab_ref[2 * k, :, cre]
                a_i = tab_ref[2 * k + 1, :, cre]
                sr = pltpu.roll(xr, d, 0)
                si = pltpu.roll(xi, d, 0)
                xr, xi = xr + a_r * sr - a_i * si, xi + a_r * si + a_i * sr
            p_r = tab_ref[6, :, cre]
            p_i = tab_ref[7, :, cre]
            cr = car_sc[:, cre]
            ci = car_sc[:, cim]
            xr, xi = xr + p_r * cr - p_i * ci, xi + p_r * ci + p_i * cr
            bu_sc[pl.ds(r0, SUBLANE), cre] = xr
            bu_sc[pl.ds(r0, SUBLANE), cim] = xi
            last = SUBLANE - 1
            car_sc[:, cre] = jnp.broadcast_to(xr[last:last + 1, :], (SUBLANE, LANE))
            car_sc[:, cim] = jnp.broadcast_to(xi[last:last + 1, :], (SUBLANE, LANE))
        return 0

    lax.fori_loop(0, tb // SUBLANE, tile, 0)

    def h_blocks(kb):
        return (bu_sc[:, kb * S5_SPB:(kb + 1) * S5_SPB],
                bu_sc[:, N_S5 + kb * S5_SPB:N_S5 + (kb + 1) * S5_SPB])

    y = _s5_tail(h_blocks, u, wcr_ref, wci_ref, d_ref, wglu_ref)
    o_ref[...] = y.astype(o_ref.dtype)
    hr_ref[...] = car_sc[0:1, :N_S5]
    hi_ref[...] = car_sc[0:1, N_S5:]


def s5_prompt(proj3, wb, wcr, wci, tab, d_skip, wglu, *, tb=256):
    B, L = proj3.shape[:2]
    const2 = lambda b, t: (0, 0)
    return pl.pallas_call(
        functools.partial(_s5_p_body, tb=tb),
        grid=(B, L // tb),
        in_specs=[pl.BlockSpec((None, tb, D_A), lambda b, t: (b, t, 0)),
                  pl.BlockSpec((S5_NB, LANE, 2 * S5_SPB), lambda b, t: (0, 0, 0)),
                  pl.BlockSpec((S5_NB, S5_SPB, LANE), lambda b, t: (0, 0, 0)),
                  pl.BlockSpec((S5_NB, S5_SPB, LANE), lambda b, t: (0, 0, 0)),
                  pl.BlockSpec((8, SUBLANE, N_S5), lambda b, t: (0, 0, 0)),
                  pl.BlockSpec((1, D_A), const2),
                  pl.BlockSpec((D_A, D_A), const2)],
        out_specs=[pl.BlockSpec((None, tb, D_A), lambda b, t: (b, t, 0)),
                   pl.BlockSpec((None, 1, N_S5), lambda b, t: (b, 0, 0)),
                   pl.BlockSpec((None, 1, N_S5), lambda b, t: (b, 0, 0))],
        out_shape=[jax.ShapeDtypeStruct((B, L, D_A), BF16),
                   jax.ShapeDtypeStruct((B, 1, N_S5), F32),
                   jax.ShapeDtypeStruct((B, 1, N_S5), F32)],
        scratch_shapes=[pltpu.VMEM((tb, 2 * N_S5), F32), pltpu.VMEM((SUBLANE, 2 * N_S5), F32)],
        compiler_params=_cp(("arbitrary", "arbitrary")),
        name="s5_prompt",
    )(proj3, wb, wcr, wci, tab, d_skip, wglu)


def _s5_s_body(u_ref, h0r_ref, h0i_ref, wb_ref, wcr_ref, wci_ref, tab_ref, d_ref, wglu_ref,
               o_ref, hr_ref, hi_ref):
    u = u_ref[...]
    ub = u.astype(BF16)
    for kb in range(S5_NB):
        cols = slice(kb * S5_SPB, (kb + 1) * S5_SPB)
        bu = _s5_in(ub, wb_ref, kb)
        a_r = tab_ref[6, 0:1, cols]
        a_i = tab_ref[7, 0:1, cols]
        h0r = h0r_ref[:, cols]
        h0i = h0i_ref[:, cols]
        hr_ref[:, cols] = a_r * h0r - a_i * h0i + bu[:, :S5_SPB]
        hi_ref[:, cols] = a_r * h0i + a_i * h0r + bu[:, S5_SPB:]

    def h_blocks(kb):
        cols = slice(kb * S5_SPB, (kb + 1) * S5_SPB)
        return hr_ref[:, cols], hi_ref[:, cols]

    o_ref[...] = _s5_tail(h_blocks, u, wcr_ref, wci_ref, d_ref, wglu_ref).astype(o_ref.dtype)


def s5_sample(proj, h0r, h0i, wb, wcr, wci, tab, d_skip, wglu):
    R = proj.shape[0]
    z2 = lambda i: (0, 0)
    return pl.pallas_call(
        _s5_s_body,
        grid=(1,),
        in_specs=[pl.BlockSpec((R, D_A), z2),
                  pl.BlockSpec((R, N_S5), z2), pl.BlockSpec((R, N_S5), z2),
                  pl.BlockSpec((S5_NB, LANE, 2 * S5_SPB), lambda i: (0, 0, 0)),
                  pl.BlockSpec((S5_NB, S5_SPB, LANE), lambda i: (0, 0, 0)),
                  pl.BlockSpec((S5_NB, S5_SPB, LANE), lambda i: (0, 0, 0)),
                  pl.BlockSpec((8, SUBLANE, N_S5), lambda i: (0, 0, 0)),
                  pl.BlockSpec((1, D_A), z2), pl.BlockSpec((D_A, D_A), z2)],
        out_specs=[pl.BlockSpec((R, D_A), z2), pl.BlockSpec((R, N_S5), z2), pl.BlockSpec((R, N_S5), z2)],
        out_shape=[jax.ShapeDtypeStruct((R, D_A), BF16),
                   jax.ShapeDtypeStruct((R, N_S5), F32),
                   jax.ShapeDtypeStruct((R, N_S5), F32)],
        compiler_params=_cp(("arbitrary",)),
        name="s5_sample",
    )(proj, h0r, h0i, wb, wcr, wci, tab, d_skip, wglu)


def _gmlp_p_body(u_ref, v_ref, gv_ref, wcat_ref, bias_ref, o_ref, *, tb):
    hd = D_B // H_B
    lane_head = lax.broadcasted_iota(jnp.int32, (GM_CHUNK, D_B), 1) // hd
    wi = lax.broadcasted_iota(jnp.int32, (GM_CHUNK, H_B * GM_CHUNK), 0)
    wj = lax.broadcasted_iota(jnp.int32, (GM_CHUNK, H_B * GM_CHUNK), 1) % GM_CHUNK
    wcat = jnp.where(wi >= wj, wcat_ref[...], 0.0).astype(BF16)
    for c in range(tb // GM_CHUNK):
        rows = slice(c * GM_CHUNK, (c + 1) * GM_CHUNK)
        vn = _rms(jax.nn.gelu(v_ref[rows, :])) * gv_ref[...]
        vb = vn.astype(BF16)
        stack = jnp.concatenate(
            [jnp.where(lane_head == h, vb, jnp.zeros_like(vb)) for h in range(H_B)], axis=0)
        s = _dot(wcat, stack) + bias_ref[...]
        o_ref[rows, :] = (jax.nn.gelu(u_ref[rows, :]) * s).astype(o_ref.dtype)


def gmlp_prompt(proj3, g_v, w_s, b_s, *, tb=512):
    B, L = proj3.shape[:2]
    hd = D_B // H_B
    wcat = jnp.transpose(w_s, (1, 0, 2)).reshape(GM_CHUNK, H_B * GM_CHUNK)
    bias = jnp.repeat(b_s.T, hd, axis=1)
    const2 = lambda b, t: (0, 0)
    ub = D_A // D_B
    return pl.pallas_call(
        functools.partial(_gmlp_p_body, tb=tb),
        grid=(B, L // tb),
        in_specs=[pl.BlockSpec((None, tb, D_B), lambda b, t: (b, t, ub)),
                  pl.BlockSpec((None, tb, D_B), lambda b, t: (b, t, ub + 1)),
                  pl.BlockSpec((1, D_B), const2),
                  pl.BlockSpec((GM_CHUNK, H_B * GM_CHUNK), const2),
                  pl.BlockSpec((GM_CHUNK, D_B), const2)],
        out_specs=pl.BlockSpec((None, tb, D_B), lambda b, t: (b, t, 0)),
        out_shape=jax.ShapeDtypeStruct((B, L, D_B), BF16),
        compiler_params=_cp(("arbitrary", "arbitrary")),
        name="gmlp_prompt",
    )(proj3, proj3, g_v, wcat, bias)


def _gmlp_s_body(u_ref, v_ref, gv_ref, w0_ref, b0_ref, o_ref, vn_ref):
    vn = _rms(jax.nn.gelu(v_ref[...])) * gv_ref[...]
    vn_ref[...] = vn
    s = w0_ref[...] * vn + b0_ref[...]
    o_ref[...] = (jax.nn.gelu(u_ref[...]) * s).astype(o_ref.dtype)


def gmlp_sample(proj, g_v, w_s, b_s):
    R = proj.shape[0]
    hd = D_B // H_B
    w0 = jnp.repeat(w_s[:, 0, 0], hd).reshape(1, D_B)
    b0 = jnp.repeat(b_s[:, 0], hd).reshape(1, D_B)
    z2 = lambda i: (0, 0)
    ub = D_A // D_B
    return pl.pallas_call(
        _gmlp_s_body,
        grid=(1,),
        in_specs=[pl.BlockSpec((R, D_B), lambda i: (0, ub)),
                  pl.BlockSpec((R, D_B), lambda i: (0, ub + 1)),
                  pl.BlockSpec((1, D_B), z2), pl.BlockSpec((1, D_B), z2), pl.BlockSpec((1, D_B), z2)],
        out_specs=[pl.BlockSpec((R, D_B), z2), pl.BlockSpec((R, D_B), z2)],
        out_shape=[jax.ShapeDtypeStruct((R, D_B), BF16), jax.ShapeDtypeStruct((R, D_B), F32)],
        compiler_params=_cp(("arbitrary",)),
        name="gmlp_sample",
    )(proj, proj, g_v, w0, b0)


def _head_expand_matrix():
    r = lax.broadcasted_iota(jnp.int32, (LANE, D_C), 0)
    c = lax.broadcasted_iota(jnp.int32, (LANE, D_C), 1) // HD_C
    return jnp.where(r == c, 1.0, 0.0).astype(BF16)


def _expand_heads(v, e):
    hi, lo = _split_bf16(v)
    return _dot(hi, e) + _dot(lo, e)


def _ssd_p_body(z_ref, xbc_ref, dt_ref, cw_ref, cb_ref, dtb_ref, alog_ref, dsk_ref, gn_ref,
                o_ref, st_ref, h_ref, h_sc, cv_sc, *, nt):
    Q = SSD_CHUNK
    t = pl.program_id(1)

    @pl.when(t == 0)
    def _():
        h_sc[...] = jnp.zeros_like(h_sc)
        cv_sc[0:SUBLANE, :] = jnp.zeros((SUBLANE, D_XBC), F32)

    xbc = xbc_ref[...]
    cv_sc[SUBLANE:SUBLANE + Q, :] = xbc
    acc = cb_ref[...]
    for k in range(K_C):
        off = SUBLANE - (K_C - 1) + k
        acc = acc + cw_ref[k:k + 1, :] * cv_sc[off:off + Q, :]
    cv_sc[0:SUBLANE, :] = xbc[Q - SUBLANE:, :]
    st_ref[...] = xbc[Q - (K_C - 1):, :]
    xc = _silu(acc)
    xs = xc[:, :D_C]

    dt = _softplus(dt_ref[...] + dtb_ref[...])
    a = -jnp.exp(alog_ref[...])
    da = dt * a
    ii = lax.broadcasted_iota(jnp.int32, (Q, Q), 0)
    jj = lax.broadcasted_iota(jnp.int32, (Q, Q), 1)
    causal = ii >= jj
    tril = jnp.where(causal, 1.0, 0.0).astype(BF16)
    d0, d1 = _split_bf16(da)
    d2 = (da - d0.astype(F32) - d1.astype(F32)).astype(BF16)
    cs = _dot(tril, d0) + _dot(tril, d1) + _dot(tril, d2)
    cst = cs.T
    cs_end = cs[Q - 1:Q, :]
    e = _head_expand_matrix()
    dt_f = _expand_heads(dt, e)
    ws_f = _expand_heads(dt * jnp.exp(cs_end - cs), e)
    ecs_f = _expand_heads(jnp.exp(cs), e)
    xdt = xs * dt_f
    xw = xs * ws_f

    lane = lax.broadcasted_iota(jnp.int32, (Q, LANE), 1)
    hpg = H_C // G_C
    gw = hpg * HD_C
    ys = []
    for g in range(G_C):
        bg = xc[:, D_C + g * N_C:D_C + (g + 1) * N_C].astype(BF16)
        cg = xc[:, D_C + G_C * N_C + g * N_C:D_C + G_C * N_C + (g + 1) * N_C].astype(BF16)
        gmat = _dot_nt(cg, bg)
        hprev = h_sc[g * gw:(g + 1) * gw, :]
        yoff = _dot_nt(cg, hprev.astype(BF16)) * ecs_f[:, g * gw:(g + 1) * gw]
        snew = _dot_tn(xw[:, g * gw:(g + 1) * gw].astype(BF16), bg)
        for hp in range(hpg // 2):
            h0 = g * hpg + 2 * hp
            xpair = xdt[:, h0 * HD_C:(h0 + 2) * HD_C]
            x_lo = jnp.where(lane < HD_C, xpair, 0.0).astype(BF16)
            x_hi = jnp.where(lane >= HD_C, xpair, 0.0).astype(BF16)
            yd = None
            for hh, xh in ((h0, x_lo), (h0 + 1, x_hi)):
                seg = cs[:, hh:hh + 1] - cst[hh:hh + 1, :]
                sc = (gmat * jnp.exp(jnp.where(causal, seg, NEG_BIG))).astype(BF16)
                part = _dot(sc, xh)
                yd = part if yd is None else yd + part
            ys.append(yd + yoff[:, 2 * hp * HD_C:(2 * hp + 2) * HD_C])
        for hh in range(hpg):
            h = g * hpg + hh
            cd = jnp.exp(cst[h:h + 1, Q - 1:Q])
            rows = slice(h * HD_C, (h + 1) * HD_C)
            h_sc[rows, :] = h_sc[rows, :] * cd + snew[hh * HD_C:(hh + 1) * HD_C, :]

    y = jnp.concatenate(ys, axis=1) + dsk_ref[...] * xs
    y = y * _silu(z_ref[...])
    o_ref[...] = (_rms(y) * gn_ref[...]).astype(o_ref.dtype)

    @pl.when(t == nt - 1)
    def _():
        h_ref[...] = h_sc[...]


def _ssd_params(dt_bias, a_log, d_skip):
    pad = LANE - H_C
    dtb = jnp.pad(dt_bias, (0, pad)).reshape(1, LANE)
    alog = jnp.pad(a_log, (0, pad)).reshape(1, LANE)
    dsk = jnp.repeat(d_skip, HD_C).reshape(1, D_C)
    return dtb, alog, dsk


def ssd_prompt(z3, xbc3, dt3, conv_w, conv_b, dt_bias, a_log, d_skip, g_norm):
    B, L = z3.shape[:2]
    Q = SSD_CHUNK
    nt = L // Q
    dtb, alog, dsk = _ssd_params(dt_bias, a_log, d_skip)
    const2 = lambda b, t: (0, 0)
    blk = lambda w: pl.BlockSpec((None, Q, w), lambda b, t: (b, t, 0))
    return pl.pallas_call(
        functools.partial(_ssd_p_body, nt=nt),
        grid=(B, nt),
        in_specs=[blk(D_C), blk(D_XBC), blk(LANE),
                  pl.BlockSpec((K_C, D_XBC), const2), pl.BlockSpec((1, D_XBC), const2),
                  pl.BlockSpec((1, LANE), const2), pl.BlockSpec((1, LANE), const2),
                  pl.BlockSpec((1, D_C), const2), pl.BlockSpec((1, D_C), const2)],
        out_specs=[blk(D_C),
                   pl.BlockSpec((None, K_C - 1, D_XBC), lambda b, t: (b, 0, 0)),
                   pl.BlockSpec((None, H_C * HD_C, N_C), lambda b, t: (b, 0, 0))],
        out_shape=[jax.ShapeDtypeStruct((B, L, D_C), BF16),
                   jax.ShapeDtypeStruct((B, K_C - 1, D_XBC), F32),
                   jax.ShapeDtypeStruct((B, H_C * HD_C, N_C), F32)],
        scratch_shapes=[pltpu.VMEM((H_C * HD_C, N_C), F32), pltpu.VMEM((SUBLANE + Q, D_XBC), F32)],
        compiler_params=_cp(("arbitrary", "arbitrary")),
        name="ssd_prompt",
    )(z3, xbc3, dt3, conv_w, conv_b.reshape(1, D_XBC), dtb, alog, dsk, g_norm.reshape(1, D_C))


def _ssd_s_body(z_ref, xbc_ref, dt_ref, buf_ref, h0_ref, cw_ref, cb_ref, dtb_ref, alog_ref,
                dsk_ref, gn_ref, *rest, tbatch, nsteps, has_prev):
    o_ref, hn_ref, xs_sc, bc_sc, xt_hi, xt_lo, at_hi, at_lo, y_sc = rest[1:] if has_prev else rest
    s = pl.program_id(0)
    R = xs_sc.shape[0]

    @pl.when(s == 0)
    def _():
        acc = cb_ref[...] + cw_ref[K_C - 1:K_C, :] * xbc_ref[...]
        for k in range(K_C - 1):
            acc = acc + cw_ref[k:k + 1, :] * buf_ref[k]
        xc = _silu(acc)
        xs = xc[:, :D_C]
        xs_sc[...] = xs
        bc_sc[...] = xc[:, D_C:]
        dt = _softplus(dt_ref[...] + dtb_ref[...])
        e = _head_expand_matrix()
        dt_f = _expand_heads(dt, e)
        da_f = jnp.exp(_expand_heads(dt * (-jnp.exp(alog_ref[...])), e))
        hi, lo = _split_bf16((xs * dt_f).T)
        xt_hi[...] = hi
        xt_lo[...] = lo
        hi, lo = _split_bf16(da_f.T)
        at_hi[...] = hi
        at_lo[...] = lo

    gw = (H_C // G_C) * HD_C
    kk = lax.broadcasted_iota(jnp.int32, (R, LANE), 0)
    r0 = pl.multiple_of(s * tbatch, tbatch)
    bc8 = bc_sc[pl.ds(r0, tbatch), :]
    ti = lax.broadcasted_iota(jnp.int32, (tbatch, gw), 0)
    ytile = [jnp.zeros((tbatch, gw), F32) for _ in range(G_C)]
    for i in range(tbatch):
        b = s * tbatch + i
        onehot = jnp.where(kk == b, 1.0, 0.0).astype(BF16)
        xb = _dot(xt_hi[...], onehot) + _dot(xt_lo[...], onehot)
        ab = _dot(at_hi[...], onehot) + _dot(at_lo[...], onehot)
        for g in range(G_C):
            rows = slice(g * gw, (g + 1) * gw)
            brow = bc8[i:i + 1, g * N_C:(g + 1) * N_C]
            hn = h0_ref[i, rows, :] * ab[rows, :] + xb[rows, :] * brow
            hn_ref[i, rows, :] = hn
            c8 = bc8[:, G_C * N_C + g * N_C:G_C * N_C + (g + 1) * N_C]
            yg = _dot_nt(c8.astype(BF16), hn.astype(BF16))
            ytile[g] = jnp.where(ti == i, yg, ytile[g])
    for g in range(G_C):
        y_sc[pl.ds(r0, tbatch), g * gw:(g + 1) * gw] = ytile[g]

    @pl.when(s == nsteps - 1)
    def _():
        y = y_sc[...] + dsk_ref[...] * xs_sc[...]
        y = y * _silu(z_ref[...])
        o_ref[...] = (_rms(y) * gn_ref[...]).astype(o_ref.dtype)


def ssd_sample(z, xbc, dt, buf, h0_all, l, hn_all, conv_w, conv_b, dt_bias, a_log, d_skip, g_norm,
               *, tbatch=8):
    R = z.shape[0]
    nsteps = R // tbatch
    dtb, alog, dsk = _ssd_params(dt_bias, a_log, d_skip)
    z2 = lambda s: (0, 0)
    hw = H_C * HD_C
    st_spec = pl.BlockSpec((None, tbatch, hw, N_C), lambda s: (l, s, 0, 0))
    in_specs = [pl.BlockSpec((R, D_C), z2), pl.BlockSpec((R, D_XBC), z2), pl.BlockSpec((R, LANE), z2),
                pl.BlockSpec((K_C - 1, R, D_XBC), lambda s: (0, 0, 0)),
                st_spec,
                pl.BlockSpec((K_C, D_XBC), z2), pl.BlockSpec((1, D_XBC), z2),
                pl.BlockSpec((1, LANE), z2), pl.BlockSpec((1, LANE), z2),
                pl.BlockSpec((1, D_C), z2), pl.BlockSpec((1, D_C), z2)]
    args = [z, xbc, dt, buf, h0_all, conv_w, conv_b.reshape(1, D_XBC), dtb, alog, dsk,
            g_norm.reshape(1, D_C)]
    aliases = {}
    if hn_all is not None:
        aliases = {len(args): 1}
        in_specs.append(pl.BlockSpec(memory_space=pl.ANY))
        args.append(hn_all)
    return pl.pallas_call(
        functools.partial(_ssd_s_body, tbatch=tbatch, nsteps=nsteps, has_prev=hn_all is not None),
        grid=(nsteps,),
        in_specs=in_specs,
        out_specs=[pl.BlockSpec((R, D_C), z2), st_spec],
        out_shape=[jax.ShapeDtypeStruct((R, D_C), BF16),
                   jax.ShapeDtypeStruct(h0_all.shape, F32)],
        scratch_shapes=[pltpu.VMEM((R, D_C), F32), pltpu.VMEM((R, 2 * G_C * N_C), F32),
                        pltpu.VMEM((D_C, R), BF16), pltpu.VMEM((D_C, R), BF16),
                        pltpu.VMEM((D_C, R), BF16), pltpu.VMEM((D_C, R), BF16),
                        pltpu.VMEM((R, D_C), F32)],
        input_output_aliases=aliases,
        compiler_params=_cp(("arbitrary",)),
        name="ssd_sample",
    )(*args)


def _up_p_body(x_ref, wa_ref, wb_ref, cwa_ref, cwb_ref, cba_ref, cbb_ref,
               o_ref, sa_ref, sb_ref, wa_sc, wb_sc, ca_sc, cb_sc, *, tm, ts, tiles_per_seq):
    i = pl.program_id(1)

    @pl.when(i == 0)
    def _():
        wa_sc[...] = wa_ref[...].astype(BF16)
        wb_sc[...] = wb_ref[...].astype(BF16)

    @pl.when(i % tiles_per_seq == 0)
    def _():
        ca_sc[0:SUBLANE, :] = jnp.zeros((SUBLANE, ca_sc.shape[1]), F32)
        cb_sc[0:SUBLANE, :] = jnp.zeros((SUBLANE, cb_sc.shape[1]), F32)

    def half(r0, w_sc, c_sc, cw_ref, cb_ref):
        up = _dot(x_ref[r0:r0 + ts, :], w_sc[...])
        c_sc[SUBLANE + r0:SUBLANE + r0 + ts, :] = up
        acc = cb_ref[...]
        for k in range(K_F):
            off = SUBLANE - (K_F - 1) + k + r0
            acc = acc + cw_ref[k:k + 1, :] * c_sc[off:off + ts, :]
        return acc

    for r0 in range(0, tm, ts):
        a = half(r0, wa_sc, ca_sc, cwa_ref, cba_ref)
        b = half(r0, wb_sc, cb_sc, cwb_ref, cbb_ref)
        o_ref[r0:r0 + ts, :] = (_silu(a) * b).astype(o_ref.dtype)

    for c_sc, st_ref in ((ca_sc, sa_ref), (cb_sc, sb_ref)):
        st_ref[...] = c_sc[SUBLANE + tm - (K_F - 1):SUBLANE + tm, :]
        c_sc[0:SUBLANE, :] = c_sc[tm:tm + SUBLANE, :]


def up_prompt(h2, w_up, l, conv_w, conv_b, *, seq_len, tm=1024, tn=512, ts=256):
    M, D = h2.shape
    B = M // seq_len
    tiles_per_seq = seq_len // tm
    nj = D_FF // tn
    cw = conv_w
    cb = conv_b.reshape(1, 2 * D_FF)
    body = functools.partial(_up_p_body, tm=tm, ts=min(ts, tm), tiles_per_seq=tiles_per_seq)
    st_spec = pl.BlockSpec((None, K_F - 1, tn), lambda j, i: (i // tiles_per_seq, 0, j))
    return pl.pallas_call(
        body,
        grid=(nj, M // tm),
        in_specs=[pl.BlockSpec((tm, D), lambda j, i: (i, 0)),
                  pl.BlockSpec((None, D, tn), lambda j, i: (l, 0, j)),
                  pl.BlockSpec((None, D, tn), lambda j, i: (l, 0, nj + j)),
                  pl.BlockSpec((K_F, tn), lambda j, i: (0, j)),
                  pl.BlockSpec((K_F, tn), lambda j, i: (0, nj + j)),
                  pl.BlockSpec((1, tn), lambda j, i: (0, j)),
                  pl.BlockSpec((1, tn), lambda j, i: (0, nj + j))],
        out_specs=[pl.BlockSpec((tm, tn), lambda j, i: (i, j)), st_spec, st_spec],
        out_shape=[jax.ShapeDtypeStruct((M, D_FF), BF16),
                   jax.ShapeDtypeStruct((B, K_F - 1, D_FF), F32),
                   jax.ShapeDtypeStruct((B, K_F - 1, D_FF), F32)],
        scratch_shapes=[pltpu.VMEM((D, tn), BF16), pltpu.VMEM((D, tn), BF16),
                        pltpu.VMEM((SUBLANE + tm, tn), F32), pltpu.VMEM((SUBLANE + tm, tn), F32)],
        compiler_params=_cp(("arbitrary", "arbitrary")),
        name="up_prompt",
    )(h2, w_up, w_up, cw, cw, cb, cb)


def _up_s_body(x_ref, wa_ref, wb_ref, bufa_ref, bufb_ref, cwa_ref, cwb_ref, cba_ref, cbb_ref,
               o_ref, ua_ref, ub_ref):
    x = x_ref[...]

    def half(w_ref, buf_ref, cw_ref, cb_ref, up_ref):
        up = _dot(x, w_ref[...].astype(BF16))
        up_ref[...] = up
        acc = cb_ref[...] + cw_ref[K_F - 1:K_F, :] * up
        for k in range(K_F - 1):
            acc = acc + cw_ref[k:k + 1, :] * buf_ref[k]
        return acc

    a = half(wa_ref, bufa_ref, cwa_ref, cba_ref, ua_ref)
    b = half(wb_ref, bufb_ref, cwb_ref, cbb_ref, ub_ref)
    o_ref[...] = (_silu(a) * b).astype(o_ref.dtype)


def up_sample(h2, w_up, l, buf, conv_w, conv_b, *, tn=512):
    R, D = h2.shape
    nj = D_FF // tn
    cb = conv_b.reshape(1, 2 * D_FF)
    out_spec = pl.BlockSpec((R, tn), lambda j: (0, j))
    return pl.pallas_call(
        _up_s_body,
        grid=(nj,),
        in_specs=[pl.BlockSpec((R, D), lambda j: (0, 0)),
                  pl.BlockSpec((None, D, tn), lambda j: (l, 0, j)),
                  pl.BlockSpec((None, D, tn), lambda j: (l, 0, nj + j)),
                  pl.BlockSpec((K_F - 1, R, tn), lambda j: (0, 0, j)),
                  pl.BlockSpec((K_F - 1, R, tn), lambda j: (0, 0, nj + j)),
                  pl.BlockSpec((K_F, tn), lambda j: (0, j)),
                  pl.BlockSpec((K_F, tn), lambda j: (0, nj + j)),
                  pl.BlockSpec((1, tn), lambda j: (0, j)),
                  pl.BlockSpec((1, tn), lambda j: (0, nj + j))],
        out_specs=[out_spec, out_spec, out_spec],
        out_shape=[jax.ShapeDtypeStruct((R, D_FF), BF16),
                   jax.ShapeDtypeStruct((R, D_FF), F32),
                   jax.ShapeDtypeStruct((R, D_FF), F32)],
        compiler_params=_cp(("arbitrary",)),
        name="up_sample",
    )(h2, w_up, w_up, buf, buf, conv_w, conv_w, cb, cb)


def _layer(x3, mods, l, p, state, *, seq_len, tm, tn_merge, tm_norm):
    B, L, D = x3.shape
    M = B * L
    sh_m, sc_m, gt_m, sh_f, sc_f, gt_f = mods
    tiles_per_seq = max(seq_len // tm, 1)
    prompt = state is None

    h = norm_mod(x3, p['g_mix'][l].reshape(1, D), sc_m, sh_m, tm=tm_norm).reshape(M, D)

    def proj(col0, n, tn, name):
        return matmul([h], [(p['w_in_t'], l, col0, True)], [], [(n, F32)], _epi_store,
                      x_of_w=(0,), tm=tm, tn=tn, name=name)[0]

    uav = proj(OFF_UA, D_A + 2 * D_B, 512, "proj_uav")
    zc = proj(OFF_Z, D_C, 512, "proj_z")
    xbc = proj(OFF_XBC, D_XBC, 512, "proj_xbc")
    dtp = proj(OFF_DT, LANE, LANE, "proj_dt")

    tab, wb = s5_prep(p['lam_re'][l], p['lam_im'][l], p['log_dt'][l], p['b_re'][l], p['b_im'][l])
    wcr, wci = s5_out_weights(p['c_re'][l], p['c_im'][l])
    d_a = p['s5_d'][l].reshape(1, D_A)
    wglu = p['w_glu'][l].astype(BF16)
    g_v = p['g_v'][l].reshape(1, D_B)

    if prompt:
        o_a, s5r, s5i = s5_prompt(uav.reshape(B, L, -1), wb, wcr, wci, tab, d_a, wglu)
        o_b = gmlp_prompt(uav.reshape(B, L, -1), g_v, p['w_s'][l], p['b_s'][l])
        v_rows = None
        o_c, convc, ssm = ssd_prompt(zc.reshape(B, L, -1), xbc.reshape(B, L, -1), dtp.reshape(B, L, -1),
                                     p['ssd_conv_w'][l], p['ssd_conv_b'][l], p['dt_bias'][l],
                                     p['a_log'][l], p['ssd_d'][l], p['ssd_g'][l])
        s5r = s5r.reshape(B, G_A, P_A)
        s5i = s5i.reshape(B, G_A, P_A)
    else:
        s5_re0, s5_im0, ssm_all, ssm_new_all, convc0, convf0 = state
        o_a, s5r, s5i = s5_sample(uav, s5_re0.reshape(M, N_S5), s5_im0.reshape(M, N_S5),
                                  wb, wcr, wci, tab, d_a, wglu)
        o_b, v_rows = gmlp_sample(uav, g_v, p['w_s'][l], p['b_s'][l])
        o_c, ssm = ssd_sample(zc, xbc, dtp, jnp.transpose(convc0, (1, 0, 2)),
                              ssm_all, l, ssm_new_all,
                              p['ssd_conv_w'][l], p['ssd_conv_b'][l], p['dt_bias'][l],
                              p['a_log'][l], p['ssd_d'][l], p['ssd_g'][l])
        convc = jnp.concatenate([convc0[:, 1:], xbc[:, None, :]], axis=1)
        s5r = s5r.reshape(M, G_A, P_A)
        s5i = s5i.reshape(M, G_A, P_A)
    o_a = o_a.reshape(M, D_A)
    o_b = o_b.reshape(M, D_B)
    o_c = o_c.reshape(M, D_C)
    if prompt:
        ssm = ssm.reshape(B, H_C, HD_C, N_C)

    w_t = p['w_in_t']
    merged = matmul(
        [h, o_a, o_b, o_c],
        [(w_t, l, OFF_GATES, True), (w_t, l, OFF_GATES + D, True), (w_t, l, OFF_GATES + 2 * D, True),
         (p['w_pa'], l, 0, False), (p['w_pb'], l, 0, False), (p['w_pc'], l, 0, False)],
        [], [(D, BF16)], _epi_merge, x_of_w=(0, 0, 0, 1, 2, 3), tm=tm, tn=tn_merge, name="merge")[0]

    def residual(x_res, act_in, w3, gt, tm_r, name, vmem_mb):
        tn = 512
        return matmul([act_in], [(w3, l, 0, False)],
                      [(x_res, _tile_spec(tm_r, tn)),
                       (gt, _gate_spec(gt, tm_r, tn, max(seq_len // tm_r, 1)))],
                      [(D, F32)], _epi_residual, x_of_w=(0,), tm=tm_r, tn=tn, name=name,
                      vmem_mb=vmem_mb)[0]

    x2 = residual(x3.reshape(M, D), merged, p['w_out'], gt_m, tm, "out_proj", 48)

    h2 = norm_mod(x2.reshape(B, L, D), p['g_ffn'][l].reshape(1, D), sc_f, sh_f, tm=tm_norm).reshape(M, D)
    if prompt:
        act, st_a, st_b = up_prompt(h2, p['w_up'], l, p['ffn_conv_w'][l], p['ffn_conv_b'][l],
                                    seq_len=seq_len, tm=tm)
        convf = jnp.concatenate([st_a, st_b], axis=-1)
    else:
        act, up_a, up_b = up_sample(h2, p['w_up'], l, jnp.transpose(convf0, (1, 0, 2)),
                                    p['ffn_conv_w'][l], p['ffn_conv_b'][l])
        convf = jnp.concatenate([convf0[:, 1:], jnp.concatenate([up_a, up_b], axis=-1)[:, None, :]], axis=1)

    x2 = residual(x2, act, p['w_down'], gt_f, min(tm, 512), "down_proj", 56)
    return x2.reshape(B, L, D), s5r, s5i, ssm, convc, convf, v_rows


def kernel(x_prompt, x_sample, c_prompt, c_sample, state_s5_re, state_s5_im, state_ssm, state_ssd_conv, state_ffn_conv, w_mod, b_mod, g_mix, w_in, s5_lam_re, s5_lam_im, s5_log_dt, s5_b_re, s5_b_im, s5_c_re, s5_c_im, s5_d, s5_w_glu, gm_g_v, gm_w_s, gm_b_s, ssd_conv_w, ssd_conv_b, ssd_dt_bias, ssd_a_log, ssd_d, ssd_g_norm, w_pa, w_pb, w_pc, w_out, g_ffn, ffn_w_up, ffn_conv_w, ffn_conv_b, ffn_w_down, g_final):
    p = {
        'g_mix': g_mix, 'w_in_t': jnp.swapaxes(w_in, 1, 2),
        'lam_re': s5_lam_re, 'lam_im': s5_lam_im, 'log_dt': s5_log_dt,
        'b_re': s5_b_re, 'b_im': s5_b_im, 'c_re': s5_c_re, 'c_im': s5_c_im,
        's5_d': s5_d, 'w_glu': s5_w_glu,
        'g_v': gm_g_v, 'w_s': gm_w_s, 'b_s': gm_b_s,
        'ssd_conv_w': ssd_conv_w, 'ssd_conv_b': ssd_conv_b, 'dt_bias': ssd_dt_bias,
        'a_log': ssd_a_log, 'ssd_d': ssd_d, 'ssd_g': ssd_g_norm,
        'w_pa': w_pa, 'w_pb': w_pb, 'w_pc': w_pc, 'w_out': w_out,
        'g_ffn': g_ffn, 'w_up': ffn_w_up, 'ffn_conv_w': ffn_conv_w,
        'ffn_conv_b': ffn_conv_b, 'w_down': ffn_w_down,
    }
    bp, seq, D = x_prompt.shape
    bs = x_sample.shape[0]

    n_c = bs + bp
    pad = (-n_c) % SUBLANE
    c_all = jnp.concatenate([c_sample, c_prompt, jnp.zeros((pad, D), F32)], axis=0)
    mod = mod_all(c_all, w_mod, b_mod)

    xp = x_prompt
    xs = x_sample.reshape(1, bs, D)
    outs_p = [[] for _ in range(5)]
    outs_s = [[] for _ in range(6)]
    ssm_all = state_ssm.reshape(DEPTH, bs, H_C * HD_C, N_C)
    ssm_new_all = None
    for l in range(DEPTH):
        mods_s = [m[None] for m in jnp.split(mod[l, :bs], 6, axis=-1)]
        mods_p = [m[:, None, :] for m in jnp.split(mod[l, bs:bs + bp], 6, axis=-1)]
        xp, *st_p = _layer(xp, mods_p, l, p, None, seq_len=seq, tm=1024, tn_merge=256, tm_norm=512)
        for acc, v in zip(outs_p, st_p[:5]):
            acc.append(v)
        state = (state_s5_re[l], state_s5_im[l], ssm_all, ssm_new_all, state_ssd_conv[l], state_ffn_conv[l])
        xs, *st_s = _layer(xs, mods_s, l, p, state, seq_len=1, tm=bs, tn_merge=256, tm_norm=bs)
        ssm_new_all = st_s[2]
        st_s[5] = st_s[5].reshape(bs, 1, D_B)
        for acc, v in zip(outs_s, st_s):
            acc.append(v)

    g_fin = g_final.reshape(1, D)
    y_prompt = final_norm(xp, g_fin, tm=512)
    y_sample = final_norm(xs, g_fin, tm=bs).reshape(bs, 1, D)
    ssm_s = ssm_new_all.reshape(DEPTH, bs, H_C, HD_C, N_C)
    outs_s = [ssm_s if k == 2 else jnp.stack(v) for k, v in enumerate(outs_s)]
    return (y_prompt, y_sample, *[jnp.stack(v) for v in outs_p], *outs_s)
```

```python
import functools

import jax
import jax.numpy as jnp
from jax import lax
from jax.experimental import pallas as pl
from jax.experimental.pallas import tpu as pltpu

F32 = jnp.float32
BF16 = jnp.bfloat16

D_MODEL = 2048
DEPTH = 2
D_A = 512
S5_GROUP = 16
G_A = D_A // S5_GROUP
P_A = 64
N_S5 = G_A * P_A
S5_GPB = 128 // S5_GROUP
S5_NB = D_A // 128
S5_SPB = S5_GPB * P_A
D_B = 512
H_B = 8
GM_CHUNK = 128
D_C = 1024
HD_C = 64
H_C = D_C // HD_C
N_C = 128
G_C = 2
K_C = 4
SSD_CHUNK = 128
D_XBC = D_C + 2 * G_C * N_C
D_FF = 5632
K_F = 3
EPS = 1e-6

OFF_UA = 0
OFF_Z = D_A + 2 * D_B
OFF_XBC = OFF_Z + D_C
OFF_DT = OFF_XBC + D_XBC
OFF_GATES = OFF_DT + H_C

LANE = 128
SUBLANE = 8
NEG_BIG = -1e30


def _cp(sem, vmem_mb=48):
    return pltpu.CompilerParams(dimension_semantics=sem, vmem_limit_bytes=vmem_mb * 1024 * 1024)


def _sigmoid(x):
    return 0.5 * jnp.tanh(0.5 * x) + 0.5


def _silu(x):
    return x * _sigmoid(x)


def _softplus(x):
    return jnp.maximum(x, 0.0) + jnp.log1p(jnp.exp(-jnp.abs(x)))


def _rms(x):
    return x * lax.rsqrt(jnp.mean(x * x, axis=-1, keepdims=True) + EPS)


def _split_bf16(x):
    hi = x.astype(BF16)
    lo = (x - hi.astype(F32)).astype(BF16)
    return hi, lo


def _dot(a, b):
    return jnp.dot(a, b, preferred_element_type=F32)


def _dot_nt(a, b):
    return lax.dot_general(a, b, (((1,), (1,)), ((), ())), preferred_element_type=F32)


def _dot_tn(a, b):
    return lax.dot_general(a, b, (((0,), (0,)), ((), ())), preferred_element_type=F32)


def _mod_body(c_ref, w_ref, b_ref, o_ref):
    a = _silu(c_ref[...]).astype(BF16)
    o_ref[...] = _dot(a, w_ref[...].astype(BF16)) + b_ref[...]


def mod_all(c_all, w_mod, b_mod, *, tn=1024):
    R, D = c_all.shape
    N = w_mod.shape[-1]
    return pl.pallas_call(
        _mod_body,
        grid=(DEPTH, N // tn),
        in_specs=[pl.BlockSpec((R, D), lambda l, j: (0, 0)),
                  pl.BlockSpec((None, D, tn), lambda l, j: (l, 0, j)),
                  pl.BlockSpec((None, 1, tn), lambda l, j: (l, 0, j))],
        out_specs=pl.BlockSpec((None, R, tn), lambda l, j: (l, 0, j)),
        out_shape=jax.ShapeDtypeStruct((DEPTH, R, N), F32),
        compiler_params=_cp(("arbitrary", "arbitrary")),
        name="mod_all",
    )(c_all, w_mod, b_mod.reshape(DEPTH, 1, N))


def _norm_mod_body(x_ref, g_ref, sc_ref, sh_ref, o_ref):
    y = _rms(x_ref[...]) * g_ref[...]
    o_ref[...] = (y * (1.0 + sc_ref[...]) + sh_ref[...]).astype(o_ref.dtype)


def norm_mod(x3, g, sc3, sh3, *, tm):
    B, L, D = x3.shape
    per_row = sc3.shape[1] != 1
    ts = tm if per_row else 1
    sidx = (lambda b, i: (b, i, 0)) if per_row else (lambda b, i: (b, 0, 0))
    return pl.pallas_call(
        _norm_mod_body,
        grid=(B, L // tm),
        in_specs=[pl.BlockSpec((None, tm, D), lambda b, i: (b, i, 0)),
                  pl.BlockSpec((1, D), lambda b, i: (0, 0)),
                  pl.BlockSpec((None, ts, D), sidx),
                  pl.BlockSpec((None, ts, D), sidx)],
        out_specs=pl.BlockSpec((None, tm, D), lambda b, i: (b, i, 0)),
        out_shape=jax.ShapeDtypeStruct((B, L, D), BF16),
        compiler_params=_cp(("arbitrary", "arbitrary")),
        name="norm_mod",
    )(x3, g, sc3, sh3)


def _rms_body(x_ref, g_ref, o_ref):
    o_ref[...] = _rms(x_ref[...]) * g_ref[...]


def final_norm(x3, g, *, tm):
    B, L, D = x3.shape
    return pl.pallas_call(
        _rms_body,
        grid=(B, L // tm),
        in_specs=[pl.BlockSpec((None, tm, D), lambda b, i: (b, i, 0)),
                  pl.BlockSpec((1, D), lambda b, i: (0, 0))],
        out_specs=pl.BlockSpec((None, tm, D), lambda b, i: (b, i, 0)),
        out_shape=jax.ShapeDtypeStruct((B, L, D), F32),
        compiler_params=_cp(("arbitrary", "arbitrary")),
        name="final_norm",
    )(x3, g)


def _mm_body(*refs, x_of_w, w_is_t, n_x, n_e, n_o, epilogue):
    n_w = len(x_of_w)
    xs = refs[:n_x]
    ws = refs[n_x:n_x + n_w]
    es = refs[n_x + n_w:n_x + n_w + n_e]
    outs = refs[n_x + n_w + n_e:n_x + n_w + n_e + n_o]
    wsc = refs[n_x + n_w + n_e + n_o:]

    @pl.when(pl.program_id(1) == 0)
    def _():
        for w, s, is_t in zip(ws, wsc, w_is_t):
            s[...] = (w[...].T if is_t else w[...]).astype(BF16)

    accs = [_dot(xs[xi][...], s[...]) for xi, s in zip(x_of_w, wsc)]
    epilogue(accs, es, outs)


def matmul(xs, ws, extras, outs, epilogue, *, x_of_w, tm, tn, name, vmem_mb=48, w_buffers=2):
    M = xs[0].shape[0]
    N = outs[0][0]
    in_specs = [pl.BlockSpec((tm, x.shape[1]), lambda j, i: (i, 0)) for x in xs]
    args = list(xs)
    scratch = []
    for w3, l, col0, is_t in ws:
        if is_t:
            K = w3.shape[2]
            assert col0 % SUBLANE == 0
            in_specs.append(pl.BlockSpec((None, pl.Element(tn), pl.Element(K)), functools.partial(
                lambda j, i, l, c: (l, (c + j * (tn // SUBLANE)) * SUBLANE, 0),
                l=l, c=col0 // SUBLANE), pipeline_mode=pl.Buffered(w_buffers)))
        else:
            K = w3.shape[1]
            assert col0 % tn == 0
            in_specs.append(pl.BlockSpec((None, K, tn), functools.partial(
                lambda j, i, l, c: (l, 0, c + j), l=l, c=col0 // tn),
                pipeline_mode=pl.Buffered(w_buffers)))
        args.append(w3)
        scratch.append(pltpu.VMEM((K, tn), BF16))
    for arr, spec in extras:
        in_specs.append(spec)
        args.append(arr)
    body = functools.partial(_mm_body, x_of_w=tuple(x_of_w), w_is_t=tuple(w[3] for w in ws),
                             n_x=len(xs), n_e=len(extras), n_o=len(outs), epilogue=epilogue)
    res = pl.pallas_call(
        body,
        grid=(N // tn, M // tm),
        in_specs=in_specs,
        out_specs=[pl.BlockSpec((tm, tn), lambda j, i: (i, j)) for _ in outs],
        out_shape=[jax.ShapeDtypeStruct((M, n), dt) for n, dt in outs],
        scratch_shapes=scratch,
        compiler_params=_cp(("arbitrary", "arbitrary"), vmem_mb),
        name=name,
    )(*args)
    return res


def _epi_store(accs, es, outs):
    outs[0][...] = accs[0].astype(outs[0].dtype)


def _epi_merge(accs, es, outs):
    ga, gb, gc, pa, pb, pc = accs
    m = _sigmoid(ga) * pa + _sigmoid(gb) * pb + _sigmoid(gc) * pc
    outs[0][...] = m.astype(outs[0].dtype)


def _epi_residual(accs, es, outs):
    res_ref, gt_ref = es
    outs[0][...] = res_ref[...] + gt_ref[...] * accs[0]


def _out_norm_body(a_ref, w_ref, res_ref, gt_ref, g_ref, sc_ref, sh_ref, x_ref, h_ref, w_sc):
    @pl.when(pl.program_id(0) == 0)
    def _():
        w_sc[...] = w_ref[...].astype(BF16)

    x = res_ref[...] + gt_ref[...] * _dot(a_ref[...], w_sc[...])
    x_ref[...] = x
    y = _rms(x) * g_ref[...]
    h_ref[...] = (y * (1.0 + sc_ref[...]) + sh_ref[...]).astype(h_ref.dtype)


def out_norm(act, w3, l, res, gt, g, sc, sh, *, tm, seq_len):
    M, K = act.shape
    D = res.shape[1]
    tiles_per_seq = max(seq_len // tm, 1)

    def mod_spec(m3):
        if m3.shape[1] == 1:
            return pl.BlockSpec((None, 1, D), lambda i: (i // tiles_per_seq, 0, 0))
        return pl.BlockSpec((None, tm, D), lambda i: (0, i, 0))

    row = lambda w: pl.BlockSpec((tm, w), lambda i: (i, 0))
    return pl.pallas_call(
        _out_norm_body,
        grid=(M // tm,),
        in_specs=[row(K),
                  pl.BlockSpec((None, K, D), lambda i: (l, 0, 0), pipeline_mode=pl.Buffered(1)),
                  row(D), mod_spec(gt),
                  pl.BlockSpec((1, D), lambda i: (0, 0)), mod_spec(sc), mod_spec(sh)],
        out_specs=[row(D), row(D)],
        out_shape=[jax.ShapeDtypeStruct((M, D), F32), jax.ShapeDtypeStruct((M, D), BF16)],
        scratch_shapes=[pltpu.VMEM((K, D), BF16)],
        compiler_params=_cp(("arbitrary",), 56),
        name="out_norm",
    )(act, w3, res, gt, g, sc, sh)


def _tile_spec(tm, tn):
    return pl.BlockSpec((tm, tn), lambda j, i: (i, j))


def _gate_spec(gt3, tm, tn, tiles_per_seq):
    if gt3.shape[1] == 1:
        return pl.BlockSpec((None, 1, tn), lambda j, i: (i // tiles_per_seq, 0, j))
    return pl.BlockSpec((None, tm, tn), lambda j, i: (0, i, j))


def _s5_prep_body(lrf_ref, lif_ref, ldf_ref, lrr_ref, lir_ref, ldr_ref, bre_ref, bim_ref,
                  tab_ref, bbr_ref, bbi_ref):
    dtf = jnp.exp(ldf_ref[...])
    n = (lax.broadcasted_iota(jnp.int32, (SUBLANE, N_S5), 0) + 1).astype(F32)
    mag = jnp.exp(n * (lrf_ref[...] * dtf))
    ang = n * (lif_ref[...] * dtf)
    pr = mag * jnp.cos(ang)
    pi = mag * jnp.sin(ang)
    row = lax.broadcasted_iota(jnp.int32, (SUBLANE, N_S5), 0)
    for k, d in enumerate((1, 2, 4)):
        keep = row >= d
        tab_ref[2 * k] = jnp.where(keep, pr[d - 1:d, :], 0.0)
        tab_ref[2 * k + 1] = jnp.where(keep, pi[d - 1:d, :], 0.0)
    tab_ref[6] = pr
    tab_ref[7] = pi
    dtr = jnp.exp(ldr_ref[...])
    lr = lrr_ref[...]
    li = lir_ref[...]
    m1 = jnp.exp(lr * dtr)
    ar = m1 * jnp.cos(li * dtr)
    ai = m1 * jnp.sin(li * dtr)
    den = lr * lr + li * li
    nr = ar - 1.0
    kr = (nr * lr + ai * li) / den
    ki = (ai * lr - nr * li) / den
    bre = bre_ref[...]
    bim = bim_ref[...]
    bbr_ref[...] = kr * bre - ki * bim
    bbi_ref[...] = kr * bim + ki * bre


def s5_prep(lam_re, lam_im, log_dt, b_re, b_im):
    lrf = lam_re.reshape(1, N_S5)
    lif = lam_im.reshape(1, N_S5)
    ldf = jnp.repeat(log_dt, P_A).reshape(1, N_S5)
    lrr = jnp.repeat(lam_re, S5_GROUP, axis=0)
    lir = jnp.repeat(lam_im, S5_GROUP, axis=0)
    ldr = jnp.repeat(log_dt, S5_GROUP).reshape(D_A, 1)
    bre = jnp.transpose(b_re, (0, 2, 1)).reshape(D_A, P_A)
    bim = jnp.transpose(b_im, (0, 2, 1)).reshape(D_A, P_A)
    tab, bbr, bbi = pl.pallas_call(
        _s5_prep_body,
        out_shape=[jax.ShapeDtypeStruct((8, SUBLANE, N_S5), F32),
                   jax.ShapeDtypeStruct((D_A, P_A), F32),
                   jax.ShapeDtypeStruct((D_A, P_A), F32)],
        name="s5_prep",
    )(lrf, lif, ldf, lrr, lir, ldr, bre, bim)
    eye = jnp.eye(S5_GPB, dtype=F32)

    def blockdiag(m):
        m = m.reshape(S5_NB, S5_GPB, S5_GROUP, P_A)
        return jnp.einsum('kghp,gj->kghjp', m, eye).reshape(S5_NB, LANE, S5_SPB)

    wb = jnp.concatenate([blockdiag(bbr), blockdiag(bbi)], axis=2).astype(BF16)
    return tab, wb


def s5_out_weights(c_re, c_im):
    eye = jnp.eye(S5_GPB, dtype=F32)

    def blockdiag(c):
        c = c.reshape(S5_NB, S5_GPB, S5_GROUP, P_A)
        return jnp.einsum('kghp,gj->kgpjh', c, eye).reshape(S5_NB, S5_SPB, LANE).astype(BF16)

    return blockdiag(c_re), blockdiag(c_im)


def _s5_in(ub, wb_ref, kb):
    return _dot(ub[:, kb * LANE:(kb + 1) * LANE], wb_ref[kb])


def _s5_tail(h_blocks, u, wcr_ref, wci_ref, d_ref, wglu_ref):
    ys = []
    for kb in range(S5_NB):
        hr, hi = h_blocks(kb)
        ys.append(_dot(hr.astype(BF16), wcr_ref[kb]) - _dot(hi.astype(BF16), wci_ref[kb]))
    y = jnp.concatenate(ys, axis=1) + d_ref[...] * u
    y = jax.nn.gelu(y)
    return y * _sigmoid(_dot(y.astype(BF16), wglu_ref[...]))


def _s5_p_body(u_ref, wb_ref, wcr_ref, wci_ref, tab_ref, d_ref, wglu_ref,
               o_ref, hr_ref, hi_ref, bu_sc, car_sc, *, tb):
    @pl.when(pl.program_id(1) == 0)
    def _():
        car_sc[...] = jnp.zeros_like(car_sc)

    u = u_ref[...]
    ub = u.astype(BF16)
    for kb in range(S5_NB):
        bu = _s5_in(ub, wb_ref, kb)
        bu_sc[:, kb * S5_SPB:(kb + 1) * S5_SPB] = bu[:, :S5_SPB]
        bu_sc[:, N_S5 + kb * S5_SPB:N_S5 + (kb + 1) * S5_SPB] = bu[:, S5_SPB:]

    def tile(rt, _):
        r0 = pl.multiple_of(rt * SUBLANE, SUBLANE)
        for lg in range(N_S5 // LANE):
            cre = slice(lg * LANE, (lg + 1) * LANE)
            cim = slice(N_S5 + lg * LANE, N_S5 + (lg + 1) * LANE)
            xr = bu_sc[pl.ds(r0, SUBLANE), cre]
            xi = bu_sc[pl.ds(r0, SUBLANE), cim]
            for k, d in enumerate((1, 2, 4)):
                a_r = tab_ref[2 * k, :, cre]
                a_i = tab_ref[2 * k + 1, :, cre]
                sr = pltpu.roll(xr, d, 0)
                si = pltpu.roll(xi, d, 0)
                xr, xi = xr + a_r * sr - a_i * si, xi + a_r * si + a_i * sr
            p_r = tab_ref[6, :, cre]
            p_i = tab_ref[7, :, cre]
            cr = car_sc[:, cre]
            ci = car_sc[:, cim]
            xr, xi = xr + p_r * cr - p_i * ci, xi + p_r * ci + p_i * cr
            bu_sc[pl.ds(r0, SUBLANE), cre] = xr
            bu_sc[pl.ds(r0, SUBLANE), cim] = xi
            last = SUBLANE - 1
            car_sc[:, cre] = jnp.broadcast_to(xr[last:last + 1, :], (SUBLANE, LANE))
            car_sc[:, cim] = jnp.broadcast_to(xi[last:last + 1, :], (SUBLANE, LANE))
        return 0

    lax.fori_loop(0, tb // SUBLANE, tile, 0)

    def h_blocks(kb):
        return (bu_sc[:, kb * S5_SPB:(kb + 1) * S5_SPB],
                bu_sc[:, N_S5 + kb * S5_SPB:N_S5 + (kb + 1) * S5_SPB])

    y = _s5_tail(h_blocks, u, wcr_ref, wci_ref, d_ref, wglu_ref)
    o_ref[...] = y.astype(o_ref.dtype)
    hr_ref[...] = car_sc[0:1, :N_S5]
    hi_ref[...] = car_sc[0:1, N_S5:]


def s5_prompt(proj3, wb, wcr, wci, tab, d_skip, wglu, *, tb=256):
    B, L = proj3.shape[:2]
    const2 = lambda b, t: (0, 0)
    return pl.pallas_call(
        functools.partial(_s5_p_body, tb=tb),
        grid=(B, L // tb),
        in_specs=[pl.BlockSpec((None, tb, D_A), lambda b, t: (b, t, 0)),
                  pl.BlockSpec((S5_NB, LANE, 2 * S5_SPB), lambda b, t: (0, 0, 0)),
                  pl.BlockSpec((S5_NB, S5_SPB, LANE), lambda b, t: (0, 0, 0)),
                  pl.BlockSpec((S5_NB, S5_SPB, LANE), lambda b, t: (0, 0, 0)),
                  pl.BlockSpec((8, SUBLANE, N_S5), lambda b, t: (0, 0, 0)),
                  pl.BlockSpec((1, D_A), const2),
                  pl.BlockSpec((D_A, D_A), const2)],
        out_specs=[pl.BlockSpec((None, tb, D_A), lambda b, t: (b, t, 0)),
                   pl.BlockSpec((None, 1, N_S5), lambda b, t: (b, 0, 0)),
                   pl.BlockSpec((None, 1, N_S5), lambda b, t: (b, 0, 0))],
        out_shape=[jax.ShapeDtypeStruct((B, L, D_A), BF16),
                   jax.ShapeDtypeStruct((B, 1, N_S5), F32),
                   jax.ShapeDtypeStruct((B, 1, N_S5), F32)],
        scratch_shapes=[pltpu.VMEM((tb, 2 * N_S5), F32), pltpu.VMEM((SUBLANE, 2 * N_S5), F32)],
        compiler_params=_cp(("arbitrary", "arbitrary")),
        name="s5_prompt",
    )(proj3, wb, wcr, wci, tab, d_skip, wglu)


def _s5_s_body(u_ref, h0r_ref, h0i_ref, wb_ref, wcr_ref, wci_ref, tab_ref, d_ref, wglu_ref,
               o_ref, hr_ref, hi_ref):
    u = u_ref[...]
    ub = u.astype(BF16)
    for kb in range(S5_NB):
        cols = slice(kb * S5_SPB, (kb + 1) * S5_SPB)
        bu = _s5_in(ub, wb_ref, kb)
        a_r = tab_ref[6, 0:1, cols]
        a_i = tab_ref[7, 0:1, cols]
        h0r = h0r_ref[:, cols]
        h0i = h0i_ref[:, cols]
        hr_ref[:, cols] = a_r * h0r - a_i * h0i + bu[:, :S5_SPB]
        hi_ref[:, cols] = a_r * h0i + a_i * h0r + bu[:, S5_SPB:]

    def h_blocks(kb):
        cols = slice(kb * S5_SPB, (kb + 1) * S5_SPB)
        return hr_ref[:, cols], hi_ref[:, cols]

    o_ref[...] = _s5_tail(h_blocks, u, wcr_ref, wci_ref, d_ref, wglu_ref).astype(o_ref.dtype)


def s5_sample(proj, h0r, h0i, wb, wcr, wci, tab, d_skip, wglu):
    R = proj.shape[0]
    z2 = lambda i: (0, 0)
    return pl.pallas_call(
        _s5_s_body,
        grid=(1,),
        in_specs=[pl.BlockSpec((R, D_A), z2),
                  pl.BlockSpec((R, N_S5), z2), pl.BlockSpec((R, N_S5), z2),
                  pl.BlockSpec((S5_NB, LANE, 2 * S5_SPB), lambda i: (0, 0, 0)),
                  pl.BlockSpec((S5_NB, S5_SPB, LANE), lambda i: (0, 0, 0)),
                  pl.BlockSpec((S5_NB, S5_SPB, LANE), lambda i: (0, 0, 0)),
                  pl.BlockSpec((8, SUBLANE, N_S5), lambda i: (0, 0, 0)),
                  pl.BlockSpec((1, D_A), z2), pl.BlockSpec((D_A, D_A), z2)],
        out_specs=[pl.BlockSpec((R, D_A), z2), pl.BlockSpec((R, N_S5), z2), pl.BlockSpec((R, N_S5), z2)],
        out_shape=[jax.ShapeDtypeStruct((R, D_A), BF16),
                   jax.ShapeDtypeStruct((R, N_S5), F32),
                   jax.ShapeDtypeStruct((R, N_S5), F32)],
        compiler_params=_cp(("arbitrary",)),
        name="s5_sample",
    )(proj, h0r, h0i, wb, wcr, wci, tab, d_skip, wglu)


def _gmlp_p_body(u_ref, v_ref, gv_ref, wcat_ref, bias_ref, o_ref, *, tb):
    hd = D_B // H_B
    lane_head = lax.broadcasted_iota(jnp.int32, (GM_CHUNK, D_B), 1) // hd
    wi = lax.broadcasted_iota(jnp.int32, (GM_CHUNK, H_B * GM_CHUNK), 0)
    wj = lax.broadcasted_iota(jnp.int32, (GM_CHUNK, H_B * GM_CHUNK), 1) % GM_CHUNK
    wcat = jnp.where(wi >= wj, wcat_ref[...], 0.0).astype(BF16)
    for c in range(tb // GM_CHUNK):
        rows = slice(c * GM_CHUNK, (c + 1) * GM_CHUNK)
        vn = _rms(jax.nn.gelu(v_ref[rows, :])) * gv_ref[...]
        vb = vn.astype(BF16)
        stack = jnp.concatenate(
            [jnp.where(lane_head == h, vb, jnp.zeros_like(vb)) for h in range(H_B)], axis=0)
        s = _dot(wcat, stack) + bias_ref[...]
        o_ref[rows, :] = (jax.nn.gelu(u_ref[rows, :]) * s).astype(o_ref.dtype)


def gmlp_prompt(proj3, g_v, w_s, b_s, *, tb=512):
    B, L = proj3.shape[:2]
    hd = D_B // H_B
    wcat = jnp.transpose(w_s, (1, 0, 2)).reshape(GM_CHUNK, H_B * GM_CHUNK)
    bias = jnp.repeat(b_s.T, hd, axis=1)
    const2 = lambda b, t: (0, 0)
    ub = D_A // D_B
    return pl.pallas_call(
        functools.partial(_gmlp_p_body, tb=tb),
        grid=(B, L // tb),
        in_specs=[pl.BlockSpec((None, tb, D_B), lambda b, t: (b, t, ub)),
                  pl.BlockSpec((None, tb, D_B), lambda b, t: (b, t, ub + 1)),
                  pl.BlockSpec((1, D_B), const2),
                  pl.BlockSpec((GM_CHUNK, H_B * GM_CHUNK), const2),
                  pl.BlockSpec((GM_CHUNK, D_B), const2)],
        out_specs=pl.BlockSpec((None, tb, D_B), lambda b, t: (b, t, 0)),
        out_shape=jax.ShapeDtypeStruct((B, L, D_B), BF16),
        compiler_params=_cp(("arbitrary", "arbitrary")),
        name="gmlp_prompt",
    )(proj3, proj3, g_v, wcat, bias)


def _gmlp_s_body(u_ref, v_ref, gv_ref, w0_ref, b0_ref, o_ref, vn_ref):
    vn = _rms(jax.nn.gelu(v_ref[...])) * gv_ref[...]
    vn_ref[...] = vn
    s = w0_ref[...] * vn + b0_ref[...]
    o_ref[...] = (jax.nn.gelu(u_ref[...]) * s).astype(o_ref.dtype)


def gmlp_sample(proj, g_v, w_s, b_s):
    R = proj.shape[0]
    hd = D_B // H_B
    w0 = jnp.repeat(w_s[:, 0, 0], hd).reshape(1, D_B)
    b0 = jnp.repeat(b_s[:, 0], hd).reshape(1, D_B)
    z2 = lambda i: (0, 0)
    ub = D_A // D_B
    return pl.pallas_call(
        _gmlp_s_body,
        grid=(1,),
        in_specs=[pl.BlockSpec((R, D_B), lambda i: (0, ub)),
                  pl.BlockSpec((R, D_B), lambda i: (0, ub + 1)),
                  pl.BlockSpec((1, D_B), z2), pl.BlockSpec((1, D_B), z2), pl.BlockSpec((1, D_B), z2)],
        out_specs=[pl.BlockSpec((R, D_B), z2), pl.BlockSpec((R, D_B), z2)],
        out_shape=[jax.ShapeDtypeStruct((R, D_B), BF16), jax.ShapeDtypeStruct((R, D_B), F32)],
        compiler_params=_cp(("arbitrary",)),
        name="gmlp_sample",
    )(proj, proj, g_v, w0, b0)


def _head_expand_matrix():
    r = lax.broadcasted_iota(jnp.int32, (LANE, D_C), 0)
    c = lax.broadcasted_iota(jnp.int32, (LANE, D_C), 1) // HD_C
    return jnp.where(r == c, 1.0, 0.0).astype(BF16)


def _expand_heads(v, e):
    hi, lo = _split_bf16(v)
    return _dot(hi, e) + _dot(lo, e)


def _ssd_p_body(z_ref, xbc_ref, dt_ref, cw_ref, cb_ref, dtb_ref, alog_ref, dsk_ref, gn_ref,
                o_ref, st_ref, h_ref, h_sc, cv_sc, *, nt):
    Q = SSD_CHUNK
    t = pl.program_id(1)

    @pl.when(t == 0)
    def _():
        h_sc[...] = jnp.zeros_like(h_sc)
        cv_sc[0:SUBLANE, :] = jnp.zeros((SUBLANE, D_XBC), F32)

    xbc = xbc_ref[...]
    cv_sc[SUBLANE:SUBLANE + Q, :] = xbc
    acc = cb_ref[...]
    for k in range(K_C):
        off = SUBLANE - (K_C - 1) + k
        acc = acc + cw_ref[k:k + 1, :] * cv_sc[off:off + Q, :]
    cv_sc[0:SUBLANE, :] = xbc[Q - SUBLANE:, :]
    st_ref[...] = xbc[Q - (K_C - 1):, :]
    xc = _silu(acc)
    xs = xc[:, :D_C]

    dt = _softplus(dt_ref[...] + dtb_ref[...])
    a = -jnp.exp(alog_ref[...])
    da = dt * a
    ii = lax.broadcasted_iota(jnp.int32, (Q, Q), 0)
    jj = lax.broadcasted_iota(jnp.int32, (Q, Q), 1)
    causal = ii >= jj
    tril = jnp.where(causal, 1.0, 0.0).astype(BF16)
    d0, d1 = _split_bf16(da)
    d2 = (da - d0.astype(F32) - d1.astype(F32)).astype(BF16)
    cs = _dot(tril, d0) + _dot(tril, d1) + _dot(tril, d2)
    cst = cs.T
    cs_end = cs[Q - 1:Q, :]
    e = _head_expand_matrix()
    dt_f = _expand_heads(dt, e)
    ws_f = _expand_heads(dt * jnp.exp(cs_end - cs), e)
    ecs_f = _expand_heads(jnp.exp(cs), e)
    xdt = xs * dt_f
    xw = xs * ws_f

    lane = lax.broadcasted_iota(jnp.int32, (Q, LANE), 1)
    hpg = H_C // G_C
    gw = hpg * HD_C
    ys = []
    for g in range(G_C):
        bg = xc[:, D_C + g * N_C:D_C + (g + 1) * N_C].astype(BF16)
        cg = xc[:, D_C + G_C * N_C + g * N_C:D_C + G_C * N_C + (g + 1) * N_C].astype(BF16)
        gmat = _dot_nt(cg, bg)
        hprev = h_sc[g * gw:(g + 1) * gw, :]
        yoff = _dot_nt(cg, hprev.astype(BF16)) * ecs_f[:, g * gw:(g + 1) * gw]
        snew = _dot_tn(xw[:, g * gw:(g + 1) * gw].astype(BF16), bg)
        for hp in range(hpg // 2):
            h0 = g * hpg + 2 * hp
            xpair = xdt[:, h0 * HD_C:(h0 + 2) * HD_C]
            x_lo = jnp.where(lane < HD_C, xpair, 0.0).astype(BF16)
            x_hi = jnp.where(lane >= HD_C, xpair, 0.0).astype(BF16)
            yd = None
            for hh, xh in ((h0, x_lo), (h0 + 1, x_hi)):
                seg = cs[:, hh:hh + 1] - cst[hh:hh + 1, :]
                sc = (gmat * jnp.exp(jnp.where(causal, seg, NEG_BIG))).astype(BF16)
                part = _dot(sc, xh)
                yd = part if yd is None else yd + part
            ys.append(yd + yoff[:, 2 * hp * HD_C:(2 * hp + 2) * HD_C])
        for hh in range(hpg):
            h = g * hpg + hh
            cd = jnp.exp(cst[h:h + 1, Q - 1:Q])
            rows = slice(h * HD_C, (h + 1) * HD_C)
            h_sc[rows, :] = h_sc[rows, :] * cd + snew[hh * HD_C:(hh + 1) * HD_C, :]

    y = jnp.concatenate(ys, axis=1) + dsk_ref[...] * xs
    y = y * _silu(z_ref[...])
    o_ref[...] = (_rms(y) * gn_ref[...]).astype(o_ref.dtype)

    @pl.when(t == nt - 1)
    def _():
        h_ref[...] = h_sc[...]


def _ssd_params(dt_bias, a_log, d_skip):
    pad = LANE - H_C
    dtb = jnp.pad(dt_bias, (0, pad)).reshape(1, LANE)
    alog = jnp.pad(a_log, (0, pad)).reshape(1, LANE)
    dsk = jnp.repeat(d_skip, HD_C).reshape(1, D_C)
    return dtb, alog, dsk


def ssd_prompt(z3, xd3, conv_w, conv_b, dt_bias, a_log, d_skip, g_norm):
    B, L = z3.shape[:2]
    Q = SSD_CHUNK
    nt = L // Q
    dtb, alog, dsk = _ssd_params(dt_bias, a_log, d_skip)
    const2 = lambda b, t: (0, 0)
    blk = lambda w: pl.BlockSpec((None, Q, w), lambda b, t: (b, t, 0))
    return pl.pallas_call(
        functools.partial(_ssd_p_body, nt=nt),
        grid=(B, nt),
        in_specs=[blk(D_C), blk(D_XBC),
                  pl.BlockSpec((None, Q, LANE), lambda b, t: (b, t, D_XBC // LANE)),
                  pl.BlockSpec((K_C, D_XBC), const2), pl.BlockSpec((1, D_XBC), const2),
                  pl.BlockSpec((1, LANE), const2), pl.BlockSpec((1, LANE), const2),
                  pl.BlockSpec((1, D_C), const2), pl.BlockSpec((1, D_C), const2)],
        out_specs=[blk(D_C),
                   pl.BlockSpec((None, K_C - 1, D_XBC), lambda b, t: (b, 0, 0)),
                   pl.BlockSpec((None, H_C * HD_C, N_C), lambda b, t: (b, 0, 0))],
        out_shape=[jax.ShapeDtypeStruct((B, L, D_C), BF16),
                   jax.ShapeDtypeStruct((B, K_C - 1, D_XBC), F32),
                   jax.ShapeDtypeStruct((B, H_C * HD_C, N_C), F32)],
        scratch_shapes=[pltpu.VMEM((H_C * HD_C, N_C), F32), pltpu.VMEM((SUBLANE + Q, D_XBC), F32)],
        compiler_params=_cp(("arbitrary", "arbitrary")),
        name="ssd_prompt",
    )(z3, xd3, xd3, conv_w, conv_b.reshape(1, D_XBC), dtb, alog, dsk, g_norm.reshape(1, D_C))


def _ssd_s_body(z_ref, xbc_ref, dt_ref, buf_ref, h0_ref, cw_ref, cb_ref, dtb_ref, alog_ref,
                dsk_ref, gn_ref, *rest, tbatch, nsteps, has_prev):
    o_ref, hn_ref, xs_sc, bc_sc, xt_hi, xt_lo, at_hi, at_lo, y_sc = rest[1:] if has_prev else rest
    s = pl.program_id(0)
    R = xs_sc.shape[0]

    @pl.when(s == 0)
    def _():
        acc = cb_ref[...] + cw_ref[K_C - 1:K_C, :] * xbc_ref[...]
        for k in range(K_C - 1):
            acc = acc + cw_ref[k:k + 1, :] * buf_ref[k]
        xc = _silu(acc)
        xs = xc[:, :D_C]
        xs_sc[...] = xs
        bc_sc[...] = xc[:, D_C:]
        dt = _softplus(dt_ref[...] + dtb_ref[...])
        e = _head_expand_matrix()
        dt_f = _expand_heads(dt, e)
        da_f = jnp.exp(_expand_heads(dt * (-jnp.exp(alog_ref[...])), e))
        hi, lo = _split_bf16((xs * dt_f).T)
        xt_hi[...] = hi
        xt_lo[...] = lo
        hi, lo = _split_bf16(da_f.T)
        at_hi[...] = hi
        at_lo[...] = lo

    gw = (H_C // G_C) * HD_C
    kk = lax.broadcasted_iota(jnp.int32, (R, LANE), 0)
    r0 = pl.multiple_of(s * tbatch, tbatch)
    bc8 = bc_sc[pl.ds(r0, tbatch), :]
    ti = lax.broadcasted_iota(jnp.int32, (tbatch, gw), 0)
    ytile = [jnp.zeros((tbatch, gw), F32) for _ in range(G_C)]
    for i in range(tbatch):
        b = s * tbatch + i
        onehot = jnp.where(kk == b, 1.0, 0.0).astype(BF16)
        xb = _dot(xt_hi[...], onehot) + _dot(xt_lo[...], onehot)
        ab = _dot(at_hi[...], onehot) + _dot(at_lo[...], onehot)
        for g in range(G_C):
            rows = slice(g * gw, (g + 1) * gw)
            brow = bc8[i:i + 1, g * N_C:(g + 1) * N_C]
            hn = h0_ref[i, rows, :] * ab[rows, :] + xb[rows, :] * brow
            hn_ref[i, rows, :] = hn
            c8 = bc8[:, G_C * N_C + g * N_C:G_C * N_C + (g + 1) * N_C]
            yg = _dot_nt(c8.astype(BF16), hn.astype(BF16))
            ytile[g] = jnp.where(ti == i, yg, ytile[g])
    for g in range(G_C):
        y_sc[pl.ds(r0, tbatch), g * gw:(g + 1) * gw] = ytile[g]

    @pl.when(s == nsteps - 1)
    def _():
        y = y_sc[...] + dsk_ref[...] * xs_sc[...]
        y = y * _silu(z_ref[...])
        o_ref[...] = (_rms(y) * gn_ref[...]).astype(o_ref.dtype)


def ssd_sample(z, xd, buf, h0_all, l, hn_all, conv_w, conv_b, dt_bias, a_log, d_skip, g_norm,
               *, tbatch=8):
    R = z.shape[0]
    nsteps = R // tbatch
    dtb, alog, dsk = _ssd_params(dt_bias, a_log, d_skip)
    z2 = lambda s: (0, 0)
    hw = H_C * HD_C
    st_spec = pl.BlockSpec((None, tbatch, hw, N_C), lambda s: (l, s, 0, 0))
    in_specs = [pl.BlockSpec((R, D_C), z2), pl.BlockSpec((R, D_XBC), z2),
                pl.BlockSpec((R, LANE), lambda s: (0, D_XBC // LANE)),
                pl.BlockSpec((K_C - 1, R, D_XBC), lambda s: (0, 0, 0)),
                st_spec,
                pl.BlockSpec((K_C, D_XBC), z2), pl.BlockSpec((1, D_XBC), z2),
                pl.BlockSpec((1, LANE), z2), pl.BlockSpec((1, LANE), z2),
                pl.BlockSpec((1, D_C), z2), pl.BlockSpec((1, D_C), z2)]
    args = [z, xd, xd, buf, h0_all, conv_w, conv_b.reshape(1, D_XBC), dtb, alog, dsk,
            g_norm.reshape(1, D_C)]
    aliases = {}
    if hn_all is not None:
        aliases = {len(args): 1}
        in_specs.append(pl.BlockSpec(memory_space=pl.ANY))
        args.append(hn_all)
    return pl.pallas_call(
        functools.partial(_ssd_s_body, tbatch=tbatch, nsteps=nsteps, has_prev=hn_all is not None),
        grid=(nsteps,),
        in_specs=in_specs,
        out_specs=[pl.BlockSpec((R, D_C), z2), st_spec],
        out_shape=[jax.ShapeDtypeStruct((R, D_C), BF16),
                   jax.ShapeDtypeStruct(h0_all.shape, F32)],
        scratch_shapes=[pltpu.VMEM((R, D_C), F32), pltpu.VMEM((R, 2 * G_C * N_C), F32),
                        pltpu.VMEM((D_C, R), BF16), pltpu.VMEM((D_C, R), BF16),
                        pltpu.VMEM((D_C, R), BF16), pltpu.VMEM((D_C, R), BF16),
                        pltpu.VMEM((R, D_C), F32)],
        input_output_aliases=aliases,
        compiler_params=_cp(("arbitrary",)),
        name="ssd_sample",
    )(*args)


def _up_p_body(x_ref, wa_ref, wb_ref, cwa_ref, cwb_ref, cba_ref, cbb_ref,
               o_ref, sa_ref, sb_ref, wa_sc, wb_sc, ca_sc, cb_sc, *, tm, ts, tiles_per_seq):
    i = pl.program_id(1)

    @pl.when(i == 0)
    def _():
        wa_sc[...] = wa_ref[...].astype(BF16)
        wb_sc[...] = wb_ref[...].astype(BF16)

    @pl.when(i % tiles_per_seq == 0)
    def _():
        ca_sc[0:SUBLANE, :] = jnp.zeros((SUBLANE, ca_sc.shape[1]), F32)
        cb_sc[0:SUBLANE, :] = jnp.zeros((SUBLANE, cb_sc.shape[1]), F32)

    def half(r0, w_sc, c_sc, cw_ref, cb_ref):
        up = _dot(x_ref[r0:r0 + ts, :], w_sc[...])
        c_sc[SUBLANE + r0:SUBLANE + r0 + ts, :] = up
        acc = cb_ref[...]
        for k in range(K_F):
            off = SUBLANE - (K_F - 1) + k + r0
            acc = acc + cw_ref[k:k + 1, :] * c_sc[off:off + ts, :]
        return acc

    for r0 in range(0, tm, ts):
        a = half(r0, wa_sc, ca_sc, cwa_ref, cba_ref)
        b = half(r0, wb_sc, cb_sc, cwb_ref, cbb_ref)
        o_ref[r0:r0 + ts, :] = (_silu(a) * b).astype(o_ref.dtype)

    for c_sc, st_ref in ((ca_sc, sa_ref), (cb_sc, sb_ref)):
        st_ref[...] = c_sc[SUBLANE + tm - (K_F - 1):SUBLANE + tm, :]
        c_sc[0:SUBLANE, :] = c_sc[tm:tm + SUBLANE, :]


def up_prompt(h2, w_up, l, conv_w, conv_b, *, seq_len, tm=1024, tn=512, ts=1024):
    M, D = h2.shape
    B = M // seq_len
    tiles_per_seq = seq_len // tm
    nj = D_FF // tn
    cw = conv_w
    cb = conv_b.reshape(1, 2 * D_FF)
    body = functools.partial(_up_p_body, tm=tm, ts=min(ts, tm), tiles_per_seq=tiles_per_seq)
    st_spec = pl.BlockSpec((None, K_F - 1, tn), lambda j, i: (i // tiles_per_seq, 0, j))
    return pl.pallas_call(
        body,
        grid=(nj, M // tm),
        in_specs=[pl.BlockSpec((tm, D), lambda j, i: (i, 0)),
                  pl.BlockSpec((None, D, tn), lambda j, i: (l, 0, j)),
                  pl.BlockSpec((None, D, tn), lambda j, i: (l, 0, nj + j)),
                  pl.BlockSpec((K_F, tn), lambda j, i: (0, j)),
                  pl.BlockSpec((K_F, tn), lambda j, i: (0, nj + j)),
                  pl.BlockSpec((1, tn), lambda j, i: (0, j)),
                  pl.BlockSpec((1, tn), lambda j, i: (0, nj + j))],
        out_specs=[pl.BlockSpec((tm, tn), lambda j, i: (i, j)), st_spec, st_spec],
        out_shape=[jax.ShapeDtypeStruct((M, D_FF), BF16),
                   jax.ShapeDtypeStruct((B, K_F - 1, D_FF), F32),
                   jax.ShapeDtypeStruct((B, K_F - 1, D_FF), F32)],
        scratch_shapes=[pltpu.VMEM((D, tn), BF16), pltpu.VMEM((D, tn), BF16),
                        pltpu.VMEM((SUBLANE + tm, tn), F32), pltpu.VMEM((SUBLANE + tm, tn), F32)],
        compiler_params=_cp(("arbitrary", "arbitrary")),
        name="up_prompt",
    )(h2, w_up, w_up, cw, cw, cb, cb)


def _up_s_body(x_ref, wa_ref, wb_ref, bufa_ref, bufb_ref, cwa_ref, cwb_ref, cba_ref, cbb_ref,
               o_ref, ua_ref, ub_ref):
    x = x_ref[...]

    def half(w_ref, buf_ref, cw_ref, cb_ref, up_ref):
        up = _dot(x, w_ref[...].astype(BF16))
        up_ref[...] = up
        acc = cb_ref[...] + cw_ref[K_F - 1:K_F, :] * up
        for k in range(K_F - 1):
            acc = acc + cw_ref[k:k + 1, :] * buf_ref[k]
        return acc

    a = half(wa_ref, bufa_ref, cwa_ref, cba_ref, ua_ref)
    b = half(wb_ref, bufb_ref, cwb_ref, cbb_ref, ub_ref)
    o_ref[...] = (_silu(a) * b).astype(o_ref.dtype)


def up_sample(h2, w_up, l, buf, conv_w, conv_b, *, tn=512):
    R, D = h2.shape
    nj = D_FF // tn
    cb = conv_b.reshape(1, 2 * D_FF)
    out_spec = pl.BlockSpec((R, tn), lambda j: (0, j))
    return pl.pallas_call(
        _up_s_body,
        grid=(nj,),
        in_specs=[pl.BlockSpec((R, D), lambda j: (0, 0)),
                  pl.BlockSpec((None, D, tn), lambda j: (l, 0, j)),
                  pl.BlockSpec((None, D, tn), lambda j: (l, 0, nj + j)),
                  pl.BlockSpec((K_F - 1, R, tn), lambda j: (0, 0, j)),
                  pl.BlockSpec((K_F - 1, R, tn), lambda j: (0, 0, nj + j)),
                  pl.BlockSpec((K_F, tn), lambda j: (0, j)),
                  pl.BlockSpec((K_F, tn), lambda j: (0, nj + j)),
                  pl.BlockSpec((1, tn), lambda j: (0, j)),
                  pl.BlockSpec((1, tn), lambda j: (0, nj + j))],
        out_specs=[out_spec, out_spec, out_spec],
        out_shape=[jax.ShapeDtypeStruct((R, D_FF), BF16),
                   jax.ShapeDtypeStruct((R, D_FF), F32),
                   jax.ShapeDtypeStruct((R, D_FF), F32)],
        compiler_params=_cp(("arbitrary",)),
        name="up_sample",
    )(h2, w_up, w_up, buf, buf, conv_w, conv_w, cb, cb)


def _layer(x3, mods, l, p, state, *, seq_len, tm, tn_merge, tm_norm):
    B, L, D = x3.shape
    M = B * L
    sh_m, sc_m, gt_m, sh_f, sc_f, gt_f = mods
    tiles_per_seq = max(seq_len // tm, 1)
    prompt = state is None

    h = norm_mod(x3, p['g_mix'][l].reshape(1, D), sc_m, sh_m, tm=tm_norm).reshape(M, D)

    def proj(col0, n, name):
        return matmul([h], [(p['w_in_t'], l, col0, True)], [], [(n, F32)], _epi_store,
                      x_of_w=(0,), tm=tm, tn=n, name=name, vmem_mb=56, w_buffers=1)[0]

    uav = proj(OFF_UA, D_A + 2 * D_B, "proj_uav")
    zc = proj(OFF_Z, D_C, "proj_z")
    xd = proj(OFF_XBC, D_XBC + LANE, "proj_xd")

    tab, wb = s5_prep(p['lam_re'][l], p['lam_im'][l], p['log_dt'][l], p['b_re'][l], p['b_im'][l])
    wcr, wci = s5_out_weights(p['c_re'][l], p['c_im'][l])
    d_a = p['s5_d'][l].reshape(1, D_A)
    wglu = p['w_glu'][l].astype(BF16)
    g_v = p['g_v'][l].reshape(1, D_B)

    if prompt:
        o_a, s5r, s5i = s5_prompt(uav.reshape(B, L, -1), wb, wcr, wci, tab, d_a, wglu)
        o_b = gmlp_prompt(uav.reshape(B, L, -1), g_v, p['w_s'][l], p['b_s'][l])
        v_rows = None
        o_c, convc, ssm = ssd_prompt(zc.reshape(B, L, -1), xd.reshape(B, L, -1),
                                     p['ssd_conv_w'][l], p['ssd_conv_b'][l], p['dt_bias'][l],
                                     p['a_log'][l], p['ssd_d'][l], p['ssd_g'][l])
        s5r = s5r.reshape(B, G_A, P_A)
        s5i = s5i.reshape(B, G_A, P_A)
    else:
        s5_re0, s5_im0, ssm_all, ssm_new_all, convc0, convf0 = state
        o_a, s5r, s5i = s5_sample(uav, s5_re0.reshape(M, N_S5), s5_im0.reshape(M, N_S5),
                                  wb, wcr, wci, tab, d_a, wglu)
        o_b, v_rows = gmlp_sample(uav, g_v, p['w_s'][l], p['b_s'][l])
        o_c, ssm = ssd_sample(zc, xd, jnp.transpose(convc0, (1, 0, 2)),
                              ssm_all, l, ssm_new_all,
                              p['ssd_conv_w'][l], p['ssd_conv_b'][l], p['dt_bias'][l],
                              p['a_log'][l], p['ssd_d'][l], p['ssd_g'][l])
        convc = jnp.concatenate([convc0[:, 1:], xd[:, None, :D_XBC]], axis=1)
        s5r = s5r.reshape(M, G_A, P_A)
        s5i = s5i.reshape(M, G_A, P_A)
    o_a = o_a.reshape(M, D_A)
    o_b = o_b.reshape(M, D_B)
    o_c = o_c.reshape(M, D_C)
    if prompt:
        ssm = ssm.reshape(B, H_C, HD_C, N_C)

    w_t = p['w_in_t']
    merged = matmul(
        [h, o_a, o_b, o_c],
        [(w_t, l, OFF_GATES, True), (w_t, l, OFF_GATES + D, True), (w_t, l, OFF_GATES + 2 * D, True),
         (p['w_pa'], l, 0, False), (p['w_pb'], l, 0, False), (p['w_pc'], l, 0, False)],
        [], [(D, BF16)], _epi_merge, x_of_w=(0, 0, 0, 1, 2, 3), tm=tm, tn=tn_merge, name="merge",
        vmem_mb=56, w_buffers=1)[0]

    def residual(x_res, act_in, w3, gt, tm_r, name, vmem_mb):
        tn = 512
        return matmul([act_in], [(w3, l, 0, False)],
                      [(x_res, _tile_spec(tm_r, tn)),
                       (gt, _gate_spec(gt, tm_r, tn, max(seq_len // tm_r, 1)))],
                      [(D, F32)], _epi_residual, x_of_w=(0,), tm=tm_r, tn=tn, name=name,
                      vmem_mb=vmem_mb)[0]

    x2, h2 = out_norm(merged, p['w_out'], l, x3.reshape(M, D), gt_m, p['g_ffn'][l].reshape(1, D),
                      sc_f, sh_f, tm=min(tm, 256), seq_len=seq_len)
    if prompt:
        act, st_a, st_b = up_prompt(h2, p['w_up'], l, p['ffn_conv_w'][l], p['ffn_conv_b'][l],
                                    seq_len=seq_len, tm=tm)
        convf = jnp.concatenate([st_a, st_b], axis=-1)
    else:
        act, up_a, up_b = up_sample(h2, p['w_up'], l, jnp.transpose(convf0, (1, 0, 2)),
                                    p['ffn_conv_w'][l], p['ffn_conv_b'][l])
        convf = jnp.concatenate([convf0[:, 1:], jnp.concatenate([up_a, up_b], axis=-1)[:, None, :]], axis=1)

    x2 = residual(x2, act, p['w_down'], gt_f, min(tm, 512), "down_proj", 56)
    return x2.reshape(B, L, D), s5r, s5i, ssm, convc, convf, v_rows


def kernel(x_prompt, x_sample, c_prompt, c_sample, state_s5_re, state_s5_im, state_ssm, state_ssd_conv, state_ffn_conv, w_mod, b_mod, g_mix, w_in, s5_lam_re, s5_lam_im, s5_log_dt, s5_b_re, s5_b_im, s5_c_re, s5_c_im, s5_d, s5_w_glu, gm_g_v, gm_w_s, gm_b_s, ssd_conv_w, ssd_conv_b, ssd_dt_bias, ssd_a_log, ssd_d, ssd_g_norm, w_pa, w_pb, w_pc, w_out, g_ffn, ffn_w_up, ffn_conv_w, ffn_conv_b, ffn_w_down, g_final):
    p = {
        'g_mix': g_mix, 'w_in_t': jnp.swapaxes(w_in, 1, 2),
        'lam_re': s5_lam_re, 'lam_im': s5_lam_im, 'log_dt': s5_log_dt,
        'b_re': s5_b_re, 'b_im': s5_b_im, 'c_re': s5_c_re, 'c_im': s5_c_im,
        's5_d': s5_d, 'w_glu': s5_w_glu,
        'g_v': gm_g_v, 'w_s': gm_w_s, 'b_s': gm_b_s,
        'ssd_conv_w': ssd_conv_w, 'ssd_conv_b': ssd_conv_b, 'dt_bias': ssd_dt_bias,
        'a_log': ssd_a_log, 'ssd_d': ssd_d, 'ssd_g': ssd_g_norm,
        'w_pa': w_pa, 'w_pb': w_pb, 'w_pc': w_pc, 'w_out': w_out,
        'g_ffn': g_ffn, 'w_up': ffn_w_up, 'ffn_conv_w': ffn_conv_w,
        'ffn_conv_b': ffn_conv_b, 'w_down': ffn_w_down,
    }
    bp, seq, D = x_prompt.shape
    bs = x_sample.shape[0]

    n_c = bs + bp
    pad = (-n_c) % SUBLANE
    c_all = jnp.concatenate([c_sample, c_prompt, jnp.zeros((pad, D), F32)], axis=0)
    mod = mod_all(c_all, w_mod, b_mod)

    xp = x_prompt
    xs = x_sample.reshape(1, bs, D)
    outs_p = [[] for _ in range(5)]
    outs_s = [[] for _ in range(6)]
    ssm_all = state_ssm.reshape(DEPTH, bs, H_C * HD_C, N_C)
    ssm_new_all = None
    for l in range(DEPTH):
        mods_s = [m[None] for m in jnp.split(mod[l, :bs], 6, axis=-1)]
        mods_p = [m[:, None, :] for m in jnp.split(mod[l, bs:bs + bp], 6, axis=-1)]
        xp, *st_p = _layer(xp, mods_p, l, p, None, seq_len=seq, tm=1024, tn_merge=512, tm_norm=512)
        for acc, v in zip(outs_p, st_p[:5]):
            acc.append(v)
        state = (state_s5_re[l], state_s5_im[l], ssm_all, ssm_new_all, state_ssd_conv[l], state_ffn_conv[l])
        xs, *st_s = _layer(xs, mods_s, l, p, state, seq_len=1, tm=bs, tn_merge=512, tm_norm=bs)
        ssm_new_all = st_s[2]
        st_s[5] = st_s[5].reshape(bs, 1, D_B)
        for acc, v in zip(outs_s, st_s):
            acc.append(v)

    g_fin = g_final.reshape(1, D)
    y_prompt = final_norm(xp, g_fin, tm=512)
    y_sample = final_norm(xs, g_fin, tm=bs).reshape(bs, 1, D)
    ssm_s = ssm_new_all.reshape(DEPTH, bs, H_C, HD_C, N_C)
    outs_s = [ssm_s if k == 2 else jnp.stack(v) for k, v in enumerate(outs_s)]
    return (y_prompt, y_sample, *[jnp.stack(v) for v in outs_p], *outs_s)
```

```python
import functools

import jax
import jax.numpy as jnp
from jax import lax
from jax.experimental import pallas as pl
from jax.experimental.pallas import tpu as pltpu

F32 = jnp.float32
BF16 = jnp.bfloat16

D_MODEL = 2048
DEPTH = 2
D_A = 512
S5_GROUP = 16
G_A = D_A // S5_GROUP
P_A = 64
N_S5 = G_A * P_A
S5_GPB = 128 // S5_GROUP
S5_NB = D_A // 128
S5_SPB = S5_GPB * P_A
D_B = 512
H_B = 8
GM_CHUNK = 128
D_C = 1024
HD_C = 64
H_C = D_C // HD_C
N_C = 128
G_C = 2
K_C = 4
SSD_CHUNK = 128
D_XBC = D_C + 2 * G_C * N_C
D_FF = 5632
K_F = 3
EPS = 1e-6

OFF_UA = 0
OFF_Z = D_A + 2 * D_B
OFF_XBC = OFF_Z + D_C
OFF_DT = OFF_XBC + D_XBC
OFF_GATES = OFF_DT + H_C

LANE = 128
SUBLANE = 8
NEG_BIG = -1e30


def _cp(sem, vmem_mb=48):
    return pltpu.CompilerParams(dimension_semantics=sem, vmem_limit_bytes=vmem_mb * 1024 * 1024)


def _sigmoid(x):
    return 0.5 * jnp.tanh(0.5 * x) + 0.5


def _silu(x):
    return x * _sigmoid(x)


def _softplus(x):
    return jnp.maximum(x, 0.0) + jnp.log1p(jnp.exp(-jnp.abs(x)))


def _rms(x):
    return x * lax.rsqrt(jnp.mean(x * x, axis=-1, keepdims=True) + EPS)


def _split_bf16(x):
    hi = x.astype(BF16)
    lo = (x - hi.astype(F32)).astype(BF16)
    return hi, lo


def _dot(a, b):
    return jnp.dot(a, b, preferred_element_type=F32)


def _dot_nt(a, b):
    return lax.dot_general(a, b, (((1,), (1,)), ((), ())), preferred_element_type=F32)


def _dot_tn(a, b):
    return lax.dot_general(a, b, (((0,), (0,)), ((), ())), preferred_element_type=F32)


def _mod_body(c_ref, w_ref, b_ref, o_ref):
    a = _silu(c_ref[...]).astype(BF16)
    o_ref[...] = _dot(a, w_ref[...].astype(BF16)) + b_ref[...]


def mod_all(c_all, w_mod, b_mod, *, tn=1024):
    R, D = c_all.shape
    N = w_mod.shape[-1]
    return pl.pallas_call(
        _mod_body,
        grid=(DEPTH, N // tn),
        in_specs=[pl.BlockSpec((R, D), lambda l, j: (0, 0)),
                  pl.BlockSpec((None, D, tn), lambda l, j: (l, 0, j)),
                  pl.BlockSpec((None, 1, tn), lambda l, j: (l, 0, j))],
        out_specs=pl.BlockSpec((None, R, tn), lambda l, j: (l, 0, j)),
        out_shape=jax.ShapeDtypeStruct((DEPTH, R, N), F32),
        compiler_params=_cp(("arbitrary", "arbitrary")),
        name="mod_all",
    )(c_all, w_mod, b_mod.reshape(DEPTH, 1, N))


def _norm_mod_body(x_ref, g_ref, sc_ref, sh_ref, o_ref):
    y = _rms(x_ref[...]) * g_ref[...]
    o_ref[...] = (y * (1.0 + sc_ref[...]) + sh_ref[...]).astype(o_ref.dtype)


def norm_mod(x3, g, sc3, sh3, *, tm):
    B, L, D = x3.shape
    per_row = sc3.shape[1] != 1
    ts = tm if per_row else 1
    sidx = (lambda b, i: (b, i, 0)) if per_row else (lambda b, i: (b, 0, 0))
    return pl.pallas_call(
        _norm_mod_body,
        grid=(B, L // tm),
        in_specs=[pl.BlockSpec((None, tm, D), lambda b, i: (b, i, 0)),
                  pl.BlockSpec((1, D), lambda b, i: (0, 0)),
                  pl.BlockSpec((None, ts, D), sidx),
                  pl.BlockSpec((None, ts, D), sidx)],
        out_specs=pl.BlockSpec((None, tm, D), lambda b, i: (b, i, 0)),
        out_shape=jax.ShapeDtypeStruct((B, L, D), BF16),
        compiler_params=_cp(("arbitrary", "arbitrary")),
        name="norm_mod",
    )(x3, g, sc3, sh3)


def _rms_body(x_ref, g_ref, o_ref):
    o_ref[...] = _rms(x_ref[...]) * g_ref[...]


def final_norm(x3, g, *, tm):
    B, L, D = x3.shape
    return pl.pallas_call(
        _rms_body,
        grid=(B, L // tm),
        in_specs=[pl.BlockSpec((None, tm, D), lambda b, i: (b, i, 0)),
                  pl.BlockSpec((1, D), lambda b, i: (0, 0))],
        out_specs=pl.BlockSpec((None, tm, D), lambda b, i: (b, i, 0)),
        out_shape=jax.ShapeDtypeStruct((B, L, D), F32),
        compiler_params=_cp(("arbitrary", "arbitrary")),
        name="final_norm",
    )(x3, g)


def _mm_body(*refs, x_of_w, w_is_t, n_x, n_e, n_o, epilogue):
    n_w = len(x_of_w)
    xs = refs[:n_x]
    ws = refs[n_x:n_x + n_w]
    es = refs[n_x + n_w:n_x + n_w + n_e]
    outs = refs[n_x + n_w + n_e:n_x + n_w + n_e + n_o]
    wsc = refs[n_x + n_w + n_e + n_o:]

    @pl.when(pl.program_id(1) == 0)
    def _():
        for w, s, is_t in zip(ws, wsc, w_is_t):
            s[...] = (w[...].T if is_t else w[...]).astype(BF16)

    accs = [_dot(xs[xi][...], s[...]) for xi, s in zip(x_of_w, wsc)]
    epilogue(accs, es, outs)


def matmul(xs, ws, extras, outs, epilogue, *, x_of_w, tm, tn, name, vmem_mb=48, w_buffers=2):
    M = xs[0].shape[0]
    N = outs[0][0]
    in_specs = [pl.BlockSpec((tm, x.shape[1]), lambda j, i: (i, 0)) for x in xs]
    args = list(xs)
    scratch = []
    for w3, l, col0, is_t in ws:
        if is_t:
            K = w3.shape[2]
            assert col0 % SUBLANE == 0
            in_specs.append(pl.BlockSpec((None, pl.Element(tn), pl.Element(K)), functools.partial(
                lambda j, i, l, c: (l, (c + j * (tn // SUBLANE)) * SUBLANE, 0),
                l=l, c=col0 // SUBLANE), pipeline_mode=pl.Buffered(w_buffers)))
        else:
            K = w3.shape[1]
            assert col0 % tn == 0
            in_specs.append(pl.BlockSpec((None, K, tn), functools.partial(
                lambda j, i, l, c: (l, 0, c + j), l=l, c=col0 // tn),
                pipeline_mode=pl.Buffered(w_buffers)))
        args.append(w3)
        scratch.append(pltpu.VMEM((K, tn), BF16))
    for arr, spec in extras:
        in_specs.append(spec)
        args.append(arr)
    body = functools.partial(_mm_body, x_of_w=tuple(x_of_w), w_is_t=tuple(w[3] for w in ws),
                             n_x=len(xs), n_e=len(extras), n_o=len(outs), epilogue=epilogue)
    res = pl.pallas_call(
        body,
        grid=(N // tn, M // tm),
        in_specs=in_specs,
        out_specs=[pl.BlockSpec((tm, tn), lambda j, i: (i, j)) for _ in outs],
        out_shape=[jax.ShapeDtypeStruct((M, n), dt) for n, dt in outs],
        scratch_shapes=scratch,
        compiler_params=_cp(("arbitrary", "arbitrary"), vmem_mb),
        name=name,
    )(*args)
    return res


def _epi_store(accs, es, outs):
    outs[0][...] = accs[0].astype(outs[0].dtype)


def _epi_merge(accs, es, outs):
    pa, pb, pc, ga, gb, gc = accs
    m = _sigmoid(ga) * pa + _sigmoid(gb) * pb + _sigmoid(gc) * pc
    outs[0][...] = m.astype(outs[0].dtype)


def _epi_residual(accs, es, outs):
    res_ref, gt_ref = es
    outs[0][...] = res_ref[...] + gt_ref[...] * accs[0]


def _out_norm_body(a_ref, w_ref, res_ref, gt_ref, g_ref, sc_ref, sh_ref, x_ref, h_ref, w_sc):
    @pl.when(pl.program_id(0) == 0)
    def _():
        w_sc[...] = w_ref[...].astype(BF16)

    x = res_ref[...] + gt_ref[...] * _dot(a_ref[...], w_sc[...])
    x_ref[...] = x
    y = _rms(x) * g_ref[...]
    h_ref[...] = (y * (1.0 + sc_ref[...]) + sh_ref[...]).astype(h_ref.dtype)


def out_norm(act, w3, l, res, gt, g, sc, sh, *, tm, seq_len):
    M, K = act.shape
    D = res.shape[1]
    tiles_per_seq = max(seq_len // tm, 1)

    def mod_spec(m3):
        if m3.shape[1] == 1:
            return pl.BlockSpec((None, 1, D), lambda i: (i // tiles_per_seq, 0, 0))
        return pl.BlockSpec((None, tm, D), lambda i: (0, i, 0))

    row = lambda w: pl.BlockSpec((tm, w), lambda i: (i, 0))
    return pl.pallas_call(
        _out_norm_body,
        grid=(M // tm,),
        in_specs=[row(K),
                  pl.BlockSpec((None, K, D), lambda i: (l, 0, 0), pipeline_mode=pl.Buffered(1)),
                  row(D), mod_spec(gt),
                  pl.BlockSpec((1, D), lambda i: (0, 0)), mod_spec(sc), mod_spec(sh)],
        out_specs=[row(D), row(D)],
        out_shape=[jax.ShapeDtypeStruct((M, D), F32), jax.ShapeDtypeStruct((M, D), BF16)],
        scratch_shapes=[pltpu.VMEM((K, D), BF16)],
        compiler_params=_cp(("arbitrary",), 56),
        name="out_norm",
    )(act, w3, res, gt, g, sc, sh)


def _tile_spec(tm, tn):
    return pl.BlockSpec((tm, tn), lambda j, i: (i, j))


def _gate_spec(gt3, tm, tn, tiles_per_seq):
    if gt3.shape[1] == 1:
        return pl.BlockSpec((None, 1, tn), lambda j, i: (i // tiles_per_seq, 0, j))
    return pl.BlockSpec((None, tm, tn), lambda j, i: (0, i, j))


def _s5_prep_body(lrf_ref, lif_ref, ldf_ref, lrr_ref, lir_ref, ldr_ref, bre_ref, bim_ref,
                  tab_ref, bbr_ref, bbi_ref):
    dtf = jnp.exp(ldf_ref[...])
    n = (lax.broadcasted_iota(jnp.int32, (SUBLANE, N_S5), 0) + 1).astype(F32)
    mag = jnp.exp(n * (lrf_ref[...] * dtf))
    ang = n * (lif_ref[...] * dtf)
    pr = mag * jnp.cos(ang)
    pi = mag * jnp.sin(ang)
    row = lax.broadcasted_iota(jnp.int32, (SUBLANE, N_S5), 0)
    for k, d in enumerate((1, 2, 4)):
        keep = row >= d
        tab_ref[2 * k] = jnp.where(keep, pr[d - 1:d, :], 0.0)
        tab_ref[2 * k + 1] = jnp.where(keep, pi[d - 1:d, :], 0.0)
    tab_ref[6] = pr
    tab_ref[7] = pi
    dtr = jnp.exp(ldr_ref[...])
    lr = lrr_ref[...]
    li = lir_ref[...]
    m1 = jnp.exp(lr * dtr)
    ar = m1 * jnp.cos(li * dtr)
    ai = m1 * jnp.sin(li * dtr)
    den = lr * lr + li * li
    nr = ar - 1.0
    kr = (nr * lr + ai * li) / den
    ki = (ai * lr - nr * li) / den
    bre = bre_ref[...]
    bim = bim_ref[...]
    bbr_ref[...] = kr * bre - ki * bim
    bbi_ref[...] = kr * bim + ki * bre


def s5_prep(lam_re, lam_im, log_dt, b_re, b_im):
    lrf = lam_re.reshape(1, N_S5)
    lif = lam_im.reshape(1, N_S5)
    ldf = jnp.repeat(log_dt, P_A).reshape(1, N_S5)
    lrr = jnp.repeat(lam_re, S5_GROUP, axis=0)
    lir = jnp.repeat(lam_im, S5_GROUP, axis=0)
    ldr = jnp.repeat(log_dt, S5_GROUP).reshape(D_A, 1)
    bre = jnp.transpose(b_re, (0, 2, 1)).reshape(D_A, P_A)
    bim = jnp.transpose(b_im, (0, 2, 1)).reshape(D_A, P_A)
    tab, bbr, bbi = pl.pallas_call(
        _s5_prep_body,
        out_shape=[jax.ShapeDtypeStruct((8, SUBLANE, N_S5), F32),
                   jax.ShapeDtypeStruct((D_A, P_A), F32),
                   jax.ShapeDtypeStruct((D_A, P_A), F32)],
        name="s5_prep",
    )(lrf, lif, ldf, lrr, lir, ldr, bre, bim)
    eye = jnp.eye(S5_GPB, dtype=F32)

    def blockdiag(m):
        m = m.reshape(S5_NB, S5_GPB, S5_GROUP, P_A)
        return jnp.einsum('kghp,gj->kghjp', m, eye).reshape(S5_NB, LANE, S5_SPB)

    wb = jnp.concatenate([blockdiag(bbr), blockdiag(bbi)], axis=2).astype(BF16)
    return tab, wb


def s5_out_weights(c_re, c_im):
    eye = jnp.eye(S5_GPB, dtype=F32)

    def blockdiag(c):
        c = c.reshape(S5_NB, S5_GPB, S5_GROUP, P_A)
        return jnp.einsum('kghp,gj->kgpjh', c, eye).reshape(S5_NB, S5_SPB, LANE).astype(BF16)

    return blockdiag(c_re), blockdiag(c_im)


def _s5_in(ub, wb_ref, kb):
    return _dot(ub[:, kb * LANE:(kb + 1) * LANE], wb_ref[kb])


def _s5_tail(h_blocks, u, wcr_ref, wci_ref, d_ref, wglu_ref):
    ys = []
    for kb in range(S5_NB):
        hr, hi = h_blocks(kb)
        ys.append(_dot(hr.astype(BF16), wcr_ref[kb]) - _dot(hi.astype(BF16), wci_ref[kb]))
    y = jnp.concatenate(ys, axis=1) + d_ref[...] * u
    y = jax.nn.gelu(y)
    return y * _sigmoid(_dot(y.astype(BF16), wglu_ref[...]))


def _s5_p_body(u_ref, wb_ref, wcr_ref, wci_ref, tab_ref, d_ref, wglu_ref,
               o_ref, hr_ref, hi_ref, bu_sc, car_sc, *, tb):
    @pl.when(pl.program_id(1) == 0)
    def _():
        car_sc[...] = jnp.zeros_like(car_sc)

    u = u_ref[...]
    ub = u.astype(BF16)
    for kb in range(S5_NB):
        bu = _s5_in(ub, wb_ref, kb)
        bu_sc[:, kb * S5_SPB:(kb + 1) * S5_SPB] = bu[:, :S5_SPB]
        bu_sc[:, N_S5 + kb * S5_SPB:N_S5 + (kb + 1) * S5_SPB] = bu[:, S5_SPB:]

    def tile(rt, _):
        r0 = pl.multiple_of(rt * SUBLANE, SUBLANE)
        for lg in range(N_S5 // LANE):
            cre = slice(lg * LANE, (lg + 1) * LANE)
            cim = slice(N_S5 + lg * LANE, N_S5 + (lg + 1) * LANE)
            xr = bu_sc[pl.ds(r0, SUBLANE), cre]
            xi = bu_sc[pl.ds(r0, SUBLANE), cim]
            for k, d in enumerate((1, 2, 4)):
                a_r = tab_ref[2 * k, :, cre]
                a_i = tab_ref[2 * k + 1, :, cre]
                sr = pltpu.roll(xr, d, 0)
                si = pltpu.roll(xi, d, 0)
                xr, xi = xr + a_r * sr - a_i * si, xi + a_r * si + a_i * sr
            p_r = tab_ref[6, :, cre]
            p_i = tab_ref[7, :, cre]
            cr = car_sc[:, cre]
            ci = car_sc[:, cim]
            xr, xi = xr + p_r * cr - p_i * ci, xi + p_r * ci + p_i * cr
            bu_sc[pl.ds(r0, SUBLANE), cre] = xr
            bu_sc[pl.ds(r0, SUBLANE), cim] = xi
            last = SUBLANE - 1
            car_sc[:, cre] = jnp.broadcast_to(xr[last:last + 1, :], (SUBLANE, LANE))
            car_sc[:, cim] = jnp.broadcast_to(xi[last:last + 1, :], (SUBLANE, LANE))
        return 0

    lax.fori_loop(0, tb // SUBLANE, tile, 0)

    def h_blocks(kb):
        return (bu_sc[:, kb * S5_SPB:(kb + 1) * S5_SPB],
                bu_sc[:, N_S5 + kb * S5_SPB:N_S5 + (kb + 1) * S5_SPB])

    y = _s5_tail(h_blocks, u, wcr_ref, wci_ref, d_ref, wglu_ref)
    o_ref[...] = y.astype(o_ref.dtype)
    hr_ref[...] = car_sc[0:1, :N_S5]
    hi_ref[...] = car_sc[0:1, N_S5:]


def s5_prompt(proj3, wb, wcr, wci, tab, d_skip, wglu, *, tb=256):
    B, L = proj3.shape[:2]
    const2 = lambda b, t: (0, 0)
    return pl.pallas_call(
        functools.partial(_s5_p_body, tb=tb),
        grid=(B, L // tb),
        in_specs=[pl.BlockSpec((None, tb, D_A), lambda b, t: (b, t, 0)),
                  pl.BlockSpec((S5_NB, LANE, 2 * S5_SPB), lambda b, t: (0, 0, 0)),
                  pl.BlockSpec((S5_NB, S5_SPB, LANE), lambda b, t: (0, 0, 0)),
                  pl.BlockSpec((S5_NB, S5_SPB, LANE), lambda b, t: (0, 0, 0)),
                  pl.BlockSpec((8, SUBLANE, N_S5), lambda b, t: (0, 0, 0)),
                  pl.BlockSpec((1, D_A), const2),
                  pl.BlockSpec((D_A, D_A), const2)],
        out_specs=[pl.BlockSpec((None, tb, D_A), lambda b, t: (b, t, 0)),
                   pl.BlockSpec((None, 1, N_S5), lambda b, t: (b, 0, 0)),
                   pl.BlockSpec((None, 1, N_S5), lambda b, t: (b, 0, 0))],
        out_shape=[jax.ShapeDtypeStruct((B, L, D_A), BF16),
                   jax.ShapeDtypeStruct((B, 1, N_S5), F32),
                   jax.ShapeDtypeStruct((B, 1, N_S5), F32)],
        scratch_shapes=[pltpu.VMEM((tb, 2 * N_S5), F32), pltpu.VMEM((SUBLANE, 2 * N_S5), F32)],
        compiler_params=_cp(("arbitrary", "arbitrary")),
        name="s5_prompt",
    )(proj3, wb, wcr, wci, tab, d_skip, wglu)


def _s5_s_body(u_ref, h0r_ref, h0i_ref, wb_ref, wcr_ref, wci_ref, tab_ref, d_ref, wglu_ref,
               o_ref, hr_ref, hi_ref):
    u = u_ref[...]
    ub = u.astype(BF16)
    for kb in range(S5_NB):
        cols = slice(kb * S5_SPB, (kb + 1) * S5_SPB)
        bu = _s5_in(ub, wb_ref, kb)
        a_r = tab_ref[6, 0:1, cols]
        a_i = tab_ref[7, 0:1, cols]
        h0r = h0r_ref[:, cols]
        h0i = h0i_ref[:, cols]
        hr_ref[:, cols] = a_r * h0r - a_i * h0i + bu[:, :S5_SPB]
        hi_ref[:, cols] = a_r * h0i + a_i * h0r + bu[:, S5_SPB:]

    def h_blocks(kb):
        cols = slice(kb * S5_SPB, (kb + 1) * S5_SPB)
        return hr_ref[:, cols], hi_ref[:, cols]

    o_ref[...] = _s5_tail(h_blocks, u, wcr_ref, wci_ref, d_ref, wglu_ref).astype(o_ref.dtype)


def s5_sample(proj, h0r, h0i, wb, wcr, wci, tab, d_skip, wglu):
    R = proj.shape[0]
    z2 = lambda i: (0, 0)
    return pl.pallas_call(
        _s5_s_body,
        grid=(1,),
        in_specs=[pl.BlockSpec((R, D_A), z2),
                  pl.BlockSpec((R, N_S5), z2), pl.BlockSpec((R, N_S5), z2),
                  pl.BlockSpec((S5_NB, LANE, 2 * S5_SPB), lambda i: (0, 0, 0)),
                  pl.BlockSpec((S5_NB, S5_SPB, LANE), lambda i: (0, 0, 0)),
                  pl.BlockSpec((S5_NB, S5_SPB, LANE), lambda i: (0, 0, 0)),
                  pl.BlockSpec((8, SUBLANE, N_S5), lambda i: (0, 0, 0)),
                  pl.BlockSpec((1, D_A), z2), pl.BlockSpec((D_A, D_A), z2)],
        out_specs=[pl.BlockSpec((R, D_A), z2), pl.BlockSpec((R, N_S5), z2), pl.BlockSpec((R, N_S5), z2)],
        out_shape=[jax.ShapeDtypeStruct((R, D_A), BF16),
                   jax.ShapeDtypeStruct((R, N_S5), F32),
                   jax.ShapeDtypeStruct((R, N_S5), F32)],
        compiler_params=_cp(("arbitrary",)),
        name="s5_sample",
    )(proj, h0r, h0i, wb, wcr, wci, tab, d_skip, wglu)


def _gmlp_p_body(u_ref, v_ref, gv_ref, wcat_ref, bias_ref, o_ref, *, tb):
    hd = D_B // H_B
    lane_head = lax.broadcasted_iota(jnp.int32, (GM_CHUNK, D_B), 1) // hd
    wi = lax.broadcasted_iota(jnp.int32, (GM_CHUNK, H_B * GM_CHUNK), 0)
    wj = lax.broadcasted_iota(jnp.int32, (GM_CHUNK, H_B * GM_CHUNK), 1) % GM_CHUNK
    wcat = jnp.where(wi >= wj, wcat_ref[...], 0.0).astype(BF16)
    for c in range(tb // GM_CHUNK):
        rows = slice(c * GM_CHUNK, (c + 1) * GM_CHUNK)
        vn = _rms(jax.nn.gelu(v_ref[rows, :])) * gv_ref[...]
        vb = vn.astype(BF16)
        stack = jnp.concatenate(
            [jnp.where(lane_head == h, vb, jnp.zeros_like(vb)) for h in range(H_B)], axis=0)
        s = _dot(wcat, stack) + bias_ref[...]
        o_ref[rows, :] = (jax.nn.gelu(u_ref[rows, :]) * s).astype(o_ref.dtype)


def gmlp_prompt(proj3, g_v, w_s, b_s, *, tb=512):
    B, L = proj3.shape[:2]
    hd = D_B // H_B
    wcat = jnp.transpose(w_s, (1, 0, 2)).reshape(GM_CHUNK, H_B * GM_CHUNK)
    bias = jnp.repeat(b_s.T, hd, axis=1)
    const2 = lambda b, t: (0, 0)
    ub = D_A // D_B
    return pl.pallas_call(
        functools.partial(_gmlp_p_body, tb=tb),
        grid=(B, L // tb),
        in_specs=[pl.BlockSpec((None, tb, D_B), lambda b, t: (b, t, ub)),
                  pl.BlockSpec((None, tb, D_B), lambda b, t: (b, t, ub + 1)),
                  pl.BlockSpec((1, D_B), const2),
                  pl.BlockSpec((GM_CHUNK, H_B * GM_CHUNK), const2),
                  pl.BlockSpec((GM_CHUNK, D_B), const2)],
        out_specs=pl.BlockSpec((None, tb, D_B), lambda b, t: (b, t, 0)),
        out_shape=jax.ShapeDtypeStruct((B, L, D_B), BF16),
        compiler_params=_cp(("arbitrary", "arbitrary")),
        name="gmlp_prompt",
    )(proj3, proj3, g_v, wcat, bias)


def _gmlp_s_body(u_ref, v_ref, gv_ref, w0_ref, b0_ref, o_ref, vn_ref):
    vn = _rms(jax.nn.gelu(v_ref[...])) * gv_ref[...]
    vn_ref[...] = vn
    s = w0_ref[...] * vn + b0_ref[...]
    o_ref[...] = (jax.nn.gelu(u_ref[...]) * s).astype(o_ref.dtype)


def gmlp_sample(proj, g_v, w_s, b_s):
    R = proj.shape[0]
    hd = D_B // H_B
    w0 = jnp.repeat(w_s[:, 0, 0], hd).reshape(1, D_B)
    b0 = jnp.repeat(b_s[:, 0], hd).reshape(1, D_B)
    z2 = lambda i: (0, 0)
    ub = D_A // D_B
    return pl.pallas_call(
        _gmlp_s_body,
        grid=(1,),
        in_specs=[pl.BlockSpec((R, D_B), lambda i: (0, ub)),
                  pl.BlockSpec((R, D_B), lambda i: (0, ub + 1)),
                  pl.BlockSpec((1, D_B), z2), pl.BlockSpec((1, D_B), z2), pl.BlockSpec((1, D_B), z2)],
        out_specs=[pl.BlockSpec((R, D_B), z2), pl.BlockSpec((R, D_B), z2)],
        out_shape=[jax.ShapeDtypeStruct((R, D_B), BF16), jax.ShapeDtypeStruct((R, D_B), F32)],
        compiler_params=_cp(("arbitrary",)),
        name="gmlp_sample",
    )(proj, proj, g_v, w0, b0)


def _head_expand_matrix():
    r = lax.broadcasted_iota(jnp.int32, (LANE, D_C), 0)
    c = lax.broadcasted_iota(jnp.int32, (LANE, D_C), 1) // HD_C
    return jnp.where(r == c, 1.0, 0.0).astype(BF16)


def _expand_heads(v, e):
    hi, lo = _split_bf16(v)
    return _dot(hi, e) + _dot(lo, e)


def _ssd_p_body(z_ref, xbc_ref, dt_ref, cw_ref, cb_ref, dtb_ref, alog_ref, dsk_ref, gn_ref,
                o_ref, st_ref, h_ref, h_sc, cv_sc, *, nt):
    Q = SSD_CHUNK
    t = pl.program_id(1)

    @pl.when(t == 0)
    def _():
        h_sc[...] = jnp.zeros_like(h_sc)
        cv_sc[0:SUBLANE, :] = jnp.zeros((SUBLANE, D_XBC), F32)

    xbc = xbc_ref[...]
    cv_sc[SUBLANE:SUBLANE + Q, :] = xbc
    acc = cb_ref[...]
    for k in range(K_C):
        off = SUBLANE - (K_C - 1) + k
        acc = acc + cw_ref[k:k + 1, :] * cv_sc[off:off + Q, :]
    cv_sc[0:SUBLANE, :] = xbc[Q - SUBLANE:, :]
    st_ref[...] = xbc[Q - (K_C - 1):, :]
    xc = _silu(acc)
    xs = xc[:, :D_C]

    dt = _softplus(dt_ref[...] + dtb_ref[...])
    a = -jnp.exp(alog_ref[...])
    da = dt * a
    ii = lax.broadcasted_iota(jnp.int32, (Q, Q), 0)
    jj = lax.broadcasted_iota(jnp.int32, (Q, Q), 1)
    causal = ii >= jj
    tril = jnp.where(causal, 1.0, 0.0).astype(BF16)
    d0, d1 = _split_bf16(da)
    d2 = (da - d0.astype(F32) - d1.astype(F32)).astype(BF16)
    cs = _dot(tril, d0) + _dot(tril, d1) + _dot(tril, d2)
    cst = cs.T
    cs_end = cs[Q - 1:Q, :]
    e = _head_expand_matrix()
    dt_f = _expand_heads(dt, e)
    ws_f = _expand_heads(dt * jnp.exp(cs_end - cs), e)
    ecs_f = _expand_heads(jnp.exp(cs), e)
    xdt = xs * dt_f
    xw = xs * ws_f

    lane = lax.broadcasted_iota(jnp.int32, (Q, LANE), 1)
    hpg = H_C // G_C
    gw = hpg * HD_C
    ys = []
    for g in range(G_C):
        bg = xc[:, D_C + g * N_C:D_C + (g + 1) * N_C].astype(BF16)
        cg = xc[:, D_C + G_C * N_C + g * N_C:D_C + G_C * N_C + (g + 1) * N_C].astype(BF16)
        gmat = _dot_nt(cg, bg)
        hprev = h_sc[g * gw:(g + 1) * gw, :]
        yoff = _dot_nt(cg, hprev.astype(BF16)) * ecs_f[:, g * gw:(g + 1) * gw]
        snew = _dot_tn(xw[:, g * gw:(g + 1) * gw].astype(BF16), bg)
        for hp in range(hpg // 2):
            h0 = g * hpg + 2 * hp
            xpair = xdt[:, h0 * HD_C:(h0 + 2) * HD_C]
            x_lo = jnp.where(lane < HD_C, xpair, 0.0).astype(BF16)
            x_hi = jnp.where(lane >= HD_C, xpair, 0.0).astype(BF16)
            yd = None
            for hh, xh in ((h0, x_lo), (h0 + 1, x_hi)):
                seg = cs[:, hh:hh + 1] - cst[hh:hh + 1, :]
                sc = (gmat * jnp.exp(jnp.where(causal, seg, NEG_BIG))).astype(BF16)
                part = _dot(sc, xh)
                yd = part if yd is None else yd + part
            ys.append(yd + yoff[:, 2 * hp * HD_C:(2 * hp + 2) * HD_C])
        for hh in range(hpg):
            h = g * hpg + hh
            cd = jnp.exp(cst[h:h + 1, Q - 1:Q])
            rows = slice(h * HD_C, (h + 1) * HD_C)
            h_sc[rows, :] = h_sc[rows, :] * cd + snew[hh * HD_C:(hh + 1) * HD_C, :]

    y = jnp.concatenate(ys, axis=1) + dsk_ref[...] * xs
    y = y * _silu(z_ref[...])
    o_ref[...] = (_rms(y) * gn_ref[...]).astype(o_ref.dtype)

    @pl.when(t == nt - 1)
    def _():
        h_ref[...] = h_sc[...]


def _ssd_params(dt_bias, a_log, d_skip):
    pad = LANE - H_C
    dtb = jnp.pad(dt_bias, (0, pad)).reshape(1, LANE)
    alog = jnp.pad(a_log, (0, pad)).reshape(1, LANE)
    dsk = jnp.repeat(d_skip, HD_C).reshape(1, D_C)
    return dtb, alog, dsk


def ssd_prompt(z3, xd3, conv_w, conv_b, dt_bias, a_log, d_skip, g_norm):
    B, L = z3.shape[:2]
    Q = SSD_CHUNK
    nt = L // Q
    dtb, alog, dsk = _ssd_params(dt_bias, a_log, d_skip)
    const2 = lambda b, t: (0, 0)
    blk = lambda w: pl.BlockSpec((None, Q, w), lambda b, t: (b, t, 0))
    return pl.pallas_call(
        functools.partial(_ssd_p_body, nt=nt),
        grid=(B, nt),
        in_specs=[blk(D_C), blk(D_XBC),
                  pl.BlockSpec((None, Q, LANE), lambda b, t: (b, t, D_XBC // LANE)),
                  pl.BlockSpec((K_C, D_XBC), const2), pl.BlockSpec((1, D_XBC), const2),
                  pl.BlockSpec((1, LANE), const2), pl.BlockSpec((1, LANE), const2),
                  pl.BlockSpec((1, D_C), const2), pl.BlockSpec((1, D_C), const2)],
        out_specs=[blk(D_C),
                   pl.BlockSpec((None, K_C - 1, D_XBC), lambda b, t: (b, 0, 0)),
                   pl.BlockSpec((None, H_C * HD_C, N_C), lambda b, t: (b, 0, 0))],
        out_shape=[jax.ShapeDtypeStruct((B, L, D_C), BF16),
                   jax.ShapeDtypeStruct((B, K_C - 1, D_XBC), F32),
                   jax.ShapeDtypeStruct((B, H_C * HD_C, N_C), F32)],
        scratch_shapes=[pltpu.VMEM((H_C * HD_C, N_C), F32), pltpu.VMEM((SUBLANE + Q, D_XBC), F32)],
        compiler_params=_cp(("arbitrary", "arbitrary")),
        name="ssd_prompt",
    )(z3, xd3, xd3, conv_w, conv_b.reshape(1, D_XBC), dtb, alog, dsk, g_norm.reshape(1, D_C))


def _ssd_s_body(z_ref, xbc_ref, dt_ref, buf_ref, h0_ref, cw_ref, cb_ref, dtb_ref, alog_ref,
                dsk_ref, gn_ref, *rest, tbatch, nsteps, has_prev):
    o_ref, hn_ref, xs_sc, bc_sc, xt_sc, at_sc, y_sc = rest[1:] if has_prev else rest
    s = pl.program_id(0)
    R = xs_sc.shape[0]

    @pl.when(s == 0)
    def _():
        acc = cb_ref[...] + cw_ref[K_C - 1:K_C, :] * xbc_ref[...]
        for k in range(K_C - 1):
            acc = acc + cw_ref[k:k + 1, :] * buf_ref[k]
        xc = _silu(acc)
        xs = xc[:, :D_C]
        xs_sc[...] = xs
        bc_sc[...] = xc[:, D_C:]
        dt = _softplus(dt_ref[...] + dtb_ref[...])
        e = _head_expand_matrix()
        dt_f = _expand_heads(dt, e)
        da_f = jnp.exp(_expand_heads(dt * (-jnp.exp(alog_ref[...])), e))
        for t_sc, v in ((xt_sc, xs * dt_f), (at_sc, da_f)):
            hi, lo = _split_bf16(v.T)
            t_sc[:, :R] = hi
            t_sc[:, R:] = lo

    gw = (H_C // G_C) * HD_C
    kk = lax.broadcasted_iota(jnp.int32, (2 * R, LANE), 0) % R
    r0 = pl.multiple_of(s * tbatch, tbatch)
    bc8 = bc_sc[pl.ds(r0, tbatch), :]
    ti = lax.broadcasted_iota(jnp.int32, (tbatch, gw), 0)
    ytile = [jnp.zeros((tbatch, gw), F32) for _ in range(G_C)]
    for i in range(tbatch):
        b = s * tbatch + i
        onehot = jnp.where(kk == b, 1.0, 0.0).astype(BF16)
        xb = _dot(xt_sc[...], onehot)
        ab = _dot(at_sc[...], onehot)
        for g in range(G_C):
            rows = slice(g * gw, (g + 1) * gw)
            brow = bc8[i:i + 1, g * N_C:(g + 1) * N_C]
            hn = h0_ref[i, rows, :] * ab[rows, :] + xb[rows, :] * brow
            hn_ref[i, rows, :] = hn
            c8 = bc8[:, G_C * N_C + g * N_C:G_C * N_C + (g + 1) * N_C]
            yg = _dot_nt(c8.astype(BF16), hn.astype(BF16))
            ytile[g] = jnp.where(ti == i, yg, ytile[g])
    for g in range(G_C):
        y_sc[pl.ds(r0, tbatch), g * gw:(g + 1) * gw] = ytile[g]

    @pl.when(s == nsteps - 1)
    def _():
        y = y_sc[...] + dsk_ref[...] * xs_sc[...]
        y = y * _silu(z_ref[...])
        o_ref[...] = (_rms(y) * gn_ref[...]).astype(o_ref.dtype)


def ssd_sample(z, xd, buf, h0_all, l, hn_all, conv_w, conv_b, dt_bias, a_log, d_skip, g_norm,
               *, tbatch=8):
    R = z.shape[0]
    nsteps = R // tbatch
    dtb, alog, dsk = _ssd_params(dt_bias, a_log, d_skip)
    z2 = lambda s: (0, 0)
    hw = H_C * HD_C
    st_spec = pl.BlockSpec((None, tbatch, hw, N_C), lambda s: (l, s, 0, 0))
    in_specs = [pl.BlockSpec((R, D_C), z2), pl.BlockSpec((R, D_XBC), z2),
                pl.BlockSpec((R, LANE), lambda s: (0, D_XBC // LANE)),
                pl.BlockSpec((K_C - 1, R, D_XBC), lambda s: (0, 0, 0)),
                st_spec,
                pl.BlockSpec((K_C, D_XBC), z2), pl.BlockSpec((1, D_XBC), z2),
                pl.BlockSpec((1, LANE), z2), pl.BlockSpec((1, LANE), z2),
                pl.BlockSpec((1, D_C), z2), pl.BlockSpec((1, D_C), z2)]
    args = [z, xd, xd, buf, h0_all, conv_w, conv_b.reshape(1, D_XBC), dtb, alog, dsk,
            g_norm.reshape(1, D_C)]
    aliases = {}
    if hn_all is not None:
        aliases = {len(args): 1}
        in_specs.append(pl.BlockSpec(memory_space=pl.ANY))
        args.append(hn_all)
    return pl.pallas_call(
        functools.partial(_ssd_s_body, tbatch=tbatch, nsteps=nsteps, has_prev=hn_all is not None),
        grid=(nsteps,),
        in_specs=in_specs,
        out_specs=[pl.BlockSpec((R, D_C), z2), st_spec],
        out_shape=[jax.ShapeDtypeStruct((R, D_C), BF16),
                   jax.ShapeDtypeStruct(h0_all.shape, F32)],
        scratch_shapes=[pltpu.VMEM((R, D_C), F32), pltpu.VMEM((R, 2 * G_C * N_C), F32),
                        pltpu.VMEM((D_C, 2 * R), BF16), pltpu.VMEM((D_C, 2 * R), BF16),
                        pltpu.VMEM((R, D_C), F32)],
        input_output_aliases=aliases,
        compiler_params=_cp(("arbitrary",)),
        name="ssd_sample",
    )(*args)


def _up_p_body(x_ref, wa_ref, wb_ref, cwa_ref, cwb_ref, cba_ref, cbb_ref,
               o_ref, sa_ref, sb_ref, wa_sc, wb_sc, ca_sc, cb_sc, *, tm, tiles_per_seq):
    i = pl.program_id(1)

    @pl.when(i == 0)
    def _():
        wa_sc[...] = wa_ref[...].astype(BF16)
        wb_sc[...] = wb_ref[...].astype(BF16)

    @pl.when(i % tiles_per_seq == 0)
    def _():
        ca_sc[0:SUBLANE, :] = jnp.zeros((SUBLANE, ca_sc.shape[1]), F32)
        cb_sc[0:SUBLANE, :] = jnp.zeros((SUBLANE, cb_sc.shape[1]), F32)

    x = x_ref[...]

    def half(w_sc, c_sc, cw_ref, cb_ref, st_ref):
        up = _dot(x, w_sc[...])
        c_sc[SUBLANE:SUBLANE + tm, :] = up
        acc = cb_ref[...]
        for k in range(K_F):
            off = SUBLANE - (K_F - 1) + k
            acc = acc + cw_ref[k:k + 1, :] * c_sc[off:off + tm, :]
        c_sc[0:SUBLANE, :] = up[tm - SUBLANE:, :]
        st_ref[...] = up[tm - (K_F - 1):, :]
        return acc

    a = half(wa_sc, ca_sc, cwa_ref, cba_ref, sa_ref)
    b = half(wb_sc, cb_sc, cwb_ref, cbb_ref, sb_ref)
    o_ref[...] = (_silu(a) * b).astype(o_ref.dtype)


def up_prompt(h2, w_up, l, conv_w, conv_b, *, seq_len, tm=1024, tn=512):
    M, D = h2.shape
    B = M // seq_len
    tiles_per_seq = seq_len // tm
    nj = D_FF // tn
    cb = conv_b.reshape(1, 2 * D_FF)
    body = functools.partial(_up_p_body, tm=tm, tiles_per_seq=tiles_per_seq)
    st_spec = pl.BlockSpec((None, K_F - 1, tn), lambda j, i: (i // tiles_per_seq, 0, j))
    return pl.pallas_call(
        body,
        grid=(nj, M // tm),
        in_specs=[pl.BlockSpec((tm, D), lambda j, i: (i, 0)),
                  pl.BlockSpec((None, D, tn), lambda j, i: (l, 0, j)),
                  pl.BlockSpec((None, D, tn), lambda j, i: (l, 0, nj + j)),
                  pl.BlockSpec((K_F, tn), lambda j, i: (0, j)),
                  pl.BlockSpec((K_F, tn), lambda j, i: (0, nj + j)),
                  pl.BlockSpec((1, tn), lambda j, i: (0, j)),
                  pl.BlockSpec((1, tn), lambda j, i: (0, nj + j))],
        out_specs=[pl.BlockSpec((tm, tn), lambda j, i: (i, j)), st_spec, st_spec],
        out_shape=[jax.ShapeDtypeStruct((M, D_FF), BF16),
                   jax.ShapeDtypeStruct((B, K_F - 1, D_FF), F32),
                   jax.ShapeDtypeStruct((B, K_F - 1, D_FF), F32)],
        scratch_shapes=[pltpu.VMEM((D, tn), BF16), pltpu.VMEM((D, tn), BF16),
                        pltpu.VMEM((SUBLANE + tm, tn), F32), pltpu.VMEM((SUBLANE + tm, tn), F32)],
        compiler_params=_cp(("arbitrary", "arbitrary")),
        name="up_prompt",
    )(h2, w_up, w_up, conv_w, conv_w, cb, cb)


def _up_s_body(x_ref, wa_ref, wb_ref, bufa_ref, bufb_ref, cwa_ref, cwb_ref, cba_ref, cbb_ref,
               *rest, has_prev):
    o_ref, na_ref, nb_ref = rest[2:] if has_prev else rest
    x = x_ref[...]

    def half(w_ref, buf_ref, cw_ref, cb_ref, new_ref):
        up = _dot(x, w_ref[...].astype(BF16))
        acc = cb_ref[...] + cw_ref[K_F - 1:K_F, :] * up
        for k in range(K_F - 1):
            acc = acc + cw_ref[k:k + 1, :] * buf_ref[:, k, :]
        for k in range(1, K_F - 1):
            new_ref[:, k - 1, :] = buf_ref[:, k, :]
        new_ref[:, K_F - 2, :] = up
        return acc

    a = half(wa_ref, bufa_ref, cwa_ref, cba_ref, na_ref)
    b = half(wb_ref, bufb_ref, cwb_ref, cbb_ref, nb_ref)
    o_ref[...] = (_silu(a) * b).astype(o_ref.dtype)


def up_sample(h2, w_up, l, buf_all, new_ab, conv_w, conv_b, *, tn=512):
    R, D = h2.shape
    nj = D_FF // tn
    cb = conv_b.reshape(1, 2 * D_FF)
    st = lambda off: pl.BlockSpec((None, R, K_F - 1, tn), lambda j: (l, 0, 0, off + j))
    in_specs = [pl.BlockSpec((R, D), lambda j: (0, 0)),
                pl.BlockSpec((None, D, tn), lambda j: (l, 0, j)),
                pl.BlockSpec((None, D, tn), lambda j: (l, 0, nj + j)),
                st(0), st(nj),
                pl.BlockSpec((K_F, tn), lambda j: (0, j)),
                pl.BlockSpec((K_F, tn), lambda j: (0, nj + j)),
                pl.BlockSpec((1, tn), lambda j: (0, j)),
                pl.BlockSpec((1, tn), lambda j: (0, nj + j))]
    args = [h2, w_up, w_up, buf_all, buf_all, conv_w, conv_w, cb, cb]
    aliases = {}
    if new_ab is not None:
        aliases = {len(args): 1, len(args) + 1: 2}
        in_specs += [pl.BlockSpec(memory_space=pl.ANY)] * 2
        args += list(new_ab)
    half_shape = jax.ShapeDtypeStruct((DEPTH, R, K_F - 1, D_FF), F32)
    act, na, nb = pl.pallas_call(
        functools.partial(_up_s_body, has_prev=new_ab is not None),
        grid=(nj,),
        in_specs=in_specs,
        out_specs=[pl.BlockSpec((R, tn), lambda j: (0, j)), st(0), st(0)],
        out_shape=[jax.ShapeDtypeStruct((R, D_FF), BF16), half_shape, half_shape],
        input_output_aliases=aliases,
        compiler_params=_cp(("arbitrary",)),
        name="up_sample",
    )(*args)
    return act, (na, nb)


def _layer(x3, mods, l, p, state, *, seq_len, tm, tn_merge, tn_down, tm_norm):
    B, L, D = x3.shape
    M = B * L
    sh_m, sc_m, gt_m, sh_f, sc_f, gt_f = mods
    prompt = state is None
    w_buffers = 1 if M > tm else 2

    h = norm_mod(x3, p['g_mix'][l].reshape(1, D), sc_m, sh_m, tm=tm_norm).reshape(M, D)

    def proj(col0, n, name):
        return matmul([h], [(p['w_in_t'], l, col0, True)], [], [(n, F32)], _epi_store,
                      x_of_w=(0,), tm=tm, tn=n, name=name, vmem_mb=56, w_buffers=1)[0]

    uav = proj(OFF_UA, D_A + 2 * D_B, "proj_uav")
    zc = proj(OFF_Z, D_C, "proj_z")
    xd = proj(OFF_XBC, D_XBC + LANE, "proj_xd")

    tab, wb = s5_prep(p['lam_re'][l], p['lam_im'][l], p['log_dt'][l], p['b_re'][l], p['b_im'][l])
    wcr, wci = s5_out_weights(p['c_re'][l], p['c_im'][l])
    d_a = p['s5_d'][l].reshape(1, D_A)
    wglu = p['w_glu'][l].astype(BF16)
    g_v = p['g_v'][l].reshape(1, D_B)

    if prompt:
        o_a, s5r, s5i = s5_prompt(uav.reshape(B, L, -1), wb, wcr, wci, tab, d_a, wglu)
        o_b = gmlp_prompt(uav.reshape(B, L, -1), g_v, p['w_s'][l], p['b_s'][l])
        v_rows = None
        o_c, convc, ssm = ssd_prompt(zc.reshape(B, L, -1), xd.reshape(B, L, -1),
                                     p['ssd_conv_w'][l], p['ssd_conv_b'][l], p['dt_bias'][l],
                                     p['a_log'][l], p['ssd_d'][l], p['ssd_g'][l])
        s5r = s5r.reshape(B, G_A, P_A)
        s5i = s5i.reshape(B, G_A, P_A)
    else:
        s5_re0, s5_im0, ssm_all, ssm_new_all, convc0, ffn_all, ffn_new_ab = state
        o_a, s5r, s5i = s5_sample(uav, s5_re0.reshape(M, N_S5), s5_im0.reshape(M, N_S5),
                                  wb, wcr, wci, tab, d_a, wglu)
        o_b, v_rows = gmlp_sample(uav, g_v, p['w_s'][l], p['b_s'][l])
        o_c, ssm = ssd_sample(zc, xd, jnp.transpose(convc0, (1, 0, 2)),
                              ssm_all, l, ssm_new_all,
                              p['ssd_conv_w'][l], p['ssd_conv_b'][l], p['dt_bias'][l],
                              p['a_log'][l], p['ssd_d'][l], p['ssd_g'][l])
        convc = jnp.concatenate([convc0[:, 1:], xd[:, None, :D_XBC]], axis=1)
        s5r = s5r.reshape(M, G_A, P_A)
        s5i = s5i.reshape(M, G_A, P_A)
    o_a = o_a.reshape(M, D_A)
    o_b = o_b.reshape(M, D_B)
    o_c = o_c.reshape(M, D_C)
    if prompt:
        ssm = ssm.reshape(B, H_C, HD_C, N_C)

    w_t = p['w_in_t']
    merged = matmul(
        [h, o_a, o_b, o_c],
        [(p['w_pa'], l, 0, False), (p['w_pb'], l, 0, False), (p['w_pc'], l, 0, False),
         (w_t, l, OFF_GATES, True), (w_t, l, OFF_GATES + D, True), (w_t, l, OFF_GATES + 2 * D, True)],
        [], [(D, BF16)], _epi_merge, x_of_w=(1, 2, 3, 0, 0, 0), tm=tm, tn=tn_merge, name="merge",
        vmem_mb=56, w_buffers=w_buffers)[0]

    def residual(x_res, act_in, w3, gt, tm_r, tn, name, vmem_mb):
        return matmul([act_in], [(w3, l, 0, False)],
                      [(x_res, _tile_spec(tm_r, tn)),
                       (gt, _gate_spec(gt, tm_r, tn, max(seq_len // tm_r, 1)))],
                      [(D, F32)], _epi_residual, x_of_w=(0,), tm=tm_r, tn=tn, name=name,
                      vmem_mb=vmem_mb, w_buffers=w_buffers)[0]

    x2, h2 = out_norm(merged, p['w_out'], l, x3.reshape(M, D), gt_m, p['g_ffn'][l].reshape(1, D),
                      sc_f, sh_f, tm=min(tm, 256), seq_len=seq_len)
    if prompt:
        act, st_a, st_b = up_prompt(h2, p['w_up'], l, p['ffn_conv_w'][l], p['ffn_conv_b'][l],
                                    seq_len=seq_len, tm=tm)
        convf = jnp.concatenate([st_a, st_b], axis=-1)
    else:
        act, convf = up_sample(h2, p['w_up'], l, ffn_all, ffn_new_ab,
                               p['ffn_conv_w'][l], p['ffn_conv_b'][l])

    x2 = residual(x2, act, p['w_down'], gt_f, min(tm, 256), tn_down, "down_proj", 56)
    return x2.reshape(B, L, D), s5r, s5i, ssm, convc, convf, v_rows


def kernel(x_prompt, x_sample, c_prompt, c_sample, state_s5_re, state_s5_im, state_ssm, state_ssd_conv, state_ffn_conv, w_mod, b_mod, g_mix, w_in, s5_lam_re, s5_lam_im, s5_log_dt, s5_b_re, s5_b_im, s5_c_re, s5_c_im, s5_d, s5_w_glu, gm_g_v, gm_w_s, gm_b_s, ssd_conv_w, ssd_conv_b, ssd_dt_bias, ssd_a_log, ssd_d, ssd_g_norm, w_pa, w_pb, w_pc, w_out, g_ffn, ffn_w_up, ffn_conv_w, ffn_conv_b, ffn_w_down, g_final):
    p = {
        'g_mix': g_mix, 'w_in_t': jnp.swapaxes(w_in, 1, 2),
        'lam_re': s5_lam_re, 'lam_im': s5_lam_im, 'log_dt': s5_log_dt,
        'b_re': s5_b_re, 'b_im': s5_b_im, 'c_re': s5_c_re, 'c_im': s5_c_im,
        's5_d': s5_d, 'w_glu': s5_w_glu,
        'g_v': gm_g_v, 'w_s': gm_w_s, 'b_s': gm_b_s,
        'ssd_conv_w': ssd_conv_w, 'ssd_conv_b': ssd_conv_b, 'dt_bias': ssd_dt_bias,
        'a_log': ssd_a_log, 'ssd_d': ssd_d, 'ssd_g': ssd_g_norm,
        'w_pa': w_pa, 'w_pb': w_pb, 'w_pc': w_pc, 'w_out': w_out,
        'g_ffn': g_ffn, 'w_up': ffn_w_up, 'ffn_conv_w': ffn_conv_w,
        'ffn_conv_b': ffn_conv_b, 'w_down': ffn_w_down,
    }
    bp, seq, D = x_prompt.shape
    bs = x_sample.shape[0]

    n_c = bs + bp
    pad = (-n_c) % SUBLANE
    c_all = jnp.concatenate([c_sample, c_prompt, jnp.zeros((pad, D), F32)], axis=0)
    mod = mod_all(c_all, w_mod, b_mod)

    xp = x_prompt
    xs = x_sample.reshape(1, bs, D)
    outs_p = [[] for _ in range(5)]
    outs_s = [[] for _ in range(6)]
    ssm_all = state_ssm.reshape(DEPTH, bs, H_C * HD_C, N_C)
    ssm_new_all = None
    ffn_new_ab = None
    for l in range(DEPTH):
        mods_s = [m[None] for m in jnp.split(mod[l, :bs], 6, axis=-1)]
        mods_p = [m[:, None, :] for m in jnp.split(mod[l, bs:bs + bp], 6, axis=-1)]
        xp, *st_p = _layer(xp, mods_p, l, p, None, seq_len=seq, tm=1024, tn_merge=512, tn_down=1024,
                           tm_norm=512)
        for acc, v in zip(outs_p, st_p[:5]):
            acc.append(v)
        state = (state_s5_re[l], state_s5_im[l], ssm_all, ssm_new_all, state_ssd_conv[l],
                 state_ffn_conv, ffn_new_ab)
        xs, *st_s = _layer(xs, mods_s, l, p, state, seq_len=1, tm=bs, tn_merge=256, tn_down=256,
                           tm_norm=bs)
        ssm_new_all = st_s[2]
        ffn_new_ab = st_s[4]
        st_s[5] = st_s[5].reshape(bs, 1, D_B)
        for acc, v in zip(outs_s, st_s):
            acc.append(v)

    g_fin = g_final.reshape(1, D)
    y_prompt = final_norm(xp, g_fin, tm=512)
    y_sample = final_norm(xs, g_fin, tm=bs).reshape(bs, 1, D)
    whole = {2: ssm_new_all.reshape(DEPTH, bs, H_C, HD_C, N_C),
             4: jnp.concatenate(ffn_new_ab, axis=-1)}
    outs_s = [whole[k] if k in whole else jnp.stack(v) for k, v in enumerate(outs_s)]
    return (y_prompt, y_sample, *[jnp.stack(v) for v in outs_p], *outs_s)
```

```python
import functools

import jax
import jax.numpy as jnp
from jax import lax
from jax.experimental import pallas as pl
from jax.experimental.pallas import tpu as pltpu

F32 = jnp.float32
BF16 = jnp.bfloat16

D_MODEL = 2048
DEPTH = 2
D_A = 512
S5_GROUP = 16
G_A = D_A // S5_GROUP
P_A = 64
N_S5 = G_A * P_A
S5_GPB = 128 // S5_GROUP
S5_NB = D_A // 128
S5_SPB = S5_GPB * P_A
S5_T = 8
D_B = 512
H_B = 8
GM_CHUNK = 128
D_C = 1024
HD_C = 64
H_C = D_C // HD_C
N_C = 128
G_C = 2
K_C = 4
SSD_CHUNK = 128
D_XBC = D_C + 2 * G_C * N_C
D_FF = 5632
K_F = 3
EPS = 1e-6

OFF_UA = 0
OFF_Z = D_A + 2 * D_B
OFF_XBC = OFF_Z + D_C
OFF_DT = OFF_XBC + D_XBC
OFF_GATES = OFF_DT + H_C

LANE = 128
SUBLANE = 8
NEG_BIG = -1e30


def _cp(sem, vmem_mb=48):
    return pltpu.CompilerParams(dimension_semantics=sem, vmem_limit_bytes=vmem_mb * 1024 * 1024)


def _sigmoid(x):
    return 0.5 * jnp.tanh(0.5 * x) + 0.5


def _silu(x):
    return x * _sigmoid(x)


def _softplus(x):
    return jnp.maximum(x, 0.0) + jnp.log1p(jnp.exp(-jnp.abs(x)))


def _rms(x):
    return x * lax.rsqrt(jnp.mean(x * x, axis=-1, keepdims=True) + EPS)


def _split_bf16(x):
    hi = x.astype(BF16)
    lo = (x - hi.astype(F32)).astype(BF16)
    return hi, lo


def _dot(a, b):
    return jnp.dot(a, b, preferred_element_type=F32)


def _dot_nt(a, b):
    return lax.dot_general(a, b, (((1,), (1,)), ((), ())), preferred_element_type=F32)


def _dot_tn(a, b):
    return lax.dot_general(a, b, (((0,), (0,)), ((), ())), preferred_element_type=F32)


def _mod_body(c_ref, w_ref, b_ref, o_ref):
    a = _silu(c_ref[...]).astype(BF16)
    o_ref[...] = _dot(a, w_ref[...].astype(BF16)) + b_ref[...]


def mod_all(c_all, w_mod, b_mod, *, tn=1024):
    R, D = c_all.shape
    N = w_mod.shape[-1]
    return pl.pallas_call(
        _mod_body,
        grid=(DEPTH, N // tn),
        in_specs=[pl.BlockSpec((R, D), lambda l, j: (0, 0)),
                  pl.BlockSpec((None, D, tn), lambda l, j: (l, 0, j)),
                  pl.BlockSpec((None, 1, tn), lambda l, j: (l, 0, j))],
        out_specs=pl.BlockSpec((None, R, tn), lambda l, j: (l, 0, j)),
        out_shape=jax.ShapeDtypeStruct((DEPTH, R, N), F32),
        compiler_params=_cp(("arbitrary", "arbitrary")),
        name="mod_all",
    )(c_all, w_mod, b_mod.reshape(DEPTH, 1, N))


def _norm_mod_body(x_ref, g_ref, sc_ref, sh_ref, o_ref):
    y = _rms(x_ref[...]) * g_ref[...]
    o_ref[...] = (y * (1.0 + sc_ref[...]) + sh_ref[...]).astype(o_ref.dtype)


def norm_mod(x3, g, sc3, sh3, *, tm):
    B, L, D = x3.shape
    per_row = sc3.shape[1] != 1
    ts = tm if per_row else 1
    sidx = (lambda b, i: (b, i, 0)) if per_row else (lambda b, i: (b, 0, 0))
    return pl.pallas_call(
        _norm_mod_body,
        grid=(B, L // tm),
        in_specs=[pl.BlockSpec((None, tm, D), lambda b, i: (b, i, 0)),
                  pl.BlockSpec((1, D), lambda b, i: (0, 0)),
                  pl.BlockSpec((None, ts, D), sidx),
                  pl.BlockSpec((None, ts, D), sidx)],
        out_specs=pl.BlockSpec((None, tm, D), lambda b, i: (b, i, 0)),
        out_shape=jax.ShapeDtypeStruct((B, L, D), BF16),
        compiler_params=_cp(("arbitrary", "arbitrary")),
        name="norm_mod",
    )(x3, g, sc3, sh3)


def _rms_body(x_ref, g_ref, o_ref):
    o_ref[...] = _rms(x_ref[...]) * g_ref[...]


def final_norm(x3, g, *, tm):
    B, L, D = x3.shape
    return pl.pallas_call(
        _rms_body,
        grid=(B, L // tm),
        in_specs=[pl.BlockSpec((None, tm, D), lambda b, i: (b, i, 0)),
                  pl.BlockSpec((1, D), lambda b, i: (0, 0))],
        out_specs=pl.BlockSpec((None, tm, D), lambda b, i: (b, i, 0)),
        out_shape=jax.ShapeDtypeStruct((B, L, D), F32),
        compiler_params=_cp(("arbitrary", "arbitrary")),
        name="final_norm",
    )(x3, g)


def _mm_body(*refs, x_of_w, w_is_t, n_x, n_e, n_o, epilogue):
    n_w = len(x_of_w)
    xs = refs[:n_x]
    ws = refs[n_x:n_x + n_w]
    es = refs[n_x + n_w:n_x + n_w + n_e]
    outs = refs[n_x + n_w + n_e:n_x + n_w + n_e + n_o]
    wsc = refs[n_x + n_w + n_e + n_o:]

    @pl.when(pl.program_id(1) == 0)
    def _():
        for w, s, is_t in zip(ws, wsc, w_is_t):
            s[...] = (w[...].T if is_t else w[...]).astype(BF16)

    accs = [_dot(xs[xi][...], s[...]) for xi, s in zip(x_of_w, wsc)]
    epilogue(accs, es, outs)


def matmul(xs, ws, extras, outs, epilogue, *, x_of_w, tm, tn, name, vmem_mb=48, w_buffers=2):
    M = xs[0].shape[0]
    N = outs[0][0]
    in_specs = [pl.BlockSpec((tm, x.shape[1]), lambda j, i: (i, 0)) for x in xs]
    args = list(xs)
    scratch = []
    for w3, l, col0, is_t in ws:
        if is_t:
            K = w3.shape[2]
            assert col0 % SUBLANE == 0
            in_specs.append(pl.BlockSpec((None, pl.Element(tn), pl.Element(K)), functools.partial(
                lambda j, i, l, c: (l, (c + j * (tn // SUBLANE)) * SUBLANE, 0),
                l=l, c=col0 // SUBLANE), pipeline_mode=pl.Buffered(w_buffers)))
        else:
            K = w3.shape[1]
            assert col0 % tn == 0
            in_specs.append(pl.BlockSpec((None, K, tn), functools.partial(
                lambda j, i, l, c: (l, 0, c + j), l=l, c=col0 // tn),
                pipeline_mode=pl.Buffered(w_buffers)))
        args.append(w3)
        scratch.append(pltpu.VMEM((K, tn), BF16))
    for arr, spec in extras:
        in_specs.append(spec)
        args.append(arr)
    body = functools.partial(_mm_body, x_of_w=tuple(x_of_w), w_is_t=tuple(w[3] for w in ws),
                             n_x=len(xs), n_e=len(extras), n_o=len(outs), epilogue=epilogue)
    res = pl.pallas_call(
        body,
        grid=(N // tn, M // tm),
        in_specs=in_specs,
        out_specs=[pl.BlockSpec((tm, tn), lambda j, i: (i, j)) for _ in outs],
        out_shape=[jax.ShapeDtypeStruct((M, n), dt) for n, dt in outs],
        scratch_shapes=scratch,
        compiler_params=_cp(("arbitrary", "arbitrary"), vmem_mb),
        name=name,
    )(*args)
    return res


def _epi_store(accs, es, outs):
    outs[0][...] = accs[0].astype(outs[0].dtype)


def _epi_merge(accs, es, outs):
    pa, pb, pc, ga, gb, gc = accs
    m = _sigmoid(ga) * pa + _sigmoid(gb) * pb + _sigmoid(gc) * pc
    outs[0][...] = m.astype(outs[0].dtype)


def _epi_residual(accs, es, outs):
    res_ref, gt_ref = es
    outs[0][...] = res_ref[...] + gt_ref[...] * accs[0]


def _out_norm_body(a_ref, w_ref, res_ref, gt_ref, g_ref, sc_ref, sh_ref, x_ref, h_ref, w_sc):
    @pl.when(pl.program_id(0) == 0)
    def _():
        w_sc[...] = w_ref[...].astype(BF16)

    x = res_ref[...] + gt_ref[...] * _dot(a_ref[...], w_sc[...])
    x_ref[...] = x
    y = _rms(x) * g_ref[...]
    h_ref[...] = (y * (1.0 + sc_ref[...]) + sh_ref[...]).astype(h_ref.dtype)


def out_norm(act, w3, l, res, gt, g, sc, sh, *, tm, seq_len):
    M, K = act.shape
    D = res.shape[1]
    tiles_per_seq = max(seq_len // tm, 1)

    def mod_spec(m3):
        if m3.shape[1] == 1:
            return pl.BlockSpec((None, 1, D), lambda i: (i // tiles_per_seq, 0, 0))
        return pl.BlockSpec((None, tm, D), lambda i: (0, i, 0))

    row = lambda w: pl.BlockSpec((tm, w), lambda i: (i, 0))
    return pl.pallas_call(
        _out_norm_body,
        grid=(M // tm,),
        in_specs=[row(K),
                  pl.BlockSpec((None, K, D), lambda i: (l, 0, 0), pipeline_mode=pl.Buffered(1)),
                  row(D), mod_spec(gt),
                  pl.BlockSpec((1, D), lambda i: (0, 0)), mod_spec(sc), mod_spec(sh)],
        out_specs=[row(D), row(D)],
        out_shape=[jax.ShapeDtypeStruct((M, D), F32), jax.ShapeDtypeStruct((M, D), BF16)],
        scratch_shapes=[pltpu.VMEM((K, D), BF16)],
        compiler_params=_cp(("arbitrary",), 56),
        name="out_norm",
    )(act, w3, res, gt, g, sc, sh)


def _tile_spec(tm, tn):
    return pl.BlockSpec((tm, tn), lambda j, i: (i, j))


def _gate_spec(gt3, tm, tn, tiles_per_seq):
    if gt3.shape[1] == 1:
        return pl.BlockSpec((None, 1, tn), lambda j, i: (i // tiles_per_seq, 0, j))
    return pl.BlockSpec((None, tm, tn), lambda j, i: (0, i, j))


def _s5_prep_body(lrf_ref, lif_ref, ldf_ref, lrr_ref, lir_ref, ldr_ref, bre_ref, bim_ref,
                  tab_ref, bbr_ref, bbi_ref):
    dtf = jnp.exp(ldf_ref[...])
    _power_tables(tab_ref, lrf_ref[...] * dtf, lif_ref[...] * dtf, 1)
    dtr = jnp.exp(ldr_ref[...])
    lr = lrr_ref[...]
    li = lir_ref[...]
    m1 = jnp.exp(lr * dtr)
    ar = m1 * jnp.cos(li * dtr)
    ai = m1 * jnp.sin(li * dtr)
    den = lr * lr + li * li
    nr = ar - 1.0
    kr = (nr * lr + ai * li) / den
    ki = (ai * lr - nr * li) / den
    bre = bre_ref[...]
    bim = bim_ref[...]
    bbr_ref[...] = kr * bre - ki * bim
    bbi_ref[...] = kr * bim + ki * bre


def s5_prep(lam_re, lam_im, log_dt, b_re, b_im):
    lrf = lam_re.reshape(1, N_S5)
    lif = lam_im.reshape(1, N_S5)
    ldf = jnp.repeat(log_dt, P_A).reshape(1, N_S5)
    lrr = jnp.repeat(lam_re, S5_GROUP, axis=0)
    lir = jnp.repeat(lam_im, S5_GROUP, axis=0)
    ldr = jnp.repeat(log_dt, S5_GROUP).reshape(D_A, 1)
    bre = jnp.transpose(b_re, (0, 2, 1)).reshape(D_A, P_A)
    bim = jnp.transpose(b_im, (0, 2, 1)).reshape(D_A, P_A)
    tab, bbr, bbi = pl.pallas_call(
        _s5_prep_body,
        out_shape=[jax.ShapeDtypeStruct((8, SUBLANE, N_S5), F32),
                   jax.ShapeDtypeStruct((D_A, P_A), F32),
                   jax.ShapeDtypeStruct((D_A, P_A), F32)],
        name="s5_prep",
    )(lrf, lif, ldf, lrr, lir, ldr, bre, bim)
    eye = jnp.eye(S5_GPB, dtype=F32)

    def blockdiag(m):
        m = m.reshape(S5_NB, S5_GPB, S5_GROUP, P_A)
        return jnp.einsum('kghp,gj->kghjp', m, eye).reshape(S5_NB, LANE, S5_SPB)

    wb = jnp.concatenate([blockdiag(bbr), blockdiag(bbi)], axis=2).astype(BF16)
    return tab, wb, bbr, bbi


def _power_tables(tab_ref, lr_dt, li_dt, stride):
    n_lanes = lr_dt.shape[1]
    row = lax.broadcasted_iota(jnp.int32, (SUBLANE, n_lanes), 0)
    n = ((row + 1) * stride).astype(F32)
    mag = jnp.exp(n * lr_dt)
    ang = n * li_dt
    pr = mag * jnp.cos(ang)
    pi = mag * jnp.sin(ang)
    for k, d in enumerate((1, 2, 4)):
        keep = row >= d
        tab_ref[2 * k] = jnp.where(keep, pr[d - 1:d, :], 0.0)
        tab_ref[2 * k + 1] = jnp.where(keep, pi[d - 1:d, :], 0.0)
    tab_ref[6] = pr
    tab_ref[7] = pi


def _s5_chunk_prep_body(lrf_ref, lif_ref, ldf_ref, lrr_ref, lir_ref, ldr_ref, bbr_ref, bbi_ref,
                        cr_ref, ci_ref, tab_ref, wsr_ref, wsi_ref, car_ref, cai_ref, kt_ref):
    dtf = jnp.exp(ldf_ref[...])
    _power_tables(tab_ref, lrf_ref[...] * dtf, lif_ref[...] * dtf, S5_T)
    dtr = jnp.exp(ldr_ref[...])
    lr = lrr_ref[...] * dtr
    li = lir_ref[...] * dtr
    bbr = bbr_ref[...]
    bbi = bbi_ref[...]
    cr = cr_ref[...]
    ci = ci_ref[...]
    gi = lax.broadcasted_iota(jnp.int32, (D_A, D_A), 0) // S5_GROUP
    gj = lax.broadcasted_iota(jnp.int32, (D_A, D_A), 1) // S5_GROUP
    same_group = gi == gj
    for n in range(S5_T + 1):
        mag = jnp.exp(float(n) * lr)
        pr = mag * jnp.cos(float(n) * li)
        pi = mag * jnp.sin(float(n) * li)
        car = cr * pr - ci * pi
        cai = cr * pi + ci * pr
        car_ref[n] = car
        cai_ref[n] = cai
        if n < S5_T:
            t = S5_T - 1 - n
            wsr_ref[t] = pr * bbr - pi * bbi
            wsi_ref[t] = pr * bbi + pi * bbr
            k = (_dot_nt(car.astype(BF16), bbr.astype(BF16))
                 - _dot_nt(cai.astype(BF16), bbi.astype(BF16)))
            kt_ref[n] = jnp.where(same_group, k, 0.0)


def s5_chunk_prep(lam_re, lam_im, log_dt, bbr, bbi, c_re, c_im):
    T = S5_T
    lrf = lam_re.reshape(1, N_S5)
    lif = lam_im.reshape(1, N_S5)
    ldf = jnp.repeat(log_dt, P_A).reshape(1, N_S5)
    lrr = jnp.repeat(lam_re, S5_GROUP, axis=0)
    lir = jnp.repeat(lam_im, S5_GROUP, axis=0)
    ldr = jnp.repeat(log_dt, S5_GROUP).reshape(D_A, 1)
    rows = lambda n: jax.ShapeDtypeStruct((n, D_A, P_A), F32)
    tab, wsr, wsi, car, cai, kt = pl.pallas_call(
        _s5_chunk_prep_body,
        out_shape=[jax.ShapeDtypeStruct((8, SUBLANE, N_S5), F32), rows(T), rows(T),
                   rows(T + 1), rows(T + 1), jax.ShapeDtypeStruct((T, D_A, D_A), F32)],
        name="s5_chunk_prep",
    )(lrf, lif, ldf, lrr, lir, ldr, bbr, bbi, c_re.reshape(D_A, P_A), c_im.reshape(D_A, P_A))
    eye = jnp.eye(S5_GPB, dtype=F32)

    def in_op(w):
        w = w.reshape(T, S5_NB, S5_GPB, S5_GROUP, P_A)
        return jnp.einsum('tkghp,gj->ktghjp', w, eye).reshape(S5_NB, T * LANE, S5_SPB)

    def out_op(c):
        c = c.reshape(T, S5_NB, S5_GPB, S5_GROUP, P_A)
        return jnp.einsum('tkghp,gj->kgptjh', c, eye).reshape(S5_NB, S5_SPB, T * LANE).astype(BF16)

    wst = jnp.concatenate([in_op(wsr), in_op(wsi)], axis=2).astype(BF16)
    kb = jnp.einsum('dkykx->dkyx', kt.reshape(T, S5_NB, LANE, S5_NB, LANE))
    ti = jnp.arange(T)
    shift = (ti[None, None, :] - ti[None, :, None] == ti[:, None, None]).astype(F32)
    ktoe = jnp.einsum('dio,dkyx->kixoy', shift, kb).reshape(S5_NB, T * LANE, T * LANE).astype(BF16)
    return tab, wst, ktoe, out_op(car[1:]), out_op(cai[1:])


def _s5_c_body(u_ref, wst_ref, kt_ref, wor_ref, woi_ref, tab_ref, d_ref, wglu_ref,
               o_ref, hr_ref, hi_ref, s_sc, car_sc, *, tb):
    T = S5_T

    @pl.when(pl.program_id(1) == 0)
    def _():
        car_sc[...] = jnp.zeros_like(car_sc)

    s_sc[0:SUBLANE, :] = car_sc[...]
    x = u_ref[...]
    xb = x.astype(BF16)
    bw = T * LANE
    for kb in range(S5_NB):
        s = _dot(xb[:, kb * bw:(kb + 1) * bw], wst_ref[kb])
        s_sc[SUBLANE:SUBLANE + tb, kb * S5_SPB:(kb + 1) * S5_SPB] = s[:, :S5_SPB]
        s_sc[SUBLANE:SUBLANE + tb, N_S5 + kb * S5_SPB:N_S5 + (kb + 1) * S5_SPB] = s[:, S5_SPB:]

    def tile(rt, _):
        r0 = pl.multiple_of(SUBLANE + rt * SUBLANE, SUBLANE)
        for lg in range(N_S5 // LANE):
            cre = slice(lg * LANE, (lg + 1) * LANE)
            cim = slice(N_S5 + lg * LANE, N_S5 + (lg + 1) * LANE)
            xr = s_sc[pl.ds(r0, SUBLANE), cre]
            xi = s_sc[pl.ds(r0, SUBLANE), cim]
            for k, d in enumerate((1, 2, 4)):
                a_r = tab_ref[2 * k, :, cre]
                a_i = tab_ref[2 * k + 1, :, cre]
                sr = pltpu.roll(xr, d, 0)
                si = pltpu.roll(xi, d, 0)
                xr, xi = xr + a_r * sr - a_i * si, xi + a_r * si + a_i * sr
            p_r = tab_ref[6, :, cre]
            p_i = tab_ref[7, :, cre]
            cr = car_sc[:, cre]
            ci = car_sc[:, cim]
            xr, xi = xr + p_r * cr - p_i * ci, xi + p_r * ci + p_i * cr
            s_sc[pl.ds(r0, SUBLANE), cre] = xr
            s_sc[pl.ds(r0, SUBLANE), cim] = xi
            last = SUBLANE - 1
            car_sc[:, cre] = jnp.broadcast_to(xr[last:last + 1, :], (SUBLANE, LANE))
            car_sc[:, cim] = jnp.broadcast_to(xi[last:last + 1, :], (SUBLANE, LANE))
        return 0

    lax.fori_loop(0, tb // SUBLANE, tile, 0)

    ys = []
    for kb in range(S5_NB):
        h_r = s_sc[SUBLANE - 1:SUBLANE - 1 + tb, kb * S5_SPB:(kb + 1) * S5_SPB].astype(BF16)
        h_i = s_sc[SUBLANE - 1:SUBLANE - 1 + tb,
                   N_S5 + kb * S5_SPB:N_S5 + (kb + 1) * S5_SPB].astype(BF16)
        ys.append(_dot(xb[:, kb * bw:(kb + 1) * bw], kt_ref[kb])
                  + _dot(h_r, wor_ref[kb]) - _dot(h_i, woi_ref[kb]))
    for t in range(T):
        y = jnp.concatenate([ys[kb][:, t * LANE:(t + 1) * LANE] for kb in range(S5_NB)], axis=1)
        u = jnp.concatenate([x[:, kb * bw + t * LANE:kb * bw + (t + 1) * LANE]
                             for kb in range(S5_NB)], axis=1)
        y = jax.nn.gelu(y + d_ref[...] * u)
        y = y * _sigmoid(_dot(y.astype(BF16), wglu_ref[...]))
        o_ref[:, t * D_A:(t + 1) * D_A] = y.astype(o_ref.dtype)
    hr_ref[...] = car_sc[0:1, :N_S5]
    hi_ref[...] = car_sc[0:1, N_S5:]


def s5_prompt(u3, ops, d_skip, wglu, *, tb=256):
    tab, wst, ktoe, wor, woi = ops
    B, L, _ = u3.shape
    T = S5_T
    nc = L // T
    tb = min(tb, nc)
    bw = T * LANE
    u8 = jnp.transpose(u3.reshape(B, nc, T, S5_NB, LANE), (0, 1, 3, 2, 4)).reshape(B, nc, S5_NB * bw)
    const3 = lambda b, t: (0, 0, 0)
    once = dict(pipeline_mode=pl.Buffered(1))
    o8, hr, hi = pl.pallas_call(
        functools.partial(_s5_c_body, tb=tb),
        grid=(B, nc // tb),
        in_specs=[pl.BlockSpec((None, tb, S5_NB * bw), lambda b, t: (b, t, 0)),
                  pl.BlockSpec((S5_NB, bw, 2 * S5_SPB), const3, **once),
                  pl.BlockSpec((S5_NB, bw, bw), const3, **once),
                  pl.BlockSpec((S5_NB, S5_SPB, bw), const3, **once),
                  pl.BlockSpec((S5_NB, S5_SPB, bw), const3, **once),
                  pl.BlockSpec((8, SUBLANE, N_S5), const3, **once),
                  pl.BlockSpec((1, D_A), lambda b, t: (0, 0)),
                  pl.BlockSpec((D_A, D_A), lambda b, t: (0, 0))],
        out_specs=[pl.BlockSpec((None, tb, T * D_A), lambda b, t: (b, t, 0)),
                   pl.BlockSpec((None, 1, N_S5), lambda b, t: (b, 0, 0)),
                   pl.BlockSpec((None, 1, N_S5), lambda b, t: (b, 0, 0))],
        out_shape=[jax.ShapeDtypeStruct((B, nc, T * D_A), BF16),
                   jax.ShapeDtypeStruct((B, 1, N_S5), F32),
                   jax.ShapeDtypeStruct((B, 1, N_S5), F32)],
        scratch_shapes=[pltpu.VMEM((SUBLANE + tb, 2 * N_S5), F32),
                        pltpu.VMEM((SUBLANE, 2 * N_S5), F32)],
        compiler_params=_cp(("arbitrary", "arbitrary"), 56),
        name="s5_prompt",
    )(u8, wst, ktoe, wor, woi, tab, d_skip, wglu)
    return o8.reshape(B, L, D_A), hr, hi


def s5_out_weights(c_re, c_im):
    eye = jnp.eye(S5_GPB, dtype=F32)

    def blockdiag(c):
        c = c.reshape(S5_NB, S5_GPB, S5_GROUP, P_A)
        return jnp.einsum('kghp,gj->kgpjh', c, eye).reshape(S5_NB, S5_SPB, LANE).astype(BF16)

    return blockdiag(c_re), blockdiag(c_im)


def _s5_in(ub, wb_ref, kb):
    return _dot(ub[:, kb * LANE:(kb + 1) * LANE], wb_ref[kb])


def _s5_tail(h_blocks, u, wcr_ref, wci_ref, d_ref, wglu_ref):
    ys = []
    for kb in range(S5_NB):
        hr, hi = h_blocks(kb)
        ys.append(_dot(hr.astype(BF16), wcr_ref[kb]) - _dot(hi.astype(BF16), wci_ref[kb]))
    y = jnp.concatenate(ys, axis=1) + d_ref[...] * u
    y = jax.nn.gelu(y)
    return y * _sigmoid(_dot(y.astype(BF16), wglu_ref[...]))


def _s5_s_body(u_ref, h0r_ref, h0i_ref, wb_ref, wcr_ref, wci_ref, tab_ref, d_ref, wglu_ref,
               o_ref, hr_ref, hi_ref):
    u = u_ref[...]
    ub = u.astype(BF16)
    for kb in range(S5_NB):
        cols = slice(kb * S5_SPB, (kb + 1) * S5_SPB)
        bu = _s5_in(ub, wb_ref, kb)
        a_r = tab_ref[6, 0:1, cols]
        a_i = tab_ref[7, 0:1, cols]
        h0r = h0r_ref[:, cols]
        h0i = h0i_ref[:, cols]
        hr_ref[:, cols] = a_r * h0r - a_i * h0i + bu[:, :S5_SPB]
        hi_ref[:, cols] = a_r * h0i + a_i * h0r + bu[:, S5_SPB:]

    def h_blocks(kb):
        cols = slice(kb * S5_SPB, (kb + 1) * S5_SPB)
        return hr_ref[:, cols], hi_ref[:, cols]

    o_ref[...] = _s5_tail(h_blocks, u, wcr_ref, wci_ref, d_ref, wglu_ref).astype(o_ref.dtype)


def s5_sample(proj, h0r, h0i, wb, wcr, wci, tab, d_skip, wglu):
    R = proj.shape[0]
    z2 = lambda i: (0, 0)
    return pl.pallas_call(
        _s5_s_body,
        grid=(1,),
        in_specs=[pl.BlockSpec((R, D_A), z2),
                  pl.BlockSpec((R, N_S5), z2), pl.BlockSpec((R, N_S5), z2),
                  pl.BlockSpec((S5_NB, LANE, 2 * S5_SPB), lambda i: (0, 0, 0)),
                  pl.BlockSpec((S5_NB, S5_SPB, LANE), lambda i: (0, 0, 0)),
                  pl.BlockSpec((S5_NB, S5_SPB, LANE), lambda i: (0, 0, 0)),
                  pl.BlockSpec((8, SUBLANE, N_S5), lambda i: (0, 0, 0)),
                  pl.BlockSpec((1, D_A), z2), pl.BlockSpec((D_A, D_A), z2)],
        out_specs=[pl.BlockSpec((R, D_A), z2), pl.BlockSpec((R, N_S5), z2), pl.BlockSpec((R, N_S5), z2)],
        out_shape=[jax.ShapeDtypeStruct((R, D_A), BF16),
                   jax.ShapeDtypeStruct((R, N_S5), F32),
                   jax.ShapeDtypeStruct((R, N_S5), F32)],
        compiler_params=_cp(("arbitrary",)),
        name="s5_sample",
    )(proj, h0r, h0i, wb, wcr, wci, tab, d_skip, wglu)


def _gmlp_p_body(u_ref, v_ref, gv_ref, wcat_ref, bias_ref, o_ref, *, tb):
    hd = D_B // H_B
    lane_head = lax.broadcasted_iota(jnp.int32, (GM_CHUNK, D_B), 1) // hd
    wi = lax.broadcasted_iota(jnp.int32, (GM_CHUNK, H_B * GM_CHUNK), 0)
    wj = lax.broadcasted_iota(jnp.int32, (GM_CHUNK, H_B * GM_CHUNK), 1) % GM_CHUNK
    wcat = jnp.where(wi >= wj, wcat_ref[...], 0.0).astype(BF16)
    for c in range(tb // GM_CHUNK):
        rows = slice(c * GM_CHUNK, (c + 1) * GM_CHUNK)
        vn = _rms(jax.nn.gelu(v_ref[rows, :])) * gv_ref[...]
        vb = vn.astype(BF16)
        stack = jnp.concatenate(
            [jnp.where(lane_head == h, vb, jnp.zeros_like(vb)) for h in range(H_B)], axis=0)
        s = _dot(wcat, stack) + bias_ref[...]
        o_ref[rows, :] = (jax.nn.gelu(u_ref[rows, :]) * s).astype(o_ref.dtype)


def gmlp_prompt(proj3, g_v, w_s, b_s, *, tb=512):
    B, L = proj3.shape[:2]
    hd = D_B // H_B
    wcat = jnp.transpose(w_s, (1, 0, 2)).reshape(GM_CHUNK, H_B * GM_CHUNK)
    bias = jnp.repeat(b_s.T, hd, axis=1)
    const2 = lambda b, t: (0, 0)
    ub = D_A // D_B
    return pl.pallas_call(
        functools.partial(_gmlp_p_body, tb=tb),
        grid=(B, L // tb),
        in_specs=[pl.BlockSpec((None, tb, D_B), lambda b, t: (b, t, ub)),
                  pl.BlockSpec((None, tb, D_B), lambda b, t: (b, t, ub + 1)),
                  pl.BlockSpec((1, D_B), const2),
                  pl.BlockSpec((GM_CHUNK, H_B * GM_CHUNK), const2),
                  pl.BlockSpec((GM_CHUNK, D_B), const2)],
        out_specs=pl.BlockSpec((None, tb, D_B), lambda b, t: (b, t, 0)),
        out_shape=jax.ShapeDtypeStruct((B, L, D_B), BF16),
        compiler_params=_cp(("arbitrary", "arbitrary")),
        name="gmlp_prompt",
    )(proj3, proj3, g_v, wcat, bias)


def _gmlp_s_body(u_ref, v_ref, gv_ref, w0_ref, b0_ref, o_ref, vn_ref):
    vn = _rms(jax.nn.gelu(v_ref[...])) * gv_ref[...]
    vn_ref[...] = vn
    s = w0_ref[...] * vn + b0_ref[...]
    o_ref[...] = (jax.nn.gelu(u_ref[...]) * s).astype(o_ref.dtype)


def gmlp_sample(proj, g_v, w_s, b_s):
    R = proj.shape[0]
    hd = D_B // H_B
    w0 = jnp.repeat(w_s[:, 0, 0], hd).reshape(1, D_B)
    b0 = jnp.repeat(b_s[:, 0], hd).reshape(1, D_B)
    z2 = lambda i: (0, 0)
    ub = D_A // D_B
    return pl.pallas_call(
        _gmlp_s_body,
        grid=(1,),
        in_specs=[pl.BlockSpec((R, D_B), lambda i: (0, ub)),
                  pl.BlockSpec((R, D_B), lambda i: (0, ub + 1)),
                  pl.BlockSpec((1, D_B), z2), pl.BlockSpec((1, D_B), z2), pl.BlockSpec((1, D_B), z2)],
        out_specs=[pl.BlockSpec((R, D_B), z2), pl.BlockSpec((R, D_B), z2)],
        out_shape=[jax.ShapeDtypeStruct((R, D_B), BF16), jax.ShapeDtypeStruct((R, D_B), F32)],
        compiler_params=_cp(("arbitrary",)),
        name="gmlp_sample",
    )(proj, proj, g_v, w0, b0)


def _head_expand_matrix():
    r = lax.broadcasted_iota(jnp.int32, (LANE, D_C), 0)
    c = lax.broadcasted_iota(jnp.int32, (LANE, D_C), 1) // HD_C
    return jnp.where(r == c, 1.0, 0.0).astype(BF16)


def _expand_heads(v, e):
    hi, lo = _split_bf16(v)
    return _dot(hi, e) + _dot(lo, e)


def _ssd_p_body(z_ref, xbc_ref, dt_ref, cw_ref, cb_ref, dtb_ref, alog_ref, dsk_ref, gn_ref,
                o_ref, st_ref, h_ref, h_sc, cv_sc, *, nt):
    Q = SSD_CHUNK
    t = pl.program_id(1)

    @pl.when(t == 0)
    def _():
        h_sc[...] = jnp.zeros_like(h_sc)
        cv_sc[0:SUBLANE, :] = jnp.zeros((SUBLANE, D_XBC), F32)

    xbc = xbc_ref[...]
    cv_sc[SUBLANE:SUBLANE + Q, :] = xbc
    acc = cb_ref[...]
    for k in range(K_C):
        off = SUBLANE - (K_C - 1) + k
        acc = acc + cw_ref[k:k + 1, :] * cv_sc[off:off + Q, :]
    cv_sc[0:SUBLANE, :] = xbc[Q - SUBLANE:, :]
    st_ref[...] = xbc[Q - (K_C - 1):, :]
    xc = _silu(acc)
    xs = xc[:, :D_C]

    dt = _softplus(dt_ref[...] + dtb_ref[...])
    a = -jnp.exp(alog_ref[...])
    da = dt * a
    ii = lax.broadcasted_iota(jnp.int32, (Q, Q), 0)
    jj = lax.broadcasted_iota(jnp.int32, (Q, Q), 1)
    causal = ii >= jj
    tril = jnp.where(causal, 1.0, 0.0).astype(BF16)
    d0, d1 = _split_bf16(da)
    d2 = (da - d0.astype(F32) - d1.astype(F32)).astype(BF16)
    cs = _dot(tril, d0) + _dot(tril, d1) + _dot(tril, d2)
    cst = cs.T
    cs_end = cs[Q - 1:Q, :]
    e = _head_expand_matrix()
    dt_f = _expand_heads(dt, e)
    ws_f = _expand_heads(dt * jnp.exp(cs_end - cs), e)
    ecs_f = _expand_heads(jnp.exp(cs), e)
    xdt = xs * dt_f
    xw = xs * ws_f

    lane = lax.broadcasted_iota(jnp.int32, (Q, LANE), 1)
    hpg = H_C // G_C
    gw = hpg * HD_C
    ys = []
    for g in range(G_C):
        bg = xc[:, D_C + g * N_C:D_C + (g + 1) * N_C].astype(BF16)
        cg = xc[:, D_C + G_C * N_C + g * N_C:D_C + G_C * N_C + (g + 1) * N_C].astype(BF16)
        gmat = _dot_nt(cg, bg)
        hprev = h_sc[g * gw:(g + 1) * gw, :]
        yoff = _dot_nt(cg, hprev.astype(BF16)) * ecs_f[:, g * gw:(g + 1) * gw]
        snew = _dot_tn(xw[:, g * gw:(g + 1) * gw].astype(BF16), bg)
        for hp in range(hpg // 2):
            h0 = g * hpg + 2 * hp
            xpair = xdt[:, h0 * HD_C:(h0 + 2) * HD_C]
            x_lo = jnp.where(lane < HD_C, xpair, 0.0).astype(BF16)
            x_hi = jnp.where(lane >= HD_C, xpair, 0.0).astype(BF16)
            yd = None
            for hh, xh in ((h0, x_lo), (h0 + 1, x_hi)):
                seg = cs[:, hh:hh + 1] - cst[hh:hh + 1, :]
                sc = (gmat * jnp.exp(jnp.where(causal, seg, NEG_BIG))).astype(BF16)
                part = _dot(sc, xh)
                yd = part if yd is None else yd + part
            ys.append(yd + yoff[:, 2 * hp * HD_C:(2 * hp + 2) * HD_C])
        for hh in range(hpg):
            h = g * hpg + hh
            cd = jnp.exp(cst[h:h + 1, Q - 1:Q])
            rows = slice(h * HD_C, (h + 1) * HD_C)
            h_sc[rows, :] = h_sc[rows, :] * cd + snew[hh * HD_C:(hh + 1) * HD_C, :]

    y = jnp.concatenate(ys, axis=1) + dsk_ref[...] * xs
    y = y * _silu(z_ref[...])
    o_ref[...] = (_rms(y) * gn_ref[...]).astype(o_ref.dtype)

    @pl.when(t == nt - 1)
    def _():
        h_ref[...] = h_sc[...]


def _ssd_params(dt_bias, a_log, d_skip):
    pad = LANE - H_C
    dtb = jnp.pad(dt_bias, (0, pad)).reshape(1, LANE)
    alog = jnp.pad(a_log, (0, pad)).reshape(1, LANE)
    dsk = jnp.repeat(d_skip, HD_C).reshape(1, D_C)
    return dtb, alog, dsk


def ssd_prompt(z3, xd3, conv_w, conv_b, dt_bias, a_log, d_skip, g_norm):
    B, L = z3.shape[:2]
    Q = SSD_CHUNK
    nt = L // Q
    dtb, alog, dsk = _ssd_params(dt_bias, a_log, d_skip)
    const2 = lambda b, t: (0, 0)
    blk = lambda w: pl.BlockSpec((None, Q, w), lambda b, t: (b, t, 0))
    return pl.pallas_call(
        functools.partial(_ssd_p_body, nt=nt),
        grid=(B, nt),
        in_specs=[blk(D_C), blk(D_XBC),
                  pl.BlockSpec((None, Q, LANE), lambda b, t: (b, t, D_XBC // LANE)),
                  pl.BlockSpec((K_C, D_XBC), const2), pl.BlockSpec((1, D_XBC), const2),
                  pl.BlockSpec((1, LANE), const2), pl.BlockSpec((1, LANE), const2),
                  pl.BlockSpec((1, D_C), const2), pl.BlockSpec((1, D_C), const2)],
        out_specs=[blk(D_C),
                   pl.BlockSpec((None, K_C - 1, D_XBC), lambda b, t: (b, 0, 0)),
                   pl.BlockSpec((None, H_C * HD_C, N_C), lambda b, t: (b, 0, 0))],
        out_shape=[jax.ShapeDtypeStruct((B, L, D_C), BF16),
                   jax.ShapeDtypeStruct((B, K_C - 1, D_XBC), F32),
                   jax.ShapeDtypeStruct((B, H_C * HD_C, N_C), F32)],
        scratch_shapes=[pltpu.VMEM((H_C * HD_C, N_C), F32), pltpu.VMEM((SUBLANE + Q, D_XBC), F32)],
        compiler_params=_cp(("arbitrary", "arbitrary")),
        name="ssd_prompt",
    )(z3, xd3, xd3, conv_w, conv_b.reshape(1, D_XBC), dtb, alog, dsk, g_norm.reshape(1, D_C))


def _ssd_s_body(z_ref, xbc_ref, dt_ref, buf_ref, h0_ref, cw_ref, cb_ref, dtb_ref, alog_ref,
                dsk_ref, gn_ref, *rest, tbatch, nsteps, has_prev):
    o_ref, hn_ref, xs_sc, bc_sc, xt_sc, at_sc, y_sc = rest[1:] if has_prev else rest
    s = pl.program_id(0)
    R = xs_sc.shape[0]

    @pl.when(s == 0)
    def _():
        acc = cb_ref[...] + cw_ref[K_C - 1:K_C, :] * xbc_ref[...]
        for k in range(K_C - 1):
            acc = acc + cw_ref[k:k + 1, :] * buf_ref[k]
        xc = _silu(acc)
        xs = xc[:, :D_C]
        xs_sc[...] = xs
        bc_sc[...] = xc[:, D_C:]
        dt = _softplus(dt_ref[...] + dtb_ref[...])
        e = _head_expand_matrix()
        dt_f = _expand_heads(dt, e)
        da_f = jnp.exp(_expand_heads(dt * (-jnp.exp(alog_ref[...])), e))
        for t_sc, v in ((xt_sc, xs * dt_f), (at_sc, da_f)):
            hi, lo = _split_bf16(v.T)
            t_sc[:, :R] = hi
            t_sc[:, R:] = lo

    gw = (H_C // G_C) * HD_C
    kk = lax.broadcasted_iota(jnp.int32, (2 * R, LANE), 0) % R
    r0 = pl.multiple_of(s * tbatch, tbatch)
    bc8 = bc_sc[pl.ds(r0, tbatch), :]
    ti = lax.broadcasted_iota(jnp.int32, (tbatch, gw), 0)
    ytile = [jnp.zeros((tbatch, gw), F32) for _ in range(G_C)]
    for i in range(tbatch):
        b = s * tbatch + i
        onehot = jnp.where(kk == b, 1.0, 0.0).astype(BF16)
        xb = _dot(xt_sc[...], onehot)
        ab = _dot(at_sc[...], onehot)
        for g in range(G_C):
            rows = slice(g * gw, (g + 1) * gw)
            brow = bc8[i:i + 1, g * N_C:(g + 1) * N_C]
            hn = h0_ref[i, rows, :] * ab[rows, :] + xb[rows, :] * brow
            hn_ref[i, rows, :] = hn
            c8 = bc8[:, G_C * N_C + g * N_C:G_C * N_C + (g + 1) * N_C]
            yg = _dot_nt(c8.astype(BF16), hn.astype(BF16))
            ytile[g] = jnp.where(ti == i, yg, ytile[g])
    for g in range(G_C):
        y_sc[pl.ds(r0, tbatch), g * gw:(g + 1) * gw] = ytile[g]

    @pl.when(s == nsteps - 1)
    def _():
        y = y_sc[...] + dsk_ref[...] * xs_sc[...]
        y = y * _silu(z_ref[...])
        o_ref[...] = (_rms(y) * gn_ref[...]).astype(o_ref.dtype)


def ssd_sample(z, xd, buf, h0_all, l, hn_all, conv_w, conv_b, dt_bias, a_log, d_skip, g_norm,
               *, tbatch=8):
    R = z.shape[0]
    nsteps = R // tbatch
    dtb, alog, dsk = _ssd_params(dt_bias, a_log, d_skip)
    z2 = lambda s: (0, 0)
    hw = H_C * HD_C
    st_spec = pl.BlockSpec((None, tbatch, hw, N_C), lambda s: (l, s, 0, 0))
    in_specs = [pl.BlockSpec((R, D_C), z2), pl.BlockSpec((R, D_XBC), z2),
                pl.BlockSpec((R, LANE), lambda s: (0, D_XBC // LANE)),
                pl.BlockSpec((K_C - 1, R, D_XBC), lambda s: (0, 0, 0)),
                st_spec,
                pl.BlockSpec((K_C, D_XBC), z2), pl.BlockSpec((1, D_XBC), z2),
                pl.BlockSpec((1, LANE), z2), pl.BlockSpec((1, LANE), z2),
                pl.BlockSpec((1, D_C), z2), pl.BlockSpec((1, D_C), z2)]
    args = [z, xd, xd, buf, h0_all, conv_w, conv_b.reshape(1, D_XBC), dtb, alog, dsk,
            g_norm.reshape(1, D_C)]
    aliases = {}
    if hn_all is not None:
        aliases = {len(args): 1}
        in_specs.append(pl.BlockSpec(memory_space=pl.ANY))
        args.append(hn_all)
    return pl.pallas_call(
        functools.partial(_ssd_s_body, tbatch=tbatch, nsteps=nsteps, has_prev=hn_all is not None),
        grid=(nsteps,),
        in_specs=in_specs,
        out_specs=[pl.BlockSpec((R, D_C), z2), st_spec],
        out_shape=[jax.ShapeDtypeStruct((R, D_C), BF16),
                   jax.ShapeDtypeStruct(h0_all.shape, F32)],
        scratch_shapes=[pltpu.VMEM((R, D_C), F32), pltpu.VMEM((R, 2 * G_C * N_C), F32),
                        pltpu.VMEM((D_C, 2 * R), BF16), pltpu.VMEM((D_C, 2 * R), BF16),
                        pltpu.VMEM((R, D_C), F32)],
        input_output_aliases=aliases,
        compiler_params=_cp(("arbitrary",)),
        name="ssd_sample",
    )(*args)


def _up_p_body(x_ref, wa_ref, wb_ref, cwa_ref, cwb_ref, cba_ref, cbb_ref,
               o_ref, sa_ref, sb_ref, wa_sc, wb_sc, ca_sc, cb_sc, *, tm, tiles_per_seq):
    i = pl.program_id(1)

    @pl.when(i == 0)
    def _():
        wa_sc[...] = wa_ref[...].astype(BF16)
        wb_sc[...] = wb_ref[...].astype(BF16)

    @pl.when(i % tiles_per_seq == 0)
    def _():
        ca_sc[0:SUBLANE, :] = jnp.zeros((SUBLANE, ca_sc.shape[1]), F32)
        cb_sc[0:SUBLANE, :] = jnp.zeros((SUBLANE, cb_sc.shape[1]), F32)

    x = x_ref[...]

    def half(w_sc, c_sc, cw_ref, cb_ref, st_ref):
        up = _dot(x, w_sc[...])
        c_sc[SUBLANE:SUBLANE + tm, :] = up
        acc = cb_ref[...]
        for k in range(K_F):
            off = SUBLANE - (K_F - 1) + k
            acc = acc + cw_ref[k:k + 1, :] * c_sc[off:off + tm, :]
        c_sc[0:SUBLANE, :] = up[tm - SUBLANE:, :]
        st_ref[...] = up[tm - (K_F - 1):, :]
        return acc

    a = half(wa_sc, ca_sc, cwa_ref, cba_ref, sa_ref)
    b = half(wb_sc, cb_sc, cwb_ref, cbb_ref, sb_ref)
    o_ref[...] = (_silu(a) * b).astype(o_ref.dtype)


def up_prompt(h2, w_up, l, conv_w, conv_b, *, seq_len, tm=1024, tn=512):
    M, D = h2.shape
    B = M // seq_len
    tiles_per_seq = seq_len // tm
    nj = D_FF // tn
    cb = conv_b.reshape(1, 2 * D_FF)
    body = functools.partial(_up_p_body, tm=tm, tiles_per_seq=tiles_per_seq)
    st_spec = pl.BlockSpec((None, K_F - 1, tn), lambda j, i: (i // tiles_per_seq, 0, j))
    return pl.pallas_call(
        body,
        grid=(nj, M // tm),
        in_specs=[pl.BlockSpec((tm, D), lambda j, i: (i, 0)),
                  pl.BlockSpec((None, D, tn), lambda j, i: (l, 0, j)),
                  pl.BlockSpec((None, D, tn), lambda j, i: (l, 0, nj + j)),
                  pl.BlockSpec((K_F, tn), lambda j, i: (0, j)),
                  pl.BlockSpec((K_F, tn), lambda j, i: (0, nj + j)),
                  pl.BlockSpec((1, tn), lambda j, i: (0, j)),
                  pl.BlockSpec((1, tn), lambda j, i: (0, nj + j))],
        out_specs=[pl.BlockSpec((tm, tn), lambda j, i: (i, j)), st_spec, st_spec],
        out_shape=[jax.ShapeDtypeStruct((M, D_FF), BF16),
                   jax.ShapeDtypeStruct((B, K_F - 1, D_FF), F32),
                   jax.ShapeDtypeStruct((B, K_F - 1, D_FF), F32)],
        scratch_shapes=[pltpu.VMEM((D, tn), BF16), pltpu.VMEM((D, tn), BF16),
                        pltpu.VMEM((SUBLANE + tm, tn), F32), pltpu.VMEM((SUBLANE + tm, tn), F32)],
        compiler_params=_cp(("arbitrary", "arbitrary")),
        name="up_prompt",
    )(h2, w_up, w_up, conv_w, conv_w, cb, cb)


def _up_s_body(x_ref, wa_ref, wb_ref, bufa_ref, bufb_ref, cwa_ref, cwb_ref, cba_ref, cbb_ref,
               *rest, has_prev):
    o_ref, na_ref, nb_ref = rest[2:] if has_prev else rest
    x = x_ref[...]

    def half(w_ref, buf_ref, cw_ref, cb_ref, new_ref):
        up = _dot(x, w_ref[...].astype(BF16))
        acc = cb_ref[...] + cw_ref[K_F - 1:K_F, :] * up
        for k in range(K_F - 1):
            acc = acc + cw_ref[k:k + 1, :] * buf_ref[:, k, :]
        for k in range(1, K_F - 1):
            new_ref[:, k - 1, :] = buf_ref[:, k, :]
        new_ref[:, K_F - 2, :] = up
        return acc

    a = half(wa_ref, bufa_ref, cwa_ref, cba_ref, na_ref)
    b = half(wb_ref, bufb_ref, cwb_ref, cbb_ref, nb_ref)
    o_ref[...] = (_silu(a) * b).astype(o_ref.dtype)


def up_sample(h2, w_up, l, buf_all, new_ab, conv_w, conv_b, *, tn=512):
    R, D = h2.shape
    nj = D_FF // tn
    cb = conv_b.reshape(1, 2 * D_FF)
    st = lambda off: pl.BlockSpec((None, R, K_F - 1, tn), lambda j: (l, 0, 0, off + j))
    in_specs = [pl.BlockSpec((R, D), lambda j: (0, 0)),
                pl.BlockSpec((None, D, tn), lambda j: (l, 0, j)),
                pl.BlockSpec((None, D, tn), lambda j: (l, 0, nj + j)),
                st(0), st(nj),
                pl.BlockSpec((K_F, tn), lambda j: (0, j)),
                pl.BlockSpec((K_F, tn), lambda j: (0, nj + j)),
                pl.BlockSpec((1, tn), lambda j: (0, j)),
                pl.BlockSpec((1, tn), lambda j: (0, nj + j))]
    args = [h2, w_up, w_up, buf_all, buf_all, conv_w, conv_w, cb, cb]
    aliases = {}
    if new_ab is not None:
        aliases = {len(args): 1, len(args) + 1: 2}
        in_specs += [pl.BlockSpec(memory_space=pl.ANY)] * 2
        args += list(new_ab)
    half_shape = jax.ShapeDtypeStruct((DEPTH, R, K_F - 1, D_FF), F32)
    act, na, nb = pl.pallas_call(
        functools.partial(_up_s_body, has_prev=new_ab is not None),
        grid=(nj,),
        in_specs=in_specs,
        out_specs=[pl.BlockSpec((R, tn), lambda j: (0, j)), st(0), st(0)],
        out_shape=[jax.ShapeDtypeStruct((R, D_FF), BF16), half_shape, half_shape],
        input_output_aliases=aliases,
        compiler_params=_cp(("arbitrary",)),
        name="up_sample",
    )(*args)
    return act, (na, nb)


def _layer(x3, mods, l, p, state, *, seq_len, tm, tn_merge, tn_down, tm_norm):
    B, L, D = x3.shape
    M = B * L
    sh_m, sc_m, gt_m, sh_f, sc_f, gt_f = mods
    prompt = state is None
    w_buffers = 1 if M > tm else 2

    h = norm_mod(x3, p['g_mix'][l].reshape(1, D), sc_m, sh_m, tm=tm_norm).reshape(M, D)

    def proj(col0, n, name):
        return matmul([h], [(p['w_in_t'], l, col0, True)], [], [(n, F32)], _epi_store,
                      x_of_w=(0,), tm=tm, tn=n, name=name, vmem_mb=56, w_buffers=1)[0]

    uav = proj(OFF_UA, D_A + 2 * D_B, "proj_uav")
    zc = proj(OFF_Z, D_C, "proj_z")
    xd = proj(OFF_XBC, D_XBC + LANE, "proj_xd")

    tab, wb, bbr, bbi = s5_prep(p['lam_re'][l], p['lam_im'][l], p['log_dt'][l],
                                p['b_re'][l], p['b_im'][l])
    d_a = p['s5_d'][l].reshape(1, D_A)
    wglu = p['w_glu'][l].astype(BF16)
    g_v = p['g_v'][l].reshape(1, D_B)

    if prompt:
        s5_ops = s5_chunk_prep(p['lam_re'][l], p['lam_im'][l], p['log_dt'][l], bbr, bbi,
                               p['c_re'][l], p['c_im'][l])
        o_a, s5r, s5i = s5_prompt(uav.reshape(B, L, -1)[..., :D_A], s5_ops, d_a, wglu)
        o_b = gmlp_prompt(uav.reshape(B, L, -1), g_v, p['w_s'][l], p['b_s'][l])
        v_rows = None
        o_c, convc, ssm = ssd_prompt(zc.reshape(B, L, -1), xd.reshape(B, L, -1),
                                     p['ssd_conv_w'][l], p['ssd_conv_b'][l], p['dt_bias'][l],
                                     p['a_log'][l], p['ssd_d'][l], p['ssd_g'][l])
        s5r = s5r.reshape(B, G_A, P_A)
        s5i = s5i.reshape(B, G_A, P_A)
    else:
        s5_re0, s5_im0, ssm_all, ssm_new_all, convc0, ffn_all, ffn_new_ab = state
        wcr, wci = s5_out_weights(p['c_re'][l], p['c_im'][l])
        o_a, s5r, s5i = s5_sample(uav, s5_re0.reshape(M, N_S5), s5_im0.reshape(M, N_S5),
                                  wb, wcr, wci, tab, d_a, wglu)
        o_b, v_rows = gmlp_sample(uav, g_v, p['w_s'][l], p['b_s'][l])
        o_c, ssm = ssd_sample(zc, xd, jnp.transpose(convc0, (1, 0, 2)),
                              ssm_all, l, ssm_new_all,
                              p['ssd_conv_w'][l], p['ssd_conv_b'][l], p['dt_bias'][l],
                              p['a_log'][l], p['ssd_d'][l], p['ssd_g'][l])
        convc = jnp.concatenate([convc0[:, 1:], xd[:, None, :D_XBC]], axis=1)
        s5r = s5r.reshape(M, G_A, P_A)
        s5i = s5i.reshape(M, G_A, P_A)
    o_a = o_a.reshape(M, D_A)
    o_b = o_b.reshape(M, D_B)
    o_c = o_c.reshape(M, D_C)
    if prompt:
        ssm = ssm.reshape(B, H_C, HD_C, N_C)

    w_t = p['w_in_t']
    merged = matmul(
        [h, o_a, o_b, o_c],
        [(p['w_pa'], l, 0, False), (p['w_pb'], l, 0, False), (p['w_pc'], l, 0, False),
         (w_t, l, OFF_GATES, True), (w_t, l, OFF_GATES + D, True), (w_t, l, OFF_GATES + 2 * D, True)],
        [], [(D, BF16)], _epi_merge, x_of_w=(1, 2, 3, 0, 0, 0), tm=tm, tn=tn_merge, name="merge",
        vmem_mb=56, w_buffers=w_buffers)[0]

    def residual(x_res, act_in, w3, gt, tm_r, tn, name, vmem_mb):
        return matmul([act_in], [(w3, l, 0, False)],
                      [(x_res, _tile_spec(tm_r, tn)),
                       (gt, _gate_spec(gt, tm_r, tn, max(seq_len // tm_r, 1)))],
                      [(D, F32)], _epi_residual, x_of_w=(0,), tm=tm_r, tn=tn, name=name,
                      vmem_mb=vmem_mb, w_buffers=w_buffers)[0]

    x2, h2 = out_norm(merged, p['w_out'], l, x3.reshape(M, D), gt_m, p['g_ffn'][l].reshape(1, D),
                      sc_f, sh_f, tm=min(tm, 256), seq_len=seq_len)
    if prompt:
        act, st_a, st_b = up_prompt(h2, p['w_up'], l, p['ffn_conv_w'][l], p['ffn_conv_b'][l],
                                    seq_len=seq_len, tm=tm)
        convf = jnp.concatenate([st_a, st_b], axis=-1)
    else:
        act, convf = up_sample(h2, p['w_up'], l, ffn_all, ffn_new_ab,
                               p['ffn_conv_w'][l], p['ffn_conv_b'][l])

    x2 = residual(x2, act, p['w_down'], gt_f, min(tm, 256), tn_down, "down_proj", 56)
    return x2.reshape(B, L, D), s5r, s5i, ssm, convc, convf, v_rows


def kernel(x_prompt, x_sample, c_prompt, c_sample, state_s5_re, state_s5_im, state_ssm, state_ssd_conv, state_ffn_conv, w_mod, b_mod, g_mix, w_in, s5_lam_re, s5_lam_im, s5_log_dt, s5_b_re, s5_b_im, s5_c_re, s5_c_im, s5_d, s5_w_glu, gm_g_v, gm_w_s, gm_b_s, ssd_conv_w, ssd_conv_b, ssd_dt_bias, ssd_a_log, ssd_d, ssd_g_norm, w_pa, w_pb, w_pc, w_out, g_ffn, ffn_w_up, ffn_conv_w, ffn_conv_b, ffn_w_down, g_final):
    p = {
        'g_mix': g_mix, 'w_in_t': jnp.swapaxes(w_in, 1, 2),
        'lam_re': s5_lam_re, 'lam_im': s5_lam_im, 'log_dt': s5_log_dt,
        'b_re': s5_b_re, 'b_im': s5_b_im, 'c_re': s5_c_re, 'c_im': s5_c_im,
        's5_d': s5_d, 'w_glu': s5_w_glu,
        'g_v': gm_g_v, 'w_s': gm_w_s, 'b_s': gm_b_s,
        'ssd_conv_w': ssd_conv_w, 'ssd_conv_b': ssd_conv_b, 'dt_bias': ssd_dt_bias,
        'a_log': ssd_a_log, 'ssd_d': ssd_d, 'ssd_g': ssd_g_norm,
        'w_pa': w_pa, 'w_pb': w_pb, 'w_pc': w_pc, 'w_out': w_out,
        'g_ffn': g_ffn, 'w_up': ffn_w_up, 'ffn_conv_w': ffn_conv_w,
        'ffn_conv_b': ffn_conv_b, 'w_down': ffn_w_down,
    }
    bp, seq, D = x_prompt.shape
    bs = x_sample.shape[0]

    n_c = bs + bp
    pad = (-n_c) % SUBLANE
    c_all = jnp.concatenate([c_sample, c_prompt, jnp.zeros((pad, D), F32)], axis=0)
    mod = mod_all(c_all, w_mod, b_mod)

    xp = x_prompt
    xs = x_sample.reshape(1, bs, D)
    outs_p = [[] for _ in range(5)]
    outs_s = [[] for _ in range(6)]
    ssm_all = state_ssm.reshape(DEPTH, bs, H_C * HD_C, N_C)
    ssm_new_all = None
    ffn_new_ab = None
    for l in range(DEPTH):
        mods_s = [m[None] for m in jnp.split(mod[l, :bs], 6, axis=-1)]
        mods_p = [m[:, None, :] for m in jnp.split(mod[l, bs:bs + bp], 6, axis=-1)]
        xp, *st_p = _layer(xp, mods_p, l, p, None, seq_len=seq, tm=1024, tn_merge=512, tn_down=1024,
                           tm_norm=512)
        for acc, v in zip(outs_p, st_p[:5]):
            acc.append(v)
        state = (state_s5_re[l], state_s5_im[l], ssm_all, ssm_new_all, state_ssd_conv[l],
                 state_ffn_conv, ffn_new_ab)
        xs, *st_s = _layer(xs, mods_s, l, p, state, seq_len=1, tm=bs, tn_merge=256, tn_down=256,
                           tm_norm=bs)
        ssm_new_all = st_s[2]
        ffn_new_ab = st_s[4]
        st_s[5] = st_s[5].reshape(bs, 1, D_B)
        for acc, v in zip(outs_s, st_s):
            acc.append(v)

    g_fin = g_final.reshape(1, D)
    y_prompt = final_norm(xp, g_fin, tm=512)
    y_sample = final_norm(xs, g_fin, tm=bs).reshape(bs, 1, D)
    whole = {2: ssm_new_all.reshape(DEPTH, bs, H_C, HD_C, N_C),
             4: jnp.concatenate(ffn_new_ab, axis=-1)}
    outs_s = [whole[k] if k in whole else jnp.stack(v) for k, v in enumerate(outs_s)]
    return (y_prompt, y_sample, *[jnp.stack(v) for v in outs_p], *outs_s)
```

```python
import functools

import jax
import jax.numpy as jnp
from jax import lax
from jax.experimental import pallas as pl
from jax.experimental.pallas import tpu as pltpu

F32 = jnp.float32
BF16 = jnp.bfloat16

D_MODEL = 2048
DEPTH = 2
D_A = 512
S5_GROUP = 16
G_A = D_A // S5_GROUP
P_A = 64
N_S5 = G_A * P_A
S5_GPB = 128 // S5_GROUP
S5_NB = D_A // 128
S5_SPB = S5_GPB * P_A
S5_T = 8
D_B = 512
H_B = 8
GM_CHUNK = 128
D_C = 1024
HD_C = 64
H_C = D_C // HD_C
N_C = 128
G_C = 2
K_C = 4
SSD_CHUNK = 128
D_XBC = D_C + 2 * G_C * N_C
D_FF = 5632
K_F = 3
EPS = 1e-6

OFF_UA = 0
OFF_Z = D_A + 2 * D_B
OFF_XBC = OFF_Z + D_C
OFF_DT = OFF_XBC + D_XBC
OFF_GATES = OFF_DT + H_C

LANE = 128
SUBLANE = 8
NEG_BIG = -1e30


def _cp(sem, vmem_mb=48):
    return pltpu.CompilerParams(dimension_semantics=sem, vmem_limit_bytes=vmem_mb * 1024 * 1024)


def _sigmoid(x):
    return 0.5 * jnp.tanh(0.5 * x) + 0.5


def _silu(x):
    return x * _sigmoid(x)


def _softplus(x):
    return jnp.maximum(x, 0.0) + jnp.log1p(jnp.exp(-jnp.abs(x)))


def _rms(x):
    return x * lax.rsqrt(jnp.mean(x * x, axis=-1, keepdims=True) + EPS)


def _split_bf16(x):
    hi = x.astype(BF16)
    lo = (x - hi.astype(F32)).astype(BF16)
    return hi, lo


def _dot(a, b):
    return jnp.dot(a, b, preferred_element_type=F32)


def _dot_nt(a, b):
    return lax.dot_general(a, b, (((1,), (1,)), ((), ())), preferred_element_type=F32)


def _dot_tn(a, b):
    return lax.dot_general(a, b, (((0,), (0,)), ((), ())), preferred_element_type=F32)


def _mod_body(c_ref, w_ref, b_ref, o_ref):
    a = _silu(c_ref[...]).astype(BF16)
    o_ref[...] = _dot(a, w_ref[...].astype(BF16)) + b_ref[...]


def mod_all(c_all, w_mod, b_mod, *, tn=1024):
    R, D = c_all.shape
    N = w_mod.shape[-1]
    return pl.pallas_call(
        _mod_body,
        grid=(DEPTH, N // tn),
        in_specs=[pl.BlockSpec((R, D), lambda l, j: (0, 0)),
                  pl.BlockSpec((None, D, tn), lambda l, j: (l, 0, j)),
                  pl.BlockSpec((None, 1, tn), lambda l, j: (l, 0, j))],
        out_specs=pl.BlockSpec((None, R, tn), lambda l, j: (l, 0, j)),
        out_shape=jax.ShapeDtypeStruct((DEPTH, R, N), F32),
        compiler_params=_cp(("arbitrary", "arbitrary")),
        name="mod_all",
    )(c_all, w_mod, b_mod.reshape(DEPTH, 1, N))


def _norm_mod_body(x_ref, g_ref, sc_ref, sh_ref, o_ref):
    y = _rms(x_ref[...]) * g_ref[...]
    o_ref[...] = (y * (1.0 + sc_ref[...]) + sh_ref[...]).astype(o_ref.dtype)


def norm_mod(x3, g, sc3, sh3, *, tm):
    B, L, D = x3.shape
    per_row = sc3.shape[1] != 1
    ts = tm if per_row else 1
    sidx = (lambda b, i: (b, i, 0)) if per_row else (lambda b, i: (b, 0, 0))
    return pl.pallas_call(
        _norm_mod_body,
        grid=(B, L // tm),
        in_specs=[pl.BlockSpec((None, tm, D), lambda b, i: (b, i, 0)),
                  pl.BlockSpec((1, D), lambda b, i: (0, 0)),
                  pl.BlockSpec((None, ts, D), sidx),
                  pl.BlockSpec((None, ts, D), sidx)],
        out_specs=pl.BlockSpec((None, tm, D), lambda b, i: (b, i, 0)),
        out_shape=jax.ShapeDtypeStruct((B, L, D), BF16),
        compiler_params=_cp(("arbitrary", "arbitrary")),
        name="norm_mod",
    )(x3, g, sc3, sh3)


def _rms_body(x_ref, g_ref, o_ref):
    o_ref[...] = _rms(x_ref[...]) * g_ref[...]


def final_norm(x3, g, *, tm):
    B, L, D = x3.shape
    return pl.pallas_call(
        _rms_body,
        grid=(B, L // tm),
        in_specs=[pl.BlockSpec((None, tm, D), lambda b, i: (b, i, 0)),
                  pl.BlockSpec((1, D), lambda b, i: (0, 0))],
        out_specs=pl.BlockSpec((None, tm, D), lambda b, i: (b, i, 0)),
        out_shape=jax.ShapeDtypeStruct((B, L, D), F32),
        compiler_params=_cp(("arbitrary", "arbitrary")),
        name="final_norm",
    )(x3, g)


def _mm_body(*refs, x_of_w, w_is_t, n_x, n_e, n_o, epilogue):
    n_w = len(x_of_w)
    xs = refs[:n_x]
    ws = refs[n_x:n_x + n_w]
    es = refs[n_x + n_w:n_x + n_w + n_e]
    outs = refs[n_x + n_w + n_e:n_x + n_w + n_e + n_o]
    wsc = refs[n_x + n_w + n_e + n_o:]

    @pl.when(pl.program_id(1) == 0)
    def _():
        for w, s, is_t in zip(ws, wsc, w_is_t):
            s[...] = (w[...].T if is_t else w[...]).astype(BF16)

    accs = [_dot(xs[xi][...], s[...]) for xi, s in zip(x_of_w, wsc)]
    epilogue(accs, es, outs)


def matmul(xs, ws, extras, outs, epilogue, *, x_of_w, tm, tn, name, vmem_mb=48, w_buffers=2):
    M = xs[0].shape[0]
    N = outs[0][0]
    in_specs = [pl.BlockSpec((tm, x.shape[1]), lambda j, i: (i, 0)) for x in xs]
    args = list(xs)
    scratch = []
    for w3, l, col0, is_t in ws:
        if is_t:
            K = w3.shape[2]
            assert col0 % SUBLANE == 0
            in_specs.append(pl.BlockSpec((None, pl.Element(tn), pl.Element(K)), functools.partial(
                lambda j, i, l, c: (l, (c + j * (tn // SUBLANE)) * SUBLANE, 0),
                l=l, c=col0 // SUBLANE), pipeline_mode=pl.Buffered(w_buffers)))
        else:
            K = w3.shape[1]
            assert col0 % tn == 0
            in_specs.append(pl.BlockSpec((None, K, tn), functools.partial(
                lambda j, i, l, c: (l, 0, c + j), l=l, c=col0 // tn),
                pipeline_mode=pl.Buffered(w_buffers)))
        args.append(w3)
        scratch.append(pltpu.VMEM((K, tn), BF16))
    for arr, spec in extras:
        in_specs.append(spec)
        args.append(arr)
    body = functools.partial(_mm_body, x_of_w=tuple(x_of_w), w_is_t=tuple(w[3] for w in ws),
                             n_x=len(xs), n_e=len(extras), n_o=len(outs), epilogue=epilogue)
    res = pl.pallas_call(
        body,
        grid=(N // tn, M // tm),
        in_specs=in_specs,
        out_specs=[pl.BlockSpec((tm, tn), lambda j, i: (i, j)) for _ in outs],
        out_shape=[jax.ShapeDtypeStruct((M, n), dt) for n, dt in outs],
        scratch_shapes=scratch,
        compiler_params=_cp(("arbitrary", "arbitrary"), vmem_mb),
        name=name,
    )(*args)
    return res


def _epi_store(accs, es, outs):
    outs[0][...] = accs[0].astype(outs[0].dtype)


def _epi_merge(accs, es, outs):
    pa, pb, pc, ga, gb, gc = accs
    m = _sigmoid(ga) * pa + _sigmoid(gb) * pb + _sigmoid(gc) * pc
    outs[0][...] = m.astype(outs[0].dtype)


def _epi_residual(accs, es, outs):
    res_ref, gt_ref = es
    outs[0][...] = res_ref[...] + gt_ref[...] * accs[0]


def _out_norm_body(a_ref, w_ref, res_ref, gt_ref, g_ref, sc_ref, sh_ref, x_ref, h_ref, w_sc):
    @pl.when(pl.program_id(0) == 0)
    def _():
        w_sc[...] = w_ref[...].astype(BF16)

    x = res_ref[...] + gt_ref[...] * _dot(a_ref[...], w_sc[...])
    x_ref[...] = x
    y = _rms(x) * g_ref[...]
    h_ref[...] = (y * (1.0 + sc_ref[...]) + sh_ref[...]).astype(h_ref.dtype)


def out_norm(act, w3, l, res, gt, g, sc, sh, *, tm, seq_len):
    M, K = act.shape
    D = res.shape[1]
    tiles_per_seq = max(seq_len // tm, 1)

    def mod_spec(m3):
        if m3.shape[1] == 1:
            return pl.BlockSpec((None, 1, D), lambda i: (i // tiles_per_seq, 0, 0))
        return pl.BlockSpec((None, tm, D), lambda i: (0, i, 0))

    row = lambda w: pl.BlockSpec((tm, w), lambda i: (i, 0))
    return pl.pallas_call(
        _out_norm_body,
        grid=(M // tm,),
        in_specs=[row(K),
                  pl.BlockSpec((None, K, D), lambda i: (l, 0, 0), pipeline_mode=pl.Buffered(1)),
                  row(D), mod_spec(gt),
                  pl.BlockSpec((1, D), lambda i: (0, 0)), mod_spec(sc), mod_spec(sh)],
        out_specs=[row(D), row(D)],
        out_shape=[jax.ShapeDtypeStruct((M, D), F32), jax.ShapeDtypeStruct((M, D), BF16)],
        scratch_shapes=[pltpu.VMEM((K, D), BF16)],
        compiler_params=_cp(("arbitrary",), 56),
        name="out_norm",
    )(act, w3, res, gt, g, sc, sh)


def _tile_spec(tm, tn):
    return pl.BlockSpec((tm, tn), lambda j, i: (i, j))


def _gate_spec(gt3, tm, tn, tiles_per_seq):
    if gt3.shape[1] == 1:
        return pl.BlockSpec((None, 1, tn), lambda j, i: (i // tiles_per_seq, 0, j))
    return pl.BlockSpec((None, tm, tn), lambda j, i: (0, i, j))


def _s5_prep_body(lrf_ref, lif_ref, ldf_ref, lrr_ref, lir_ref, ldr_ref, bre_ref, bim_ref,
                  tab_ref, bbr_ref, bbi_ref):
    dtf = jnp.exp(ldf_ref[...])
    _power_tables(tab_ref, lrf_ref[...] * dtf, lif_ref[...] * dtf, 1)
    dtr = jnp.exp(ldr_ref[...])
    lr = lrr_ref[...]
    li = lir_ref[...]
    m1 = jnp.exp(lr * dtr)
    ar = m1 * jnp.cos(li * dtr)
    ai = m1 * jnp.sin(li * dtr)
    den = lr * lr + li * li
    nr = ar - 1.0
    kr = (nr * lr + ai * li) / den
    ki = (ai * lr - nr * li) / den
    bre = bre_ref[...]
    bim = bim_ref[...]
    bbr_ref[...] = kr * bre - ki * bim
    bbi_ref[...] = kr * bim + ki * bre


def s5_prep(lam_re, lam_im, log_dt, b_re, b_im):
    lrf = lam_re.reshape(1, N_S5)
    lif = lam_im.reshape(1, N_S5)
    ldf = jnp.repeat(log_dt, P_A).reshape(1, N_S5)
    lrr = jnp.repeat(lam_re, S5_GROUP, axis=0)
    lir = jnp.repeat(lam_im, S5_GROUP, axis=0)
    ldr = jnp.repeat(log_dt, S5_GROUP).reshape(D_A, 1)
    bre = jnp.transpose(b_re, (0, 2, 1)).reshape(D_A, P_A)
    bim = jnp.transpose(b_im, (0, 2, 1)).reshape(D_A, P_A)
    tab, bbr, bbi = pl.pallas_call(
        _s5_prep_body,
        out_shape=[jax.ShapeDtypeStruct((8, SUBLANE, N_S5), F32),
                   jax.ShapeDtypeStruct((D_A, P_A), F32),
                   jax.ShapeDtypeStruct((D_A, P_A), F32)],
        name="s5_prep",
    )(lrf, lif, ldf, lrr, lir, ldr, bre, bim)
    eye = jnp.eye(S5_GPB, dtype=F32)

    def blockdiag(m):
        m = m.reshape(S5_NB, S5_GPB, S5_GROUP, P_A)
        return jnp.einsum('kghp,gj->kghjp', m, eye).reshape(S5_NB, LANE, S5_SPB)

    wb = jnp.concatenate([blockdiag(bbr), blockdiag(bbi)], axis=2).astype(BF16)
    return tab, wb, bbr, bbi


def _power_tables(tab_ref, lr_dt, li_dt, stride):
    n_lanes = lr_dt.shape[1]
    row = lax.broadcasted_iota(jnp.int32, (SUBLANE, n_lanes), 0)
    n = ((row + 1) * stride).astype(F32)
    mag = jnp.exp(n * lr_dt)
    ang = n * li_dt
    pr = mag * jnp.cos(ang)
    pi = mag * jnp.sin(ang)
    for k, d in enumerate((1, 2, 4)):
        keep = row >= d
        tab_ref[2 * k] = jnp.where(keep, pr[d - 1:d, :], 0.0)
        tab_ref[2 * k + 1] = jnp.where(keep, pi[d - 1:d, :], 0.0)
    tab_ref[6] = pr
    tab_ref[7] = pi


def _s5_chunk_prep_body(lrf_ref, lif_ref, ldf_ref, lrr_ref, lir_ref, ldr_ref, bbr_ref, bbi_ref,
                        cr_ref, ci_ref, tab_ref, wst_ref, kt_ref, wor_ref, woi_ref):
    T = S5_T
    dtf = jnp.exp(ldf_ref[...])
    _power_tables(tab_ref, lrf_ref[...] * dtf, lif_ref[...] * dtf, T)
    dtr = jnp.exp(ldr_ref[...])
    lr = lrr_ref[...] * dtr
    li = lir_ref[...] * dtr
    bbr = bbr_ref[...]
    bbi = bbi_ref[...]
    cr = cr_ref[...]
    ci = ci_ref[...]
    spread = jnp.where(lax.broadcasted_iota(jnp.int32, (P_A, S5_SPB), 0)
                       == lax.broadcasted_iota(jnp.int32, (P_A, S5_SPB), 1) % P_A, 1.0, 0.0).astype(BF16)
    spread_t = jnp.where(lax.broadcasted_iota(jnp.int32, (S5_SPB, P_A), 0) % P_A
                         == lax.broadcasted_iota(jnp.int32, (S5_SPB, P_A), 1), 1.0, 0.0).astype(BF16)
    in_mask = (lax.broadcasted_iota(jnp.int32, (LANE, S5_SPB), 0) // S5_GROUP
               == lax.broadcasted_iota(jnp.int32, (LANE, S5_SPB), 1) // P_A)
    out_mask = (lax.broadcasted_iota(jnp.int32, (S5_SPB, LANE), 0) // P_A
                == lax.broadcasted_iota(jnp.int32, (S5_SPB, LANE), 1) // S5_GROUP)
    k_mask = (lax.broadcasted_iota(jnp.int32, (LANE, LANE), 0) // S5_GROUP
              == lax.broadcasted_iota(jnp.int32, (LANE, LANE), 1) // S5_GROUP)
    zero_tile = jnp.zeros((LANE, LANE), BF16)
    mag = jnp.exp(lr)
    a_r = mag * jnp.cos(li)
    a_i = mag * jnp.sin(li)
    pr = jnp.ones_like(lr)
    pi = jnp.zeros_like(lr)
    for n in range(T + 1):
        if n >= 1:
            pr, pi = pr * a_r - pi * a_i, pr * a_i + pi * a_r
        car = (cr * pr - ci * pi).astype(BF16)
        cai = (cr * pi + ci * pr).astype(BF16)
        for k in range(S5_NB):
            blk = slice(k * LANE, (k + 1) * LANE)
            if n >= 1:
                cols = slice((n - 1) * LANE, n * LANE)
                wor_ref[k, :, cols] = jnp.where(out_mask, _dot_nt(spread_t, car[blk]), 0.0).astype(BF16)
                woi_ref[k, :, cols] = jnp.where(out_mask, _dot_nt(spread_t, cai[blk]), 0.0).astype(BF16)
            if n < T:
                rows = slice((T - 1 - n) * LANE, (T - n) * LANE)
                wr = (pr * bbr - pi * bbi)[blk].astype(BF16)
                wi = (pr * bbi + pi * bbr)[blk].astype(BF16)
                wst_ref[k, rows, 0:S5_SPB] = jnp.where(in_mask, _dot(wr, spread), 0.0).astype(BF16)
                wst_ref[k, rows, S5_SPB:2 * S5_SPB] = jnp.where(in_mask, _dot(wi, spread), 0.0).astype(BF16)
                kt = (_dot_nt(bbr[blk].astype(BF16), car[blk])
                      - _dot_nt(bbi[blk].astype(BF16), cai[blk]))
                kt = jnp.where(k_mask, kt, 0.0).astype(BF16)
                for ti in range(T - n):
                    to = ti + n
                    kt_ref[k, ti * LANE:(ti + 1) * LANE, to * LANE:(to + 1) * LANE] = kt
                    if n >= 1:
                        kt_ref[k, to * LANE:(to + 1) * LANE, ti * LANE:(ti + 1) * LANE] = zero_tile


def s5_chunk_prep(lam_re, lam_im, log_dt, bbr, bbi, c_re, c_im):
    T = S5_T
    lrf = lam_re.reshape(1, N_S5)
    lif = lam_im.reshape(1, N_S5)
    ldf = jnp.repeat(log_dt, P_A).reshape(1, N_S5)
    lrr = jnp.repeat(lam_re, S5_GROUP, axis=0)
    lir = jnp.repeat(lam_im, S5_GROUP, axis=0)
    ldr = jnp.repeat(log_dt, S5_GROUP).reshape(D_A, 1)
    op = lambda r, c: jax.ShapeDtypeStruct((S5_NB, r, c), BF16)
    return pl.pallas_call(
        _s5_chunk_prep_body,
        out_shape=[jax.ShapeDtypeStruct((8, SUBLANE, N_S5), F32), op(T * LANE, 2 * S5_SPB),
                   op(T * LANE, T * LANE), op(S5_SPB, T * LANE), op(S5_SPB, T * LANE)],
        compiler_params=pltpu.CompilerParams(vmem_limit_bytes=48 * 1024 * 1024),
        name="s5_chunk_prep",
    )(lrf, lif, ldf, lrr, lir, ldr, bbr, bbi, c_re.reshape(D_A, P_A), c_im.reshape(D_A, P_A))


def _s5_c_body(u_ref, wst_ref, kt_ref, wor_ref, woi_ref, tab_ref, d_ref, wglu_ref,
               o_ref, hr_ref, hi_ref, s_sc, car_sc, *, tb):
    T = S5_T

    @pl.when(pl.program_id(1) == 0)
    def _():
        car_sc[...] = jnp.zeros_like(car_sc)

    s_sc[0:SUBLANE, :] = car_sc[...]
    x = u_ref[...]
    xb = x.astype(BF16)
    bw = T * LANE
    for kb in range(S5_NB):
        s = _dot(xb[:, kb * bw:(kb + 1) * bw], wst_ref[kb])
        s_sc[SUBLANE:SUBLANE + tb, kb * S5_SPB:(kb + 1) * S5_SPB] = s[:, :S5_SPB]
        s_sc[SUBLANE:SUBLANE + tb, N_S5 + kb * S5_SPB:N_S5 + (kb + 1) * S5_SPB] = s[:, S5_SPB:]

    def tile(rt, _):
        r0 = pl.multiple_of(SUBLANE + rt * SUBLANE, SUBLANE)
        for lg in range(N_S5 // LANE):
            cre = slice(lg * LANE, (lg + 1) * LANE)
            cim = slice(N_S5 + lg * LANE, N_S5 + (lg + 1) * LANE)
            xr = s_sc[pl.ds(r0, SUBLANE), cre]
            xi = s_sc[pl.ds(r0, SUBLANE), cim]
            for k, d in enumerate((1, 2, 4)):
                a_r = tab_ref[2 * k, :, cre]
                a_i = tab_ref[2 * k + 1, :, cre]
                sr = pltpu.roll(xr, d, 0)
                si = pltpu.roll(xi, d, 0)
                xr, xi = xr + a_r * sr - a_i * si, xi + a_r * si + a_i * sr
            p_r = tab_ref[6, :, cre]
            p_i = tab_ref[7, :, cre]
            cr = car_sc[:, cre]
            ci = car_sc[:, cim]
            xr, xi = xr + p_r * cr - p_i * ci, xi + p_r * ci + p_i * cr
            s_sc[pl.ds(r0, SUBLANE), cre] = xr
            s_sc[pl.ds(r0, SUBLANE), cim] = xi
            last = SUBLANE - 1
            car_sc[:, cre] = jnp.broadcast_to(xr[last:last + 1, :], (SUBLANE, LANE))
            car_sc[:, cim] = jnp.broadcast_to(xi[last:last + 1, :], (SUBLANE, LANE))
        return 0

    lax.fori_loop(0, tb // SUBLANE, tile, 0)

    ys = []
    for kb in range(S5_NB):
        h_r = s_sc[SUBLANE - 1:SUBLANE - 1 + tb, kb * S5_SPB:(kb + 1) * S5_SPB].astype(BF16)
        h_i = s_sc[SUBLANE - 1:SUBLANE - 1 + tb,
                   N_S5 + kb * S5_SPB:N_S5 + (kb + 1) * S5_SPB].astype(BF16)
        ys.append(_dot(xb[:, kb * bw:(kb + 1) * bw], kt_ref[kb])
                  + _dot(h_r, wor_ref[kb]) - _dot(h_i, woi_ref[kb]))
    for t in range(T):
        y = jnp.concatenate([ys[kb][:, t * LANE:(t + 1) * LANE] for kb in range(S5_NB)], axis=1)
        u = jnp.concatenate([x[:, kb * bw + t * LANE:kb * bw + (t + 1) * LANE]
                             for kb in range(S5_NB)], axis=1)
        y = jax.nn.gelu(y + d_ref[...] * u)
        y = y * _sigmoid(_dot(y.astype(BF16), wglu_ref[...]))
        o_ref[:, t * D_A:(t + 1) * D_A] = y.astype(o_ref.dtype)
    hr_ref[...] = car_sc[0:1, :N_S5]
    hi_ref[...] = car_sc[0:1, N_S5:]


def s5_prompt(u3, ops, d_skip, wglu, *, tb=256):
    tab, wst, ktoe, wor, woi = ops
    B, L, _ = u3.shape
    T = S5_T
    nc = L // T
    tb = min(tb, nc)
    bw = T * LANE
    u8 = jnp.transpose(u3.reshape(B, nc, T, S5_NB, LANE), (0, 1, 3, 2, 4)).reshape(B, nc, S5_NB * bw)
    const3 = lambda b, t: (0, 0, 0)
    once = dict(pipeline_mode=pl.Buffered(1))
    o8, hr, hi = pl.pallas_call(
        functools.partial(_s5_c_body, tb=tb),
        grid=(B, nc // tb),
        in_specs=[pl.BlockSpec((None, tb, S5_NB * bw), lambda b, t: (b, t, 0)),
                  pl.BlockSpec((S5_NB, bw, 2 * S5_SPB), const3, **once),
                  pl.BlockSpec((S5_NB, bw, bw), const3, **once),
                  pl.BlockSpec((S5_NB, S5_SPB, bw), const3, **once),
                  pl.BlockSpec((S5_NB, S5_SPB, bw), const3, **once),
                  pl.BlockSpec((8, SUBLANE, N_S5), const3, **once),
                  pl.BlockSpec((1, D_A), lambda b, t: (0, 0)),
                  pl.BlockSpec((D_A, D_A), lambda b, t: (0, 0))],
        out_specs=[pl.BlockSpec((None, tb, T * D_A), lambda b, t: (b, t, 0)),
                   pl.BlockSpec((None, 1, N_S5), lambda b, t: (b, 0, 0)),
                   pl.BlockSpec((None, 1, N_S5), lambda b, t: (b, 0, 0))],
        out_shape=[jax.ShapeDtypeStruct((B, nc, T * D_A), BF16),
                   jax.ShapeDtypeStruct((B, 1, N_S5), F32),
                   jax.ShapeDtypeStruct((B, 1, N_S5), F32)],
        scratch_shapes=[pltpu.VMEM((SUBLANE + tb, 2 * N_S5), F32),
                        pltpu.VMEM((SUBLANE, 2 * N_S5), F32)],
        compiler_params=_cp(("arbitrary", "arbitrary"), 56),
        name="s5_prompt",
    )(u8, wst, ktoe, wor, woi, tab, d_skip, wglu)
    return o8.reshape(B, L, D_A), hr, hi


def s5_out_weights(c_re, c_im):
    eye = jnp.eye(S5_GPB, dtype=F32)

    def blockdiag(c):
        c = c.reshape(S5_NB, S5_GPB, S5_GROUP, P_A)
        return jnp.einsum('kghp,gj->kgpjh', c, eye).reshape(S5_NB, S5_SPB, LANE).astype(BF16)

    return blockdiag(c_re), blockdiag(c_im)


def _s5_in(ub, wb_ref, kb):
    return _dot(ub[:, kb * LANE:(kb + 1) * LANE], wb_ref[kb])


def _s5_tail(h_blocks, u, wcr_ref, wci_ref, d_ref, wglu_ref):
    ys = []
    for kb in range(S5_NB):
        hr, hi = h_blocks(kb)
        ys.append(_dot(hr.astype(BF16), wcr_ref[kb]) - _dot(hi.astype(BF16), wci_ref[kb]))
    y = jnp.concatenate(ys, axis=1) + d_ref[...] * u
    y = jax.nn.gelu(y)
    return y * _sigmoid(_dot(y.astype(BF16), wglu_ref[...]))


def _s5_s_body(u_ref, h0r_ref, h0i_ref, wb_ref, wcr_ref, wci_ref, tab_ref, d_ref, wglu_ref,
               o_ref, hr_ref, hi_ref):
    u = u_ref[...]
    ub = u.astype(BF16)
    for kb in range(S5_NB):
        cols = slice(kb * S5_SPB, (kb + 1) * S5_SPB)
        bu = _s5_in(ub, wb_ref, kb)
        a_r = tab_ref[6, 0:1, cols]
        a_i = tab_ref[7, 0:1, cols]
        h0r = h0r_ref[:, cols]
        h0i = h0i_ref[:, cols]
        hr_ref[:, cols] = a_r * h0r - a_i * h0i + bu[:, :S5_SPB]
        hi_ref[:, cols] = a_r * h0i + a_i * h0r + bu[:, S5_SPB:]

    def h_blocks(kb):
        cols = slice(kb * S5_SPB, (kb + 1) * S5_SPB)
        return hr_ref[:, cols], hi_ref[:, cols]

    o_ref[...] = _s5_tail(h_blocks, u, wcr_ref, wci_ref, d_ref, wglu_ref).astype(o_ref.dtype)


def s5_sample(proj, h0r, h0i, wb, wcr, wci, tab, d_skip, wglu):
    R = proj.shape[0]
    z2 = lambda i: (0, 0)
    return pl.pallas_call(
        _s5_s_body,
        grid=(1,),
        in_specs=[pl.BlockSpec((R, D_A), z2),
                  pl.BlockSpec((R, N_S5), z2), pl.BlockSpec((R, N_S5), z2),
                  pl.BlockSpec((S5_NB, LANE, 2 * S5_SPB), lambda i: (0, 0, 0)),
                  pl.BlockSpec((S5_NB, S5_SPB, LANE), lambda i: (0, 0, 0)),
                  pl.BlockSpec((S5_NB, S5_SPB, LANE), lambda i: (0, 0, 0)),
                  pl.BlockSpec((8, SUBLANE, N_S5), lambda i: (0, 0, 0)),
                  pl.BlockSpec((1, D_A), z2), pl.BlockSpec((D_A, D_A), z2)],
        out_specs=[pl.BlockSpec((R, D_A), z2), pl.BlockSpec((R, N_S5), z2), pl.BlockSpec((R, N_S5), z2)],
        out_shape=[jax.ShapeDtypeStruct((R, D_A), BF16),
                   jax.ShapeDtypeStruct((R, N_S5), F32),
                   jax.ShapeDtypeStruct((R, N_S5), F32)],
        compiler_params=_cp(("arbitrary",)),
        name="s5_sample",
    )(proj, h0r, h0i, wb, wcr, wci, tab, d_skip, wglu)


def _gmlp_p_body(u_ref, v_ref, gv_ref, wcat_ref, bias_ref, o_ref, *, tb):
    hd = D_B // H_B
    lane_head = lax.broadcasted_iota(jnp.int32, (GM_CHUNK, D_B), 1) // hd
    wi = lax.broadcasted_iota(jnp.int32, (GM_CHUNK, H_B * GM_CHUNK), 0)
    wj = lax.broadcasted_iota(jnp.int32, (GM_CHUNK, H_B * GM_CHUNK), 1) % GM_CHUNK
    wcat = jnp.where(wi >= wj, wcat_ref[...], 0.0).astype(BF16)
    for c in range(tb // GM_CHUNK):
        rows = slice(c * GM_CHUNK, (c + 1) * GM_CHUNK)
        vn = _rms(jax.nn.gelu(v_ref[rows, :])) * gv_ref[...]
        vb = vn.astype(BF16)
        stack = jnp.concatenate(
            [jnp.where(lane_head == h, vb, jnp.zeros_like(vb)) for h in range(H_B)], axis=0)
        s = _dot(wcat, stack) + bias_ref[...]
        o_ref[rows, :] = (jax.nn.gelu(u_ref[rows, :]) * s).astype(o_ref.dtype)


def gmlp_prompt(proj3, g_v, w_s, b_s, *, tb=512):
    B, L = proj3.shape[:2]
    hd = D_B // H_B
    wcat = jnp.transpose(w_s, (1, 0, 2)).reshape(GM_CHUNK, H_B * GM_CHUNK)
    bias = jnp.repeat(b_s.T, hd, axis=1)
    const2 = lambda b, t: (0, 0)
    ub = D_A // D_B
    return pl.pallas_call(
        functools.partial(_gmlp_p_body, tb=tb),
        grid=(B, L // tb),
        in_specs=[pl.BlockSpec((None, tb, D_B), lambda b, t: (b, t, ub)),
                  pl.BlockSpec((None, tb, D_B), lambda b, t: (b, t, ub + 1)),
                  pl.BlockSpec((1, D_B), const2),
                  pl.BlockSpec((GM_CHUNK, H_B * GM_CHUNK), const2),
                  pl.BlockSpec((GM_CHUNK, D_B), const2)],
        out_specs=pl.BlockSpec((None, tb, D_B), lambda b, t: (b, t, 0)),
        out_shape=jax.ShapeDtypeStruct((B, L, D_B), BF16),
        compiler_params=_cp(("arbitrary", "arbitrary")),
        name="gmlp_prompt",
    )(proj3, proj3, g_v, wcat, bias)


def _gmlp_s_body(u_ref, v_ref, gv_ref, w0_ref, b0_ref, o_ref, vn_ref):
    vn = _rms(jax.nn.gelu(v_ref[...])) * gv_ref[...]
    vn_ref[...] = vn
    s = w0_ref[...] * vn + b0_ref[...]
    o_ref[...] = (jax.nn.gelu(u_ref[...]) * s).astype(o_ref.dtype)


def gmlp_sample(proj, g_v, w_s, b_s):
    R = proj.shape[0]
    hd = D_B // H_B
    w0 = jnp.repeat(w_s[:, 0, 0], hd).reshape(1, D_B)
    b0 = jnp.repeat(b_s[:, 0], hd).reshape(1, D_B)
    z2 = lambda i: (0, 0)
    ub = D_A // D_B
    return pl.pallas_call(
        _gmlp_s_body,
        grid=(1,),
        in_specs=[pl.BlockSpec((R, D_B), lambda i: (0, ub)),
                  pl.BlockSpec((R, D_B), lambda i: (0, ub + 1)),
                  pl.BlockSpec((1, D_B), z2), pl.BlockSpec((1, D_B), z2), pl.BlockSpec((1, D_B), z2)],
        out_specs=[pl.BlockSpec((R, D_B), z2), pl.BlockSpec((R, D_B), z2)],
        out_shape=[jax.ShapeDtypeStruct((R, D_B), BF16), jax.ShapeDtypeStruct((R, D_B), F32)],
        compiler_params=_cp(("arbitrary",)),
        name="gmlp_sample",
    )(proj, proj, g_v, w0, b0)


def _head_expand_matrix():
    r = lax.broadcasted_iota(jnp.int32, (LANE, D_C), 0)
    c = lax.broadcasted_iota(jnp.int32, (LANE, D_C), 1) // HD_C
    return jnp.where(r == c, 1.0, 0.0).astype(BF16)


def _expand_heads(v, e):
    hi, lo = _split_bf16(v)
    return _dot(hi, e) + _dot(lo, e)


def _ssd_p_body(z_ref, xbc_ref, dt_ref, cw_ref, cb_ref, dtb_ref, alog_ref, dsk_ref, gn_ref,
                o_ref, st_ref, h_ref, h_sc, cv_sc, *, nt):
    Q = SSD_CHUNK
    t = pl.program_id(1)

    @pl.when(t == 0)
    def _():
        h_sc[...] = jnp.zeros_like(h_sc)
        cv_sc[0:SUBLANE, :] = jnp.zeros((SUBLANE, D_XBC), F32)

    xbc = xbc_ref[...]
    cv_sc[SUBLANE:SUBLANE + Q, :] = xbc
    acc = cb_ref[...]
    for k in range(K_C):
        off = SUBLANE - (K_C - 1) + k
        acc = acc + cw_ref[k:k + 1, :] * cv_sc[off:off + Q, :]
    cv_sc[0:SUBLANE, :] = xbc[Q - SUBLANE:, :]
    st_ref[...] = xbc[Q - (K_C - 1):, :]
    xc = _silu(acc)
    xs = xc[:, :D_C]

    dt = _softplus(dt_ref[...] + dtb_ref[...])
    a = -jnp.exp(alog_ref[...])
    da = dt * a
    ii = lax.broadcasted_iota(jnp.int32, (Q, Q), 0)
    jj = lax.broadcasted_iota(jnp.int32, (Q, Q), 1)
    causal = ii >= jj
    tril = jnp.where(causal, 1.0, 0.0).astype(BF16)
    d0, d1 = _split_bf16(da)
    d2 = (da - d0.astype(F32) - d1.astype(F32)).astype(BF16)
    cs = _dot(tril, d0) + _dot(tril, d1) + _dot(tril, d2)
    cst = cs.T
    cs_end = cs[Q - 1:Q, :]
    e = _head_expand_matrix()
    dt_f = _expand_heads(dt, e)
    ws_f = _expand_heads(dt * jnp.exp(cs_end - cs), e)
    ecs_f = _expand_heads(jnp.exp(cs), e)
    xdt = xs * dt_f
    xw = xs * ws_f

    lane = lax.broadcasted_iota(jnp.int32, (Q, LANE), 1)
    hpg = H_C // G_C
    gw = hpg * HD_C
    ys = []
    for g in range(G_C):
        bg = xc[:, D_C + g * N_C:D_C + (g + 1) * N_C].astype(BF16)
        cg = xc[:, D_C + G_C * N_C + g * N_C:D_C + G_C * N_C + (g + 1) * N_C].astype(BF16)
        gmat = _dot_nt(cg, bg)
        hprev = h_sc[g * gw:(g + 1) * gw, :]
        yoff = _dot_nt(cg, hprev.astype(BF16)) * ecs_f[:, g * gw:(g + 1) * gw]
        snew = _dot_tn(xw[:, g * gw:(g + 1) * gw].astype(BF16), bg)
        for hp in range(hpg // 2):
            h0 = g * hpg + 2 * hp
            xpair = xdt[:, h0 * HD_C:(h0 + 2) * HD_C]
            x_lo = jnp.where(lane < HD_C, xpair, 0.0).astype(BF16)
            x_hi = jnp.where(lane >= HD_C, xpair, 0.0).astype(BF16)
            yd = None
            for hh, xh in ((h0, x_lo), (h0 + 1, x_hi)):
                seg = cs[:, hh:hh + 1] - cst[hh:hh + 1, :]
                sc = (gmat * jnp.exp(jnp.where(causal, seg, NEG_BIG))).astype(BF16)
                part = _dot(sc, xh)
                yd = part if yd is None else yd + part
            ys.append(yd + yoff[:, 2 * hp * HD_C:(2 * hp + 2) * HD_C])
        for hh in range(hpg):
            h = g * hpg + hh
            cd = jnp.exp(cst[h:h + 1, Q - 1:Q])
            rows = slice(h * HD_C, (h + 1) * HD_C)
            h_sc[rows, :] = h_sc[rows, :] * cd + snew[hh * HD_C:(hh + 1) * HD_C, :]

    y = jnp.concatenate(ys, axis=1) + dsk_ref[...] * xs
    y = y * _silu(z_ref[...])
    o_ref[...] = (_rms(y) * gn_ref[...]).astype(o_ref.dtype)

    @pl.when(t == nt - 1)
    def _():
        h_ref[...] = h_sc[...]


def _ssd_params(dt_bias, a_log, d_skip):
    pad = LANE - H_C
    dtb = jnp.pad(dt_bias, (0, pad)).reshape(1, LANE)
    alog = jnp.pad(a_log, (0, pad)).reshape(1, LANE)
    dsk = jnp.repeat(d_skip, HD_C).reshape(1, D_C)
    return dtb, alog, dsk


def ssd_prompt(z3, xd3, conv_w, conv_b, dt_bias, a_log, d_skip, g_norm):
    B, L = z3.shape[:2]
    Q = SSD_CHUNK
    nt = L // Q
    dtb, alog, dsk = _ssd_params(dt_bias, a_log, d_skip)
    const2 = lambda b, t: (0, 0)
    blk = lambda w: pl.BlockSpec((None, Q, w), lambda b, t: (b, t, 0))
    return pl.pallas_call(
        functools.partial(_ssd_p_body, nt=nt),
        grid=(B, nt),
        in_specs=[blk(D_C), blk(D_XBC),
                  pl.BlockSpec((None, Q, LANE), lambda b, t: (b, t, D_XBC // LANE)),
                  pl.BlockSpec((K_C, D_XBC), const2), pl.BlockSpec((1, D_XBC), const2),
                  pl.BlockSpec((1, LANE), const2), pl.BlockSpec((1, LANE), const2),
                  pl.BlockSpec((1, D_C), const2), pl.BlockSpec((1, D_C), const2)],
        out_specs=[blk(D_C),
                   pl.BlockSpec((None, K_C - 1, D_XBC), lambda b, t: (b, 0, 0)),
                   pl.BlockSpec((None, H_C * HD_C, N_C), lambda b, t: (b, 0, 0))],
        out_shape=[jax.ShapeDtypeStruct((B, L, D_C), BF16),
                   jax.ShapeDtypeStruct((B, K_C - 1, D_XBC), F32),
                   jax.ShapeDtypeStruct((B, H_C * HD_C, N_C), F32)],
        scratch_shapes=[pltpu.VMEM((H_C * HD_C, N_C), F32), pltpu.VMEM((SUBLANE + Q, D_XBC), F32)],
        compiler_params=_cp(("arbitrary", "arbitrary")),
        name="ssd_prompt",
    )(z3, xd3, xd3, conv_w, conv_b.reshape(1, D_XBC), dtb, alog, dsk, g_norm.reshape(1, D_C))


def _ssd_s_body(z_ref, xbc_ref, dt_ref, buf_ref, h0_ref, cw_ref, cb_ref, dtb_ref, alog_ref,
                dsk_ref, gn_ref, *rest, tbatch, nsteps, has_prev):
    o_ref, hn_ref, xs_sc, bc_sc, xt_sc, at_sc, y_sc = rest[1:] if has_prev else rest
    s = pl.program_id(0)
    R = xs_sc.shape[0]

    @pl.when(s == 0)
    def _():
        acc = cb_ref[...] + cw_ref[K_C - 1:K_C, :] * xbc_ref[...]
        for k in range(K_C - 1):
            acc = acc + cw_ref[k:k + 1, :] * buf_ref[k]
        xc = _silu(acc)
        xs = xc[:, :D_C]
        xs_sc[...] = xs
        bc_sc[...] = xc[:, D_C:]
        dt = _softplus(dt_ref[...] + dtb_ref[...])
        e = _head_expand_matrix()
        dt_f = _expand_heads(dt, e)
        da_f = jnp.exp(_expand_heads(dt * (-jnp.exp(alog_ref[...])), e))
        for t_sc, v in ((xt_sc, xs * dt_f), (at_sc, da_f)):
            hi, lo = _split_bf16(v.T)
            t_sc[:, :R] = hi
            t_sc[:, R:] = lo

    gw = (H_C // G_C) * HD_C
    kk = lax.broadcasted_iota(jnp.int32, (2 * R, LANE), 0) % R
    r0 = pl.multiple_of(s * tbatch, tbatch)
    bc8 = bc_sc[pl.ds(r0, tbatch), :]
    ti = lax.broadcasted_iota(jnp.int32, (tbatch, gw), 0)
    ytile = [jnp.zeros((tbatch, gw), F32) for _ in range(G_C)]
    for i in range(tbatch):
        b = s * tbatch + i
        onehot = jnp.where(kk == b, 1.0, 0.0).astype(BF16)
        xb = _dot(xt_sc[...], onehot)
        ab = _dot(at_sc[...], onehot)
        for g in range(G_C):
            rows = slice(g * gw, (g + 1) * gw)
            brow = bc8[i:i + 1, g * N_C:(g + 1) * N_C]
            hn = h0_ref[i, rows, :] * ab[rows, :] + xb[rows, :] * brow
            hn_ref[i, rows, :] = hn
            c8 = bc8[:, G_C * N_C + g * N_C:G_C * N_C + (g + 1) * N_C]
            yg = _dot_nt(c8.astype(BF16), hn.astype(BF16))
            ytile[g] = jnp.where(ti == i, yg, ytile[g])
    for g in range(G_C):
        y_sc[pl.ds(r0, tbatch), g * gw:(g + 1) * gw] = ytile[g]

    @pl.when(s == nsteps - 1)
    def _():
        y = y_sc[...] + dsk_ref[...] * xs_sc[...]
        y = y * _silu(z_ref[...])
        o_ref[...] = (_rms(y) * gn_ref[...]).astype(o_ref.dtype)


def ssd_sample(z, xd, buf, h0_all, l, hn_all, conv_w, conv_b, dt_bias, a_log, d_skip, g_norm,
               *, tbatch=8):
    R = z.shape[0]
    nsteps = R // tbatch
    dtb, alog, dsk = _ssd_params(dt_bias, a_log, d_skip)
    z2 = lambda s: (0, 0)
    hw = H_C * HD_C
    st_spec = pl.BlockSpec((None, tbatch, hw, N_C), lambda s: (l, s, 0, 0))
    in_specs = [pl.BlockSpec((R, D_C), z2), pl.BlockSpec((R, D_XBC), z2),
                pl.BlockSpec((R, LANE), lambda s: (0, D_XBC // LANE)),
                pl.BlockSpec((K_C - 1, R, D_XBC), lambda s: (0, 0, 0)),
                st_spec,
                pl.BlockSpec((K_C, D_XBC), z2), pl.BlockSpec((1, D_XBC), z2),
                pl.BlockSpec((1, LANE), z2), pl.BlockSpec((1, LANE), z2),
                pl.BlockSpec((1, D_C), z2), pl.BlockSpec((1, D_C), z2)]
    args = [z, xd, xd, buf, h0_all, conv_w, conv_b.reshape(1, D_XBC), dtb, alog, dsk,
            g_norm.reshape(1, D_C)]
    aliases = {}
    if hn_all is not None:
        aliases = {len(args): 1}
        in_specs.append(pl.BlockSpec(memory_space=pl.ANY))
        args.append(hn_all)
    return pl.pallas_call(
        functools.partial(_ssd_s_body, tbatch=tbatch, nsteps=nsteps, has_prev=hn_all is not None),
        grid=(nsteps,),
        in_specs=in_specs,
        out_specs=[pl.BlockSpec((R, D_C), z2), st_spec],
        out_shape=[jax.ShapeDtypeStruct((R, D_C), BF16),
                   jax.ShapeDtypeStruct(h0_all.shape, F32)],
        scratch_shapes=[pltpu.VMEM((R, D_C), F32), pltpu.VMEM((R, 2 * G_C * N_C), F32),
                        pltpu.VMEM((D_C, 2 * R), BF16), pltpu.VMEM((D_C, 2 * R), BF16),
                        pltpu.VMEM((R, D_C), F32)],
        input_output_aliases=aliases,
        compiler_params=_cp(("arbitrary",)),
        name="ssd_sample",
    )(*args)


def _up_p_body(x_ref, wa_ref, wb_ref, cwa_ref, cwb_ref, cba_ref, cbb_ref,
               o_ref, sa_ref, sb_ref, wa_sc, wb_sc, ca_sc, cb_sc, *, tm, tiles_per_seq):
    i = pl.program_id(1)

    @pl.when(i == 0)
    def _():
        wa_sc[...] = wa_ref[...].astype(BF16)
        wb_sc[...] = wb_ref[...].astype(BF16)

    @pl.when(i % tiles_per_seq == 0)
    def _():
        ca_sc[0:SUBLANE, :] = jnp.zeros((SUBLANE, ca_sc.shape[1]), F32)
        cb_sc[0:SUBLANE, :] = jnp.zeros((SUBLANE, cb_sc.shape[1]), F32)

    x = x_ref[...]

    def half(w_sc, c_sc, cw_ref, cb_ref, st_ref):
        up = _dot(x, w_sc[...])
        c_sc[SUBLANE:SUBLANE + tm, :] = up
        acc = cb_ref[...]
        for k in range(K_F):
            off = SUBLANE - (K_F - 1) + k
            acc = acc + cw_ref[k:k + 1, :] * c_sc[off:off + tm, :]
        c_sc[0:SUBLANE, :] = up[tm - SUBLANE:, :]
        st_ref[...] = up[tm - (K_F - 1):, :]
        return acc

    a = half(wa_sc, ca_sc, cwa_ref, cba_ref, sa_ref)
    b = half(wb_sc, cb_sc, cwb_ref, cbb_ref, sb_ref)
    o_ref[...] = (_silu(a) * b).astype(o_ref.dtype)


def up_prompt(h2, w_up, l, conv_w, conv_b, *, seq_len, tm=1024, tn=512):
    M, D = h2.shape
    B = M // seq_len
    tiles_per_seq = seq_len // tm
    nj = D_FF // tn
    cb = conv_b.reshape(1, 2 * D_FF)
    body = functools.partial(_up_p_body, tm=tm, tiles_per_seq=tiles_per_seq)
    st_spec = pl.BlockSpec((None, K_F - 1, tn), lambda j, i: (i // tiles_per_seq, 0, j))
    return pl.pallas_call(
        body,
        grid=(nj, M // tm),
        in_specs=[pl.BlockSpec((tm, D), lambda j, i: (i, 0)),
                  pl.BlockSpec((None, D, tn), lambda j, i: (l, 0, j)),
                  pl.BlockSpec((None, D, tn), lambda j, i: (l, 0, nj + j)),
                  pl.BlockSpec((K_F, tn), lambda j, i: (0, j)),
                  pl.BlockSpec((K_F, tn), lambda j, i: (0, nj + j)),
                  pl.BlockSpec((1, tn), lambda j, i: (0, j)),
                  pl.BlockSpec((1, tn), lambda j, i: (0, nj + j))],
        out_specs=[pl.BlockSpec((tm, tn), lambda j, i: (i, j)), st_spec, st_spec],
        out_shape=[jax.ShapeDtypeStruct((M, D_FF), BF16),
                   jax.ShapeDtypeStruct((B, K_F - 1, D_FF), F32),
                   jax.ShapeDtypeStruct((B, K_F - 1, D_FF), F32)],
        scratch_shapes=[pltpu.VMEM((D, tn), BF16), pltpu.VMEM((D, tn), BF16),
                        pltpu.VMEM((SUBLANE + tm, tn), F32), pltpu.VMEM((SUBLANE + tm, tn), F32)],
        compiler_params=_cp(("arbitrary", "arbitrary")),
        name="up_prompt",
    )(h2, w_up, w_up, conv_w, conv_w, cb, cb)


def _up_s_body(x_ref, wa_ref, wb_ref, bufa_ref, bufb_ref, cwa_ref, cwb_ref, cba_ref, cbb_ref,
               *rest, has_prev):
    o_ref, na_ref, nb_ref = rest[2:] if has_prev else rest
    x = x_ref[...]

    def half(w_ref, buf_ref, cw_ref, cb_ref, new_ref):
        up = _dot(x, w_ref[...].astype(BF16))
        acc = cb_ref[...] + cw_ref[K_F - 1:K_F, :] * up
        for k in range(K_F - 1):
            acc = acc + cw_ref[k:k + 1, :] * buf_ref[:, k, :]
        for k in range(1, K_F - 1):
            new_ref[:, k - 1, :] = buf_ref[:, k, :]
        new_ref[:, K_F - 2, :] = up
        return acc

    a = half(wa_ref, bufa_ref, cwa_ref, cba_ref, na_ref)
    b = half(wb_ref, bufb_ref, cwb_ref, cbb_ref, nb_ref)
    o_ref[...] = (_silu(a) * b).astype(o_ref.dtype)


def up_sample(h2, w_up, l, buf_all, new_ab, conv_w, conv_b, *, tn=512):
    R, D = h2.shape
    nj = D_FF // tn
    cb = conv_b.reshape(1, 2 * D_FF)
    st = lambda off: pl.BlockSpec((None, R, K_F - 1, tn), lambda j: (l, 0, 0, off + j))
    in_specs = [pl.BlockSpec((R, D), lambda j: (0, 0)),
                pl.BlockSpec((None, D, tn), lambda j: (l, 0, j)),
                pl.BlockSpec((None, D, tn), lambda j: (l, 0, nj + j)),
                st(0), st(nj),
                pl.BlockSpec((K_F, tn), lambda j: (0, j)),
                pl.BlockSpec((K_F, tn), lambda j: (0, nj + j)),
                pl.BlockSpec((1, tn), lambda j: (0, j)),
                pl.BlockSpec((1, tn), lambda j: (0, nj + j))]
    args = [h2, w_up, w_up, buf_all, buf_all, conv_w, conv_w, cb, cb]
    aliases = {}
    if new_ab is not None:
        aliases = {len(args): 1, len(args) + 1: 2}
        in_specs += [pl.BlockSpec(memory_space=pl.ANY)] * 2
        args += list(new_ab)
    half_shape = jax.ShapeDtypeStruct((DEPTH, R, K_F - 1, D_FF), F32)
    act, na, nb = pl.pallas_call(
        functools.partial(_up_s_body, has_prev=new_ab is not None),
        grid=(nj,),
        in_specs=in_specs,
        out_specs=[pl.BlockSpec((R, tn), lambda j: (0, j)), st(0), st(0)],
        out_shape=[jax.ShapeDtypeStruct((R, D_FF), BF16), half_shape, half_shape],
        input_output_aliases=aliases,
        compiler_params=_cp(("arbitrary",)),
        name="up_sample",
    )(*args)
    return act, (na, nb)


def _layer(x3, mods, l, p, state, *, seq_len, tm, tn_merge, tn_down, tm_norm):
    B, L, D = x3.shape
    M = B * L
    sh_m, sc_m, gt_m, sh_f, sc_f, gt_f = mods
    prompt = state is None
    w_buffers = 1 if M > tm else 2

    h = norm_mod(x3, p['g_mix'][l].reshape(1, D), sc_m, sh_m, tm=tm_norm).reshape(M, D)

    def proj(col0, n, name):
        return matmul([h], [(p['w_in_t'], l, col0, True)], [], [(n, F32)], _epi_store,
                      x_of_w=(0,), tm=tm, tn=n, name=name, vmem_mb=56, w_buffers=1)[0]

    uav = proj(OFF_UA, D_A + 2 * D_B, "proj_uav")
    zc = proj(OFF_Z, D_C, "proj_z")
    xd = proj(OFF_XBC, D_XBC + LANE, "proj_xd")

    tab, wb, bbr, bbi = s5_prep(p['lam_re'][l], p['lam_im'][l], p['log_dt'][l],
                                p['b_re'][l], p['b_im'][l])
    d_a = p['s5_d'][l].reshape(1, D_A)
    wglu = p['w_glu'][l].astype(BF16)
    g_v = p['g_v'][l].reshape(1, D_B)

    if prompt:
        s5_ops = s5_chunk_prep(p['lam_re'][l], p['lam_im'][l], p['log_dt'][l], bbr, bbi,
                               p['c_re'][l], p['c_im'][l])
        o_a, s5r, s5i = s5_prompt(uav.reshape(B, L, -1)[..., :D_A], s5_ops, d_a, wglu)
        o_b = gmlp_prompt(uav.reshape(B, L, -1), g_v, p['w_s'][l], p['b_s'][l])
        v_rows = None
        o_c, convc, ssm = ssd_prompt(zc.reshape(B, L, -1), xd.reshape(B, L, -1),
                                     p['ssd_conv_w'][l], p['ssd_conv_b'][l], p['dt_bias'][l],
                                     p['a_log'][l], p['ssd_d'][l], p['ssd_g'][l])
        s5r = s5r.reshape(B, G_A, P_A)
        s5i = s5i.reshape(B, G_A, P_A)
    else:
        s5_re0, s5_im0, ssm_all, ssm_new_all, convc0, ffn_all, ffn_new_ab = state
        wcr, wci = s5_out_weights(p['c_re'][l], p['c_im'][l])
        o_a, s5r, s5i = s5_sample(uav, s5_re0.reshape(M, N_S5), s5_im0.reshape(M, N_S5),
                                  wb, wcr, wci, tab, d_a, wglu)
        o_b, v_rows = gmlp_sample(uav, g_v, p['w_s'][l], p['b_s'][l])
        o_c, ssm = ssd_sample(zc, xd, jnp.transpose(convc0, (1, 0, 2)),
                              ssm_all, l, ssm_new_all,
                              p['ssd_conv_w'][l], p['ssd_conv_b'][l], p['dt_bias'][l],
                              p['a_log'][l], p['ssd_d'][l], p['ssd_g'][l])
        convc = jnp.concatenate([convc0[:, 1:], xd[:, None, :D_XBC]], axis=1)
        s5r = s5r.reshape(M, G_A, P_A)
        s5i = s5i.reshape(M, G_A, P_A)
    o_a = o_a.reshape(M, D_A)
    o_b = o_b.reshape(M, D_B)
    o_c = o_c.reshape(M, D_C)
    if prompt:
        ssm = ssm.reshape(B, H_C, HD_C, N_C)

    w_t = p['w_in_t']
    merged = matmul(
        [h, o_a, o_b, o_c],
        [(p['w_pa'], l, 0, False), (p['w_pb'], l, 0, False), (p['w_pc'], l, 0, False),
         (w_t, l, OFF_GATES, True), (w_t, l, OFF_GATES + D, True), (w_t, l, OFF_GATES + 2 * D, True)],
        [], [(D, BF16)], _epi_merge, x_of_w=(1, 2, 3, 0, 0, 0), tm=tm, tn=tn_merge, name="merge",
        vmem_mb=56, w_buffers=w_buffers)[0]

    def residual(x_res, act_in, w3, gt, tm_r, tn, name, vmem_mb):
        return matmul([act_in], [(w3, l, 0, False)],
                      [(x_res, _tile_spec(tm_r, tn)),
                       (gt, _gate_spec(gt, tm_r, tn, max(seq_len // tm_r, 1)))],
                      [(D, F32)], _epi_residual, x_of_w=(0,), tm=tm_r, tn=tn, name=name,
                      vmem_mb=vmem_mb, w_buffers=w_buffers)[0]

    x2, h2 = out_norm(merged, p['w_out'], l, x3.reshape(M, D), gt_m, p['g_ffn'][l].reshape(1, D),
                      sc_f, sh_f, tm=min(tm, 256), seq_len=seq_len)
    if prompt:
        act, st_a, st_b = up_prompt(h2, p['w_up'], l, p['ffn_conv_w'][l], p['ffn_conv_b'][l],
                                    seq_len=seq_len, tm=tm)
        convf = jnp.concatenate([st_a, st_b], axis=-1)
    else:
        act, convf = up_sample(h2, p['w_up'], l, ffn_all, ffn_new_ab,
                               p['ffn_conv_w'][l], p['ffn_conv_b'][l])

    x2 = residual(x2, act, p['w_down'], gt_f, min(tm, 256), tn_down, "down_proj", 56)
    return x2.reshape(B, L, D), s5r, s5i, ssm, convc, convf, v_rows


def kernel(x_prompt, x_sample, c_prompt, c_sample, state_s5_re, state_s5_im, state_ssm, state_ssd_conv, state_ffn_conv, w_mod, b_mod, g_mix, w_in, s5_lam_re, s5_lam_im, s5_log_dt, s5_b_re, s5_b_im, s5_c_re, s5_c_im, s5_d, s5_w_glu, gm_g_v, gm_w_s, gm_b_s, ssd_conv_w, ssd_conv_b, ssd_dt_bias, ssd_a_log, ssd_d, ssd_g_norm, w_pa, w_pb, w_pc, w_out, g_ffn, ffn_w_up, ffn_conv_w, ffn_conv_b, ffn_w_down, g_final):
    p = {
        'g_mix': g_mix, 'w_in_t': jnp.swapaxes(w_in, 1, 2),
        'lam_re': s5_lam_re, 'lam_im': s5_lam_im, 'log_dt': s5_log_dt,
        'b_re': s5_b_re, 'b_im': s5_b_im, 'c_re': s5_c_re, 'c_im': s5_c_im,
        's5_d': s5_d, 'w_glu': s5_w_glu,
        'g_v': gm_g_v, 'w_s': gm_w_s, 'b_s': gm_b_s,
        'ssd_conv_w': ssd_conv_w, 'ssd_conv_b': ssd_conv_b, 'dt_bias': ssd_dt_bias,
        'a_log': ssd_a_log, 'ssd_d': ssd_d, 'ssd_g': ssd_g_norm,
        'w_pa': w_pa, 'w_pb': w_pb, 'w_pc': w_pc, 'w_out': w_out,
        'g_ffn': g_ffn, 'w_up': ffn_w_up, 'ffn_conv_w': ffn_conv_w,
        'ffn_conv_b': ffn_conv_b, 'w_down': ffn_w_down,
    }
    bp, seq, D = x_prompt.shape
    bs = x_sample.shape[0]

    n_c = bs + bp
    pad = (-n_c) % SUBLANE
    c_all = jnp.concatenate([c_sample, c_prompt, jnp.zeros((pad, D), F32)], axis=0)
    mod = mod_all(c_all, w_mod, b_mod)

    xp = x_prompt
    xs = x_sample.reshape(1, bs, D)
    outs_p = [[] for _ in range(5)]
    outs_s = [[] for _ in range(6)]
    ssm_all = state_ssm.reshape(DEPTH, bs, H_C * HD_C, N_C)
    ssm_new_all = None
    ffn_new_ab = None
    for l in range(DEPTH):
        mods_s = [m[None] for m in jnp.split(mod[l, :bs], 6, axis=-1)]
        mods_p = [m[:, None, :] for m in jnp.split(mod[l, bs:bs + bp], 6, axis=-1)]
        xp, *st_p = _layer(xp, mods_p, l, p, None, seq_len=seq, tm=1024, tn_merge=512, tn_down=1024,
                           tm_norm=512)
        for acc, v in zip(outs_p, st_p[:5]):
            acc.append(v)
        state = (state_s5_re[l], state_s5_im[l], ssm_all, ssm_new_all, state_ssd_conv[l],
                 state_ffn_conv, ffn_new_ab)
        xs, *st_s = _layer(xs, mods_s, l, p, state, seq_len=1, tm=bs, tn_merge=256, tn_down=256,
                           tm_norm=bs)
        ssm_new_all = st_s[2]
        ffn_new_ab = st_s[4]
        st_s[5] = st_s[5].reshape(bs, 1, D_B)
        for acc, v in zip(outs_s, st_s):
            acc.append(v)

    g_fin = g_final.reshape(1, D)
    y_prompt = final_norm(xp, g_fin, tm=512)
    y_sample = final_norm(xs, g_fin, tm=bs).reshape(bs, 1, D)
    whole = {2: ssm_new_all.reshape(DEPTH, bs, H_C, HD_C, N_C),
             4: jnp.concatenate(ffn_new_ab, axis=-1)}
    outs_s = [whole[k] if k in whole else jnp.stack(v) for k, v in enumerate(outs_s)]
    return (y_prompt, y_sample, *[jnp.stack(v) for v in outs_p], *outs_s)
```

```python
import functools

import jax
import jax.numpy as jnp
from jax import lax
from jax.experimental import pallas as pl
from jax.experimental.pallas import tpu as pltpu

F32 = jnp.float32
BF16 = jnp.bfloat16

D_MODEL = 2048
DEPTH = 2
D_A = 512
S5_GROUP = 16
G_A = D_A // S5_GROUP
P_A = 64
N_S5 = G_A * P_A
S5_GPB = 128 // S5_GROUP
S5_NB = D_A // 128
S5_SPB = S5_GPB * P_A
S5_T = 8
D_B = 512
H_B = 8
GM_CHUNK = 128
D_C = 1024
HD_C = 64
H_C = D_C // HD_C
N_C = 128
G_C = 2
K_C = 4
SSD_CHUNK = 128
D_XBC = D_C + 2 * G_C * N_C
D_FF = 5632
K_F = 3
EPS = 1e-6

OFF_UA = 0
OFF_Z = D_A + 2 * D_B
OFF_XBC = OFF_Z + D_C
OFF_DT = OFF_XBC + D_XBC
OFF_GATES = OFF_DT + H_C

LANE = 128
SUBLANE = 8
NEG_BIG = -1e30


def _cp(sem, vmem_mb=48):
    return pltpu.CompilerParams(dimension_semantics=sem, vmem_limit_bytes=vmem_mb * 1024 * 1024)


def _sigmoid(x):
    return 0.5 * jnp.tanh(0.5 * x) + 0.5


def _silu(x):
    return x * _sigmoid(x)


def _softplus(x):
    return jnp.maximum(x, 0.0) + jnp.log1p(jnp.exp(-jnp.abs(x)))


def _rms(x):
    return x * lax.rsqrt(jnp.mean(x * x, axis=-1, keepdims=True) + EPS)


def _split_bf16(x):
    hi = x.astype(BF16)
    lo = (x - hi.astype(F32)).astype(BF16)
    return hi, lo


def _dot(a, b):
    return jnp.dot(a, b, preferred_element_type=F32)


def _dot_nt(a, b):
    return lax.dot_general(a, b, (((1,), (1,)), ((), ())), preferred_element_type=F32)


def _dot_tn(a, b):
    return lax.dot_general(a, b, (((0,), (0,)), ((), ())), preferred_element_type=F32)


def _mod_body(c_ref, w_ref, b_ref, o_ref):
    a = _silu(c_ref[...]).astype(BF16)
    o_ref[...] = _dot(a, w_ref[...].astype(BF16)) + b_ref[...]


def mod_all(c_all, w_mod, b_mod, *, tn=1024):
    R, D = c_all.shape
    N = w_mod.shape[-1]
    return pl.pallas_call(
        _mod_body,
        grid=(DEPTH, N // tn),
        in_specs=[pl.BlockSpec((R, D), lambda l, j: (0, 0)),
                  pl.BlockSpec((None, D, tn), lambda l, j: (l, 0, j)),
                  pl.BlockSpec((None, 1, tn), lambda l, j: (l, 0, j))],
        out_specs=pl.BlockSpec((None, R, tn), lambda l, j: (l, 0, j)),
        out_shape=jax.ShapeDtypeStruct((DEPTH, R, N), F32),
        compiler_params=_cp(("arbitrary", "arbitrary")),
        name="mod_all",
    )(c_all, w_mod, b_mod.reshape(DEPTH, 1, N))


def _norm_mod_body(x_ref, g_ref, sc_ref, sh_ref, o_ref):
    y = _rms(x_ref[...]) * g_ref[...]
    o_ref[...] = (y * (1.0 + sc_ref[...]) + sh_ref[...]).astype(o_ref.dtype)


def norm_mod(x3, g, sc3, sh3, *, tm):
    B, L, D = x3.shape
    per_row = sc3.shape[1] != 1
    ts = tm if per_row else 1
    sidx = (lambda b, i: (b, i, 0)) if per_row else (lambda b, i: (b, 0, 0))
    return pl.pallas_call(
        _norm_mod_body,
        grid=(B, L // tm),
        in_specs=[pl.BlockSpec((None, tm, D), lambda b, i: (b, i, 0)),
                  pl.BlockSpec((1, D), lambda b, i: (0, 0)),
                  pl.BlockSpec((None, ts, D), sidx),
                  pl.BlockSpec((None, ts, D), sidx)],
        out_specs=pl.BlockSpec((None, tm, D), lambda b, i: (b, i, 0)),
        out_shape=jax.ShapeDtypeStruct((B, L, D), BF16),
        compiler_params=_cp(("arbitrary", "arbitrary")),
        name="norm_mod",
    )(x3, g, sc3, sh3)


def _mm_body(*refs, x_of_w, w_is_t, n_x, n_e, n_o, epilogue):
    n_w = len(x_of_w)
    xs = refs[:n_x]
    ws = refs[n_x:n_x + n_w]
    es = refs[n_x + n_w:n_x + n_w + n_e]
    outs = refs[n_x + n_w + n_e:n_x + n_w + n_e + n_o]
    wsc = refs[n_x + n_w + n_e + n_o:]

    @pl.when(pl.program_id(1) == 0)
    def _():
        for w, s, is_t in zip(ws, wsc, w_is_t):
            s[...] = (w[...].T if is_t else w[...]).astype(BF16)

    accs = [_dot(xs[xi][...], s[...]) for xi, s in zip(x_of_w, wsc)]
    epilogue(accs, es, outs)


def matmul(xs, ws, extras, outs, epilogue, *, x_of_w, tm, tn, name, vmem_mb=48, w_buffers=2):
    M = xs[0].shape[0]
    N = outs[0][0]
    in_specs = [pl.BlockSpec((tm, x.shape[1]), lambda j, i: (i, 0)) for x in xs]
    args = list(xs)
    scratch = []
    for w3, l, col0, is_t in ws:
        if is_t:
            K = w3.shape[2]
            assert col0 % SUBLANE == 0
            in_specs.append(pl.BlockSpec((None, pl.Element(tn), pl.Element(K)), functools.partial(
                lambda j, i, l, c: (l, (c + j * (tn // SUBLANE)) * SUBLANE, 0),
                l=l, c=col0 // SUBLANE), pipeline_mode=pl.Buffered(w_buffers)))
        else:
            K = w3.shape[1]
            assert col0 % tn == 0
            in_specs.append(pl.BlockSpec((None, K, tn), functools.partial(
                lambda j, i, l, c: (l, 0, c + j), l=l, c=col0 // tn),
                pipeline_mode=pl.Buffered(w_buffers)))
        args.append(w3)
        scratch.append(pltpu.VMEM((K, tn), BF16))
    for arr, spec in extras:
        in_specs.append(spec)
        args.append(arr)
    body = functools.partial(_mm_body, x_of_w=tuple(x_of_w), w_is_t=tuple(w[3] for w in ws),
                             n_x=len(xs), n_e=len(extras), n_o=len(outs), epilogue=epilogue)
    res = pl.pallas_call(
        body,
        grid=(N // tn, M // tm),
        in_specs=in_specs,
        out_specs=[pl.BlockSpec((tm, tn), lambda j, i: (i, j)) for _ in outs],
        out_shape=[jax.ShapeDtypeStruct((M, n), dt) for n, dt in outs],
        scratch_shapes=scratch,
        compiler_params=_cp(("arbitrary", "arbitrary"), vmem_mb),
        name=name,
    )(*args)
    return res


def _epi_store(accs, es, outs):
    outs[0][...] = accs[0].astype(outs[0].dtype)


def _epi_merge(accs, es, outs):
    pa, pb, pc, ga, gb, gc = accs
    m = _sigmoid(ga) * pa + _sigmoid(gb) * pb + _sigmoid(gc) * pc
    outs[0][...] = m.astype(outs[0].dtype)


def _res_norm_body(*refs, cast_w, modulated, emit_x):
    a_ref, w_ref, res_ref, gt_ref, g_ref = refs[:5]
    refs = refs[5:]
    if modulated:
        sc_ref, sh_ref = refs[:2]
        refs = refs[2:]
    if emit_x:
        x_ref = refs[0]
        refs = refs[1:]
    h_ref = refs[0]
    if cast_w:
        w_sc = refs[1]

        @pl.when(pl.program_id(0) == 0)
        def _():
            w_sc[...] = w_ref[...].astype(BF16)

        w = w_sc[...]
    else:
        w = w_ref[...]
    x = res_ref[...] + gt_ref[...] * _dot(a_ref[...], w)
    if emit_x:
        x_ref[...] = x
    y = _rms(x) * g_ref[...]
    if modulated:
        y = y * (1.0 + sc_ref[...]) + sh_ref[...]
    h_ref[...] = y.astype(h_ref.dtype)


def residual_norm(act, w3, l, res, gt, g, mod, *, tm, seq_len, emit_x, name):
    M, K = act.shape
    D = res.shape[1]
    tiles_per_seq = max(seq_len // tm, 1)
    cast_w = w3.dtype != BF16

    def mod_spec(m3):
        if m3.shape[1] == 1:
            return pl.BlockSpec((None, 1, D), lambda i: (i // tiles_per_seq, 0, 0))
        return pl.BlockSpec((None, tm, D), lambda i: (0, i, 0))

    row = lambda w: pl.BlockSpec((tm, w), lambda i: (i, 0))
    in_specs = [row(K),
                pl.BlockSpec((None, K, D), lambda i: (l, 0, 0), pipeline_mode=pl.Buffered(1)),
                row(D), mod_spec(gt), pl.BlockSpec((1, D), lambda i: (0, 0))]
    args = [act, w3, res, gt, g]
    if mod is not None:
        in_specs += [mod_spec(mod[0]), mod_spec(mod[1])]
        args += list(mod)
    out_specs = [row(D)] * (2 if emit_x else 1)
    out_shape = ([jax.ShapeDtypeStruct((M, D), F32)] if emit_x else []) + [
        jax.ShapeDtypeStruct((M, D), BF16 if mod is not None else F32)]
    outs = pl.pallas_call(
        functools.partial(_res_norm_body, cast_w=cast_w, modulated=mod is not None, emit_x=emit_x),
        grid=(M // tm,),
        in_specs=in_specs,
        out_specs=out_specs,
        out_shape=out_shape,
        scratch_shapes=[pltpu.VMEM((K, D), BF16)] if cast_w else [],
        compiler_params=_cp(("arbitrary",), 56),
        name=name,
    )(*args)
    return (outs[0], outs[1]) if emit_x else (None, outs[0])


def _s5_prep_body(lrf_ref, lif_ref, ldf_ref, lrr_ref, lir_ref, ldr_ref, bre_ref, bim_ref,
                  tab_ref, bbr_ref, bbi_ref):
    dtf = jnp.exp(ldf_ref[...])
    _power_tables(tab_ref, lrf_ref[...] * dtf, lif_ref[...] * dtf, 1)
    dtr = jnp.exp(ldr_ref[...])
    lr = lrr_ref[...]
    li = lir_ref[...]
    m1 = jnp.exp(lr * dtr)
    ar = m1 * jnp.cos(li * dtr)
    ai = m1 * jnp.sin(li * dtr)
    den = lr * lr + li * li
    nr = ar - 1.0
    kr = (nr * lr + ai * li) / den
    ki = (ai * lr - nr * li) / den
    bre = bre_ref[...]
    bim = bim_ref[...]
    bbr_ref[...] = kr * bre - ki * bim
    bbi_ref[...] = kr * bim + ki * bre


def s5_prep(lam_re, lam_im, log_dt, b_re, b_im):
    lrf = lam_re.reshape(1, N_S5)
    lif = lam_im.reshape(1, N_S5)
    ldf = jnp.repeat(log_dt, P_A).reshape(1, N_S5)
    lrr = jnp.repeat(lam_re, S5_GROUP, axis=0)
    lir = jnp.repeat(lam_im, S5_GROUP, axis=0)
    ldr = jnp.repeat(log_dt, S5_GROUP).reshape(D_A, 1)
    bre = jnp.transpose(b_re, (0, 2, 1)).reshape(D_A, P_A)
    bim = jnp.transpose(b_im, (0, 2, 1)).reshape(D_A, P_A)
    tab, bbr, bbi = pl.pallas_call(
        _s5_prep_body,
        out_shape=[jax.ShapeDtypeStruct((8, SUBLANE, N_S5), F32),
                   jax.ShapeDtypeStruct((D_A, P_A), F32),
                   jax.ShapeDtypeStruct((D_A, P_A), F32)],
        name="s5_prep",
    )(lrf, lif, ldf, lrr, lir, ldr, bre, bim)
    eye = jnp.eye(S5_GPB, dtype=F32)

    def blockdiag(m):
        m = m.reshape(S5_NB, S5_GPB, S5_GROUP, P_A)
        return jnp.einsum('kghp,gj->kghjp', m, eye).reshape(S5_NB, LANE, S5_SPB)

    wb = jnp.concatenate([blockdiag(bbr), blockdiag(bbi)], axis=2).astype(BF16)
    return tab, wb, bbr, bbi


def _power_tables(tab_ref, lr_dt, li_dt, stride):
    n_lanes = lr_dt.shape[1]
    row = lax.broadcasted_iota(jnp.int32, (SUBLANE, n_lanes), 0)
    n = ((row + 1) * stride).astype(F32)
    mag = jnp.exp(n * lr_dt)
    ang = n * li_dt
    pr = mag * jnp.cos(ang)
    pi = mag * jnp.sin(ang)
    for k, d in enumerate((1, 2, 4)):
        keep = row >= d
        tab_ref[2 * k] = jnp.where(keep, pr[d - 1:d, :], 0.0)
        tab_ref[2 * k + 1] = jnp.where(keep, pi[d - 1:d, :], 0.0)
    tab_ref[6] = pr
    tab_ref[7] = pi


def _s5_chunk_prep_body(lrf_ref, lif_ref, ldf_ref, lrr_ref, lir_ref, ldr_ref, bbr_ref, bbi_ref,
                        cr_ref, ci_ref, tab_ref, wst_ref, kt_ref, wor_ref, woi_ref):
    T = S5_T
    dtf = jnp.exp(ldf_ref[...])
    _power_tables(tab_ref, lrf_ref[...] * dtf, lif_ref[...] * dtf, T)
    dtr = jnp.exp(ldr_ref[...])
    lr = lrr_ref[...] * dtr
    li = lir_ref[...] * dtr
    bbr = bbr_ref[...]
    bbi = bbi_ref[...]
    cr = cr_ref[...]
    ci = ci_ref[...]
    spread = jnp.where(lax.broadcasted_iota(jnp.int32, (P_A, S5_SPB), 0)
                       == lax.broadcasted_iota(jnp.int32, (P_A, S5_SPB), 1) % P_A, 1.0, 0.0).astype(BF16)
    spread_t = jnp.where(lax.broadcasted_iota(jnp.int32, (S5_SPB, P_A), 0) % P_A
                         == lax.broadcasted_iota(jnp.int32, (S5_SPB, P_A), 1), 1.0, 0.0).astype(BF16)
    in_mask = (lax.broadcasted_iota(jnp.int32, (LANE, S5_SPB), 0) // S5_GROUP
               == lax.broadcasted_iota(jnp.int32, (LANE, S5_SPB), 1) // P_A)
    out_mask = (lax.broadcasted_iota(jnp.int32, (S5_SPB, LANE), 0) // P_A
                == lax.broadcasted_iota(jnp.int32, (S5_SPB, LANE), 1) // S5_GROUP)
    k_mask = (lax.broadcasted_iota(jnp.int32, (LANE, LANE), 0) // S5_GROUP
              == lax.broadcasted_iota(jnp.int32, (LANE, LANE), 1) // S5_GROUP)
    zero_tile = jnp.zeros((LANE, LANE), BF16)
    mag = jnp.exp(lr)
    a_r = mag * jnp.cos(li)
    a_i = mag * jnp.sin(li)
    pr = jnp.ones_like(lr)
    pi = jnp.zeros_like(lr)
    for n in range(T + 1):
        if n >= 1:
            pr, pi = pr * a_r - pi * a_i, pr * a_i + pi * a_r
        car = (cr * pr - ci * pi).astype(BF16)
        cai = (cr * pi + ci * pr).astype(BF16)
        for k in range(S5_NB):
            blk = slice(k * LANE, (k + 1) * LANE)
            if n >= 1:
                cols = slice((n - 1) * LANE, n * LANE)
                wor_ref[k, :, cols] = jnp.where(out_mask, _dot_nt(spread_t, car[blk]), 0.0).astype(BF16)
                woi_ref[k, :, cols] = jnp.where(out_mask, _dot_nt(spread_t, cai[blk]), 0.0).astype(BF16)
            if n < T:
                rows = slice((T - 1 - n) * LANE, (T - n) * LANE)
                wr = (pr * bbr - pi * bbi)[blk].astype(BF16)
                wi = (pr * bbi + pi * bbr)[blk].astype(BF16)
                wst_ref[k, rows, 0:S5_SPB] = jnp.where(in_mask, _dot(wr, spread), 0.0).astype(BF16)
                wst_ref[k, rows, S5_SPB:2 * S5_SPB] = jnp.where(in_mask, _dot(wi, spread), 0.0).astype(BF16)
                kt = (_dot_nt(bbr[blk].astype(BF16), car[blk])
                      - _dot_nt(bbi[blk].astype(BF16), cai[blk]))
                kt = jnp.where(k_mask, kt, 0.0).astype(BF16)
                for ti in range(T - n):
                    to = ti + n
                    kt_ref[k, ti * LANE:(ti + 1) * LANE, to * LANE:(to + 1) * LANE] = kt
                    if n >= 1:
                        kt_ref[k, to * LANE:(to + 1) * LANE, ti * LANE:(ti + 1) * LANE] = zero_tile


def s5_chunk_prep(lam_re, lam_im, log_dt, bbr, bbi, c_re, c_im):
    T = S5_T
    lrf = lam_re.reshape(1, N_S5)
    lif = lam_im.reshape(1, N_S5)
    ldf = jnp.repeat(log_dt, P_A).reshape(1, N_S5)
    lrr = jnp.repeat(lam_re, S5_GROUP, axis=0)
    lir = jnp.repeat(lam_im, S5_GROUP, axis=0)
    ldr = jnp.repeat(log_dt, S5_GROUP).reshape(D_A, 1)
    op = lambda r, c: jax.ShapeDtypeStruct((S5_NB, r, c), BF16)
    return pl.pallas_call(
        _s5_chunk_prep_body,
        out_shape=[jax.ShapeDtypeStruct((8, SUBLANE, N_S5), F32), op(T * LANE, 2 * S5_SPB),
                   op(T * LANE, T * LANE), op(S5_SPB, T * LANE), op(S5_SPB, T * LANE)],
        compiler_params=pltpu.CompilerParams(vmem_limit_bytes=48 * 1024 * 1024),
        name="s5_chunk_prep",
    )(lrf, lif, ldf, lrr, lir, ldr, bbr, bbi, c_re.reshape(D_A, P_A), c_im.reshape(D_A, P_A))


def _s5_c_body(u_ref, wst_ref, kt_ref, wor_ref, woi_ref, tab_ref, d_ref, wglu_ref,
               o_ref, hr_ref, hi_ref, s_sc, car_sc, *, tb):
    T = S5_T

    @pl.when(pl.program_id(1) == 0)
    def _():
        car_sc[...] = jnp.zeros_like(car_sc)

    s_sc[0:SUBLANE, :] = car_sc[...]
    x = u_ref[...]
    xb = x.astype(BF16)
    bw = T * LANE
    for kb in range(S5_NB):
        s = _dot(xb[:, kb * bw:(kb + 1) * bw], wst_ref[kb])
        s_sc[SUBLANE:SUBLANE + tb, kb * S5_SPB:(kb + 1) * S5_SPB] = s[:, :S5_SPB]
        s_sc[SUBLANE:SUBLANE + tb, N_S5 + kb * S5_SPB:N_S5 + (kb + 1) * S5_SPB] = s[:, S5_SPB:]

    def tile(rt, _):
        r0 = pl.multiple_of(SUBLANE + rt * SUBLANE, SUBLANE)
        for lg in range(N_S5 // LANE):
            cre = slice(lg * LANE, (lg + 1) * LANE)
            cim = slice(N_S5 + lg * LANE, N_S5 + (lg + 1) * LANE)
            xr = s_sc[pl.ds(r0, SUBLANE), cre]
            xi = s_sc[pl.ds(r0, SUBLANE), cim]
            for k, d in enumerate((1, 2, 4)):
                a_r = tab_ref[2 * k, :, cre]
                a_i = tab_ref[2 * k + 1, :, cre]
                sr = pltpu.roll(xr, d, 0)
                si = pltpu.roll(xi, d, 0)
                xr, xi = xr + a_r * sr - a_i * si, xi + a_r * si + a_i * sr
            p_r = tab_ref[6, :, cre]
            p_i = tab_ref[7, :, cre]
            cr = car_sc[:, cre]
            ci = car_sc[:, cim]
            xr, xi = xr + p_r * cr - p_i * ci, xi + p_r * ci + p_i * cr
            s_sc[pl.ds(r0, SUBLANE), cre] = xr
            s_sc[pl.ds(r0, SUBLANE), cim] = xi
            last = SUBLANE - 1
            car_sc[:, cre] = jnp.broadcast_to(xr[last:last + 1, :], (SUBLANE, LANE))
            car_sc[:, cim] = jnp.broadcast_to(xi[last:last + 1, :], (SUBLANE, LANE))
        return 0

    lax.fori_loop(0, tb // SUBLANE, tile, 0)

    ys = []
    for kb in range(S5_NB):
        h_r = s_sc[SUBLANE - 1:SUBLANE - 1 + tb, kb * S5_SPB:(kb + 1) * S5_SPB].astype(BF16)
        h_i = s_sc[SUBLANE - 1:SUBLANE - 1 + tb,
                   N_S5 + kb * S5_SPB:N_S5 + (kb + 1) * S5_SPB].astype(BF16)
        ys.append(_dot(xb[:, kb * bw:(kb + 1) * bw], kt_ref[kb])
                  + _dot(h_r, wor_ref[kb]) - _dot(h_i, woi_ref[kb]))
    for t in range(T):
        y = jnp.concatenate([ys[kb][:, t * LANE:(t + 1) * LANE] for kb in range(S5_NB)], axis=1)
        u = jnp.concatenate([x[:, kb * bw + t * LANE:kb * bw + (t + 1) * LANE]
                             for kb in range(S5_NB)], axis=1)
        y = jax.nn.gelu(y + d_ref[...] * u)
        y = y * _sigmoid(_dot(y.astype(BF16), wglu_ref[...]))
        o_ref[:, t * D_A:(t + 1) * D_A] = y.astype(o_ref.dtype)
    hr_ref[...] = car_sc[0:1, :N_S5]
    hi_ref[...] = car_sc[0:1, N_S5:]


def s5_prompt(u3, ops, d_skip, wglu, *, tb=256):
    tab, wst, ktoe, wor, woi = ops
    B, L, _ = u3.shape
    T = S5_T
    nc = L // T
    tb = min(tb, nc)
    bw = T * LANE
    u8 = jnp.transpose(u3.reshape(B, nc, T, S5_NB, LANE), (0, 1, 3, 2, 4)).reshape(B, nc, S5_NB * bw)
    const3 = lambda b, t: (0, 0, 0)
    once = dict(pipeline_mode=pl.Buffered(1))
    o8, hr, hi = pl.pallas_call(
        functools.partial(_s5_c_body, tb=tb),
        grid=(B, nc // tb),
        in_specs=[pl.BlockSpec((None, tb, S5_NB * bw), lambda b, t: (b, t, 0)),
                  pl.BlockSpec((S5_NB, bw, 2 * S5_SPB), const3, **once),
                  pl.BlockSpec((S5_NB, bw, bw), const3, **once),
                  pl.BlockSpec((S5_NB, S5_SPB, bw), const3, **once),
                  pl.BlockSpec((S5_NB, S5_SPB, bw), const3, **once),
                  pl.BlockSpec((8, SUBLANE, N_S5), const3, **once),
                  pl.BlockSpec((1, D_A), lambda b, t: (0, 0)),
                  pl.BlockSpec((D_A, D_A), lambda b, t: (0, 0))],
        out_specs=[pl.BlockSpec((None, tb, T * D_A), lambda b, t: (b, t, 0)),
                   pl.BlockSpec((None, 1, N_S5), lambda b, t: (b, 0, 0)),
                   pl.BlockSpec((None, 1, N_S5), lambda b, t: (b, 0, 0))],
        out_shape=[jax.ShapeDtypeStruct((B, nc, T * D_A), BF16),
                   jax.ShapeDtypeStruct((B, 1, N_S5), F32),
                   jax.ShapeDtypeStruct((B, 1, N_S5), F32)],
        scratch_shapes=[pltpu.VMEM((SUBLANE + tb, 2 * N_S5), F32),
                        pltpu.VMEM((SUBLANE, 2 * N_S5), F32)],
        compiler_params=_cp(("arbitrary", "arbitrary"), 56),
        name="s5_prompt",
    )(u8, wst, ktoe, wor, woi, tab, d_skip, wglu)
    return o8.reshape(B, L, D_A), hr, hi


def s5_out_weights(c_re, c_im):
    eye = jnp.eye(S5_GPB, dtype=F32)

    def blockdiag(c):
        c = c.reshape(S5_NB, S5_GPB, S5_GROUP, P_A)
        return jnp.einsum('kghp,gj->kgpjh', c, eye).reshape(S5_NB, S5_SPB, LANE).astype(BF16)

    return blockdiag(c_re), blockdiag(c_im)


def _s5_in(ub, wb_ref, kb):
    return _dot(ub[:, kb * LANE:(kb + 1) * LANE], wb_ref[kb])


def _s5_tail(h_blocks, u, wcr_ref, wci_ref, d_ref, wglu_ref):
    ys = []
    for kb in range(S5_NB):
        hr, hi = h_blocks(kb)
        ys.append(_dot(hr.astype(BF16), wcr_ref[kb]) - _dot(hi.astype(BF16), wci_ref[kb]))
    y = jnp.concatenate(ys, axis=1) + d_ref[...] * u
    y = jax.nn.gelu(y)
    return y * _sigmoid(_dot(y.astype(BF16), wglu_ref[...]))


def _s5_s_body(u_ref, h0r_ref, h0i_ref, wb_ref, wcr_ref, wci_ref, tab_ref, d_ref, wglu_ref,
               o_ref, hr_ref, hi_ref):
    u = u_ref[...]
    ub = u.astype(BF16)
    for kb in range(S5_NB):
        cols = slice(kb * S5_SPB, (kb + 1) * S5_SPB)
        bu = _s5_in(ub, wb_ref, kb)
        a_r = tab_ref[6, 0:1, cols]
        a_i = tab_ref[7, 0:1, cols]
        h0r = h0r_ref[:, cols]
        h0i = h0i_ref[:, cols]
        hr_ref[:, cols] = a_r * h0r - a_i * h0i + bu[:, :S5_SPB]
        hi_ref[:, cols] = a_r * h0i + a_i * h0r + bu[:, S5_SPB:]

    def h_blocks(kb):
        cols = slice(kb * S5_SPB, (kb + 1) * S5_SPB)
        return hr_ref[:, cols], hi_ref[:, cols]

    o_ref[...] = _s5_tail(h_blocks, u, wcr_ref, wci_ref, d_ref, wglu_ref).astype(o_ref.dtype)


def s5_sample(proj, h0r, h0i, wb, wcr, wci, tab, d_skip, wglu):
    R = proj.shape[0]
    z2 = lambda i: (0, 0)
    return pl.pallas_call(
        _s5_s_body,
        grid=(1,),
        in_specs=[pl.BlockSpec((R, D_A), z2),
                  pl.BlockSpec((R, N_S5), z2), pl.BlockSpec((R, N_S5), z2),
                  pl.BlockSpec((S5_NB, LANE, 2 * S5_SPB), lambda i: (0, 0, 0)),
                  pl.BlockSpec((S5_NB, S5_SPB, LANE), lambda i: (0, 0, 0)),
                  pl.BlockSpec((S5_NB, S5_SPB, LANE), lambda i: (0, 0, 0)),
                  pl.BlockSpec((8, SUBLANE, N_S5), lambda i: (0, 0, 0)),
                  pl.BlockSpec((1, D_A), z2), pl.BlockSpec((D_A, D_A), z2)],
        out_specs=[pl.BlockSpec((R, D_A), z2), pl.BlockSpec((R, N_S5), z2), pl.BlockSpec((R, N_S5), z2)],
        out_shape=[jax.ShapeDtypeStruct((R, D_A), BF16),
                   jax.ShapeDtypeStruct((R, N_S5), F32),
                   jax.ShapeDtypeStruct((R, N_S5), F32)],
        compiler_params=_cp(("arbitrary",)),
        name="s5_sample",
    )(proj, h0r, h0i, wb, wcr, wci, tab, d_skip, wglu)


def _gmlp_p_body(u_ref, v_ref, gv_ref, wcat_ref, bias_ref, o_ref, *, tb):
    hd = D_B // H_B
    lane_head = lax.broadcasted_iota(jnp.int32, (GM_CHUNK, D_B), 1) // hd
    wi = lax.broadcasted_iota(jnp.int32, (GM_CHUNK, H_B * GM_CHUNK), 0)
    wj = lax.broadcasted_iota(jnp.int32, (GM_CHUNK, H_B * GM_CHUNK), 1) % GM_CHUNK
    wcat = jnp.where(wi >= wj, wcat_ref[...], 0.0).astype(BF16)
    for c in range(tb // GM_CHUNK):
        rows = slice(c * GM_CHUNK, (c + 1) * GM_CHUNK)
        vn = _rms(jax.nn.gelu(v_ref[rows, :])) * gv_ref[...]
        vb = vn.astype(BF16)
        stack = jnp.concatenate(
            [jnp.where(lane_head == h, vb, jnp.zeros_like(vb)) for h in range(H_B)], axis=0)
        s = _dot(wcat, stack) + bias_ref[...]
        o_ref[rows, :] = (jax.nn.gelu(u_ref[rows, :]) * s).astype(o_ref.dtype)


def gmlp_prompt(proj3, g_v, w_s, b_s, *, tb=512):
    B, L = proj3.shape[:2]
    hd = D_B // H_B
    wcat = jnp.transpose(w_s, (1, 0, 2)).reshape(GM_CHUNK, H_B * GM_CHUNK)
    bias = jnp.repeat(b_s.T, hd, axis=1)
    const2 = lambda b, t: (0, 0)
    ub = D_A // D_B
    return pl.pallas_call(
        functools.partial(_gmlp_p_body, tb=tb),
        grid=(B, L // tb),
        in_specs=[pl.BlockSpec((None, tb, D_B), lambda b, t: (b, t, ub)),
                  pl.BlockSpec((None, tb, D_B), lambda b, t: (b, t, ub + 1)),
                  pl.BlockSpec((1, D_B), const2),
                  pl.BlockSpec((GM_CHUNK, H_B * GM_CHUNK), const2),
                  pl.BlockSpec((GM_CHUNK, D_B), const2)],
        out_specs=pl.BlockSpec((None, tb, D_B), lambda b, t: (b, t, 0)),
        out_shape=jax.ShapeDtypeStruct((B, L, D_B), BF16),
        compiler_params=_cp(("arbitrary", "arbitrary")),
        name="gmlp_prompt",
    )(proj3, proj3, g_v, wcat, bias)


def _gmlp_s_body(u_ref, v_ref, gv_ref, w0_ref, b0_ref, o_ref, vn_ref):
    vn = _rms(jax.nn.gelu(v_ref[...])) * gv_ref[...]
    vn_ref[...] = vn
    s = w0_ref[...] * vn + b0_ref[...]
    o_ref[...] = (jax.nn.gelu(u_ref[...]) * s).astype(o_ref.dtype)


def gmlp_sample(proj, g_v, w_s, b_s):
    R = proj.shape[0]
    hd = D_B // H_B
    w0 = jnp.repeat(w_s[:, 0, 0], hd).reshape(1, D_B)
    b0 = jnp.repeat(b_s[:, 0], hd).reshape(1, D_B)
    z2 = lambda i: (0, 0)
    ub = D_A // D_B
    return pl.pallas_call(
        _gmlp_s_body,
        grid=(1,),
        in_specs=[pl.BlockSpec((R, D_B), lambda i: (0, ub)),
                  pl.BlockSpec((R, D_B), lambda i: (0, ub + 1)),
                  pl.BlockSpec((1, D_B), z2), pl.BlockSpec((1, D_B), z2), pl.BlockSpec((1, D_B), z2)],
        out_specs=[pl.BlockSpec((R, D_B), z2), pl.BlockSpec((R, D_B), z2)],
        out_shape=[jax.ShapeDtypeStruct((R, D_B), BF16), jax.ShapeDtypeStruct((R, D_B), F32)],
        compiler_params=_cp(("arbitrary",)),
        name="gmlp_sample",
    )(proj, proj, g_v, w0, b0)


def _head_expand_matrix():
    r = lax.broadcasted_iota(jnp.int32, (LANE, D_C), 0)
    c = lax.broadcasted_iota(jnp.int32, (LANE, D_C), 1) // HD_C
    return jnp.where(r == c, 1.0, 0.0).astype(BF16)


def _expand_heads(v, e):
    hi, lo = _split_bf16(v)
    return _dot(hi, e) + _dot(lo, e)


def _ssd_p_body(z_ref, xbc_ref, dt_ref, cw_ref, cb_ref, dtb_ref, alog_ref, dsk_ref, gn_ref,
                o_ref, st_ref, h_ref, h_sc, cv_sc, *, nt):
    Q = SSD_CHUNK
    t = pl.program_id(1)

    @pl.when(t == 0)
    def _():
        h_sc[...] = jnp.zeros_like(h_sc)
        cv_sc[0:SUBLANE, :] = jnp.zeros((SUBLANE, D_XBC), F32)

    xbc = xbc_ref[...]
    cv_sc[SUBLANE:SUBLANE + Q, :] = xbc
    acc = cb_ref[...]
    for k in range(K_C):
        off = SUBLANE - (K_C - 1) + k
        acc = acc + cw_ref[k:k + 1, :] * cv_sc[off:off + Q, :]
    cv_sc[0:SUBLANE, :] = xbc[Q - SUBLANE:, :]
    st_ref[...] = xbc[Q - (K_C - 1):, :]
    xc = _silu(acc)
    xs = xc[:, :D_C]

    dt = _softplus(dt_ref[...] + dtb_ref[...])
    a = -jnp.exp(alog_ref[...])
    da = dt * a
    ii = lax.broadcasted_iota(jnp.int32, (Q, Q), 0)
    jj = lax.broadcasted_iota(jnp.int32, (Q, Q), 1)
    causal = ii >= jj
    tril = jnp.where(causal, 1.0, 0.0).astype(BF16)
    d0, d1 = _split_bf16(da)
    d2 = (da - d0.astype(F32) - d1.astype(F32)).astype(BF16)
    cs = _dot(tril, d0) + _dot(tril, d1) + _dot(tril, d2)
    cst = cs.T
    cs_end = cs[Q - 1:Q, :]
    e = _head_expand_matrix()
    dt_f = _expand_heads(dt, e)
    ws_f = _expand_heads(dt * jnp.exp(cs_end - cs), e)
    ecs_f = _expand_heads(jnp.exp(cs), e)
    xdt = xs * dt_f
    xw = xs * ws_f

    lane = lax.broadcasted_iota(jnp.int32, (Q, LANE), 1)
    hpg = H_C // G_C
    gw = hpg * HD_C
    ys = []
    for g in range(G_C):
        bg = xc[:, D_C + g * N_C:D_C + (g + 1) * N_C].astype(BF16)
        cg = xc[:, D_C + G_C * N_C + g * N_C:D_C + G_C * N_C + (g + 1) * N_C].astype(BF16)
        gmat = _dot_nt(cg, bg)
        hprev = h_sc[g * gw:(g + 1) * gw, :]
        yoff = _dot_nt(cg, hprev.astype(BF16)) * ecs_f[:, g * gw:(g + 1) * gw]
        snew = _dot_tn(xw[:, g * gw:(g + 1) * gw].astype(BF16), bg)
        for hp in range(hpg // 2):
            h0 = g * hpg + 2 * hp
            xpair = xdt[:, h0 * HD_C:(h0 + 2) * HD_C]
            x_lo = jnp.where(lane < HD_C, xpair, 0.0).astype(BF16)
            x_hi = jnp.where(lane >= HD_C, xpair, 0.0).astype(BF16)
            yd = None
            for hh, xh in ((h0, x_lo), (h0 + 1, x_hi)):
                seg = cs[:, hh:hh + 1] - cst[hh:hh + 1, :]
                sc = (gmat * jnp.exp(jnp.where(causal, seg, NEG_BIG))).astype(BF16)
                part = _dot(sc, xh)
                yd = part if yd is None else yd + part
            ys.append(yd + yoff[:, 2 * hp * HD_C:(2 * hp + 2) * HD_C])
        for hh in range(hpg):
            h = g * hpg + hh
            cd = jnp.exp(cst[h:h + 1, Q - 1:Q])
            rows = slice(h * HD_C, (h + 1) * HD_C)
            h_sc[rows, :] = h_sc[rows, :] * cd + snew[hh * HD_C:(hh + 1) * HD_C, :]

    y = jnp.concatenate(ys, axis=1) + dsk_ref[...] * xs
    y = y * _silu(z_ref[...])
    o_ref[...] = (_rms(y) * gn_ref[...]).astype(o_ref.dtype)

    @pl.when(t == nt - 1)
    def _():
        h_ref[...] = h_sc[...]


def _ssd_params(dt_bias, a_log, d_skip):
    pad = LANE - H_C
    dtb = jnp.pad(dt_bias, (0, pad)).reshape(1, LANE)
    alog = jnp.pad(a_log, (0, pad)).reshape(1, LANE)
    dsk = jnp.repeat(d_skip, HD_C).reshape(1, D_C)
    return dtb, alog, dsk


def ssd_prompt(z3, xd3, conv_w, conv_b, dt_bias, a_log, d_skip, g_norm):
    B, L = z3.shape[:2]
    Q = SSD_CHUNK
    nt = L // Q
    dtb, alog, dsk = _ssd_params(dt_bias, a_log, d_skip)
    const2 = lambda b, t: (0, 0)
    blk = lambda w: pl.BlockSpec((None, Q, w), lambda b, t: (b, t, 0))
    return pl.pallas_call(
        functools.partial(_ssd_p_body, nt=nt),
        grid=(B, nt),
        in_specs=[blk(D_C), blk(D_XBC),
                  pl.BlockSpec((None, Q, LANE), lambda b, t: (b, t, D_XBC // LANE)),
                  pl.BlockSpec((K_C, D_XBC), const2), pl.BlockSpec((1, D_XBC), const2),
                  pl.BlockSpec((1, LANE), const2), pl.BlockSpec((1, LANE), const2),
                  pl.BlockSpec((1, D_C), const2), pl.BlockSpec((1, D_C), const2)],
        out_specs=[blk(D_C),
                   pl.BlockSpec((None, K_C - 1, D_XBC), lambda b, t: (b, 0, 0)),
                   pl.BlockSpec((None, H_C * HD_C, N_C), lambda b, t: (b, 0, 0))],
        out_shape=[jax.ShapeDtypeStruct((B, L, D_C), BF16),
                   jax.ShapeDtypeStruct((B, K_C - 1, D_XBC), F32),
                   jax.ShapeDtypeStruct((B, H_C * HD_C, N_C), F32)],
        scratch_shapes=[pltpu.VMEM((H_C * HD_C, N_C), F32), pltpu.VMEM((SUBLANE + Q, D_XBC), F32)],
        compiler_params=_cp(("arbitrary", "arbitrary")),
        name="ssd_prompt",
    )(z3, xd3, xd3, conv_w, conv_b.reshape(1, D_XBC), dtb, alog, dsk, g_norm.reshape(1, D_C))


def _ssd_s_body(z_ref, xbc_ref, dt_ref, buf_ref, h0_ref, cw_ref, cb_ref, dtb_ref, alog_ref,
                dsk_ref, gn_ref, *rest, tbatch, nsteps, has_prev):
    o_ref, hn_ref, xs_sc, bc_sc, xt_sc, at_sc, y_sc = rest[1:] if has_prev else rest
    s = pl.program_id(0)
    R = xs_sc.shape[0]

    @pl.when(s == 0)
    def _():
        acc = cb_ref[...] + cw_ref[K_C - 1:K_C, :] * xbc_ref[...]
        for k in range(K_C - 1):
            acc = acc + cw_ref[k:k + 1, :] * buf_ref[k]
        xc = _silu(acc)
        xs = xc[:, :D_C]
        xs_sc[...] = xs
        bc_sc[...] = xc[:, D_C:]
        dt = _softplus(dt_ref[...] + dtb_ref[...])
        e = _head_expand_matrix()
        dt_f = _expand_heads(dt, e)
        da_f = jnp.exp(_expand_heads(dt * (-jnp.exp(alog_ref[...])), e))
        for t_sc, v in ((xt_sc, xs * dt_f), (at_sc, da_f)):
            hi, lo = _split_bf16(v.T)
            t_sc[:, :R] = hi
            t_sc[:, R:] = lo

    gw = (H_C // G_C) * HD_C
    kk = lax.broadcasted_iota(jnp.int32, (2 * R, LANE), 0) % R
    r0 = pl.multiple_of(s * tbatch, tbatch)
    bc8 = bc_sc[pl.ds(r0, tbatch), :]
    ti = lax.broadcasted_iota(jnp.int32, (tbatch, gw), 0)
    ytile = [jnp.zeros((tbatch, gw), F32) for _ in range(G_C)]
    for i in range(tbatch):
        b = s * tbatch + i
        onehot = jnp.where(kk == b, 1.0, 0.0).astype(BF16)
        xb = _dot(xt_sc[...], onehot)
        ab = _dot(at_sc[...], onehot)
        for g in range(G_C):
            rows = slice(g * gw, (g + 1) * gw)
            brow = bc8[i:i + 1, g * N_C:(g + 1) * N_C]
            hn = h0_ref[i, rows, :] * ab[rows, :] + xb[rows, :] * brow
            hn_ref[i, rows, :] = hn
            c8 = bc8[:, G_C * N_C + g * N_C:G_C * N_C + (g + 1) * N_C]
            yg = _dot_nt(c8.astype(BF16), hn.astype(BF16))
            ytile[g] = jnp.where(ti == i, yg, ytile[g])
    for g in range(G_C):
        y_sc[pl.ds(r0, tbatch), g * gw:(g + 1) * gw] = ytile[g]

    @pl.when(s == nsteps - 1)
    def _():
        y = y_sc[...] + dsk_ref[...] * xs_sc[...]
        y = y * _silu(z_ref[...])
        o_ref[...] = (_rms(y) * gn_ref[...]).astype(o_ref.dtype)


def ssd_sample(z, xd, buf, h0_all, l, hn_all, conv_w, conv_b, dt_bias, a_log, d_skip, g_norm,
               *, tbatch=8):
    R = z.shape[0]
    nsteps = R // tbatch
    dtb, alog, dsk = _ssd_params(dt_bias, a_log, d_skip)
    z2 = lambda s: (0, 0)
    hw = H_C * HD_C
    st_spec = pl.BlockSpec((None, tbatch, hw, N_C), lambda s: (l, s, 0, 0))
    in_specs = [pl.BlockSpec((R, D_C), z2), pl.BlockSpec((R, D_XBC), z2),
                pl.BlockSpec((R, LANE), lambda s: (0, D_XBC // LANE)),
                pl.BlockSpec((K_C - 1, R, D_XBC), lambda s: (0, 0, 0)),
                st_spec,
                pl.BlockSpec((K_C, D_XBC), z2), pl.BlockSpec((1, D_XBC), z2),
                pl.BlockSpec((1, LANE), z2), pl.BlockSpec((1, LANE), z2),
                pl.BlockSpec((1, D_C), z2), pl.BlockSpec((1, D_C), z2)]
    args = [z, xd, xd, buf, h0_all, conv_w, conv_b.reshape(1, D_XBC), dtb, alog, dsk,
            g_norm.reshape(1, D_C)]
    aliases = {}
    if hn_all is not None:
        aliases = {len(args): 1}
        in_specs.append(pl.BlockSpec(memory_space=pl.ANY))
        args.append(hn_all)
    return pl.pallas_call(
        functools.partial(_ssd_s_body, tbatch=tbatch, nsteps=nsteps, has_prev=hn_all is not None),
        grid=(nsteps,),
        in_specs=in_specs,
        out_specs=[pl.BlockSpec((R, D_C), z2), st_spec],
        out_shape=[jax.ShapeDtypeStruct((R, D_C), BF16),
                   jax.ShapeDtypeStruct(h0_all.shape, F32)],
        scratch_shapes=[pltpu.VMEM((R, D_C), F32), pltpu.VMEM((R, 2 * G_C * N_C), F32),
                        pltpu.VMEM((D_C, 2 * R), BF16), pltpu.VMEM((D_C, 2 * R), BF16),
                        pltpu.VMEM((R, D_C), F32)],
        input_output_aliases=aliases,
        compiler_params=_cp(("arbitrary",)),
        name="ssd_sample",
    )(*args)


def _up_p_body(x_ref, wa_ref, wb_ref, cwa_ref, cwb_ref, cba_ref, cbb_ref,
               o_ref, sa_ref, sb_ref, wa_sc, wb_sc, ca_sc, cb_sc, *, tm, tiles_per_seq):
    i = pl.program_id(1)

    @pl.when(i == 0)
    def _():
        wa_sc[...] = wa_ref[...].astype(BF16)
        wb_sc[...] = wb_ref[...].astype(BF16)

    @pl.when(i % tiles_per_seq == 0)
    def _():
        ca_sc[0:SUBLANE, :] = jnp.zeros((SUBLANE, ca_sc.shape[1]), F32)
        cb_sc[0:SUBLANE, :] = jnp.zeros((SUBLANE, cb_sc.shape[1]), F32)

    x = x_ref[...]

    def half(w_sc, c_sc, cw_ref, cb_ref, st_ref):
        up = _dot(x, w_sc[...])
        c_sc[SUBLANE:SUBLANE + tm, :] = up
        acc = cb_ref[...]
        for k in range(K_F):
            off = SUBLANE - (K_F - 1) + k
            acc = acc + cw_ref[k:k + 1, :] * c_sc[off:off + tm, :]
        c_sc[0:SUBLANE, :] = up[tm - SUBLANE:, :]
        st_ref[...] = up[tm - (K_F - 1):, :]
        return acc

    a = half(wa_sc, ca_sc, cwa_ref, cba_ref, sa_ref)
    b = half(wb_sc, cb_sc, cwb_ref, cbb_ref, sb_ref)
    o_ref[...] = (_silu(a) * b).astype(o_ref.dtype)


def up_prompt(h2, w_up, l, conv_w, conv_b, *, seq_len, tm=1024, tn=512):
    M, D = h2.shape
    B = M // seq_len
    tiles_per_seq = seq_len // tm
    nj = D_FF // tn
    cb = conv_b.reshape(1, 2 * D_FF)
    body = functools.partial(_up_p_body, tm=tm, tiles_per_seq=tiles_per_seq)
    st_spec = pl.BlockSpec((None, K_F - 1, tn), lambda j, i: (i // tiles_per_seq, 0, j))
    return pl.pallas_call(
        body,
        grid=(nj, M // tm),
        in_specs=[pl.BlockSpec((tm, D), lambda j, i: (i, 0)),
                  pl.BlockSpec((None, D, tn), lambda j, i: (l, 0, j)),
                  pl.BlockSpec((None, D, tn), lambda j, i: (l, 0, nj + j)),
                  pl.BlockSpec((K_F, tn), lambda j, i: (0, j)),
                  pl.BlockSpec((K_F, tn), lambda j, i: (0, nj + j)),
                  pl.BlockSpec((1, tn), lambda j, i: (0, j)),
                  pl.BlockSpec((1, tn), lambda j, i: (0, nj + j))],
        out_specs=[pl.BlockSpec((tm, tn), lambda j, i: (i, j)), st_spec, st_spec],
        out_shape=[jax.ShapeDtypeStruct((M, D_FF), BF16),
                   jax.ShapeDtypeStruct((B, K_F - 1, D_FF), F32),
                   jax.ShapeDtypeStruct((B, K_F - 1, D_FF), F32)],
        scratch_shapes=[pltpu.VMEM((D, tn), BF16), pltpu.VMEM((D, tn), BF16),
                        pltpu.VMEM((SUBLANE + tm, tn), F32), pltpu.VMEM((SUBLANE + tm, tn), F32)],
        compiler_params=_cp(("arbitrary", "arbitrary")),
        name="up_prompt",
    )(h2, w_up, w_up, conv_w, conv_w, cb, cb)


def _up_s_body(x_ref, w_ref, buf_ref, cw_ref, cb_ref, *rest, nj, has_prev):
    o_ref, new_ref, a_sc = rest[1:] if has_prev else rest
    s = pl.program_id(0)
    up = _dot(x_ref[...], w_ref[...].astype(BF16))
    acc = cb_ref[...] + cw_ref[K_F - 1:K_F, :] * up
    for k in range(K_F - 1):
        acc = acc + cw_ref[k:k + 1, :] * buf_ref[:, k, :]
    for k in range(1, K_F - 1):
        new_ref[:, k - 1, :] = buf_ref[:, k, :]
    new_ref[:, K_F - 2, :] = up

    @pl.when(s < nj)
    def _():
        a_sc[s] = acc

    @pl.when(s >= nj)
    def _():
        o_ref[...] = (_silu(a_sc[s - nj]) * acc).astype(o_ref.dtype)


def up_sample(h2, w_up, l, buf_all, new_all, conv_w, conv_b, *, tn=512):
    R, D = h2.shape
    nj = D_FF // tn
    st = pl.BlockSpec((None, R, K_F - 1, tn), lambda s: (l, 0, 0, s))
    in_specs = [pl.BlockSpec((R, D), lambda s: (0, 0)),
                pl.BlockSpec((None, D, tn), lambda s: (l, 0, s)),
                st,
                pl.BlockSpec((K_F, tn), lambda s: (0, s)),
                pl.BlockSpec((1, tn), lambda s: (0, s))]
    args = [h2, w_up, buf_all, conv_w, conv_b.reshape(1, 2 * D_FF)]
    aliases = {}
    if new_all is not None:
        aliases = {len(args): 1}
        in_specs.append(pl.BlockSpec(memory_space=pl.ANY))
        args.append(new_all)
    act, new = pl.pallas_call(
        functools.partial(_up_s_body, nj=nj, has_prev=new_all is not None),
        grid=(2 * nj,),
        in_specs=in_specs,
        out_specs=[pl.BlockSpec((R, tn), lambda s: (0, jnp.maximum(s - nj, 0))), st],
        out_shape=[jax.ShapeDtypeStruct((R, D_FF), BF16),
                   jax.ShapeDtypeStruct(buf_all.shape, F32)],
        scratch_shapes=[pltpu.VMEM((nj, R, tn), F32)],
        input_output_aliases=aliases,
        compiler_params=_cp(("arbitrary",)),
        name="up_sample",
    )(*args)
    return act, new


def _layer(x3, h, mods, l, p, state, next_norm, *, seq_len, tm, tn_merge):
    B, L, D = x3.shape
    M = B * L
    sh_m, sc_m, gt_m, sh_f, sc_f, gt_f = mods
    prompt = state is None
    w_buffers = 1 if M > tm else 2

    def proj(col0, n, name):
        return matmul([h], [(p['w_in_t'], l, col0, True)], [], [(n, F32)], _epi_store,
                      x_of_w=(0,), tm=tm, tn=n, name=name, vmem_mb=56, w_buffers=1)[0]

    uav = proj(OFF_UA, D_A + 2 * D_B, "proj_uav")
    zc = proj(OFF_Z, D_C, "proj_z")
    xd = proj(OFF_XBC, D_XBC + LANE, "proj_xd")

    tab, wb, bbr, bbi = s5_prep(p['lam_re'][l], p['lam_im'][l], p['log_dt'][l],
                                p['b_re'][l], p['b_im'][l])
    d_a = p['s5_d'][l].reshape(1, D_A)
    wglu = p['w_glu'][l].astype(BF16)
    g_v = p['g_v'][l].reshape(1, D_B)

    if prompt:
        s5_ops = s5_chunk_prep(p['lam_re'][l], p['lam_im'][l], p['log_dt'][l], bbr, bbi,
                               p['c_re'][l], p['c_im'][l])
        o_a, s5r, s5i = s5_prompt(uav.reshape(B, L, -1)[..., :D_A], s5_ops, d_a, wglu)
        o_b = gmlp_prompt(uav.reshape(B, L, -1), g_v, p['w_s'][l], p['b_s'][l])
        v_rows = None
        o_c, convc, ssm = ssd_prompt(zc.reshape(B, L, -1), xd.reshape(B, L, -1),
                                     p['ssd_conv_w'][l], p['ssd_conv_b'][l], p['dt_bias'][l],
                                     p['a_log'][l], p['ssd_d'][l], p['ssd_g'][l])
        s5r = s5r.reshape(B, G_A, P_A)
        s5i = s5i.reshape(B, G_A, P_A)
    else:
        s5_re0, s5_im0, ssm_all, ssm_new_all, convc0, ffn_all, ffn_new_all = state
        wcr, wci = s5_out_weights(p['c_re'][l], p['c_im'][l])
        o_a, s5r, s5i = s5_sample(uav, s5_re0.reshape(M, N_S5), s5_im0.reshape(M, N_S5),
                                  wb, wcr, wci, tab, d_a, wglu)
        o_b, v_rows = gmlp_sample(uav, g_v, p['w_s'][l], p['b_s'][l])
        o_c, ssm = ssd_sample(zc, xd, jnp.transpose(convc0, (1, 0, 2)),
                              ssm_all, l, ssm_new_all,
                              p['ssd_conv_w'][l], p['ssd_conv_b'][l], p['dt_bias'][l],
                              p['a_log'][l], p['ssd_d'][l], p['ssd_g'][l])
        convc = jnp.concatenate([convc0[:, 1:], xd[:, None, :D_XBC]], axis=1)
        s5r = s5r.reshape(M, G_A, P_A)
        s5i = s5i.reshape(M, G_A, P_A)
    o_a = o_a.reshape(M, D_A)
    o_b = o_b.reshape(M, D_B)
    o_c = o_c.reshape(M, D_C)
    if prompt:
        ssm = ssm.reshape(B, H_C, HD_C, N_C)

    w_t = p['w_in_t']
    merged = matmul(
        [h, o_a, o_b, o_c],
        [(p['w_pa'], l, 0, False), (p['w_pb'], l, 0, False), (p['w_pc'], l, 0, False),
         (w_t, l, OFF_GATES, True), (w_t, l, OFF_GATES + D, True), (w_t, l, OFF_GATES + 2 * D, True)],
        [], [(D, BF16)], _epi_merge, x_of_w=(1, 2, 3, 0, 0, 0), tm=tm, tn=tn_merge, name="merge",
        vmem_mb=56, w_buffers=w_buffers)[0]

    tm_r = min(tm, 256)
    x2, h2 = residual_norm(merged, p['w_out'], l, x3.reshape(M, D), gt_m,
                           p['g_ffn'][l].reshape(1, D), (sc_f, sh_f), tm=tm_r, seq_len=seq_len,
                           emit_x=True, name="out_norm")
    if prompt:
        act, st_a, st_b = up_prompt(h2, p['w_up'], l, p['ffn_conv_w'][l], p['ffn_conv_b'][l],
                                    seq_len=seq_len, tm=tm)
        convf = jnp.concatenate([st_a, st_b], axis=-1)
    else:
        act, convf = up_sample(h2, p['w_up'], l, ffn_all, ffn_new_all,
                               p['ffn_conv_w'][l], p['ffn_conv_b'][l])

    g_next, mod_next = next_norm
    x2, h_next = residual_norm(act, p['w_down_bf16'], l, x2, gt_f, g_next, mod_next, tm=tm_r,
                               seq_len=seq_len, emit_x=mod_next is not None, name="down_norm")
    x_out = None if x2 is None else x2.reshape(B, L, D)
    return x_out, h_next, s5r, s5i, ssm, convc, convf, v_rows


def kernel(x_prompt, x_sample, c_prompt, c_sample, state_s5_re, state_s5_im, state_ssm, state_ssd_conv, state_ffn_conv, w_mod, b_mod, g_mix, w_in, s5_lam_re, s5_lam_im, s5_log_dt, s5_b_re, s5_b_im, s5_c_re, s5_c_im, s5_d, s5_w_glu, gm_g_v, gm_w_s, gm_b_s, ssd_conv_w, ssd_conv_b, ssd_dt_bias, ssd_a_log, ssd_d, ssd_g_norm, w_pa, w_pb, w_pc, w_out, g_ffn, ffn_w_up, ffn_conv_w, ffn_conv_b, ffn_w_down, g_final):
    p = {
        'g_mix': g_mix, 'w_in_t': jnp.swapaxes(w_in, 1, 2),
        'lam_re': s5_lam_re, 'lam_im': s5_lam_im, 'log_dt': s5_log_dt,
        'b_re': s5_b_re, 'b_im': s5_b_im, 'c_re': s5_c_re, 'c_im': s5_c_im,
        's5_d': s5_d, 'w_glu': s5_w_glu,
        'g_v': gm_g_v, 'w_s': gm_w_s, 'b_s': gm_b_s,
        'ssd_conv_w': ssd_conv_w, 'ssd_conv_b': ssd_conv_b, 'dt_bias': ssd_dt_bias,
        'a_log': ssd_a_log, 'ssd_d': ssd_d, 'ssd_g': ssd_g_norm,
        'w_pa': w_pa, 'w_pb': w_pb, 'w_pc': w_pc, 'w_out': w_out,
        'g_ffn': g_ffn, 'w_up': ffn_w_up, 'ffn_conv_w': ffn_conv_w,
        'ffn_conv_b': ffn_conv_b, 'w_down_bf16': ffn_w_down.astype(BF16),
    }
    bp, seq, D = x_prompt.shape
    bs = x_sample.shape[0]

    n_c = bs + bp
    pad = (-n_c) % SUBLANE
    c_all = jnp.concatenate([c_sample, c_prompt, jnp.zeros((pad, D), F32)], axis=0)
    mod = mod_all(c_all, w_mod, b_mod)

    xp = x_prompt
    xs = x_sample.reshape(1, bs, D)
    outs_p = [[] for _ in range(5)]
    outs_s = [[] for _ in range(6)]
    ssm_all = state_ssm.reshape(DEPTH, bs, H_C * HD_C, N_C)
    ssm_new_all = None
    ffn_new_all = None
    mods_s = [[m[None] for m in jnp.split(mod[l, :bs], 6, axis=-1)] for l in range(DEPTH)]
    mods_p = [[m[:, None, :] for m in jnp.split(mod[l, bs:bs + bp], 6, axis=-1)] for l in range(DEPTH)]

    def norm_after(l, mods):
        if l + 1 == DEPTH:
            return g_final.reshape(1, D), None
        return g_mix[l + 1].reshape(1, D), (mods[l + 1][1], mods[l + 1][0])

    g0 = g_mix[0].reshape(1, D)
    hp = norm_mod(xp, g0, mods_p[0][1], mods_p[0][0], tm=512).reshape(bp * seq, D)
    hs = norm_mod(xs, g0, mods_s[0][1], mods_s[0][0], tm=bs).reshape(bs, D)
    for l in range(DEPTH):
        xp, hp, *st_p = _layer(xp, hp, mods_p[l], l, p, None, norm_after(l, mods_p),
                               seq_len=seq, tm=1024, tn_merge=512)
        for acc, v in zip(outs_p, st_p[:5]):
            acc.append(v)
        state = (state_s5_re[l], state_s5_im[l], ssm_all, ssm_new_all, state_ssd_conv[l],
                 state_ffn_conv, ffn_new_all)
        xs, hs, *st_s = _layer(xs, hs, mods_s[l], l, p, state, norm_after(l, mods_s),
                               seq_len=1, tm=bs, tn_merge=256)
        ssm_new_all = st_s[2]
        ffn_new_all = st_s[4]
        st_s[5] = st_s[5].reshape(bs, 1, D_B)
        for acc, v in zip(outs_s, st_s):
            acc.append(v)

    y_prompt = hp.reshape(bp, seq, D)
    y_sample = hs.reshape(bs, 1, D)
    whole = {2: ssm_new_all.reshape(DEPTH, bs, H_C, HD_C, N_C),
             4: ffn_new_all}
    outs_s = [whole[k] if k in whole else jnp.stack(v) for k, v in enumerate(outs_s)]
    return (y_prompt, y_sample, *[jnp.stack(v) for v in outs_p], *outs_s)
```

```python
import functools
from typing import NamedTuple

import jax
import jax.numpy as jnp
from jax import lax
from jax.experimental import pallas as pl
from jax.experimental.pallas import tpu as pltpu

F32 = jnp.float32
BF16 = jnp.bfloat16

D_MODEL = 2048
DEPTH = 2
D_A = 512
S5_GROUP = 16
G_A = D_A // S5_GROUP
P_A = 64
N_S5 = G_A * P_A
S5_GPB = 128 // S5_GROUP
S5_NB = D_A // 128
S5_SPB = S5_GPB * P_A
S5_T = 8
D_B = 512
H_B = 8
GM_CHUNK = 128
D_C = 1024
HD_C = 64
H_C = D_C // HD_C
N_C = 128
G_C = 2
K_C = 4
SSD_CHUNK = 128
D_XBC = D_C + 2 * G_C * N_C
D_FF = 5632
K_F = 3
EPS = 1e-6

OFF_UA = 0
OFF_Z = D_A + 2 * D_B
OFF_XBC = OFF_Z + D_C
OFF_DT = OFF_XBC + D_XBC
OFF_GATES = OFF_DT + H_C

LANE = 128
SUBLANE = 8
NEG_BIG = -1e30


def _cp(sem, vmem_mb=48):
    return pltpu.CompilerParams(dimension_semantics=sem, vmem_limit_bytes=vmem_mb * 1024 * 1024)


def _sigmoid(x):
    return 0.5 * jnp.tanh(0.5 * x) + 0.5


def _silu(x):
    return x * _sigmoid(x)


def _softplus(x):
    return jnp.maximum(x, 0.0) + jnp.log1p(jnp.exp(-jnp.abs(x)))


def _rms(x):
    return x * lax.rsqrt(jnp.mean(x * x, axis=-1, keepdims=True) + EPS)


def _split_bf16(x):
    hi = x.astype(BF16)
    lo = (x - hi.astype(F32)).astype(BF16)
    return hi, lo


def _dot(a, b):
    return jnp.dot(a, b, preferred_element_type=F32)


def _dot_nt(a, b):
    return lax.dot_general(a, b, (((1,), (1,)), ((), ())), preferred_element_type=F32)


def _dot_tn(a, b):
    return lax.dot_general(a, b, (((0,), (0,)), ((), ())), preferred_element_type=F32)


def _mod_body(c_ref, w_ref, b_ref, o_ref):
    a = _silu(c_ref[...]).astype(BF16)
    o_ref[...] = _dot(a, w_ref[...].astype(BF16)) + b_ref[...]


def mod_all(c_all, w_mod, b_mod, *, tn=1024):
    R, D = c_all.shape
    N = w_mod.shape[-1]
    return pl.pallas_call(
        _mod_body,
        grid=(DEPTH, N // tn),
        in_specs=[pl.BlockSpec((R, D), lambda l, j: (0, 0)),
                  pl.BlockSpec((None, D, tn), lambda l, j: (l, 0, j)),
                  pl.BlockSpec((None, 1, tn), lambda l, j: (l, 0, j))],
        out_specs=pl.BlockSpec((None, R, tn), lambda l, j: (l, 0, j)),
        out_shape=jax.ShapeDtypeStruct((DEPTH, R, N), F32),
        compiler_params=_cp(("arbitrary", "arbitrary")),
        name="mod_all",
    )(c_all, w_mod, b_mod.reshape(DEPTH, 1, N))


def _norm_mod_body(x_ref, g_ref, sc_ref, sh_ref, o_ref):
    y = _rms(x_ref[...]) * g_ref[...]
    o_ref[...] = (y * (1.0 + sc_ref[...]) + sh_ref[...]).astype(o_ref.dtype)


class TokenMod(NamedTuple):
    arr: jax.Array
    l: int
    k: int


def _mod_operand(m, tm, D, tile_of, seq_of):
    if isinstance(m, TokenMod):
        return m.arr, pl.BlockSpec((None, tm, D), lambda *g: (m.l, tile_of(*g), m.k))
    return m, pl.BlockSpec((None, 1, D), lambda *g: (seq_of(*g), 0, 0))


def norm_mod(x3, g, sc, sh, *, tm):
    B, L, D = x3.shape
    ops = [_mod_operand(m, tm, D, lambda b, i: i, lambda b, i: b) for m in (sc, sh)]
    return pl.pallas_call(
        _norm_mod_body,
        grid=(B, L // tm),
        in_specs=[pl.BlockSpec((None, tm, D), lambda b, i: (b, i, 0)),
                  pl.BlockSpec((1, D), lambda b, i: (0, 0)),
                  ops[0][1], ops[1][1]],
        out_specs=pl.BlockSpec((None, tm, D), lambda b, i: (b, i, 0)),
        out_shape=jax.ShapeDtypeStruct((B, L, D), BF16),
        compiler_params=_cp(("arbitrary", "arbitrary")),
        name="norm_mod",
    )(x3, g, ops[0][0], ops[1][0])


def _mm_body(*refs, x_of_w, w_is_t, n_x, n_e, n_o, epilogue):
    n_w = len(x_of_w)
    xs = refs[:n_x]
    ws = refs[n_x:n_x + n_w]
    es = refs[n_x + n_w:n_x + n_w + n_e]
    outs = refs[n_x + n_w + n_e:n_x + n_w + n_e + n_o]
    wsc = refs[n_x + n_w + n_e + n_o:]

    @pl.when(pl.program_id(1) == 0)
    def _():
        for w, s, is_t in zip(ws, wsc, w_is_t):
            s[...] = (w[...].T if is_t else w[...]).astype(BF16)

    accs = [_dot(xs[xi][...], s[...]) for xi, s in zip(x_of_w, wsc)]
    epilogue(accs, es, outs)


def matmul(xs, ws, extras, outs, epilogue, *, x_of_w, tm, tn, name, vmem_mb=48, w_buffers=2):
    M = xs[0].shape[0]
    N = outs[0][0]
    in_specs = [pl.BlockSpec((tm, x.shape[1]), lambda j, i: (i, 0)) for x in xs]
    args = list(xs)
    scratch = []
    for w3, l, col0, is_t in ws:
        if is_t:
            K = w3.shape[2]
            assert col0 % SUBLANE == 0
            in_specs.append(pl.BlockSpec((None, pl.Element(tn), pl.Element(K)), functools.partial(
                lambda j, i, l, c: (l, (c + j * (tn // SUBLANE)) * SUBLANE, 0),
                l=l, c=col0 // SUBLANE), pipeline_mode=pl.Buffered(w_buffers)))
        else:
            K = w3.shape[1]
            assert col0 % tn == 0
            in_specs.append(pl.BlockSpec((None, K, tn), functools.partial(
                lambda j, i, l, c: (l, 0, c + j), l=l, c=col0 // tn),
                pipeline_mode=pl.Buffered(w_buffers)))
        args.append(w3)
        scratch.append(pltpu.VMEM((K, tn), BF16))
    for arr, spec in extras:
        in_specs.append(spec)
        args.append(arr)
    body = functools.partial(_mm_body, x_of_w=tuple(x_of_w), w_is_t=tuple(w[3] for w in ws),
                             n_x=len(xs), n_e=len(extras), n_o=len(outs), epilogue=epilogue)
    res = pl.pallas_call(
        body,
        grid=(N // tn, M // tm),
        in_specs=in_specs,
        out_specs=[pl.BlockSpec((tm, tn), lambda j, i: (i, j)) for _ in outs],
        out_shape=[jax.ShapeDtypeStruct((M, n), dt) for n, dt in outs],
        scratch_shapes=scratch,
        compiler_params=_cp(("arbitrary", "arbitrary"), vmem_mb),
        name=name,
    )(*args)
    return res


def _epi_store(accs, es, outs):
    outs[0][...] = accs[0].astype(outs[0].dtype)


def _epi_merge(accs, es, outs):
    pa, pb, pc, ga, gb, gc = accs
    m = _sigmoid(ga) * pa + _sigmoid(gb) * pb + _sigmoid(gc) * pc
    outs[0][...] = m.astype(outs[0].dtype)


def _res_norm_body(*refs, cast_w, modulated, emit_x):
    a_ref, w_ref, res_ref, gt_ref, g_ref = refs[:5]
    refs = refs[5:]
    if modulated:
        sc_ref, sh_ref = refs[:2]
        refs = refs[2:]
    if emit_x:
        x_ref = refs[0]
        refs = refs[1:]
    h_ref = refs[0]
    if cast_w:
        w_sc = refs[1]

        @pl.when(pl.program_id(0) == 0)
        def _():
            w_sc[...] = w_ref[...].astype(BF16)

        w = w_sc[...]
    else:
        w = w_ref[...]
    x = res_ref[...] + gt_ref[...] * _dot(a_ref[...], w)
    if emit_x:
        x_ref[...] = x
    y = _rms(x) * g_ref[...]
    if modulated:
        y = y * (1.0 + sc_ref[...]) + sh_ref[...]
    h_ref[...] = y.astype(h_ref.dtype)


def residual_norm(act, w3, l, res, gt, g, mod, *, tm, seq_len, emit_x, name):
    M, K = act.shape
    D = res.shape[1]
    tiles_per_seq = max(seq_len // tm, 1)
    cast_w = w3.dtype != BF16
    operand = lambda m: _mod_operand(m, tm, D, lambda i: i, lambda i: i // tiles_per_seq)

    row = lambda w: pl.BlockSpec((tm, w), lambda i: (i, 0))
    gt_arr, gt_spec = operand(gt)
    in_specs = [row(K),
                pl.BlockSpec((None, K, D), lambda i: (l, 0, 0), pipeline_mode=pl.Buffered(1)),
                row(D), gt_spec, pl.BlockSpec((1, D), lambda i: (0, 0))]
    args = [act, w3, res, gt_arr, g]
    if mod is not None:
        for m_arr, m_spec in map(operand, mod):
            in_specs.append(m_spec)
            args.append(m_arr)
    out_specs = [row(D)] * (2 if emit_x else 1)
    out_shape = ([jax.ShapeDtypeStruct((M, D), F32)] if emit_x else []) + [
        jax.ShapeDtypeStruct((M, D), BF16 if mod is not None else F32)]
    outs = pl.pallas_call(
        functools.partial(_res_norm_body, cast_w=cast_w, modulated=mod is not None, emit_x=emit_x),
        grid=(M // tm,),
        in_specs=in_specs,
        out_specs=out_specs,
        out_shape=out_shape,
        scratch_shapes=[pltpu.VMEM((K, D), BF16)] if cast_w else [],
        compiler_params=_cp(("arbitrary",), 56),
        name=name,
    )(*args)
    return (outs[0], outs[1]) if emit_x else (None, outs[0])


def _s5_prep_body(lrf_ref, lif_ref, ldf_ref, lrr_ref, lir_ref, ldr_ref, bre_ref, bim_ref,
                  tab_ref, bbr_ref, bbi_ref):
    dtf = jnp.exp(ldf_ref[...])
    _power_tables(tab_ref, lrf_ref[...] * dtf, lif_ref[...] * dtf, 1)
    dtr = jnp.exp(ldr_ref[...])
    lr = lrr_ref[...]
    li = lir_ref[...]
    m1 = jnp.exp(lr * dtr)
    ar = m1 * jnp.cos(li * dtr)
    ai = m1 * jnp.sin(li * dtr)
    den = lr * lr + li * li
    nr = ar - 1.0
    kr = (nr * lr + ai * li) / den
    ki = (ai * lr - nr * li) / den
    bre = bre_ref[...]
    bim = bim_ref[...]
    bbr_ref[...] = kr * bre - ki * bim
    bbi_ref[...] = kr * bim + ki * bre


def s5_prep(lam_re, lam_im, log_dt, b_re, b_im):
    lrf = lam_re.reshape(1, N_S5)
    lif = lam_im.reshape(1, N_S5)
    ldf = jnp.repeat(log_dt, P_A).reshape(1, N_S5)
    lrr = jnp.repeat(lam_re, S5_GROUP, axis=0)
    lir = jnp.repeat(lam_im, S5_GROUP, axis=0)
    ldr = jnp.repeat(log_dt, S5_GROUP).reshape(D_A, 1)
    bre = jnp.transpose(b_re, (0, 2, 1)).reshape(D_A, P_A)
    bim = jnp.transpose(b_im, (0, 2, 1)).reshape(D_A, P_A)
    tab, bbr, bbi = pl.pallas_call(
        _s5_prep_body,
        out_shape=[jax.ShapeDtypeStruct((8, SUBLANE, N_S5), F32),
                   jax.ShapeDtypeStruct((D_A, P_A), F32),
                   jax.ShapeDtypeStruct((D_A, P_A), F32)],
        name="s5_prep",
    )(lrf, lif, ldf, lrr, lir, ldr, bre, bim)
    eye = jnp.eye(S5_GPB, dtype=F32)

    def blockdiag(m):
        m = m.reshape(S5_NB, S5_GPB, S5_GROUP, P_A)
        return jnp.einsum('kghp,gj->kghjp', m, eye).reshape(S5_NB, LANE, S5_SPB)

    wb = jnp.concatenate([blockdiag(bbr), blockdiag(bbi)], axis=2).astype(BF16)
    return tab, wb, bbr, bbi


def _power_tables(tab_ref, lr_dt, li_dt, stride):
    n_lanes = lr_dt.shape[1]
    row = lax.broadcasted_iota(jnp.int32, (SUBLANE, n_lanes), 0)
    n = ((row + 1) * stride).astype(F32)
    mag = jnp.exp(n * lr_dt)
    ang = n * li_dt
    pr = mag * jnp.cos(ang)
    pi = mag * jnp.sin(ang)
    for k, d in enumerate((1, 2, 4)):
        keep = row >= d
        tab_ref[2 * k] = jnp.where(keep, pr[d - 1:d, :], 0.0)
        tab_ref[2 * k + 1] = jnp.where(keep, pi[d - 1:d, :], 0.0)
    tab_ref[6] = pr
    tab_ref[7] = pi


def _s5_chunk_prep_body(lrf_ref, lif_ref, ldf_ref, lrr_ref, lir_ref, ldr_ref, bbr_ref, bbi_ref,
                        cr_ref, ci_ref, tab_ref, wst_ref, kt_ref, wor_ref, woi_ref):
    T = S5_T
    dtf = jnp.exp(ldf_ref[...])
    _power_tables(tab_ref, lrf_ref[...] * dtf, lif_ref[...] * dtf, T)
    dtr = jnp.exp(ldr_ref[...])
    lr = lrr_ref[...] * dtr
    li = lir_ref[...] * dtr
    bbr = bbr_ref[...]
    bbi = bbi_ref[...]
    cr = cr_ref[...]
    ci = ci_ref[...]
    spread = jnp.where(lax.broadcasted_iota(jnp.int32, (P_A, S5_SPB), 0)
                       == lax.broadcasted_iota(jnp.int32, (P_A, S5_SPB), 1) % P_A, 1.0, 0.0).astype(BF16)
    spread_t = jnp.where(lax.broadcasted_iota(jnp.int32, (S5_SPB, P_A), 0) % P_A
                         == lax.broadcasted_iota(jnp.int32, (S5_SPB, P_A), 1), 1.0, 0.0).astype(BF16)
    in_mask = (lax.broadcasted_iota(jnp.int32, (LANE, S5_SPB), 0) // S5_GROUP
               == lax.broadcasted_iota(jnp.int32, (LANE, S5_SPB), 1) // P_A)
    out_mask = (lax.broadcasted_iota(jnp.int32, (S5_SPB, LANE), 0) // P_A
                == lax.broadcasted_iota(jnp.int32, (S5_SPB, LANE), 1) // S5_GROUP)
    k_mask = (lax.broadcasted_iota(jnp.int32, (LANE, LANE), 0) // S5_GROUP
              == lax.broadcasted_iota(jnp.int32, (LANE, LANE), 1) // S5_GROUP)
    zero_tile = jnp.zeros((LANE, LANE), BF16)
    mag = jnp.exp(lr)
    a_r = mag * jnp.cos(li)
    a_i = mag * jnp.sin(li)
    pr = jnp.ones_like(lr)
    pi = jnp.zeros_like(lr)
    for n in range(T + 1):
        if n >= 1:
            pr, pi = pr * a_r - pi * a_i, pr * a_i + pi * a_r
        car = (cr * pr - ci * pi).astype(BF16)
        cai = (cr * pi + ci * pr).astype(BF16)
        for k in range(S5_NB):
            blk = slice(k * LANE, (k + 1) * LANE)
            if n >= 1:
                cols = slice((n - 1) * LANE, n * LANE)
                wor_ref[k, :, cols] = jnp.where(out_mask, _dot_nt(spread_t, car[blk]), 0.0).astype(BF16)
                woi_ref[k, :, cols] = jnp.where(out_mask, _dot_nt(spread_t, cai[blk]), 0.0).astype(BF16)
            if n < T:
                rows = slice((T - 1 - n) * LANE, (T - n) * LANE)
                wr = (pr * bbr - pi * bbi)[blk].astype(BF16)
                wi = (pr * bbi + pi * bbr)[blk].astype(BF16)
                wst_ref[k, rows, 0:S5_SPB] = jnp.where(in_mask, _dot(wr, spread), 0.0).astype(BF16)
                wst_ref[k, rows, S5_SPB:2 * S5_SPB] = jnp.where(in_mask, _dot(wi, spread), 0.0).astype(BF16)
                kt = (_dot_nt(bbr[blk].astype(BF16), car[blk])
                      - _dot_nt(bbi[blk].astype(BF16), cai[blk]))
                kt = jnp.where(k_mask, kt, 0.0).astype(BF16)
                for ti in range(T - n):
                    to = ti + n
                    kt_ref[k, ti * LANE:(ti + 1) * LANE, to * LANE:(to + 1) * LANE] = kt
                    if n >= 1:
                        kt_ref[k, to * LANE:(to + 1) * LANE, ti * LANE:(ti + 1) * LANE] = zero_tile


def s5_chunk_prep(lam_re, lam_im, log_dt, bbr, bbi, c_re, c_im):
    T = S5_T
    lrf = lam_re.reshape(1, N_S5)
    lif = lam_im.reshape(1, N_S5)
    ldf = jnp.repeat(log_dt, P_A).reshape(1, N_S5)
    lrr = jnp.repeat(lam_re, S5_GROUP, axis=0)
    lir = jnp.repeat(lam_im, S5_GROUP, axis=0)
    ldr = jnp.repeat(log_dt, S5_GROUP).reshape(D_A, 1)
    op = lambda r, c: jax.ShapeDtypeStruct((S5_NB, r, c), BF16)
    return pl.pallas_call(
        _s5_chunk_prep_body,
        out_shape=[jax.ShapeDtypeStruct((8, SUBLANE, N_S5), F32), op(T * LANE, 2 * S5_SPB),
                   op(T * LANE, T * LANE), op(S5_SPB, T * LANE), op(S5_SPB, T * LANE)],
        compiler_params=pltpu.CompilerParams(vmem_limit_bytes=48 * 1024 * 1024),
        name="s5_chunk_prep",
    )(lrf, lif, ldf, lrr, lir, ldr, bbr, bbi, c_re.reshape(D_A, P_A), c_im.reshape(D_A, P_A))


def _s5_c_body(u_ref, wst_ref, kt_ref, wor_ref, woi_ref, tab_ref, d_ref, wglu_ref,
               o_ref, hr_ref, hi_ref, s_sc, car_sc, *, tb):
    T = S5_T

    @pl.when(pl.program_id(1) == 0)
    def _():
        car_sc[...] = jnp.zeros_like(car_sc)

    s_sc[0:SUBLANE, :] = car_sc[...]
    x = u_ref[...]
    xb = x.astype(BF16)
    bw = T * LANE
    for kb in range(S5_NB):
        s = _dot(xb[:, kb * bw:(kb + 1) * bw], wst_ref[kb])
        s_sc[SUBLANE:SUBLANE + tb, kb * S5_SPB:(kb + 1) * S5_SPB] = s[:, :S5_SPB]
        s_sc[SUBLANE:SUBLANE + tb, N_S5 + kb * S5_SPB:N_S5 + (kb + 1) * S5_SPB] = s[:, S5_SPB:]

    def tile(rt, _):
        r0 = pl.multiple_of(SUBLANE + rt * SUBLANE, SUBLANE)
        for lg in range(N_S5 // LANE):
            cre = slice(lg * LANE, (lg + 1) * LANE)
            cim = slice(N_S5 + lg * LANE, N_S5 + (lg + 1) * LANE)
            xr = s_sc[pl.ds(r0, SUBLANE), cre]
            xi = s_sc[pl.ds(r0, SUBLANE), cim]
            for k, d in enumerate((1, 2, 4)):
                a_r = tab_ref[2 * k, :, cre]
                a_i = tab_ref[2 * k + 1, :, cre]
                sr = pltpu.roll(xr, d, 0)
                si = pltpu.roll(xi, d, 0)
                xr, xi = xr + a_r * sr - a_i * si, xi + a_r * si + a_i * sr
            p_r = tab_ref[6, :, cre]
            p_i = tab_ref[7, :, cre]
            cr = car_sc[:, cre]
            ci = car_sc[:, cim]
            xr, xi = xr + p_r * cr - p_i * ci, xi + p_r * ci + p_i * cr
            s_sc[pl.ds(r0, SUBLANE), cre] = xr
            s_sc[pl.ds(r0, SUBLANE), cim] = xi
            last = SUBLANE - 1
            car_sc[:, cre] = jnp.broadcast_to(xr[last:last + 1, :], (SUBLANE, LANE))
            car_sc[:, cim] = jnp.broadcast_to(xi[last:last + 1, :], (SUBLANE, LANE))
        return 0

    lax.fori_loop(0, tb // SUBLANE, tile, 0)

    ys = []
    for kb in range(S5_NB):
        h_r = s_sc[SUBLANE - 1:SUBLANE - 1 + tb, kb * S5_SPB:(kb + 1) * S5_SPB].astype(BF16)
        h_i = s_sc[SUBLANE - 1:SUBLANE - 1 + tb,
                   N_S5 + kb * S5_SPB:N_S5 + (kb + 1) * S5_SPB].astype(BF16)
        ys.append(_dot(xb[:, kb * bw:(kb + 1) * bw], kt_ref[kb])
                  + _dot(h_r, wor_ref[kb]) - _dot(h_i, woi_ref[kb]))
    for t in range(T):
        y = jnp.concatenate([ys[kb][:, t * LANE:(t + 1) * LANE] for kb in range(S5_NB)], axis=1)
        u = jnp.concatenate([x[:, kb * bw + t * LANE:kb * bw + (t + 1) * LANE]
                             for kb in range(S5_NB)], axis=1)
        y = jax.nn.gelu(y + d_ref[...] * u)
        y = y * _sigmoid(_dot(y.astype(BF16), wglu_ref[...]))
        o_ref[:, t * D_A:(t + 1) * D_A] = y.astype(o_ref.dtype)
    hr_ref[...] = car_sc[0:1, :N_S5]
    hi_ref[...] = car_sc[0:1, N_S5:]


def s5_prompt(u3, ops, d_skip, wglu, *, tb=256):
    tab, wst, ktoe, wor, woi = ops
    B, L, _ = u3.shape
    T = S5_T
    nc = L // T
    tb = min(tb, nc)
    bw = T * LANE
    u8 = jnp.transpose(u3.reshape(B, nc, T, S5_NB, LANE), (0, 1, 3, 2, 4)).reshape(B, nc, S5_NB * bw)
    const3 = lambda b, t: (0, 0, 0)
    once = dict(pipeline_mode=pl.Buffered(1))
    o8, hr, hi = pl.pallas_call(
        functools.partial(_s5_c_body, tb=tb),
        grid=(B, nc // tb),
        in_specs=[pl.BlockSpec((None, tb, S5_NB * bw), lambda b, t: (b, t, 0)),
                  pl.BlockSpec((S5_NB, bw, 2 * S5_SPB), const3, **once),
                  pl.BlockSpec((S5_NB, bw, bw), const3, **once),
                  pl.BlockSpec((S5_NB, S5_SPB, bw), const3, **once),
                  pl.BlockSpec((S5_NB, S5_SPB, bw), const3, **once),
                  pl.BlockSpec((8, SUBLANE, N_S5), const3, **once),
                  pl.BlockSpec((1, D_A), lambda b, t: (0, 0)),
                  pl.BlockSpec((D_A, D_A), lambda b, t: (0, 0))],
        out_specs=[pl.BlockSpec((None, tb, T * D_A), lambda b, t: (b, t, 0)),
                   pl.BlockSpec((None, 1, N_S5), lambda b, t: (b, 0, 0)),
                   pl.BlockSpec((None, 1, N_S5), lambda b, t: (b, 0, 0))],
        out_shape=[jax.ShapeDtypeStruct((B, nc, T * D_A), BF16),
                   jax.ShapeDtypeStruct((B, 1, N_S5), F32),
                   jax.ShapeDtypeStruct((B, 1, N_S5), F32)],
        scratch_shapes=[pltpu.VMEM((SUBLANE + tb, 2 * N_S5), F32),
                        pltpu.VMEM((SUBLANE, 2 * N_S5), F32)],
        compiler_params=_cp(("arbitrary", "arbitrary"), 56),
        name="s5_prompt",
    )(u8, wst, ktoe, wor, woi, tab, d_skip, wglu)
    return o8.reshape(B, L, D_A), hr, hi


def s5_out_weights(c_re, c_im):
    eye = jnp.eye(S5_GPB, dtype=F32)

    def blockdiag(c):
        c = c.reshape(S5_NB, S5_GPB, S5_GROUP, P_A)
        return jnp.einsum('kghp,gj->kgpjh', c, eye).reshape(S5_NB, S5_SPB, LANE).astype(BF16)

    return blockdiag(c_re), blockdiag(c_im)


def _s5_in(ub, wb_ref, kb):
    return _dot(ub[:, kb * LANE:(kb + 1) * LANE], wb_ref[kb])


def _s5_tail(h_blocks, u, wcr_ref, wci_ref, d_ref, wglu_ref):
    ys = []
    for kb in range(S5_NB):
        hr, hi = h_blocks(kb)
        ys.append(_dot(hr.astype(BF16), wcr_ref[kb]) - _dot(hi.astype(BF16), wci_ref[kb]))
    y = jnp.concatenate(ys, axis=1) + d_ref[...] * u
    y = jax.nn.gelu(y)
    return y * _sigmoid(_dot(y.astype(BF16), wglu_ref[...]))


def _s5_s_body(u_ref, h0r_ref, h0i_ref, wb_ref, wcr_ref, wci_ref, tab_ref, d_ref, wglu_ref,
               o_ref, hr_ref, hi_ref):
    u = u_ref[...]
    ub = u.astype(BF16)
    for kb in range(S5_NB):
        cols = slice(kb * S5_SPB, (kb + 1) * S5_SPB)
        bu = _s5_in(ub, wb_ref, kb)
        a_r = tab_ref[6, 0:1, cols]
        a_i = tab_ref[7, 0:1, cols]
        h0r = h0r_ref[:, cols]
        h0i = h0i_ref[:, cols]
        hr_ref[:, cols] = a_r * h0r - a_i * h0i + bu[:, :S5_SPB]
        hi_ref[:, cols] = a_r * h0i + a_i * h0r + bu[:, S5_SPB:]

    def h_blocks(kb):
        cols = slice(kb * S5_SPB, (kb + 1) * S5_SPB)
        return hr_ref[:, cols], hi_ref[:, cols]

    o_ref[...] = _s5_tail(h_blocks, u, wcr_ref, wci_ref, d_ref, wglu_ref).astype(o_ref.dtype)


def s5_sample(proj, h0r, h0i, wb, wcr, wci, tab, d_skip, wglu):
    R = proj.shape[0]
    z2 = lambda i: (0, 0)
    return pl.pallas_call(
        _s5_s_body,
        grid=(1,),
        in_specs=[pl.BlockSpec((R, D_A), z2),
                  pl.BlockSpec((R, N_S5), z2), pl.BlockSpec((R, N_S5), z2),
                  pl.BlockSpec((S5_NB, LANE, 2 * S5_SPB), lambda i: (0, 0, 0)),
                  pl.BlockSpec((S5_NB, S5_SPB, LANE), lambda i: (0, 0, 0)),
                  pl.BlockSpec((S5_NB, S5_SPB, LANE), lambda i: (0, 0, 0)),
                  pl.BlockSpec((8, SUBLANE, N_S5), lambda i: (0, 0, 0)),
                  pl.BlockSpec((1, D_A), z2), pl.BlockSpec((D_A, D_A), z2)],
        out_specs=[pl.BlockSpec((R, D_A), z2), pl.BlockSpec((R, N_S5), z2), pl.BlockSpec((R, N_S5), z2)],
        out_shape=[jax.ShapeDtypeStruct((R, D_A), BF16),
                   jax.ShapeDtypeStruct((R, N_S5), F32),
                   jax.ShapeDtypeStruct((R, N_S5), F32)],
        compiler_params=_cp(("arbitrary",)),
        name="s5_sample",
    )(proj, h0r, h0i, wb, wcr, wci, tab, d_skip, wglu)


def _gmlp_p_body(u_ref, v_ref, gv_ref, wcat_ref, bias_ref, o_ref, *, tb):
    hd = D_B // H_B
    lane_head = lax.broadcasted_iota(jnp.int32, (GM_CHUNK, D_B), 1) // hd
    wi = lax.broadcasted_iota(jnp.int32, (GM_CHUNK, H_B * GM_CHUNK), 0)
    wj = lax.broadcasted_iota(jnp.int32, (GM_CHUNK, H_B * GM_CHUNK), 1) % GM_CHUNK
    wcat = jnp.where(wi >= wj, wcat_ref[...], 0.0).astype(BF16)
    for c in range(tb // GM_CHUNK):
        rows = slice(c * GM_CHUNK, (c + 1) * GM_CHUNK)
        vn = _rms(jax.nn.gelu(v_ref[rows, :])) * gv_ref[...]
        vb = vn.astype(BF16)
        stack = jnp.concatenate(
            [jnp.where(lane_head == h, vb, jnp.zeros_like(vb)) for h in range(H_B)], axis=0)
        s = _dot(wcat, stack) + bias_ref[...]
        o_ref[rows, :] = (jax.nn.gelu(u_ref[rows, :]) * s).astype(o_ref.dtype)


def gmlp_prompt(proj3, g_v, w_s, b_s, *, tb=512):
    B, L = proj3.shape[:2]
    hd = D_B // H_B
    wcat = jnp.transpose(w_s, (1, 0, 2)).reshape(GM_CHUNK, H_B * GM_CHUNK)
    bias = jnp.repeat(b_s.T, hd, axis=1)
    const2 = lambda b, t: (0, 0)
    ub = D_A // D_B
    return pl.pallas_call(
        functools.partial(_gmlp_p_body, tb=tb),
        grid=(B, L // tb),
        in_specs=[pl.BlockSpec((None, tb, D_B), lambda b, t: (b, t, ub)),
                  pl.BlockSpec((None, tb, D_B), lambda b, t: (b, t, ub + 1)),
                  pl.BlockSpec((1, D_B), const2),
                  pl.BlockSpec((GM_CHUNK, H_B * GM_CHUNK), const2),
                  pl.BlockSpec((GM_CHUNK, D_B), const2)],
        out_specs=pl.BlockSpec((None, tb, D_B), lambda b, t: (b, t, 0)),
        out_shape=jax.ShapeDtypeStruct((B, L, D_B), BF16),
        compiler_params=_cp(("arbitrary", "arbitrary")),
        name="gmlp_prompt",
    )(proj3, proj3, g_v, wcat, bias)


def _gmlp_s_body(u_ref, v_ref, gv_ref, w0_ref, b0_ref, o_ref, vn_ref):
    vn = _rms(jax.nn.gelu(v_ref[...])) * gv_ref[...]
    vn_ref[...] = vn
    s = w0_ref[...] * vn + b0_ref[...]
    o_ref[...] = (jax.nn.gelu(u_ref[...]) * s).astype(o_ref.dtype)


def gmlp_sample(proj, g_v, w_s, b_s):
    R = proj.shape[0]
    hd = D_B // H_B
    w0 = jnp.repeat(w_s[:, 0, 0], hd).reshape(1, D_B)
    b0 = jnp.repeat(b_s[:, 0], hd).reshape(1, D_B)
    z2 = lambda i: (0, 0)
    ub = D_A // D_B
    return pl.pallas_call(
        _gmlp_s_body,
        grid=(1,),
        in_specs=[pl.BlockSpec((R, D_B), lambda i: (0, ub)),
                  pl.BlockSpec((R, D_B), lambda i: (0, ub + 1)),
                  pl.BlockSpec((1, D_B), z2), pl.BlockSpec((1, D_B), z2), pl.BlockSpec((1, D_B), z2)],
        out_specs=[pl.BlockSpec((R, D_B), z2), pl.BlockSpec((R, D_B), z2)],
        out_shape=[jax.ShapeDtypeStruct((R, D_B), BF16), jax.ShapeDtypeStruct((R, D_B), F32)],
        compiler_params=_cp(("arbitrary",)),
        name="gmlp_sample",
    )(proj, proj, g_v, w0, b0)


def _head_expand_matrix():
    r = lax.broadcasted_iota(jnp.int32, (LANE, D_C), 0)
    c = lax.broadcasted_iota(jnp.int32, (LANE, D_C), 1) // HD_C
    return jnp.where(r == c, 1.0, 0.0).astype(BF16)


def _expand_heads(v, e):
    hi, lo = _split_bf16(v)
    return _dot(hi, e) + _dot(lo, e)


def _ssd_p_body(z_ref, xbc_ref, dt_ref, cw_ref, cb_ref, dtb_ref, alog_ref, dsk_ref, gn_ref,
                o_ref, st_ref, h_ref, h_sc, cv_sc, *, nt):
    Q = SSD_CHUNK
    t = pl.program_id(1)

    @pl.when(t == 0)
    def _():
        h_sc[...] = jnp.zeros_like(h_sc)
        cv_sc[0:SUBLANE, :] = jnp.zeros((SUBLANE, D_XBC), F32)

    xbc = xbc_ref[...]
    cv_sc[SUBLANE:SUBLANE + Q, :] = xbc
    acc = cb_ref[...]
    for k in range(K_C):
        off = SUBLANE - (K_C - 1) + k
        acc = acc + cw_ref[k:k + 1, :] * cv_sc[off:off + Q, :]
    cv_sc[0:SUBLANE, :] = xbc[Q - SUBLANE:, :]
    st_ref[...] = xbc[Q - (K_C - 1):, :]
    xc = _silu(acc)
    xs = xc[:, :D_C]

    dt = _softplus(dt_ref[...] + dtb_ref[...])
    a = -jnp.exp(alog_ref[...])
    da = dt * a
    ii = lax.broadcasted_iota(jnp.int32, (Q, Q), 0)
    jj = lax.broadcasted_iota(jnp.int32, (Q, Q), 1)
    causal = ii >= jj
    tril = jnp.where(causal, 1.0, 0.0).astype(BF16)
    d0, d1 = _split_bf16(da)
    d2 = (da - d0.astype(F32) - d1.astype(F32)).astype(BF16)
    cs = _dot(tril, d0) + _dot(tril, d1) + _dot(tril, d2)
    cst = cs.T
    cs_end = cs[Q - 1:Q, :]
    e = _head_expand_matrix()
    dt_f = _expand_heads(dt, e)
    ws_f = _expand_heads(dt * jnp.exp(cs_end - cs), e)
    ecs_f = _expand_heads(jnp.exp(cs), e)
    xdt = xs * dt_f
    xw = xs * ws_f

    lane = lax.broadcasted_iota(jnp.int32, (Q, LANE), 1)
    hpg = H_C // G_C
    gw = hpg * HD_C
    ys = []
    for g in range(G_C):
        bg = xc[:, D_C + g * N_C:D_C + (g + 1) * N_C].astype(BF16)
        cg = xc[:, D_C + G_C * N_C + g * N_C:D_C + G_C * N_C + (g + 1) * N_C].astype(BF16)
        gmat = _dot_nt(cg, bg)
        hprev = h_sc[g * gw:(g + 1) * gw, :]
        yoff = _dot_nt(cg, hprev.astype(BF16)) * ecs_f[:, g * gw:(g + 1) * gw]
        snew = _dot_tn(xw[:, g * gw:(g + 1) * gw].astype(BF16), bg)
        for hp in range(hpg // 2):
            h0 = g * hpg + 2 * hp
            xpair = xdt[:, h0 * HD_C:(h0 + 2) * HD_C]
            x_lo = jnp.where(lane < HD_C, xpair, 0.0).astype(BF16)
            x_hi = jnp.where(lane >= HD_C, xpair, 0.0).astype(BF16)
            yd = None
            for hh, xh in ((h0, x_lo), (h0 + 1, x_hi)):
                seg = cs[:, hh:hh + 1] - cst[hh:hh + 1, :]
                sc = (gmat * jnp.exp(jnp.where(causal, seg, NEG_BIG))).astype(BF16)
                part = _dot(sc, xh)
                yd = part if yd is None else yd + part
            ys.append(yd + yoff[:, 2 * hp * HD_C:(2 * hp + 2) * HD_C])
        for hh in range(hpg):
            h = g * hpg + hh
            cd = jnp.exp(cst[h:h + 1, Q - 1:Q])
            rows = slice(h * HD_C, (h + 1) * HD_C)
            h_sc[rows, :] = h_sc[rows, :] * cd + snew[hh * HD_C:(hh + 1) * HD_C, :]

    y = jnp.concatenate(ys, axis=1) + dsk_ref[...] * xs
    y = y * _silu(z_ref[...])
    o_ref[...] = (_rms(y) * gn_ref[...]).astype(o_ref.dtype)

    @pl.when(t == nt - 1)
    def _():
        h_ref[...] = h_sc[...]


def _ssd_params(dt_bias, a_log, d_skip):
    pad = LANE - H_C
    dtb = jnp.pad(dt_bias, (0, pad)).reshape(1, LANE)
    alog = jnp.pad(a_log, (0, pad)).reshape(1, LANE)
    dsk = jnp.repeat(d_skip, HD_C).reshape(1, D_C)
    return dtb, alog, dsk


def ssd_prompt(z3, xd3, conv_w, conv_b, dt_bias, a_log, d_skip, g_norm):
    B, L = z3.shape[:2]
    Q = SSD_CHUNK
    nt = L // Q
    dtb, alog, dsk = _ssd_params(dt_bias, a_log, d_skip)
    const2 = lambda b, t: (0, 0)
    blk = lambda w: pl.BlockSpec((None, Q, w), lambda b, t: (b, t, 0))
    return pl.pallas_call(
        functools.partial(_ssd_p_body, nt=nt),
        grid=(B, nt),
        in_specs=[blk(D_C), blk(D_XBC),
                  pl.BlockSpec((None, Q, LANE), lambda b, t: (b, t, D_XBC // LANE)),
                  pl.BlockSpec((K_C, D_XBC), const2), pl.BlockSpec((1, D_XBC), const2),
                  pl.BlockSpec((1, LANE), const2), pl.BlockSpec((1, LANE), const2),
                  pl.BlockSpec((1, D_C), const2), pl.BlockSpec((1, D_C), const2)],
        out_specs=[blk(D_C),
                   pl.BlockSpec((None, K_C - 1, D_XBC), lambda b, t: (b, 0, 0)),
                   pl.BlockSpec((None, H_C * HD_C, N_C), lambda b, t: (b, 0, 0))],
        out_shape=[jax.ShapeDtypeStruct((B, L, D_C), BF16),
                   jax.ShapeDtypeStruct((B, K_C - 1, D_XBC), F32),
                   jax.ShapeDtypeStruct((B, H_C * HD_C, N_C), F32)],
        scratch_shapes=[pltpu.VMEM((H_C * HD_C, N_C), F32), pltpu.VMEM((SUBLANE + Q, D_XBC), F32)],
        compiler_params=_cp(("arbitrary", "arbitrary")),
        name="ssd_prompt",
    )(z3, xd3, xd3, conv_w, conv_b.reshape(1, D_XBC), dtb, alog, dsk, g_norm.reshape(1, D_C))


def _ssd_s_body(z_ref, xbc_ref, dt_ref, buf_ref, h0_ref, cw_ref, cb_ref, dtb_ref, alog_ref,
                dsk_ref, gn_ref, *rest, tbatch, nsteps, has_prev):
    o_ref, hn_ref, xs_sc, bc_sc, xt_sc, at_sc, y_sc = rest[1:] if has_prev else rest
    s = pl.program_id(0)
    R = xs_sc.shape[0]

    @pl.when(s == 0)
    def _():
        acc = cb_ref[...] + cw_ref[K_C - 1:K_C, :] * xbc_ref[...]
        for k in range(K_C - 1):
            acc = acc + cw_ref[k:k + 1, :] * buf_ref[k]
        xc = _silu(acc)
        xs = xc[:, :D_C]
        xs_sc[...] = xs
        bc_sc[...] = xc[:, D_C:]
        dt = _softplus(dt_ref[...] + dtb_ref[...])
        e = _head_expand_matrix()
        dt_f = _expand_heads(dt, e)
        da_f = jnp.exp(_expand_heads(dt * (-jnp.exp(alog_ref[...])), e))
        for t_sc, v in ((xt_sc, xs * dt_f), (at_sc, da_f)):
            hi, lo = _split_bf16(v.T)
            t_sc[:, :R] = hi
            t_sc[:, R:] = lo

    gw = (H_C // G_C) * HD_C
    kk = lax.broadcasted_iota(jnp.int32, (2 * R, LANE), 0) % R
    r0 = pl.multiple_of(s * tbatch, tbatch)
    bc8 = bc_sc[pl.ds(r0, tbatch), :]
    ti = lax.broadcasted_iota(jnp.int32, (tbatch, gw), 0)
    ytile = [jnp.zeros((tbatch, gw), F32) for _ in range(G_C)]
    for i in range(tbatch):
        b = s * tbatch + i
        onehot = jnp.where(kk == b, 1.0, 0.0).astype(BF16)
        xb = _dot(xt_sc[...], onehot)
        ab = _dot(at_sc[...], onehot)
        for g in range(G_C):
            rows = slice(g * gw, (g + 1) * gw)
            brow = bc8[i:i + 1, g * N_C:(g + 1) * N_C]
            hn = h0_ref[i, rows, :] * ab[rows, :] + xb[rows, :] * brow
            hn_ref[i, rows, :] = hn
            c8 = bc8[:, G_C * N_C + g * N_C:G_C * N_C + (g + 1) * N_C]
            yg = _dot_nt(c8.astype(BF16), hn.astype(BF16))
            ytile[g] = jnp.where(ti == i, yg, ytile[g])
    for g in range(G_C):
        y_sc[pl.ds(r0, tbatch), g * gw:(g + 1) * gw] = ytile[g]

    @pl.when(s == nsteps - 1)
    def _():
        y = y_sc[...] + dsk_ref[...] * xs_sc[...]
        y = y * _silu(z_ref[...])
        o_ref[...] = (_rms(y) * gn_ref[...]).astype(o_ref.dtype)


def ssd_sample(z, xd, buf, h0_all, l, hn_all, conv_w, conv_b, dt_bias, a_log, d_skip, g_norm,
               *, tbatch=16):
    R = z.shape[0]
    nsteps = R // tbatch
    dtb, alog, dsk = _ssd_params(dt_bias, a_log, d_skip)
    z2 = lambda s: (0, 0)
    hw = H_C * HD_C
    st_spec = pl.BlockSpec((None, tbatch, hw, N_C), lambda s: (l, s, 0, 0))
    in_specs = [pl.BlockSpec((R, D_C), z2), pl.BlockSpec((R, D_XBC), z2),
                pl.BlockSpec((R, LANE), lambda s: (0, D_XBC // LANE)),
                pl.BlockSpec((K_C - 1, R, D_XBC), lambda s: (0, 0, 0)),
                st_spec,
                pl.BlockSpec((K_C, D_XBC), z2), pl.BlockSpec((1, D_XBC), z2),
                pl.BlockSpec((1, LANE), z2), pl.BlockSpec((1, LANE), z2),
                pl.BlockSpec((1, D_C), z2), pl.BlockSpec((1, D_C), z2)]
    args = [z, xd, xd, buf, h0_all, conv_w, conv_b.reshape(1, D_XBC), dtb, alog, dsk,
            g_norm.reshape(1, D_C)]
    aliases = {}
    if hn_all is not None:
        aliases = {len(args): 1}
        in_specs.append(pl.BlockSpec(memory_space=pl.ANY))
        args.append(hn_all)
    return pl.pallas_call(
        functools.partial(_ssd_s_body, tbatch=tbatch, nsteps=nsteps, has_prev=hn_all is not None),
        grid=(nsteps,),
        in_specs=in_specs,
        out_specs=[pl.BlockSpec((R, D_C), z2), st_spec],
        out_shape=[jax.ShapeDtypeStruct((R, D_C), BF16),
                   jax.ShapeDtypeStruct(h0_all.shape, F32)],
        scratch_shapes=[pltpu.VMEM((R, D_C), F32), pltpu.VMEM((R, 2 * G_C * N_C), F32),
                        pltpu.VMEM((D_C, 2 * R), BF16), pltpu.VMEM((D_C, 2 * R), BF16),
                        pltpu.VMEM((R, D_C), F32)],
        input_output_aliases=aliases,
        compiler_params=_cp(("arbitrary",)),
        name="ssd_sample",
    )(*args)


def _up_p_body(x_ref, wa_ref, wb_ref, cwa_ref, cwb_ref, cba_ref, cbb_ref,
               o_ref, sa_ref, sb_ref, wa_sc, wb_sc, ca_sc, cb_sc, *, tm, tiles_per_seq):
    i = pl.program_id(1)

    @pl.when(i == 0)
    def _():
        wa_sc[...] = wa_ref[...].astype(BF16)
        wb_sc[...] = wb_ref[...].astype(BF16)

    @pl.when(i % tiles_per_seq == 0)
    def _():
        ca_sc[0:SUBLANE, :] = jnp.zeros((SUBLANE, ca_sc.shape[1]), F32)
        cb_sc[0:SUBLANE, :] = jnp.zeros((SUBLANE, cb_sc.shape[1]), F32)

    x = x_ref[...]

    def half(w_sc, c_sc, cw_ref, cb_ref, st_ref):
        up = _dot(x, w_sc[...])
        c_sc[SUBLANE:SUBLANE + tm, :] = up
        acc = cb_ref[...]
        for k in range(K_F):
            off = SUBLANE - (K_F - 1) + k
            acc = acc + cw_ref[k:k + 1, :] * c_sc[off:off + tm, :]
        c_sc[0:SUBLANE, :] = up[tm - SUBLANE:, :]
        st_ref[...] = up[tm - (K_F - 1):, :]
        return acc

    a = half(wa_sc, ca_sc, cwa_ref, cba_ref, sa_ref)
    b = half(wb_sc, cb_sc, cwb_ref, cbb_ref, sb_ref)
    o_ref[...] = (_silu(a) * b).astype(o_ref.dtype)


def up_prompt(h2, w_up, l, conv_w, conv_b, *, seq_len, tm=1024, tn=512):
    M, D = h2.shape
    B = M // seq_len
    tiles_per_seq = seq_len // tm
    nj = D_FF // tn
    cb = conv_b.reshape(1, 2 * D_FF)
    body = functools.partial(_up_p_body, tm=tm, tiles_per_seq=tiles_per_seq)
    st_spec = pl.BlockSpec((None, K_F - 1, tn), lambda j, i: (i // tiles_per_seq, 0, j))
    return pl.pallas_call(
        body,
        grid=(nj, M // tm),
        in_specs=[pl.BlockSpec((tm, D), lambda j, i: (i, 0)),
                  pl.BlockSpec((None, D, tn), lambda j, i: (l, 0, j)),
                  pl.BlockSpec((None, D, tn), lambda j, i: (l, 0, nj + j)),
                  pl.BlockSpec((K_F, tn), lambda j, i: (0, j)),
                  pl.BlockSpec((K_F, tn), lambda j, i: (0, nj + j)),
                  pl.BlockSpec((1, tn), lambda j, i: (0, j)),
                  pl.BlockSpec((1, tn), lambda j, i: (0, nj + j))],
        out_specs=[pl.BlockSpec((tm, tn), lambda j, i: (i, j)), st_spec, st_spec],
        out_shape=[jax.ShapeDtypeStruct((M, D_FF), BF16),
                   jax.ShapeDtypeStruct((B, K_F - 1, D_FF), F32),
                   jax.ShapeDtypeStruct((B, K_F - 1, D_FF), F32)],
        scratch_shapes=[pltpu.VMEM((D, tn), BF16), pltpu.VMEM((D, tn), BF16),
                        pltpu.VMEM((SUBLANE + tm, tn), F32), pltpu.VMEM((SUBLANE + tm, tn), F32)],
        compiler_params=_cp(("arbitrary", "arbitrary")),
        name="up_prompt",
    )(h2, w_up, w_up, conv_w, conv_w, cb, cb)


def _up_s_body(x_ref, w_ref, buf_ref, cw_ref, cb_ref, *rest, nj, has_prev):
    o_ref, new_ref, a_sc = rest[1:] if has_prev else rest
    s = pl.program_id(0)
    up = _dot(x_ref[...], w_ref[...].astype(BF16))
    acc = cb_ref[...] + cw_ref[K_F - 1:K_F, :] * up
    for k in range(K_F - 1):
        acc = acc + cw_ref[k:k + 1, :] * buf_ref[:, k, :]
    for k in range(1, K_F - 1):
        new_ref[:, k - 1, :] = buf_ref[:, k, :]
    new_ref[:, K_F - 2, :] = up

    @pl.when(s < nj)
    def _():
        a_sc[s] = acc

    @pl.when(s >= nj)
    def _():
        o_ref[...] = (_silu(a_sc[s - nj]) * acc).astype(o_ref.dtype)


def up_sample(h2, w_up, l, buf_all, new_all, conv_w, conv_b, *, tn=512):
    R, D = h2.shape
    nj = D_FF // tn
    st = pl.BlockSpec((None, R, K_F - 1, tn), lambda s: (l, 0, 0, s))
    in_specs = [pl.BlockSpec((R, D), lambda s: (0, 0)),
                pl.BlockSpec((None, D, tn), lambda s: (l, 0, s)),
                st,
                pl.BlockSpec((K_F, tn), lambda s: (0, s)),
                pl.BlockSpec((1, tn), lambda s: (0, s))]
    args = [h2, w_up, buf_all, conv_w, conv_b.reshape(1, 2 * D_FF)]
    aliases = {}
    if new_all is not None:
        aliases = {len(args): 1}
        in_specs.append(pl.BlockSpec(memory_space=pl.ANY))
        args.append(new_all)
    act, new = pl.pallas_call(
        functools.partial(_up_s_body, nj=nj, has_prev=new_all is not None),
        grid=(2 * nj,),
        in_specs=in_specs,
        out_specs=[pl.BlockSpec((R, tn), lambda s: (0, jnp.maximum(s - nj, 0))), st],
        out_shape=[jax.ShapeDtypeStruct((R, D_FF), BF16),
                   jax.ShapeDtypeStruct(buf_all.shape, F32)],
        scratch_shapes=[pltpu.VMEM((nj, R, tn), F32)],
        input_output_aliases=aliases,
        compiler_params=_cp(("arbitrary",)),
        name="up_sample",
    )(*args)
    return act, new


def _layer(x3, h, mods, l, p, state, next_norm, *, seq_len, tm, tn_merge):
    B, L, D = x3.shape
    M = B * L
    sh_m, sc_m, gt_m, sh_f, sc_f, gt_f = mods
    prompt = state is None
    w_buffers = 1 if M > tm else 2

    def proj(col0, n, name):
        return matmul([h], [(p['w_in_t'], l, col0, True)], [], [(n, F32)], _epi_store,
                      x_of_w=(0,), tm=tm, tn=n, name=name, vmem_mb=56, w_buffers=1)[0]

    uav = proj(OFF_UA, D_A + 2 * D_B, "proj_uav")
    zc = proj(OFF_Z, D_C, "proj_z")
    xd = proj(OFF_XBC, D_XBC + LANE, "proj_xd")

    tab, wb, bbr, bbi = s5_prep(p['lam_re'][l], p['lam_im'][l], p['log_dt'][l],
                                p['b_re'][l], p['b_im'][l])
    d_a = p['s5_d'][l].reshape(1, D_A)
    wglu = p['w_glu'][l].astype(BF16)
    g_v = p['g_v'][l].reshape(1, D_B)

    if prompt:
        s5_ops = s5_chunk_prep(p['lam_re'][l], p['lam_im'][l], p['log_dt'][l], bbr, bbi,
                               p['c_re'][l], p['c_im'][l])
        o_a, s5r, s5i = s5_prompt(uav.reshape(B, L, -1)[..., :D_A], s5_ops, d_a, wglu)
        o_b = gmlp_prompt(uav.reshape(B, L, -1), g_v, p['w_s'][l], p['b_s'][l])
        v_rows = None
        o_c, convc, ssm = ssd_prompt(zc.reshape(B, L, -1), xd.reshape(B, L, -1),
                                     p['ssd_conv_w'][l], p['ssd_conv_b'][l], p['dt_bias'][l],
                                     p['a_log'][l], p['ssd_d'][l], p['ssd_g'][l])
        s5r = s5r.reshape(B, G_A, P_A)
        s5i = s5i.reshape(B, G_A, P_A)
    else:
        s5_re0, s5_im0, ssm_all, ssm_new_all, convc0, ffn_all, ffn_new_all = state
        wcr, wci = s5_out_weights(p['c_re'][l], p['c_im'][l])
        o_a, s5r, s5i = s5_sample(uav, s5_re0.reshape(M, N_S5), s5_im0.reshape(M, N_S5),
                                  wb, wcr, wci, tab, d_a, wglu)
        o_b, v_rows = gmlp_sample(uav, g_v, p['w_s'][l], p['b_s'][l])
        o_c, ssm = ssd_sample(zc, xd, jnp.transpose(convc0, (1, 0, 2)),
                              ssm_all, l, ssm_new_all,
                              p['ssd_conv_w'][l], p['ssd_conv_b'][l], p['dt_bias'][l],
                              p['a_log'][l], p['ssd_d'][l], p['ssd_g'][l])
        convc = jnp.concatenate([convc0[:, 1:], xd[:, None, :D_XBC]], axis=1)
        s5r = s5r.reshape(M, G_A, P_A)
        s5i = s5i.reshape(M, G_A, P_A)
    o_a = o_a.reshape(M, D_A)
    o_b = o_b.reshape(M, D_B)
    o_c = o_c.reshape(M, D_C)
    if prompt:
        ssm = ssm.reshape(B, H_C, HD_C, N_C)

    w_t = p['w_in_t']
    merged = matmul(
        [h, o_a, o_b, o_c],
        [(p['w_pa'], l, 0, False), (p['w_pb'], l, 0, False), (p['w_pc'], l, 0, False),
         (w_t, l, OFF_GATES, True), (w_t, l, OFF_GATES + D, True), (w_t, l, OFF_GATES + 2 * D, True)],
        [], [(D, BF16)], _epi_merge, x_of_w=(1, 2, 3, 0, 0, 0), tm=tm, tn=tn_merge, name="merge",
        vmem_mb=56, w_buffers=w_buffers)[0]

    tm_r = min(tm, 256)
    x2, h2 = residual_norm(merged, p['w_out'], l, x3.reshape(M, D), gt_m,
                           p['g_ffn'][l].reshape(1, D), (sc_f, sh_f), tm=tm_r, seq_len=seq_len,
                           emit_x=True, name="out_norm")
    if prompt:
        act, st_a, st_b = up_prompt(h2, p['w_up'], l, p['ffn_conv_w'][l], p['ffn_conv_b'][l],
                                    seq_len=seq_len, tm=tm)
        convf = jnp.concatenate([st_a, st_b], axis=-1)
    else:
        act, convf = up_sample(h2, p['w_up'], l, ffn_all, ffn_new_all,
                               p['ffn_conv_w'][l], p['ffn_conv_b'][l])

    g_next, mod_next = next_norm
    x2, h_next = residual_norm(act, p['w_down_bf16'], l, x2, gt_f, g_next, mod_next, tm=tm_r,
                               seq_len=seq_len, emit_x=mod_next is not None, name="down_norm")
    x_out = None if x2 is None else x2.reshape(B, L, D)
    return x_out, h_next, s5r, s5i, ssm, convc, convf, v_rows


def kernel(x_prompt, x_sample, c_prompt, c_sample, state_s5_re, state_s5_im, state_ssm, state_ssd_conv, state_ffn_conv, w_mod, b_mod, g_mix, w_in, s5_lam_re, s5_lam_im, s5_log_dt, s5_b_re, s5_b_im, s5_c_re, s5_c_im, s5_d, s5_w_glu, gm_g_v, gm_w_s, gm_b_s, ssd_conv_w, ssd_conv_b, ssd_dt_bias, ssd_a_log, ssd_d, ssd_g_norm, w_pa, w_pb, w_pc, w_out, g_ffn, ffn_w_up, ffn_conv_w, ffn_conv_b, ffn_w_down, g_final):
    p = {
        'g_mix': g_mix, 'w_in_t': jnp.swapaxes(w_in, 1, 2),
        'lam_re': s5_lam_re, 'lam_im': s5_lam_im, 'log_dt': s5_log_dt,
        'b_re': s5_b_re, 'b_im': s5_b_im, 'c_re': s5_c_re, 'c_im': s5_c_im,
        's5_d': s5_d, 'w_glu': s5_w_glu,
        'g_v': gm_g_v, 'w_s': gm_w_s, 'b_s': gm_b_s,
        'ssd_conv_w': ssd_conv_w, 'ssd_conv_b': ssd_conv_b, 'dt_bias': ssd_dt_bias,
        'a_log': ssd_a_log, 'ssd_d': ssd_d, 'ssd_g': ssd_g_norm,
        'w_pa': w_pa, 'w_pb': w_pb, 'w_pc': w_pc, 'w_out': w_out,
        'g_ffn': g_ffn, 'w_up': ffn_w_up, 'ffn_conv_w': ffn_conv_w,
        'ffn_conv_b': ffn_conv_b, 'w_down_bf16': ffn_w_down.astype(BF16),
    }
    bp, seq, D = x_prompt.shape
    bs = x_sample.shape[0]

    n_c = bs + bp
    pad = (-n_c) % SUBLANE
    c_all = jnp.concatenate([c_sample, c_prompt, jnp.zeros((pad, D), F32)], axis=0)
    mod = mod_all(c_all, w_mod, b_mod)

    xp = x_prompt
    xs = x_sample.reshape(1, bs, D)
    outs_p = [[] for _ in range(5)]
    outs_s = [[] for _ in range(6)]
    ssm_all = state_ssm.reshape(DEPTH, bs, H_C * HD_C, N_C)
    ssm_new_all = None
    ffn_new_all = None
    mods_s = [[TokenMod(mod, l, k) for k in range(6)] for l in range(DEPTH)]
    mods_p = [[m[:, None, :] for m in jnp.split(mod[l, bs:bs + bp], 6, axis=-1)] for l in range(DEPTH)]

    def norm_after(l, mods):
        if l + 1 == DEPTH:
            return g_final.reshape(1, D), None
        return g_mix[l + 1].reshape(1, D), (mods[l + 1][1], mods[l + 1][0])

    g0 = g_mix[0].reshape(1, D)
    hp = norm_mod(xp, g0, mods_p[0][1], mods_p[0][0], tm=512).reshape(bp * seq, D)
    hs = norm_mod(xs, g0, mods_s[0][1], mods_s[0][0], tm=bs).reshape(bs, D)
    for l in range(DEPTH):
        xp, hp, *st_p = _layer(xp, hp, mods_p[l], l, p, None, norm_after(l, mods_p),
                               seq_len=seq, tm=1024, tn_merge=512)
        for acc, v in zip(outs_p, st_p[:5]):
            acc.append(v)
        state = (state_s5_re[l], state_s5_im[l], ssm_all, ssm_new_all, state_ssd_conv[l],
                 state_ffn_conv, ffn_new_all)
        xs, hs, *st_s = _layer(xs, hs, mods_s[l], l, p, state, norm_after(l, mods_s),
                               seq_len=1, tm=bs, tn_merge=256)
        ssm_new_all = st_s[2]
        ffn_new_all = st_s[4]
        st_s[5] = st_s[5].reshape(bs, 1, D_B)
        for acc, v in zip(outs_s, st_s):
            acc.append(v)

    y_prompt = hp.reshape(bp, seq, D)
    y_sample = hs.reshape(bs, 1, D)
    whole = {2: ssm_new_all.reshape(DEPTH, bs, H_C, HD_C, N_C),
             4: ffn_new_all}
    outs_s = [whole[k] if k in whole else jnp.stack(v) for k, v in enumerate(outs_s)]
    return (y_prompt, y_sample, *[jnp.stack(v) for v in outs_p], *outs_s)
```

```python
import functools
from typing import NamedTuple

import jax
import jax.numpy as jnp
from jax import lax
from jax.experimental import pallas as pl
from jax.experimental.pallas import tpu as pltpu

F32 = jnp.float32
BF16 = jnp.bfloat16

D_MODEL = 2048
DEPTH = 2
D_A = 512
S5_GROUP = 16
G_A = D_A // S5_GROUP
P_A = 64
N_S5 = G_A * P_A
S5_GPB = 128 // S5_GROUP
S5_NB = D_A // 128
S5_SPB = S5_GPB * P_A
S5_T = 8
D_B = 512
H_B = 8
GM_CHUNK = 128
D_C = 1024
HD_C = 64
H_C = D_C // HD_C
N_C = 128
G_C = 2
K_C = 4
SSD_CHUNK = 128
D_XBC = D_C + 2 * G_C * N_C
D_FF = 5632
K_F = 3
EPS = 1e-6

OFF_UA = 0
OFF_Z = D_A + 2 * D_B
OFF_XBC = OFF_Z + D_C
OFF_DT = OFF_XBC + D_XBC
OFF_GATES = OFF_DT + H_C

LANE = 128
SUBLANE = 8
NEG_BIG = -1e30


def _cp(sem, vmem_mb=48):
    return pltpu.CompilerParams(dimension_semantics=sem, vmem_limit_bytes=vmem_mb * 1024 * 1024)


def _sigmoid(x):
    return 0.5 * jnp.tanh(0.5 * x) + 0.5


def _silu(x):
    return x * _sigmoid(x)


def _softplus(x):
    return jnp.maximum(x, 0.0) + jnp.log1p(jnp.exp(-jnp.abs(x)))


def _rms(x):
    return x * lax.rsqrt(jnp.mean(x * x, axis=-1, keepdims=True) + EPS)


def _split_bf16(x):
    hi = x.astype(BF16)
    lo = (x - hi.astype(F32)).astype(BF16)
    return hi, lo


def _dot(a, b):
    return jnp.dot(a, b, preferred_element_type=F32)


def _dot_nt(a, b):
    return lax.dot_general(a, b, (((1,), (1,)), ((), ())), preferred_element_type=F32)


def _dot_tn(a, b):
    return lax.dot_general(a, b, (((0,), (0,)), ((), ())), preferred_element_type=F32)


def _mod_body(c_ref, w_ref, b_ref, o_ref):
    a = _silu(c_ref[...]).astype(BF16)
    o_ref[...] = _dot(a, w_ref[...].astype(BF16)) + b_ref[...]


def mod_all(c_all, w_mod, b_mod, *, tn=1024):
    R, D = c_all.shape
    N = w_mod.shape[-1]
    return pl.pallas_call(
        _mod_body,
        grid=(DEPTH, N // tn),
        in_specs=[pl.BlockSpec((R, D), lambda l, j: (0, 0)),
                  pl.BlockSpec((None, D, tn), lambda l, j: (l, 0, j)),
                  pl.BlockSpec((None, 1, tn), lambda l, j: (l, 0, j))],
        out_specs=pl.BlockSpec((None, R, tn), lambda l, j: (l, 0, j)),
        out_shape=jax.ShapeDtypeStruct((DEPTH, R, N), F32),
        compiler_params=_cp(("arbitrary", "arbitrary")),
        name="mod_all",
    )(c_all, w_mod, b_mod.reshape(DEPTH, 1, N))


def _norm_mod_body(x_ref, g_ref, sc_ref, sh_ref, o_ref):
    y = _rms(x_ref[...]) * g_ref[...]
    o_ref[...] = (y * (1.0 + sc_ref[...]) + sh_ref[...]).astype(o_ref.dtype)


class TokenMod(NamedTuple):
    arr: jax.Array
    l: int
    k: int


def _mod_operand(m, tm, D, tile_of, seq_of):
    if isinstance(m, TokenMod):
        return m.arr, pl.BlockSpec((None, tm, D), lambda *g: (m.l, tile_of(*g), m.k))
    return m, pl.BlockSpec((None, 1, D), lambda *g: (seq_of(*g), 0, 0))


def norm_mod(x3, g, sc, sh, *, tm):
    B, L, D = x3.shape
    ops = [_mod_operand(m, tm, D, lambda b, i: i, lambda b, i: b) for m in (sc, sh)]
    return pl.pallas_call(
        _norm_mod_body,
        grid=(B, L // tm),
        in_specs=[pl.BlockSpec((None, tm, D), lambda b, i: (b, i, 0)),
                  pl.BlockSpec((1, D), lambda b, i: (0, 0)),
                  ops[0][1], ops[1][1]],
        out_specs=pl.BlockSpec((None, tm, D), lambda b, i: (b, i, 0)),
        out_shape=jax.ShapeDtypeStruct((B, L, D), BF16),
        compiler_params=_cp(("arbitrary", "arbitrary")),
        name="norm_mod",
    )(x3, g, ops[0][0], ops[1][0])


def _mm_body(*refs, x_of_w, w_is_t, n_x, n_e, n_o, epilogue):
    n_w = len(x_of_w)
    xs = refs[:n_x]
    ws = refs[n_x:n_x + n_w]
    es = refs[n_x + n_w:n_x + n_w + n_e]
    outs = refs[n_x + n_w + n_e:n_x + n_w + n_e + n_o]
    wsc = refs[n_x + n_w + n_e + n_o:]

    @pl.when(pl.program_id(1) == 0)
    def _():
        for w, s, is_t in zip(ws, wsc, w_is_t):
            s[...] = (w[...].T if is_t else w[...]).astype(BF16)

    accs = [_dot(xs[xi][...], s[...]) for xi, s in zip(x_of_w, wsc)]
    epilogue(accs, es, outs)


def matmul(xs, ws, extras, outs, epilogue, *, x_of_w, tm, tn, name, vmem_mb=48, w_buffers=2):
    M = xs[0].shape[0]
    N = outs[0][0]
    in_specs = [pl.BlockSpec((tm, x.shape[1]), lambda j, i: (i, 0)) for x in xs]
    args = list(xs)
    scratch = []
    for w3, l, col0, is_t in ws:
        if is_t:
            K = w3.shape[2]
            assert col0 % SUBLANE == 0
            in_specs.append(pl.BlockSpec((None, pl.Element(tn), pl.Element(K)), functools.partial(
                lambda j, i, l, c: (l, (c + j * (tn // SUBLANE)) * SUBLANE, 0),
                l=l, c=col0 // SUBLANE), pipeline_mode=pl.Buffered(w_buffers)))
        else:
            K = w3.shape[1]
            assert col0 % tn == 0
            in_specs.append(pl.BlockSpec((None, K, tn), functools.partial(
                lambda j, i, l, c: (l, 0, c + j), l=l, c=col0 // tn),
                pipeline_mode=pl.Buffered(w_buffers)))
        args.append(w3)
        scratch.append(pltpu.VMEM((K, tn), BF16))
    for arr, spec in extras:
        in_specs.append(spec)
        args.append(arr)
    body = functools.partial(_mm_body, x_of_w=tuple(x_of_w), w_is_t=tuple(w[3] for w in ws),
                             n_x=len(xs), n_e=len(extras), n_o=len(outs), epilogue=epilogue)
    res = pl.pallas_call(
        body,
        grid=(N // tn, M // tm),
        in_specs=in_specs,
        out_specs=[pl.BlockSpec((tm, tn), lambda j, i: (i, j)) for _ in outs],
        out_shape=[jax.ShapeDtypeStruct((M, n), dt) for n, dt in outs],
        scratch_shapes=scratch,
        compiler_params=_cp(("arbitrary", "arbitrary"), vmem_mb),
        name=name,
    )(*args)
    return res


def _epi_store(accs, es, outs):
    outs[0][...] = accs[0].astype(outs[0].dtype)


def _epi_merge(accs, es, outs):
    pa, pb, pc, ga, gb, gc = accs
    m = _sigmoid(ga) * pa + _sigmoid(gb) * pb + _sigmoid(gc) * pc
    outs[0][...] = m.astype(outs[0].dtype)


def _res_norm_body(*refs, cast_w, modulated, emit_x):
    a_ref, w_ref, res_ref, gt_ref, g_ref = refs[:5]
    refs = refs[5:]
    if modulated:
        sc_ref, sh_ref = refs[:2]
        refs = refs[2:]
    if emit_x:
        x_ref = refs[0]
        refs = refs[1:]
    h_ref = refs[0]
    if cast_w:
        w_sc = refs[1]

        @pl.when(pl.program_id(0) == 0)
        def _():
            w_sc[...] = w_ref[...].astype(BF16)

        w = w_sc[...]
    else:
        w = w_ref[...]
    x = res_ref[...] + gt_ref[...] * _dot(a_ref[...], w)
    if emit_x:
        x_ref[...] = x
    y = _rms(x) * g_ref[...]
    if modulated:
        y = y * (1.0 + sc_ref[...]) + sh_ref[...]
    h_ref[...] = y.astype(h_ref.dtype)


def residual_norm(act, w3, l, res, gt, g, mod, *, tm, seq_len, emit_x, name):
    M, K = act.shape
    D = res.shape[1]
    tiles_per_seq = max(seq_len // tm, 1)
    cast_w = w3.dtype != BF16
    operand = lambda m: _mod_operand(m, tm, D, lambda i: i, lambda i: i // tiles_per_seq)

    row = lambda w: pl.BlockSpec((tm, w), lambda i: (i, 0))
    gt_arr, gt_spec = operand(gt)
    in_specs = [row(K),
                pl.BlockSpec((None, K, D), lambda i: (l, 0, 0), pipeline_mode=pl.Buffered(1)),
                row(D), gt_spec, pl.BlockSpec((1, D), lambda i: (0, 0))]
    args = [act, w3, res, gt_arr, g]
    if mod is not None:
        for m_arr, m_spec in map(operand, mod):
            in_specs.append(m_spec)
            args.append(m_arr)
    out_specs = [row(D)] * (2 if emit_x else 1)
    out_shape = ([jax.ShapeDtypeStruct((M, D), F32)] if emit_x else []) + [
        jax.ShapeDtypeStruct((M, D), BF16 if mod is not None else F32)]
    outs = pl.pallas_call(
        functools.partial(_res_norm_body, cast_w=cast_w, modulated=mod is not None, emit_x=emit_x),
        grid=(M // tm,),
        in_specs=in_specs,
        out_specs=out_specs,
        out_shape=out_shape,
        scratch_shapes=[pltpu.VMEM((K, D), BF16)] if cast_w else [],
        compiler_params=_cp(("arbitrary",), 56),
        name=name,
    )(*args)
    return (outs[0], outs[1]) if emit_x else (None, outs[0])


def _s5_prep_body(lrf_ref, lif_ref, ldf_ref, lrr_ref, lir_ref, ldr_ref, bre_ref, bim_ref,
                  tab_ref, bbr_ref, bbi_ref):
    dtf = jnp.exp(ldf_ref[...])
    _power_tables(tab_ref, lrf_ref[...] * dtf, lif_ref[...] * dtf, 1)
    dtr = jnp.exp(ldr_ref[...])
    lr = lrr_ref[...]
    li = lir_ref[...]
    m1 = jnp.exp(lr * dtr)
    ar = m1 * jnp.cos(li * dtr)
    ai = m1 * jnp.sin(li * dtr)
    den = lr * lr + li * li
    nr = ar - 1.0
    kr = (nr * lr + ai * li) / den
    ki = (ai * lr - nr * li) / den
    bre = bre_ref[...]
    bim = bim_ref[...]
    bbr_ref[...] = kr * bre - ki * bim
    bbi_ref[...] = kr * bim + ki * bre


def s5_prep(lam_re, lam_im, log_dt, b_re, b_im):
    lrf = lam_re.reshape(1, N_S5)
    lif = lam_im.reshape(1, N_S5)
    ldf = jnp.repeat(log_dt, P_A).reshape(1, N_S5)
    lrr = jnp.repeat(lam_re, S5_GROUP, axis=0)
    lir = jnp.repeat(lam_im, S5_GROUP, axis=0)
    ldr = jnp.repeat(log_dt, S5_GROUP).reshape(D_A, 1)
    bre = jnp.transpose(b_re, (0, 2, 1)).reshape(D_A, P_A)
    bim = jnp.transpose(b_im, (0, 2, 1)).reshape(D_A, P_A)
    tab, bbr, bbi = pl.pallas_call(
        _s5_prep_body,
        out_shape=[jax.ShapeDtypeStruct((8, SUBLANE, N_S5), F32),
                   jax.ShapeDtypeStruct((D_A, P_A), F32),
                   jax.ShapeDtypeStruct((D_A, P_A), F32)],
        name="s5_prep",
    )(lrf, lif, ldf, lrr, lir, ldr, bre, bim)
    eye = jnp.eye(S5_GPB, dtype=F32)

    def blockdiag(m):
        m = m.reshape(S5_NB, S5_GPB, S5_GROUP, P_A)
        return jnp.einsum('kghp,gj->kghjp', m, eye).reshape(S5_NB, LANE, S5_SPB)

    wb = jnp.concatenate([blockdiag(bbr), blockdiag(bbi)], axis=2).astype(BF16)
    return tab, wb, bbr, bbi


def _power_tables(tab_ref, lr_dt, li_dt, stride):
    n_lanes = lr_dt.shape[1]
    row = lax.broadcasted_iota(jnp.int32, (SUBLANE, n_lanes), 0)
    n = ((row + 1) * stride).astype(F32)
    mag = jnp.exp(n * lr_dt)
    ang = n * li_dt
    pr = mag * jnp.cos(ang)
    pi = mag * jnp.sin(ang)
    for k, d in enumerate((1, 2, 4)):
        keep = row >= d
        tab_ref[2 * k] = jnp.where(keep, pr[d - 1:d, :], 0.0)
        tab_ref[2 * k + 1] = jnp.where(keep, pi[d - 1:d, :], 0.0)
    tab_ref[6] = pr
    tab_ref[7] = pi


def _s5_chunk_prep_body(lrf_ref, lif_ref, ldf_ref, lrr_ref, lir_ref, ldr_ref, bbr_ref, bbi_ref,
                        cr_ref, ci_ref, tab_ref, wst_ref, kt_ref, wor_ref, woi_ref):
    T = S5_T
    dtf = jnp.exp(ldf_ref[...])
    _power_tables(tab_ref, lrf_ref[...] * dtf, lif_ref[...] * dtf, T)
    dtr = jnp.exp(ldr_ref[...])
    lr = lrr_ref[...] * dtr
    li = lir_ref[...] * dtr
    bbr = bbr_ref[...]
    bbi = bbi_ref[...]
    cr = cr_ref[...]
    ci = ci_ref[...]
    spread = jnp.where(lax.broadcasted_iota(jnp.int32, (P_A, S5_SPB), 0)
                       == lax.broadcasted_iota(jnp.int32, (P_A, S5_SPB), 1) % P_A, 1.0, 0.0).astype(BF16)
    spread_t = jnp.where(lax.broadcasted_iota(jnp.int32, (S5_SPB, P_A), 0) % P_A
                         == lax.broadcasted_iota(jnp.int32, (S5_SPB, P_A), 1), 1.0, 0.0).astype(BF16)
    in_mask = (lax.broadcasted_iota(jnp.int32, (LANE, S5_SPB), 0) // S5_GROUP
               == lax.broadcasted_iota(jnp.int32, (LANE, S5_SPB), 1) // P_A)
    out_mask = (lax.broadcasted_iota(jnp.int32, (S5_SPB, LANE), 0) // P_A
                == lax.broadcasted_iota(jnp.int32, (S5_SPB, LANE), 1) // S5_GROUP)
    k_mask = (lax.broadcasted_iota(jnp.int32, (LANE, LANE), 0) // S5_GROUP
              == lax.broadcasted_iota(jnp.int32, (LANE, LANE), 1) // S5_GROUP)
    zero_tile = jnp.zeros((LANE, LANE), BF16)
    mag = jnp.exp(lr)
    a_r = mag * jnp.cos(li)
    a_i = mag * jnp.sin(li)
    pr = jnp.ones_like(lr)
    pi = jnp.zeros_like(lr)
    for n in range(T + 1):
        if n >= 1:
            pr, pi = pr * a_r - pi * a_i, pr * a_i + pi * a_r
        car = (cr * pr - ci * pi).astype(BF16)
        cai = (cr * pi + ci * pr).astype(BF16)
        for k in range(S5_NB):
            blk = slice(k * LANE, (k + 1) * LANE)
            if n >= 1:
                cols = slice((n - 1) * LANE, n * LANE)
                wor_ref[k, :, cols] = jnp.where(out_mask, _dot_nt(spread_t, car[blk]), 0.0).astype(BF16)
                woi_ref[k, :, cols] = jnp.where(out_mask, _dot_nt(spread_t, cai[blk]), 0.0).astype(BF16)
            if n < T:
                rows = slice((T - 1 - n) * LANE, (T - n) * LANE)
                wr = (pr * bbr - pi * bbi)[blk].astype(BF16)
                wi = (pr * bbi + pi * bbr)[blk].astype(BF16)
                wst_ref[k, rows, 0:S5_SPB] = jnp.where(in_mask, _dot(wr, spread), 0.0).astype(BF16)
                wst_ref[k, rows, S5_SPB:2 * S5_SPB] = jnp.where(in_mask, _dot(wi, spread), 0.0).astype(BF16)
                kt = (_dot_nt(bbr[blk].astype(BF16), car[blk])
                      - _dot_nt(bbi[blk].astype(BF16), cai[blk]))
                kt = jnp.where(k_mask, kt, 0.0).astype(BF16)
                for ti in range(T - n):
                    to = ti + n
                    kt_ref[k, ti * LANE:(ti + 1) * LANE, to * LANE:(to + 1) * LANE] = kt
                    if n >= 1:
                        kt_ref[k, to * LANE:(to + 1) * LANE, ti * LANE:(ti + 1) * LANE] = zero_tile


def s5_chunk_prep(lam_re, lam_im, log_dt, bbr, bbi, c_re, c_im):
    T = S5_T
    lrf = lam_re.reshape(1, N_S5)
    lif = lam_im.reshape(1, N_S5)
    ldf = jnp.repeat(log_dt, P_A).reshape(1, N_S5)
    lrr = jnp.repeat(lam_re, S5_GROUP, axis=0)
    lir = jnp.repeat(lam_im, S5_GROUP, axis=0)
    ldr = jnp.repeat(log_dt, S5_GROUP).reshape(D_A, 1)
    op = lambda r, c: jax.ShapeDtypeStruct((S5_NB, r, c), BF16)
    return pl.pallas_call(
        _s5_chunk_prep_body,
        out_shape=[jax.ShapeDtypeStruct((8, SUBLANE, N_S5), F32), op(T * LANE, 2 * S5_SPB),
                   op(T * LANE, T * LANE), op(S5_SPB, T * LANE), op(S5_SPB, T * LANE)],
        compiler_params=pltpu.CompilerParams(vmem_limit_bytes=48 * 1024 * 1024),
        name="s5_chunk_prep",
    )(lrf, lif, ldf, lrr, lir, ldr, bbr, bbi, c_re.reshape(D_A, P_A), c_im.reshape(D_A, P_A))


def _s5_c_body(*refs, tb):
    T = S5_T
    u_refs = refs[:S5_NB]
    (wst_ref, kt_ref, wor_ref, woi_ref, tab_ref, d_ref, wglu_ref,
     o_ref, hr_ref, hi_ref, s_sc, car_sc) = refs[S5_NB:]

    def step_rows(kb, t):
        return u_refs[kb][pl.ds(t, tb, stride=T), :]

    @pl.when(pl.program_id(1) == 0)
    def _():
        car_sc[...] = jnp.zeros_like(car_sc)

    s_sc[0:SUBLANE, :] = car_sc[...]
    xks = [jnp.concatenate([step_rows(kb, t).astype(BF16) for t in range(T)], axis=1)
           for kb in range(S5_NB)]
    for kb in range(S5_NB):
        s = _dot(xks[kb], wst_ref[kb])
        s_sc[SUBLANE:SUBLANE + tb, kb * S5_SPB:(kb + 1) * S5_SPB] = s[:, :S5_SPB]
        s_sc[SUBLANE:SUBLANE + tb, N_S5 + kb * S5_SPB:N_S5 + (kb + 1) * S5_SPB] = s[:, S5_SPB:]

    def tile(rt, _):
        r0 = pl.multiple_of(SUBLANE + rt * SUBLANE, SUBLANE)
        for lg in range(N_S5 // LANE):
            cre = slice(lg * LANE, (lg + 1) * LANE)
            cim = slice(N_S5 + lg * LANE, N_S5 + (lg + 1) * LANE)
            xr = s_sc[pl.ds(r0, SUBLANE), cre]
            xi = s_sc[pl.ds(r0, SUBLANE), cim]
            for k, d in enumerate((1, 2, 4)):
                a_r = tab_ref[2 * k, :, cre]
                a_i = tab_ref[2 * k + 1, :, cre]
                sr = pltpu.roll(xr, d, 0)
                si = pltpu.roll(xi, d, 0)
                xr, xi = xr + a_r * sr - a_i * si, xi + a_r * si + a_i * sr
            p_r = tab_ref[6, :, cre]
            p_i = tab_ref[7, :, cre]
            cr = car_sc[:, cre]
            ci = car_sc[:, cim]
            xr, xi = xr + p_r * cr - p_i * ci, xi + p_r * ci + p_i * cr
            s_sc[pl.ds(r0, SUBLANE), cre] = xr
            s_sc[pl.ds(r0, SUBLANE), cim] = xi
            last = SUBLANE - 1
            car_sc[:, cre] = jnp.broadcast_to(xr[last:last + 1, :], (SUBLANE, LANE))
            car_sc[:, cim] = jnp.broadcast_to(xi[last:last + 1, :], (SUBLANE, LANE))
        return 0

    lax.fori_loop(0, tb // SUBLANE, tile, 0)

    ys = []
    for kb in range(S5_NB):
        h_r = s_sc[SUBLANE - 1:SUBLANE - 1 + tb, kb * S5_SPB:(kb + 1) * S5_SPB].astype(BF16)
        h_i = s_sc[SUBLANE - 1:SUBLANE - 1 + tb,
                   N_S5 + kb * S5_SPB:N_S5 + (kb + 1) * S5_SPB].astype(BF16)
        ys.append(_dot(xks[kb], kt_ref[kb])
                  + _dot(h_r, wor_ref[kb]) - _dot(h_i, woi_ref[kb]))
    for t in range(T):
        y = jnp.concatenate([ys[kb][:, t * LANE:(t + 1) * LANE] for kb in range(S5_NB)], axis=1)
        u = jnp.concatenate([step_rows(kb, t) for kb in range(S5_NB)], axis=1)
        y = jax.nn.gelu(y + d_ref[...] * u)
        y = y * _sigmoid(_dot(y.astype(BF16), wglu_ref[...]))
        o_ref[:, t * D_A:(t + 1) * D_A] = y.astype(o_ref.dtype)
    hr_ref[...] = car_sc[0:1, :N_S5]
    hi_ref[...] = car_sc[0:1, N_S5:]


def s5_prompt(proj3, ops, d_skip, wglu, *, tb=256):
    tab, wst, ktoe, wor, woi = ops
    B, L, _ = proj3.shape
    T = S5_T
    nc = L // T
    tb = min(tb, nc)
    bw = T * LANE
    const3 = lambda b, t: (0, 0, 0)
    once = dict(pipeline_mode=pl.Buffered(1))
    u_specs = [pl.BlockSpec((None, tb * T, LANE), functools.partial(lambda b, t, kb: (b, t, kb), kb=kb))
               for kb in range(S5_NB)]
    o8, hr, hi = pl.pallas_call(
        functools.partial(_s5_c_body, tb=tb),
        grid=(B, nc // tb),
        in_specs=[*u_specs,
                  pl.BlockSpec((S5_NB, bw, 2 * S5_SPB), const3, **once),
                  pl.BlockSpec((S5_NB, bw, bw), const3, **once),
                  pl.BlockSpec((S5_NB, S5_SPB, bw), const3, **once),
                  pl.BlockSpec((S5_NB, S5_SPB, bw), const3, **once),
                  pl.BlockSpec((8, SUBLANE, N_S5), const3, **once),
                  pl.BlockSpec((1, D_A), lambda b, t: (0, 0)),
                  pl.BlockSpec((D_A, D_A), lambda b, t: (0, 0))],
        out_specs=[pl.BlockSpec((None, tb, T * D_A), lambda b, t: (b, t, 0)),
                   pl.BlockSpec((None, 1, N_S5), lambda b, t: (b, 0, 0)),
                   pl.BlockSpec((None, 1, N_S5), lambda b, t: (b, 0, 0))],
        out_shape=[jax.ShapeDtypeStruct((B, nc, T * D_A), BF16),
                   jax.ShapeDtypeStruct((B, 1, N_S5), F32),
                   jax.ShapeDtypeStruct((B, 1, N_S5), F32)],
        scratch_shapes=[pltpu.VMEM((SUBLANE + tb, 2 * N_S5), F32),
                        pltpu.VMEM((SUBLANE, 2 * N_S5), F32)],
        compiler_params=_cp(("arbitrary", "arbitrary"), 56),
        name="s5_prompt",
    )(*[proj3] * S5_NB, wst, ktoe, wor, woi, tab, d_skip, wglu)
    return o8.reshape(B, L, D_A), hr, hi


def s5_out_weights(c_re, c_im):
    eye = jnp.eye(S5_GPB, dtype=F32)

    def blockdiag(c):
        c = c.reshape(S5_NB, S5_GPB, S5_GROUP, P_A)
        return jnp.einsum('kghp,gj->kgpjh', c, eye).reshape(S5_NB, S5_SPB, LANE).astype(BF16)

    return blockdiag(c_re), blockdiag(c_im)


def _s5_in(ub, wb_ref, kb):
    return _dot(ub[:, kb * LANE:(kb + 1) * LANE], wb_ref[kb])


def _s5_tail(h_blocks, u, wcr_ref, wci_ref, d_ref, wglu_ref):
    ys = []
    for kb in range(S5_NB):
        hr, hi = h_blocks(kb)
        ys.append(_dot(hr.astype(BF16), wcr_ref[kb]) - _dot(hi.astype(BF16), wci_ref[kb]))
    y = jnp.concatenate(ys, axis=1) + d_ref[...] * u
    y = jax.nn.gelu(y)
    return y * _sigmoid(_dot(y.astype(BF16), wglu_ref[...]))


def _s5_s_body(u_ref, h0r_ref, h0i_ref, wb_ref, wcr_ref, wci_ref, tab_ref, d_ref, wglu_ref,
               o_ref, hr_ref, hi_ref):
    u = u_ref[...]
    ub = u.astype(BF16)
    for kb in range(S5_NB):
        cols = slice(kb * S5_SPB, (kb + 1) * S5_SPB)
        bu = _s5_in(ub, wb_ref, kb)
        a_r = tab_ref[6, 0:1, cols]
        a_i = tab_ref[7, 0:1, cols]
        h0r = h0r_ref[:, cols]
        h0i = h0i_ref[:, cols]
        hr_ref[:, cols] = a_r * h0r - a_i * h0i + bu[:, :S5_SPB]
        hi_ref[:, cols] = a_r * h0i + a_i * h0r + bu[:, S5_SPB:]

    def h_blocks(kb):
        cols = slice(kb * S5_SPB, (kb + 1) * S5_SPB)
        return hr_ref[:, cols], hi_ref[:, cols]

    o_ref[...] = _s5_tail(h_blocks, u, wcr_ref, wci_ref, d_ref, wglu_ref).astype(o_ref.dtype)


def s5_sample(proj, h0r, h0i, wb, wcr, wci, tab, d_skip, wglu):
    R = proj.shape[0]
    z2 = lambda i: (0, 0)
    return pl.pallas_call(
        _s5_s_body,
        grid=(1,),
        in_specs=[pl.BlockSpec((R, D_A), z2),
                  pl.BlockSpec((R, N_S5), z2), pl.BlockSpec((R, N_S5), z2),
                  pl.BlockSpec((S5_NB, LANE, 2 * S5_SPB), lambda i: (0, 0, 0)),
                  pl.BlockSpec((S5_NB, S5_SPB, LANE), lambda i: (0, 0, 0)),
                  pl.BlockSpec((S5_NB, S5_SPB, LANE), lambda i: (0, 0, 0)),
                  pl.BlockSpec((8, SUBLANE, N_S5), lambda i: (0, 0, 0)),
                  pl.BlockSpec((1, D_A), z2), pl.BlockSpec((D_A, D_A), z2)],
        out_specs=[pl.BlockSpec((R, D_A), z2), pl.BlockSpec((R, N_S5), z2), pl.BlockSpec((R, N_S5), z2)],
        out_shape=[jax.ShapeDtypeStruct((R, D_A), BF16),
                   jax.ShapeDtypeStruct((R, N_S5), F32),
                   jax.ShapeDtypeStruct((R, N_S5), F32)],
        compiler_params=_cp(("arbitrary",)),
        name="s5_sample",
    )(proj, h0r, h0i, wb, wcr, wci, tab, d_skip, wglu)


def _gmlp_p_body(u_ref, v_ref, gv_ref, wcat_ref, bias_ref, o_ref, *, tb):
    hd = D_B // H_B
    lane_head = lax.broadcasted_iota(jnp.int32, (GM_CHUNK, D_B), 1) // hd
    wi = lax.broadcasted_iota(jnp.int32, (GM_CHUNK, H_B * GM_CHUNK), 0)
    wj = lax.broadcasted_iota(jnp.int32, (GM_CHUNK, H_B * GM_CHUNK), 1) % GM_CHUNK
    wcat = jnp.where(wi >= wj, wcat_ref[...], 0.0).astype(BF16)
    for c in range(tb // GM_CHUNK):
        rows = slice(c * GM_CHUNK, (c + 1) * GM_CHUNK)
        vn = _rms(jax.nn.gelu(v_ref[rows, :])) * gv_ref[...]
        vb = vn.astype(BF16)
        stack = jnp.concatenate(
            [jnp.where(lane_head == h, vb, jnp.zeros_like(vb)) for h in range(H_B)], axis=0)
        s = _dot(wcat, stack) + bias_ref[...]
        o_ref[rows, :] = (jax.nn.gelu(u_ref[rows, :]) * s).astype(o_ref.dtype)


def gmlp_prompt(proj3, g_v, w_s, b_s, *, tb=512):
    B, L = proj3.shape[:2]
    hd = D_B // H_B
    wcat = jnp.transpose(w_s, (1, 0, 2)).reshape(GM_CHUNK, H_B * GM_CHUNK)
    bias = jnp.repeat(b_s.T, hd, axis=1)
    const2 = lambda b, t: (0, 0)
    ub = D_A // D_B
    return pl.pallas_call(
        functools.partial(_gmlp_p_body, tb=tb),
        grid=(B, L // tb),
        in_specs=[pl.BlockSpec((None, tb, D_B), lambda b, t: (b, t, ub)),
                  pl.BlockSpec((None, tb, D_B), lambda b, t: (b, t, ub + 1)),
                  pl.BlockSpec((1, D_B), const2),
                  pl.BlockSpec((GM_CHUNK, H_B * GM_CHUNK), const2),
                  pl.BlockSpec((GM_CHUNK, D_B), const2)],
        out_specs=pl.BlockSpec((None, tb, D_B), lambda b, t: (b, t, 0)),
        out_shape=jax.ShapeDtypeStruct((B, L, D_B), BF16),
        compiler_params=_cp(("arbitrary", "arbitrary")),
        name="gmlp_prompt",
    )(proj3, proj3, g_v, wcat, bias)


def _gmlp_s_body(u_ref, v_ref, gv_ref, w0_ref, b0_ref, o_ref, vn_ref):
    vn = _rms(jax.nn.gelu(v_ref[...])) * gv_ref[...]
    vn_ref[...] = vn
    s = w0_ref[...] * vn + b0_ref[...]
    o_ref[...] = (jax.nn.gelu(u_ref[...]) * s).astype(o_ref.dtype)


def gmlp_sample(proj, g_v, w_s, b_s):
    R = proj.shape[0]
    hd = D_B // H_B
    w0 = jnp.repeat(w_s[:, 0, 0], hd).reshape(1, D_B)
    b0 = jnp.repeat(b_s[:, 0], hd).reshape(1, D_B)
    z2 = lambda i: (0, 0)
    ub = D_A // D_B
    return pl.pallas_call(
        _gmlp_s_body,
        grid=(1,),
        in_specs=[pl.BlockSpec((R, D_B), lambda i: (0, ub)),
                  pl.BlockSpec((R, D_B), lambda i: (0, ub + 1)),
                  pl.BlockSpec((1, D_B), z2), pl.BlockSpec((1, D_B), z2), pl.BlockSpec((1, D_B), z2)],
        out_specs=[pl.BlockSpec((R, D_B), z2), pl.BlockSpec((R, D_B), z2)],
        out_shape=[jax.ShapeDtypeStruct((R, D_B), BF16), jax.ShapeDtypeStruct((R, D_B), F32)],
        compiler_params=_cp(("arbitrary",)),
        name="gmlp_sample",
    )(proj, proj, g_v, w0, b0)


def _head_expand_matrix():
    r = lax.broadcasted_iota(jnp.int32, (LANE, D_C), 0)
    c = lax.broadcasted_iota(jnp.int32, (LANE, D_C), 1) // HD_C
    return jnp.where(r == c, 1.0, 0.0).astype(BF16)


def _expand_heads(v, e):
    hi, lo = _split_bf16(v)
    return _dot(hi, e) + _dot(lo, e)


def _ssd_p_body(z_ref, xbc_ref, dt_ref, cw_ref, cb_ref, dtb_ref, alog_ref, dsk_ref, gn_ref,
                o_ref, st_ref, h_ref, h_sc, cv_sc, *, nt):
    Q = SSD_CHUNK
    t = pl.program_id(1)

    @pl.when(t == 0)
    def _():
        h_sc[...] = jnp.zeros_like(h_sc)
        cv_sc[0:SUBLANE, :] = jnp.zeros((SUBLANE, D_XBC), F32)

    xbc = xbc_ref[...]
    cv_sc[SUBLANE:SUBLANE + Q, :] = xbc
    acc = cb_ref[...]
    for k in range(K_C):
        off = SUBLANE - (K_C - 1) + k
        acc = acc + cw_ref[k:k + 1, :] * cv_sc[off:off + Q, :]
    cv_sc[0:SUBLANE, :] = xbc[Q - SUBLANE:, :]
    st_ref[...] = xbc[Q - (K_C - 1):, :]
    xc = _silu(acc)
    xs = xc[:, :D_C]

    dt = _softplus(dt_ref[...] + dtb_ref[...])
    a = -jnp.exp(alog_ref[...])
    da = dt * a
    ii = lax.broadcasted_iota(jnp.int32, (Q, Q), 0)
    jj = lax.broadcasted_iota(jnp.int32, (Q, Q), 1)
    causal = ii >= jj
    tril = jnp.where(causal, 1.0, 0.0).astype(BF16)
    d0, d1 = _split_bf16(da)
    d2 = (da - d0.astype(F32) - d1.astype(F32)).astype(BF16)
    cs = _dot(tril, d0) + _dot(tril, d1) + _dot(tril, d2)
    cst = cs.T
    cs_end = cs[Q - 1:Q, :]
    e = _head_expand_matrix()
    dt_f = _expand_heads(dt, e)
    ws_f = _expand_heads(dt * jnp.exp(cs_end - cs), e)
    ecs_f = _expand_heads(jnp.exp(cs), e)
    xdt = xs * dt_f
    xw = xs * ws_f

    lane = lax.broadcasted_iota(jnp.int32, (Q, LANE), 1)
    hpg = H_C // G_C
    gw = hpg * HD_C
    ys = []
    for g in range(G_C):
        bg = xc[:, D_C + g * N_C:D_C + (g + 1) * N_C].astype(BF16)
        cg = xc[:, D_C + G_C * N_C + g * N_C:D_C + G_C * N_C + (g + 1) * N_C].astype(BF16)
        gmat = _dot_nt(cg, bg)
        hprev = h_sc[g * gw:(g + 1) * gw, :]
        yoff = _dot_nt(cg, hprev.astype(BF16)) * ecs_f[:, g * gw:(g + 1) * gw]
        snew = _dot_tn(xw[:, g * gw:(g + 1) * gw].astype(BF16), bg)
        for hp in range(hpg // 2):
            h0 = g * hpg + 2 * hp
            xpair = xdt[:, h0 * HD_C:(h0 + 2) * HD_C]
            x_lo = jnp.where(lane < HD_C, xpair, 0.0).astype(BF16)
            x_hi = jnp.where(lane >= HD_C, xpair, 0.0).astype(BF16)
            yd = None
            for hh, xh in ((h0, x_lo), (h0 + 1, x_hi)):
                seg = cs[:, hh:hh + 1] - cst[hh:hh + 1, :]
                sc = (gmat * jnp.exp(jnp.where(causal, seg, NEG_BIG))).astype(BF16)
                part = _dot(sc, xh)
                yd = part if yd is None else yd + part
            ys.append(yd + yoff[:, 2 * hp * HD_C:(2 * hp + 2) * HD_C])
        for hh in range(hpg):
            h = g * hpg + hh
            cd = jnp.exp(cst[h:h + 1, Q - 1:Q])
            rows = slice(h * HD_C, (h + 1) * HD_C)
            h_sc[rows, :] = h_sc[rows, :] * cd + snew[hh * HD_C:(hh + 1) * HD_C, :]

    y = jnp.concatenate(ys, axis=1) + dsk_ref[...] * xs
    y = y * _silu(z_ref[...])
    o_ref[...] = (_rms(y) * gn_ref[...]).astype(o_ref.dtype)

    @pl.when(t == nt - 1)
    def _():
        h_ref[...] = h_sc[...]


def _ssd_params(dt_bias, a_log, d_skip):
    pad = LANE - H_C
    dtb = jnp.pad(dt_bias, (0, pad)).reshape(1, LANE)
    alog = jnp.pad(a_log, (0, pad)).reshape(1, LANE)
    dsk = jnp.repeat(d_skip, HD_C).reshape(1, D_C)
    return dtb, alog, dsk


def ssd_prompt(z3, xd3, conv_w, conv_b, dt_bias, a_log, d_skip, g_norm):
    B, L = z3.shape[:2]
    Q = SSD_CHUNK
    nt = L // Q
    dtb, alog, dsk = _ssd_params(dt_bias, a_log, d_skip)
    const2 = lambda b, t: (0, 0)
    blk = lambda w: pl.BlockSpec((None, Q, w), lambda b, t: (b, t, 0))
    return pl.pallas_call(
        functools.partial(_ssd_p_body, nt=nt),
        grid=(B, nt),
        in_specs=[blk(D_C), blk(D_XBC),
                  pl.BlockSpec((None, Q, LANE), lambda b, t: (b, t, D_XBC // LANE)),
                  pl.BlockSpec((K_C, D_XBC), const2), pl.BlockSpec((1, D_XBC), const2),
                  pl.BlockSpec((1, LANE), const2), pl.BlockSpec((1, LANE), const2),
                  pl.BlockSpec((1, D_C), const2), pl.BlockSpec((1, D_C), const2)],
        out_specs=[blk(D_C),
                   pl.BlockSpec((None, K_C - 1, D_XBC), lambda b, t: (b, 0, 0)),
                   pl.BlockSpec((None, H_C * HD_C, N_C), lambda b, t: (b, 0, 0))],
        out_shape=[jax.ShapeDtypeStruct((B, L, D_C), BF16),
                   jax.ShapeDtypeStruct((B, K_C - 1, D_XBC), F32),
                   jax.ShapeDtypeStruct((B, H_C * HD_C, N_C), F32)],
        scratch_shapes=[pltpu.VMEM((H_C * HD_C, N_C), F32), pltpu.VMEM((SUBLANE + Q, D_XBC), F32)],
        compiler_params=_cp(("arbitrary", "arbitrary")),
        name="ssd_prompt",
    )(z3, xd3, xd3, conv_w, conv_b.reshape(1, D_XBC), dtb, alog, dsk, g_norm.reshape(1, D_C))


def _ssd_s_body(z_ref, xbc_ref, dt_ref, buf_ref, h0_ref, cw_ref, cb_ref, dtb_ref, alog_ref,
                dsk_ref, gn_ref, *rest, tbatch, nsteps, has_prev):
    o_ref, hn_ref, xs_sc, bc_sc, xt_sc, at_sc, y_sc = rest[1:] if has_prev else rest
    s = pl.program_id(0)
    R = xs_sc.shape[0]

    @pl.when(s == 0)
    def _():
        acc = cb_ref[...] + cw_ref[K_C - 1:K_C, :] * xbc_ref[...]
        for k in range(K_C - 1):
            acc = acc + cw_ref[k:k + 1, :] * buf_ref[k]
        xc = _silu(acc)
        xs = xc[:, :D_C]
        xs_sc[...] = xs
        bc_sc[...] = xc[:, D_C:]
        dt = _softplus(dt_ref[...] + dtb_ref[...])
        e = _head_expand_matrix()
        dt_f = _expand_heads(dt, e)
        da_f = jnp.exp(_expand_heads(dt * (-jnp.exp(alog_ref[...])), e))
        for t_sc, v in ((xt_sc, xs * dt_f), (at_sc, da_f)):
            hi, lo = _split_bf16(v.T)
            t_sc[:, :R] = hi
            t_sc[:, R:] = lo

    gw = (H_C // G_C) * HD_C
    kk = lax.broadcasted_iota(jnp.int32, (2 * R, LANE), 0) % R
    r0 = pl.multiple_of(s * tbatch, tbatch)
    bc8 = bc_sc[pl.ds(r0, tbatch), :]
    ti = lax.broadcasted_iota(jnp.int32, (tbatch, gw), 0)
    ytile = [jnp.zeros((tbatch, gw), F32) for _ in range(G_C)]
    for i in range(tbatch):
        b = s * tbatch + i
        onehot = jnp.where(kk == b, 1.0, 0.0).astype(BF16)
        xb = _dot(xt_sc[...], onehot)
        ab = _dot(at_sc[...], onehot)
        for g in range(G_C):
            rows = slice(g * gw, (g + 1) * gw)
            brow = bc8[i:i + 1, g * N_C:(g + 1) * N_C]
            hn = h0_ref[i, rows, :] * ab[rows, :] + xb[rows, :] * brow
            hn_ref[i, rows, :] = hn
            c8 = bc8[:, G_C * N_C + g * N_C:G_C * N_C + (g + 1) * N_C]
            yg = _dot_nt(c8.astype(BF16), hn.astype(BF16))
            ytile[g] = jnp.where(ti == i, yg, ytile[g])
    for g in range(G_C):
        y_sc[pl.ds(r0, tbatch), g * gw:(g + 1) * gw] = ytile[g]

    @pl.when(s == nsteps - 1)
    def _():
        y = y_sc[...] + dsk_ref[...] * xs_sc[...]
        y = y * _silu(z_ref[...])
        o_ref[...] = (_rms(y) * gn_ref[...]).astype(o_ref.dtype)


def ssd_sample(z, xd, buf, h0_all, l, hn_all, conv_w, conv_b, dt_bias, a_log, d_skip, g_norm,
               *, tbatch=16):
    R = z.shape[0]
    nsteps = R // tbatch
    dtb, alog, dsk = _ssd_params(dt_bias, a_log, d_skip)
    z2 = lambda s: (0, 0)
    hw = H_C * HD_C
    st_spec = pl.BlockSpec((None, tbatch, hw, N_C), lambda s: (l, s, 0, 0))
    in_specs = [pl.BlockSpec((R, D_C), z2), pl.BlockSpec((R, D_XBC), z2),
                pl.BlockSpec((R, LANE), lambda s: (0, D_XBC // LANE)),
                pl.BlockSpec((K_C - 1, R, D_XBC), lambda s: (0, 0, 0)),
                st_spec,
                pl.BlockSpec((K_C, D_XBC), z2), pl.BlockSpec((1, D_XBC), z2),
                pl.BlockSpec((1, LANE), z2), pl.BlockSpec((1, LANE), z2),
                pl.BlockSpec((1, D_C), z2), pl.BlockSpec((1, D_C), z2)]
    args = [z, xd, xd, buf, h0_all, conv_w, conv_b.reshape(1, D_XBC), dtb, alog, dsk,
            g_norm.reshape(1, D_C)]
    aliases = {}
    if hn_all is not None:
        aliases = {len(args): 1}
        in_specs.append(pl.BlockSpec(memory_space=pl.ANY))
        args.append(hn_all)
    return pl.pallas_call(
        functools.partial(_ssd_s_body, tbatch=tbatch, nsteps=nsteps, has_prev=hn_all is not None),
        grid=(nsteps,),
        in_specs=in_specs,
        out_specs=[pl.BlockSpec((R, D_C), z2), st_spec],
        out_shape=[jax.ShapeDtypeStruct((R, D_C), BF16),
                   jax.ShapeDtypeStruct(h0_all.shape, F32)],
        scratch_shapes=[pltpu.VMEM((R, D_C), F32), pltpu.VMEM((R, 2 * G_C * N_C), F32),
                        pltpu.VMEM((D_C, 2 * R), BF16), pltpu.VMEM((D_C, 2 * R), BF16),
                        pltpu.VMEM((R, D_C), F32)],
        input_output_aliases=aliases,
        compiler_params=_cp(("arbitrary",)),
        name="ssd_sample",
    )(*args)


def _up_p_body(x_ref, wa_ref, wb_ref, cwa_ref, cwb_ref, cba_ref, cbb_ref,
               o_ref, sa_ref, sb_ref, wa_sc, wb_sc, ca_sc, cb_sc, *, tm, tiles_per_seq):
    i = pl.program_id(1)

    @pl.when(i == 0)
    def _():
        wa_sc[...] = wa_ref[...].astype(BF16)
        wb_sc[...] = wb_ref[...].astype(BF16)

    @pl.when(i % tiles_per_seq == 0)
    def _():
        ca_sc[0:SUBLANE, :] = jnp.zeros((SUBLANE, ca_sc.shape[1]), F32)
        cb_sc[0:SUBLANE, :] = jnp.zeros((SUBLANE, cb_sc.shape[1]), F32)

    x = x_ref[...]

    def half(w_sc, c_sc, cw_ref, cb_ref, st_ref):
        up = _dot(x, w_sc[...])
        c_sc[SUBLANE:SUBLANE + tm, :] = up
        acc = cb_ref[...]
        for k in range(K_F):
            off = SUBLANE - (K_F - 1) + k
            acc = acc + cw_ref[k:k + 1, :] * c_sc[off:off + tm, :]
        c_sc[0:SUBLANE, :] = up[tm - SUBLANE:, :]
        st_ref[...] = up[tm - (K_F - 1):, :]
        return acc

    a = half(wa_sc, ca_sc, cwa_ref, cba_ref, sa_ref)
    b = half(wb_sc, cb_sc, cwb_ref, cbb_ref, sb_ref)
    o_ref[...] = (_silu(a) * b).astype(o_ref.dtype)


def up_prompt(h2, w_up, l, conv_w, conv_b, *, seq_len, tm=1024, tn=512):
    M, D = h2.shape
    B = M // seq_len
    tiles_per_seq = seq_len // tm
    nj = D_FF // tn
    cb = conv_b.reshape(1, 2 * D_FF)
    body = functools.partial(_up_p_body, tm=tm, tiles_per_seq=tiles_per_seq)
    st_spec = pl.BlockSpec((None, K_F - 1, tn), lambda j, i: (i // tiles_per_seq, 0, j))
    return pl.pallas_call(
        body,
        grid=(nj, M // tm),
        in_specs=[pl.BlockSpec((tm, D), lambda j, i: (i, 0)),
                  pl.BlockSpec((None, D, tn), lambda j, i: (l, 0, j)),
                  pl.BlockSpec((None, D, tn), lambda j, i: (l, 0, nj + j)),
                  pl.BlockSpec((K_F, tn), lambda j, i: (0, j)),
                  pl.BlockSpec((K_F, tn), lambda j, i: (0, nj + j)),
                  pl.BlockSpec((1, tn), lambda j, i: (0, j)),
                  pl.BlockSpec((1, tn), lambda j, i: (0, nj + j))],
        out_specs=[pl.BlockSpec((tm, tn), lambda j, i: (i, j)), st_spec, st_spec],
        out_shape=[jax.ShapeDtypeStruct((M, D_FF), BF16),
                   jax.ShapeDtypeStruct((B, K_F - 1, D_FF), F32),
                   jax.ShapeDtypeStruct((B, K_F - 1, D_FF), F32)],
        scratch_shapes=[pltpu.VMEM((D, tn), BF16), pltpu.VMEM((D, tn), BF16),
                        pltpu.VMEM((SUBLANE + tm, tn), F32), pltpu.VMEM((SUBLANE + tm, tn), F32)],
        compiler_params=_cp(("arbitrary", "arbitrary")),
        name="up_prompt",
    )(h2, w_up, w_up, conv_w, conv_w, cb, cb)


def _up_s_body(x_ref, w_ref, buf_ref, cw_ref, cb_ref, *rest, nj, has_prev):
    o_ref, new_ref, a_sc = rest[1:] if has_prev else rest
    s = pl.program_id(0)
    up = _dot(x_ref[...], w_ref[...].astype(BF16))
    acc = cb_ref[...] + cw_ref[K_F - 1:K_F, :] * up
    for k in range(K_F - 1):
        acc = acc + cw_ref[k:k + 1, :] * buf_ref[:, k, :]
    for k in range(1, K_F - 1):
        new_ref[:, k - 1, :] = buf_ref[:, k, :]
    new_ref[:, K_F - 2, :] = up

    @pl.when(s < nj)
    def _():
        a_sc[s] = acc

    @pl.when(s >= nj)
    def _():
        o_ref[...] = (_silu(a_sc[s - nj]) * acc).astype(o_ref.dtype)


def up_sample(h2, w_up, l, buf_all, new_all, conv_w, conv_b, *, tn=512):
    R, D = h2.shape
    nj = D_FF // tn
    st = pl.BlockSpec((None, R, K_F - 1, tn), lambda s: (l, 0, 0, s))
    in_specs = [pl.BlockSpec((R, D), lambda s: (0, 0)),
                pl.BlockSpec((None, D, tn), lambda s: (l, 0, s)),
                st,
                pl.BlockSpec((K_F, tn), lambda s: (0, s)),
                pl.BlockSpec((1, tn), lambda s: (0, s))]
    args = [h2, w_up, buf_all, conv_w, conv_b.reshape(1, 2 * D_FF)]
    aliases = {}
    if new_all is not None:
        aliases = {len(args): 1}
        in_specs.append(pl.BlockSpec(memory_space=pl.ANY))
        args.append(new_all)
    act, new = pl.pallas_call(
        functools.partial(_up_s_body, nj=nj, has_prev=new_all is not None),
        grid=(2 * nj,),
        in_specs=in_specs,
        out_specs=[pl.BlockSpec((R, tn), lambda s: (0, jnp.maximum(s - nj, 0))), st],
        out_shape=[jax.ShapeDtypeStruct((R, D_FF), BF16),
                   jax.ShapeDtypeStruct(buf_all.shape, F32)],
        scratch_shapes=[pltpu.VMEM((nj, R, tn), F32)],
        input_output_aliases=aliases,
        compiler_params=_cp(("arbitrary",)),
        name="up_sample",
    )(*args)
    return act, new


def _layer(x3, h, mods, l, p, state, next_norm, *, seq_len, tm, tn_merge):
    B, L, D = x3.shape
    M = B * L
    sh_m, sc_m, gt_m, sh_f, sc_f, gt_f = mods
    prompt = state is None
    w_buffers = 1 if M > tm else 2

    def proj(col0, n, name):
        return matmul([h], [(p['w_in_t'], l, col0, True)], [], [(n, F32)], _epi_store,
                      x_of_w=(0,), tm=tm, tn=n, name=name, vmem_mb=56, w_buffers=1)[0]

    uav = proj(OFF_UA, D_A + 2 * D_B, "proj_uav")
    zc = proj(OFF_Z, D_C, "proj_z")
    xd = proj(OFF_XBC, D_XBC + LANE, "proj_xd")

    tab, wb, bbr, bbi = s5_prep(p['lam_re'][l], p['lam_im'][l], p['log_dt'][l],
                                p['b_re'][l], p['b_im'][l])
    d_a = p['s5_d'][l].reshape(1, D_A)
    wglu = p['w_glu'][l].astype(BF16)
    g_v = p['g_v'][l].reshape(1, D_B)

    if prompt:
        s5_ops = s5_chunk_prep(p['lam_re'][l], p['lam_im'][l], p['log_dt'][l], bbr, bbi,
                               p['c_re'][l], p['c_im'][l])
        o_a, s5r, s5i = s5_prompt(uav.reshape(B, L, -1), s5_ops, d_a, wglu)
        o_b = gmlp_prompt(uav.reshape(B, L, -1), g_v, p['w_s'][l], p['b_s'][l])
        v_rows = None
        o_c, convc, ssm = ssd_prompt(zc.reshape(B, L, -1), xd.reshape(B, L, -1),
                                     p['ssd_conv_w'][l], p['ssd_conv_b'][l], p['dt_bias'][l],
                                     p['a_log'][l], p['ssd_d'][l], p['ssd_g'][l])
        s5r = s5r.reshape(B, G_A, P_A)
        s5i = s5i.reshape(B, G_A, P_A)
    else:
        s5_re0, s5_im0, ssm_all, ssm_new_all, convc0, ffn_all, ffn_new_all = state
        wcr, wci = s5_out_weights(p['c_re'][l], p['c_im'][l])
        o_a, s5r, s5i = s5_sample(uav, s5_re0.reshape(M, N_S5), s5_im0.reshape(M, N_S5),
                                  wb, wcr, wci, tab, d_a, wglu)
        o_b, v_rows = gmlp_sample(uav, g_v, p['w_s'][l], p['b_s'][l])
        o_c, ssm = ssd_sample(zc, xd, jnp.transpose(convc0, (1, 0, 2)),
                              ssm_all, l, ssm_new_all,
                              p['ssd_conv_w'][l], p['ssd_conv_b'][l], p['dt_bias'][l],
                              p['a_log'][l], p['ssd_d'][l], p['ssd_g'][l])
        convc = jnp.concatenate([convc0[:, 1:], xd[:, None, :D_XBC]], axis=1)
        s5r = s5r.reshape(M, G_A, P_A)
        s5i = s5i.reshape(M, G_A, P_A)
    o_a = o_a.reshape(M, D_A)
    o_b = o_b.reshape(M, D_B)
    o_c = o_c.reshape(M, D_C)
    if prompt:
        ssm = ssm.reshape(B, H_C, HD_C, N_C)

    w_t = p['w_in_t']
    merged = matmul(
        [h, o_a, o_b, o_c],
        [(p['w_pa'], l, 0, False), (p['w_pb'], l, 0, False), (p['w_pc'], l, 0, False),
         (w_t, l, OFF_GATES, True), (w_t, l, OFF_GATES + D, True), (w_t, l, OFF_GATES + 2 * D, True)],
        [], [(D, BF16)], _epi_merge, x_of_w=(1, 2, 3, 0, 0, 0), tm=tm, tn=tn_merge, name="merge",
        vmem_mb=56, w_buffers=w_buffers)[0]

    tm_r = min(tm, 256)
    x2, h2 = residual_norm(merged, p['w_out'], l, x3.reshape(M, D), gt_m,
                           p['g_ffn'][l].reshape(1, D), (sc_f, sh_f), tm=min(tm, 512),
                           seq_len=seq_len, emit_x=True, name="out_norm")
    if prompt:
        act, st_a, st_b = up_prompt(h2, p['w_up'], l, p['ffn_conv_w'][l], p['ffn_conv_b'][l],
                                    seq_len=seq_len, tm=tm)
        convf = jnp.concatenate([st_a, st_b], axis=-1)
    else:
        act, convf = up_sample(h2, p['w_up'], l, ffn_all, ffn_new_all,
                               p['ffn_conv_w'][l], p['ffn_conv_b'][l])

    g_next, mod_next = next_norm
    x2, h_next = residual_norm(act, p['w_down_bf16'], l, x2, gt_f, g_next, mod_next, tm=tm_r,
                               seq_len=seq_len, emit_x=mod_next is not None, name="down_norm")
    x_out = None if x2 is None else x2.reshape(B, L, D)
    return x_out, h_next, s5r, s5i, ssm, convc, convf, v_rows


def kernel(x_prompt, x_sample, c_prompt, c_sample, state_s5_re, state_s5_im, state_ssm, state_ssd_conv, state_ffn_conv, w_mod, b_mod, g_mix, w_in, s5_lam_re, s5_lam_im, s5_log_dt, s5_b_re, s5_b_im, s5_c_re, s5_c_im, s5_d, s5_w_glu, gm_g_v, gm_w_s, gm_b_s, ssd_conv_w, ssd_conv_b, ssd_dt_bias, ssd_a_log, ssd_d, ssd_g_norm, w_pa, w_pb, w_pc, w_out, g_ffn, ffn_w_up, ffn_conv_w, ffn_conv_b, ffn_w_down, g_final):
    p = {
        'g_mix': g_mix, 'w_in_t': jnp.swapaxes(w_in, 1, 2),
        'lam_re': s5_lam_re, 'lam_im': s5_lam_im, 'log_dt': s5_log_dt,
        'b_re': s5_b_re, 'b_im': s5_b_im, 'c_re': s5_c_re, 'c_im': s5_c_im,
        's5_d': s5_d, 'w_glu': s5_w_glu,
        'g_v': gm_g_v, 'w_s': gm_w_s, 'b_s': gm_b_s,
        'ssd_conv_w': ssd_conv_w, 'ssd_conv_b': ssd_conv_b, 'dt_bias': ssd_dt_bias,
        'a_log': ssd_a_log, 'ssd_d': ssd_d, 'ssd_g': ssd_g_norm,
        'w_pa': w_pa, 'w_pb': w_pb, 'w_pc': w_pc, 'w_out': w_out,
        'g_ffn': g_ffn, 'w_up': ffn_w_up, 'ffn_conv_w': ffn_conv_w,
        'ffn_conv_b': ffn_conv_b, 'w_down_bf16': ffn_w_down.astype(BF16),
    }
    bp, seq, D = x_prompt.shape
    bs = x_sample.shape[0]

    n_c = bs + bp
    pad = (-n_c) % SUBLANE
    c_all = jnp.concatenate([c_sample, c_prompt, jnp.zeros((pad, D), F32)], axis=0)
    mod = mod_all(c_all, w_mod, b_mod)

    xp = x_prompt
    xs = x_sample.reshape(1, bs, D)
    outs_p = [[] for _ in range(5)]
    outs_s = [[] for _ in range(6)]
    ssm_all = state_ssm.reshape(DEPTH, bs, H_C * HD_C, N_C)
    ssm_new_all = None
    ffn_new_all = None
    mods_s = [[TokenMod(mod, l, k) for k in range(6)] for l in range(DEPTH)]
    mods_p = [[m[:, None, :] for m in jnp.split(mod[l, bs:bs + bp], 6, axis=-1)] for l in range(DEPTH)]

    def norm_after(l, mods):
        if l + 1 == DEPTH:
            return g_final.reshape(1, D), None
        return g_mix[l + 1].reshape(1, D), (mods[l + 1][1], mods[l + 1][0])

    g0 = g_mix[0].reshape(1, D)
    hp = norm_mod(xp, g0, mods_p[0][1], mods_p[0][0], tm=512).reshape(bp * seq, D)
    hs = norm_mod(xs, g0, mods_s[0][1], mods_s[0][0], tm=bs).reshape(bs, D)
    for l in range(DEPTH):
        xp, hp, *st_p = _layer(xp, hp, mods_p[l], l, p, None, norm_after(l, mods_p),
                               seq_len=seq, tm=1024, tn_merge=512)
        for acc, v in zip(outs_p, st_p[:5]):
            acc.append(v)
        state = (state_s5_re[l], state_s5_im[l], ssm_all, ssm_new_all, state_ssd_conv[l],
                 state_ffn_conv, ffn_new_all)
        xs, hs, *st_s = _layer(xs, hs, mods_s[l], l, p, state, norm_after(l, mods_s),
                               seq_len=1, tm=bs, tn_merge=256)
        ssm_new_all = st_s[2]
        ffn_new_all = st_s[4]
        st_s[5] = st_s[5].reshape(bs, 1, D_B)
        for acc, v in zip(outs_s, st_s):
            acc.append(v)

    y_prompt = hp.reshape(bp, seq, D)
    y_sample = hs.reshape(bs, 1, D)
    whole = {2: ssm_new_all.reshape(DEPTH, bs, H_C, HD_C, N_C),
             4: ffn_new_all}
    outs_s = [whole[k] if k in whole else jnp.stack(v) for k, v in enumerate(outs_s)]
    return (y_prompt, y_sample, *[jnp.stack(v) for v in outs_p], *outs_s)
```

```python
import functools
from typing import NamedTuple

import jax
import jax.numpy as jnp
from jax import lax
from jax.experimental import pallas as pl
from jax.experimental.pallas import tpu as pltpu

F32 = jnp.float32
BF16 = jnp.bfloat16

D_MODEL = 2048
DEPTH = 2
D_A = 512
S5_GROUP = 16
G_A = D_A // S5_GROUP
P_A = 64
N_S5 = G_A * P_A
S5_GPB = 128 // S5_GROUP
S5_NB = D_A // 128
S5_SPB = S5_GPB * P_A
S5_T = 8
D_B = 512
H_B = 8
GM_CHUNK = 128
D_C = 1024
HD_C = 64
H_C = D_C // HD_C
N_C = 128
G_C = 2
K_C = 4
SSD_CHUNK = 128
D_XBC = D_C + 2 * G_C * N_C
D_FF = 5632
K_F = 3
EPS = 1e-6

OFF_UA = 0
OFF_Z = D_A + 2 * D_B
OFF_XBC = OFF_Z + D_C
OFF_DT = OFF_XBC + D_XBC
OFF_GATES = OFF_DT + H_C

LANE = 128
SUBLANE = 8
NEG_BIG = -1e30


def _cp(sem, vmem_mb=48):
    return pltpu.CompilerParams(dimension_semantics=sem, vmem_limit_bytes=vmem_mb * 1024 * 1024)


def _sigmoid(x):
    return 0.5 * jnp.tanh(0.5 * x) + 0.5


def _silu(x):
    return x * _sigmoid(x)


def _softplus(x):
    return jnp.maximum(x, 0.0) + jnp.log1p(jnp.exp(-jnp.abs(x)))


def _rms(x):
    return x * lax.rsqrt(jnp.mean(x * x, axis=-1, keepdims=True) + EPS)


def _split_bf16(x):
    hi = x.astype(BF16)
    lo = (x - hi.astype(F32)).astype(BF16)
    return hi, lo


def _dot(a, b):
    return jnp.dot(a, b, preferred_element_type=F32)


def _dot_nt(a, b):
    return lax.dot_general(a, b, (((1,), (1,)), ((), ())), preferred_element_type=F32)


def _dot_tn(a, b):
    return lax.dot_general(a, b, (((0,), (0,)), ((), ())), preferred_element_type=F32)


def _mod_body(c_ref, w_ref, b_ref, o_ref):
    a = _silu(c_ref[...]).astype(BF16)
    o_ref[...] = _dot(a, w_ref[...].astype(BF16)) + b_ref[...]


def mod_all(c_all, w_mod, b_mod, *, tn=1024):
    R, D = c_all.shape
    N = w_mod.shape[-1]
    return pl.pallas_call(
        _mod_body,
        grid=(DEPTH, N // tn),
        in_specs=[pl.BlockSpec((R, D), lambda l, j: (0, 0)),
                  pl.BlockSpec((None, D, tn), lambda l, j: (l, 0, j)),
                  pl.BlockSpec((None, 1, tn), lambda l, j: (l, 0, j))],
        out_specs=pl.BlockSpec((None, R, tn), lambda l, j: (l, 0, j)),
        out_shape=jax.ShapeDtypeStruct((DEPTH, R, N), F32),
        compiler_params=_cp(("arbitrary", "arbitrary")),
        name="mod_all",
    )(c_all, w_mod, b_mod.reshape(DEPTH, 1, N))


def _norm_mod_body(x_ref, g_ref, sc_ref, sh_ref, o_ref):
    y = _rms(x_ref[...]) * g_ref[...]
    o_ref[...] = (y * (1.0 + sc_ref[...]) + sh_ref[...]).astype(o_ref.dtype)


class TokenMod(NamedTuple):
    arr: jax.Array
    l: int
    k: int


def _mod_operand(m, tm, D, tile_of, seq_of):
    if isinstance(m, TokenMod):
        return m.arr, pl.BlockSpec((None, tm, D), lambda *g: (m.l, tile_of(*g), m.k))
    return m, pl.BlockSpec((None, 1, D), lambda *g: (seq_of(*g), 0, 0))


def norm_mod(x3, g, sc, sh, *, tm):
    B, L, D = x3.shape
    ops = [_mod_operand(m, tm, D, lambda b, i: i, lambda b, i: b) for m in (sc, sh)]
    return pl.pallas_call(
        _norm_mod_body,
        grid=(B, L // tm),
        in_specs=[pl.BlockSpec((None, tm, D), lambda b, i: (b, i, 0)),
                  pl.BlockSpec((1, D), lambda b, i: (0, 0)),
                  ops[0][1], ops[1][1]],
        out_specs=pl.BlockSpec((None, tm, D), lambda b, i: (b, i, 0)),
        out_shape=jax.ShapeDtypeStruct((B, L, D), BF16),
        compiler_params=_cp(("arbitrary", "arbitrary")),
        name="norm_mod",
    )(x3, g, ops[0][0], ops[1][0])


def _mm_body(*refs, x_of_w, w_is_t, n_x, n_e, n_o, epilogue):
    n_w = len(x_of_w)
    xs = refs[:n_x]
    ws = refs[n_x:n_x + n_w]
    es = refs[n_x + n_w:n_x + n_w + n_e]
    outs = refs[n_x + n_w + n_e:n_x + n_w + n_e + n_o]
    wsc = refs[n_x + n_w + n_e + n_o:]

    @pl.when(pl.program_id(1) == 0)
    def _():
        for w, s, is_t in zip(ws, wsc, w_is_t):
            s[...] = (w[...].T if is_t else w[...]).astype(BF16)

    accs = [_dot(xs[xi][...], s[...]) for xi, s in zip(x_of_w, wsc)]
    epilogue(accs, es, outs)


def matmul(xs, ws, extras, outs, epilogue, *, x_of_w, tm, tn, name, vmem_mb=48, w_buffers=2):
    M = xs[0].shape[0]
    N = outs[0][0]
    in_specs = [pl.BlockSpec((tm, x.shape[1]), lambda j, i: (i, 0)) for x in xs]
    args = list(xs)
    scratch = []
    for w3, l, col0, is_t in ws:
        if is_t:
            K = w3.shape[2]
            assert col0 % SUBLANE == 0
            in_specs.append(pl.BlockSpec((None, pl.Element(tn), pl.Element(K)), functools.partial(
                lambda j, i, l, c: (l, (c + j * (tn // SUBLANE)) * SUBLANE, 0),
                l=l, c=col0 // SUBLANE), pipeline_mode=pl.Buffered(w_buffers)))
        else:
            K = w3.shape[1]
            assert col0 % tn == 0
            in_specs.append(pl.BlockSpec((None, K, tn), functools.partial(
                lambda j, i, l, c: (l, 0, c + j), l=l, c=col0 // tn),
                pipeline_mode=pl.Buffered(w_buffers)))
        args.append(w3)
        scratch.append(pltpu.VMEM((K, tn), BF16))
    for arr, spec in extras:
        in_specs.append(spec)
        args.append(arr)
    body = functools.partial(_mm_body, x_of_w=tuple(x_of_w), w_is_t=tuple(w[3] for w in ws),
                             n_x=len(xs), n_e=len(extras), n_o=len(outs), epilogue=epilogue)
    res = pl.pallas_call(
        body,
        grid=(N // tn, M // tm),
        in_specs=in_specs,
        out_specs=[pl.BlockSpec((tm, tn), lambda j, i: (i, j)) for _ in outs],
        out_shape=[jax.ShapeDtypeStruct((M, n), dt) for n, dt in outs],
        scratch_shapes=scratch,
        compiler_params=_cp(("arbitrary", "arbitrary"), vmem_mb),
        name=name,
    )(*args)
    return res


def _epi_store(accs, es, outs):
    outs[0][...] = accs[0].astype(outs[0].dtype)


def _epi_merge(accs, es, outs):
    pa, pb, pc, ga, gb, gc = accs
    m = _sigmoid(ga) * pa + _sigmoid(gb) * pb + _sigmoid(gc) * pc
    outs[0][...] = m.astype(outs[0].dtype)


def _res_norm_body(*refs, cast_w, modulated, emit_x):
    a_ref, w_ref, res_ref, gt_ref, g_ref = refs[:5]
    refs = refs[5:]
    if modulated:
        sc_ref, sh_ref = refs[:2]
        refs = refs[2:]
    if emit_x:
        x_ref = refs[0]
        refs = refs[1:]
    h_ref = refs[0]
    if cast_w:
        w_sc = refs[1]

        @pl.when(pl.program_id(0) == 0)
        def _():
            w_sc[...] = w_ref[...].astype(BF16)

        w = w_sc[...]
    else:
        w = w_ref[...]
    x = res_ref[...] + gt_ref[...] * _dot(a_ref[...], w)
    if emit_x:
        x_ref[...] = x
    y = _rms(x) * g_ref[...]
    if modulated:
        y = y * (1.0 + sc_ref[...]) + sh_ref[...]
    h_ref[...] = y.astype(h_ref.dtype)


def residual_norm(act, w3, l, res, gt, g, mod, *, tm, seq_len, emit_x, name):
    M, K = act.shape
    D = res.shape[1]
    tiles_per_seq = max(seq_len // tm, 1)
    cast_w = w3.dtype != BF16
    operand = lambda m: _mod_operand(m, tm, D, lambda i: i, lambda i: i // tiles_per_seq)

    row = lambda w: pl.BlockSpec((tm, w), lambda i: (i, 0))
    gt_arr, gt_spec = operand(gt)
    in_specs = [row(K),
                pl.BlockSpec((None, K, D), lambda i: (l, 0, 0), pipeline_mode=pl.Buffered(1)),
                row(D), gt_spec, pl.BlockSpec((1, D), lambda i: (0, 0))]
    args = [act, w3, res, gt_arr, g]
    if mod is not None:
        for m_arr, m_spec in map(operand, mod):
            in_specs.append(m_spec)
            args.append(m_arr)
    out_specs = [row(D)] * (2 if emit_x else 1)
    out_shape = ([jax.ShapeDtypeStruct((M, D), F32)] if emit_x else []) + [
        jax.ShapeDtypeStruct((M, D), BF16 if mod is not None else F32)]
    outs = pl.pallas_call(
        functools.partial(_res_norm_body, cast_w=cast_w, modulated=mod is not None, emit_x=emit_x),
        grid=(M // tm,),
        in_specs=in_specs,
        out_specs=out_specs,
        out_shape=out_shape,
        scratch_shapes=[pltpu.VMEM((K, D), BF16)] if cast_w else [],
        compiler_params=_cp(("arbitrary",), 56),
        name=name,
    )(*args)
    return (outs[0], outs[1]) if emit_x else (None, outs[0])


def _s5_prep_body(lrf_ref, lif_ref, ldf_ref, lrr_ref, lir_ref, ldr_ref, bre_ref, bim_ref,
                  tab_ref, bbr_ref, bbi_ref):
    dtf = jnp.exp(ldf_ref[...])
    _power_tables(tab_ref, lrf_ref[...] * dtf, lif_ref[...] * dtf, 1)
    dtr = jnp.exp(ldr_ref[...])
    lr = lrr_ref[...]
    li = lir_ref[...]
    m1 = jnp.exp(lr * dtr)
    ar = m1 * jnp.cos(li * dtr)
    ai = m1 * jnp.sin(li * dtr)
    den = lr * lr + li * li
    nr = ar - 1.0
    kr = (nr * lr + ai * li) / den
    ki = (ai * lr - nr * li) / den
    bre = bre_ref[...]
    bim = bim_ref[...]
    bbr_ref[...] = kr * bre - ki * bim
    bbi_ref[...] = kr * bim + ki * bre


def s5_prep(lam_re, lam_im, log_dt, b_re, b_im):
    lrf = lam_re.reshape(1, N_S5)
    lif = lam_im.reshape(1, N_S5)
    ldf = jnp.repeat(log_dt, P_A).reshape(1, N_S5)
    lrr = jnp.repeat(lam_re, S5_GROUP, axis=0)
    lir = jnp.repeat(lam_im, S5_GROUP, axis=0)
    ldr = jnp.repeat(log_dt, S5_GROUP).reshape(D_A, 1)
    bre = jnp.transpose(b_re, (0, 2, 1)).reshape(D_A, P_A)
    bim = jnp.transpose(b_im, (0, 2, 1)).reshape(D_A, P_A)
    tab, bbr, bbi = pl.pallas_call(
        _s5_prep_body,
        out_shape=[jax.ShapeDtypeStruct((8, SUBLANE, N_S5), F32),
                   jax.ShapeDtypeStruct((D_A, P_A), F32),
                   jax.ShapeDtypeStruct((D_A, P_A), F32)],
        name="s5_prep",
    )(lrf, lif, ldf, lrr, lir, ldr, bre, bim)
    eye = jnp.eye(S5_GPB, dtype=F32)

    def blockdiag(m):
        m = m.reshape(S5_NB, S5_GPB, S5_GROUP, P_A)
        return jnp.einsum('kghp,gj->kghjp', m, eye).reshape(S5_NB, LANE, S5_SPB)

    wb = jnp.concatenate([blockdiag(bbr), blockdiag(bbi)], axis=2).astype(BF16)
    return tab, wb, bbr, bbi


def _power_tables(tab_ref, lr_dt, li_dt, stride):
    n_lanes = lr_dt.shape[1]
    row = lax.broadcasted_iota(jnp.int32, (SUBLANE, n_lanes), 0)
    n = ((row + 1) * stride).astype(F32)
    mag = jnp.exp(n * lr_dt)
    ang = n * li_dt
    pr = mag * jnp.cos(ang)
    pi = mag * jnp.sin(ang)
    for k, d in enumerate((1, 2, 4)):
        keep = row >= d
        tab_ref[2 * k] = jnp.where(keep, pr[d - 1:d, :], 0.0)
        tab_ref[2 * k + 1] = jnp.where(keep, pi[d - 1:d, :], 0.0)
    tab_ref[6] = pr
    tab_ref[7] = pi


def _s5_chunk_prep_body(lrf_ref, lif_ref, ldf_ref, lrr_ref, lir_ref, ldr_ref, bbr_ref, bbi_ref,
                        cr_ref, ci_ref, tab_ref, wst_ref, kt_ref, wor_ref, woi_ref):
    T = S5_T
    dtf = jnp.exp(ldf_ref[...])
    _power_tables(tab_ref, lrf_ref[...] * dtf, lif_ref[...] * dtf, T)
    dtr = jnp.exp(ldr_ref[...])
    lr = lrr_ref[...] * dtr
    li = lir_ref[...] * dtr
    bbr = bbr_ref[...]
    bbi = bbi_ref[...]
    cr = cr_ref[...]
    ci = ci_ref[...]
    spread = jnp.where(lax.broadcasted_iota(jnp.int32, (P_A, S5_SPB), 0)
                       == lax.broadcasted_iota(jnp.int32, (P_A, S5_SPB), 1) % P_A, 1.0, 0.0).astype(BF16)
    spread_t = jnp.where(lax.broadcasted_iota(jnp.int32, (S5_SPB, P_A), 0) % P_A
                         == lax.broadcasted_iota(jnp.int32, (S5_SPB, P_A), 1), 1.0, 0.0).astype(BF16)
    in_mask = (lax.broadcasted_iota(jnp.int32, (LANE, S5_SPB), 0) // S5_GROUP
               == lax.broadcasted_iota(jnp.int32, (LANE, S5_SPB), 1) // P_A)
    out_mask = (lax.broadcasted_iota(jnp.int32, (S5_SPB, LANE), 0) // P_A
                == lax.broadcasted_iota(jnp.int32, (S5_SPB, LANE), 1) // S5_GROUP)
    k_mask = (lax.broadcasted_iota(jnp.int32, (LANE, LANE), 0) // S5_GROUP
              == lax.broadcasted_iota(jnp.int32, (LANE, LANE), 1) // S5_GROUP)
    zero_tile = jnp.zeros((LANE, LANE), BF16)
    mag = jnp.exp(lr)
    a_r = mag * jnp.cos(li)
    a_i = mag * jnp.sin(li)
    pr = jnp.ones_like(lr)
    pi = jnp.zeros_like(lr)
    for n in range(T + 1):
        if n >= 1:
            pr, pi = pr * a_r - pi * a_i, pr * a_i + pi * a_r
        car = (cr * pr - ci * pi).astype(BF16)
        cai = (cr * pi + ci * pr).astype(BF16)
        for k in range(S5_NB):
            blk = slice(k * LANE, (k + 1) * LANE)
            if n >= 1:
                cols = slice((n - 1) * LANE, n * LANE)
                wor_ref[k, :, cols] = jnp.where(out_mask, _dot_nt(spread_t, car[blk]), 0.0).astype(BF16)
                woi_ref[k, :, cols] = jnp.where(out_mask, _dot_nt(spread_t, cai[blk]), 0.0).astype(BF16)
            if n < T:
                rows = slice((T - 1 - n) * LANE, (T - n) * LANE)
                wr = (pr * bbr - pi * bbi)[blk].astype(BF16)
                wi = (pr * bbi + pi * bbr)[blk].astype(BF16)
                wst_ref[k, rows, 0:S5_SPB] = jnp.where(in_mask, _dot(wr, spread), 0.0).astype(BF16)
                wst_ref[k, rows, S5_SPB:2 * S5_SPB] = jnp.where(in_mask, _dot(wi, spread), 0.0).astype(BF16)
                kt = (_dot_nt(bbr[blk].astype(BF16), car[blk])
                      - _dot_nt(bbi[blk].astype(BF16), cai[blk]))
                kt = jnp.where(k_mask, kt, 0.0).astype(BF16)
                for ti in range(T - n):
                    to = ti + n
                    kt_ref[k, ti * LANE:(ti + 1) * LANE, to * LANE:(to + 1) * LANE] = kt
                    if n >= 1:
                        kt_ref[k, to * LANE:(to + 1) * LANE, ti * LANE:(ti + 1) * LANE] = zero_tile


def s5_chunk_prep(lam_re, lam_im, log_dt, bbr, bbi, c_re, c_im):
    T = S5_T
    lrf = lam_re.reshape(1, N_S5)
    lif = lam_im.reshape(1, N_S5)
    ldf = jnp.repeat(log_dt, P_A).reshape(1, N_S5)
    lrr = jnp.repeat(lam_re, S5_GROUP, axis=0)
    lir = jnp.repeat(lam_im, S5_GROUP, axis=0)
    ldr = jnp.repeat(log_dt, S5_GROUP).reshape(D_A, 1)
    op = lambda r, c: jax.ShapeDtypeStruct((S5_NB, r, c), BF16)
    return pl.pallas_call(
        _s5_chunk_prep_body,
        out_shape=[jax.ShapeDtypeStruct((8, SUBLANE, N_S5), F32), op(T * LANE, 2 * S5_SPB),
                   op(T * LANE, T * LANE), op(S5_SPB, T * LANE), op(S5_SPB, T * LANE)],
        compiler_params=pltpu.CompilerParams(vmem_limit_bytes=48 * 1024 * 1024),
        name="s5_chunk_prep",
    )(lrf, lif, ldf, lrr, lir, ldr, bbr, bbi, c_re.reshape(D_A, P_A), c_im.reshape(D_A, P_A))


def _s5_c_body(*refs, tb):
    T = S5_T
    u_refs = refs[:S5_NB]
    (wst_ref, kt_ref, wor_ref, woi_ref, tab_ref, d_ref, wglu_ref,
     o_ref, hr_ref, hi_ref, s_sc, car_sc) = refs[S5_NB:-S5_NB]
    o_scs = refs[-S5_NB:]

    def step_rows(kb, t):
        return u_refs[kb][pl.ds(t, tb, stride=T), :]

    @pl.when(pl.program_id(1) == 0)
    def _():
        car_sc[...] = jnp.zeros_like(car_sc)

    s_sc[0:SUBLANE, :] = car_sc[...]
    xks = [jnp.concatenate([step_rows(kb, t).astype(BF16) for t in range(T)], axis=1)
           for kb in range(S5_NB)]
    for kb in range(S5_NB):
        s = _dot(xks[kb], wst_ref[kb])
        s_sc[SUBLANE:SUBLANE + tb, kb * S5_SPB:(kb + 1) * S5_SPB] = s[:, :S5_SPB]
        s_sc[SUBLANE:SUBLANE + tb, N_S5 + kb * S5_SPB:N_S5 + (kb + 1) * S5_SPB] = s[:, S5_SPB:]

    def tile(rt, _):
        r0 = pl.multiple_of(SUBLANE + rt * SUBLANE, SUBLANE)
        for lg in range(N_S5 // LANE):
            cre = slice(lg * LANE, (lg + 1) * LANE)
            cim = slice(N_S5 + lg * LANE, N_S5 + (lg + 1) * LANE)
            xr = s_sc[pl.ds(r0, SUBLANE), cre]
            xi = s_sc[pl.ds(r0, SUBLANE), cim]
            for k, d in enumerate((1, 2, 4)):
                a_r = tab_ref[2 * k, :, cre]
                a_i = tab_ref[2 * k + 1, :, cre]
                sr = pltpu.roll(xr, d, 0)
                si = pltpu.roll(xi, d, 0)
                xr, xi = xr + a_r * sr - a_i * si, xi + a_r * si + a_i * sr
            p_r = tab_ref[6, :, cre]
            p_i = tab_ref[7, :, cre]
            cr = car_sc[:, cre]
            ci = car_sc[:, cim]
            xr, xi = xr + p_r * cr - p_i * ci, xi + p_r * ci + p_i * cr
            s_sc[pl.ds(r0, SUBLANE), cre] = xr
            s_sc[pl.ds(r0, SUBLANE), cim] = xi
            last = SUBLANE - 1
            car_sc[:, cre] = jnp.broadcast_to(xr[last:last + 1, :], (SUBLANE, LANE))
            car_sc[:, cim] = jnp.broadcast_to(xi[last:last + 1, :], (SUBLANE, LANE))
        return 0

    lax.fori_loop(0, tb // SUBLANE, tile, 0)

    ys = []
    for kb in range(S5_NB):
        h_r = s_sc[SUBLANE - 1:SUBLANE - 1 + tb, kb * S5_SPB:(kb + 1) * S5_SPB].astype(BF16)
        h_i = s_sc[SUBLANE - 1:SUBLANE - 1 + tb,
                   N_S5 + kb * S5_SPB:N_S5 + (kb + 1) * S5_SPB].astype(BF16)
        ys.append(_dot(xks[kb], kt_ref[kb])
                  + _dot(h_r, wor_ref[kb]) - _dot(h_i, woi_ref[kb]))
    for t in range(T):
        y = jnp.concatenate([ys[kb][:, t * LANE:(t + 1) * LANE] for kb in range(S5_NB)], axis=1)
        u = jnp.concatenate([step_rows(kb, t) for kb in range(S5_NB)], axis=1)
        y = jax.nn.gelu(y + d_ref[...] * u)
        y = y * _sigmoid(_dot(y.astype(BF16), wglu_ref[...]))
        for kb in range(S5_NB):
            o_scs[kb][pl.ds(t, tb, stride=T), :] = y[:, kb * LANE:(kb + 1) * LANE]
    for kb in range(S5_NB):
        o_ref[:, kb * LANE:(kb + 1) * LANE] = o_scs[kb][...].astype(o_ref.dtype)
    hr_ref[...] = car_sc[0:1, :N_S5]
    hi_ref[...] = car_sc[0:1, N_S5:]


def s5_prompt(proj3, ops, d_skip, wglu, *, tb=256):
    tab, wst, ktoe, wor, woi = ops
    B, L, _ = proj3.shape
    T = S5_T
    nc = L // T
    tb = min(tb, nc)
    bw = T * LANE
    const3 = lambda b, t: (0, 0, 0)
    once = dict(pipeline_mode=pl.Buffered(1))
    u_specs = [pl.BlockSpec((None, tb * T, LANE), functools.partial(lambda b, t, kb: (b, t, kb), kb=kb))
               for kb in range(S5_NB)]
    return pl.pallas_call(
        functools.partial(_s5_c_body, tb=tb),
        grid=(B, nc // tb),
        in_specs=[*u_specs,
                  pl.BlockSpec((S5_NB, bw, 2 * S5_SPB), const3, **once),
                  pl.BlockSpec((S5_NB, bw, bw), const3, **once),
                  pl.BlockSpec((S5_NB, S5_SPB, bw), const3, **once),
                  pl.BlockSpec((S5_NB, S5_SPB, bw), const3, **once),
                  pl.BlockSpec((8, SUBLANE, N_S5), const3, **once),
                  pl.BlockSpec((1, D_A), lambda b, t: (0, 0)),
                  pl.BlockSpec((D_A, D_A), lambda b, t: (0, 0))],
        out_specs=[pl.BlockSpec((None, tb * T, D_A), lambda b, t: (b, t, 0)),
                   pl.BlockSpec((None, 1, N_S5), lambda b, t: (b, 0, 0)),
                   pl.BlockSpec((None, 1, N_S5), lambda b, t: (b, 0, 0))],
        out_shape=[jax.ShapeDtypeStruct((B, L, D_A), BF16),
                   jax.ShapeDtypeStruct((B, 1, N_S5), F32),
                   jax.ShapeDtypeStruct((B, 1, N_S5), F32)],
        scratch_shapes=[pltpu.VMEM((SUBLANE + tb, 2 * N_S5), F32),
                        pltpu.VMEM((SUBLANE, 2 * N_S5), F32)]
        + [pltpu.VMEM((tb * T, LANE), F32)] * S5_NB,
        compiler_params=_cp(("arbitrary", "arbitrary"), 58),
        name="s5_prompt",
    )(*[proj3] * S5_NB, wst, ktoe, wor, woi, tab, d_skip, wglu)


def s5_out_weights(c_re, c_im):
    eye = jnp.eye(S5_GPB, dtype=F32)

    def blockdiag(c):
        c = c.reshape(S5_NB, S5_GPB, S5_GROUP, P_A)
        return jnp.einsum('kghp,gj->kgpjh', c, eye).reshape(S5_NB, S5_SPB, LANE).astype(BF16)

    return blockdiag(c_re), blockdiag(c_im)


def _s5_in(ub, wb_ref, kb):
    return _dot(ub[:, kb * LANE:(kb + 1) * LANE], wb_ref[kb])


def _s5_tail(h_blocks, u, wcr_ref, wci_ref, d_ref, wglu_ref):
    ys = []
    for kb in range(S5_NB):
        hr, hi = h_blocks(kb)
        ys.append(_dot(hr.astype(BF16), wcr_ref[kb]) - _dot(hi.astype(BF16), wci_ref[kb]))
    y = jnp.concatenate(ys, axis=1) + d_ref[...] * u
    y = jax.nn.gelu(y)
    return y * _sigmoid(_dot(y.astype(BF16), wglu_ref[...]))


def _s5_s_body(u_ref, h0r_ref, h0i_ref, wb_ref, wcr_ref, wci_ref, tab_ref, d_ref, wglu_ref,
               o_ref, hr_ref, hi_ref):
    u = u_ref[...]
    ub = u.astype(BF16)
    for kb in range(S5_NB):
        cols = slice(kb * S5_SPB, (kb + 1) * S5_SPB)
        bu = _s5_in(ub, wb_ref, kb)
        a_r = tab_ref[6, 0:1, cols]
        a_i = tab_ref[7, 0:1, cols]
        h0r = h0r_ref[:, cols]
        h0i = h0i_ref[:, cols]
        hr_ref[:, cols] = a_r * h0r - a_i * h0i + bu[:, :S5_SPB]
        hi_ref[:, cols] = a_r * h0i + a_i * h0r + bu[:, S5_SPB:]

    def h_blocks(kb):
        cols = slice(kb * S5_SPB, (kb + 1) * S5_SPB)
        return hr_ref[:, cols], hi_ref[:, cols]

    o_ref[...] = _s5_tail(h_blocks, u, wcr_ref, wci_ref, d_ref, wglu_ref).astype(o_ref.dtype)


def s5_sample(proj, h0r, h0i, wb, wcr, wci, tab, d_skip, wglu):
    R = proj.shape[0]
    z2 = lambda i: (0, 0)
    return pl.pallas_call(
        _s5_s_body,
        grid=(1,),
        in_specs=[pl.BlockSpec((R, D_A), z2),
                  pl.BlockSpec((R, N_S5), z2), pl.BlockSpec((R, N_S5), z2),
                  pl.BlockSpec((S5_NB, LANE, 2 * S5_SPB), lambda i: (0, 0, 0)),
                  pl.BlockSpec((S5_NB, S5_SPB, LANE), lambda i: (0, 0, 0)),
                  pl.BlockSpec((S5_NB, S5_SPB, LANE), lambda i: (0, 0, 0)),
                  pl.BlockSpec((8, SUBLANE, N_S5), lambda i: (0, 0, 0)),
                  pl.BlockSpec((1, D_A), z2), pl.BlockSpec((D_A, D_A), z2)],
        out_specs=[pl.BlockSpec((R, D_A), z2), pl.BlockSpec((R, N_S5), z2), pl.BlockSpec((R, N_S5), z2)],
        out_shape=[jax.ShapeDtypeStruct((R, D_A), BF16),
                   jax.ShapeDtypeStruct((R, N_S5), F32),
                   jax.ShapeDtypeStruct((R, N_S5), F32)],
        compiler_params=_cp(("arbitrary",)),
        name="s5_sample",
    )(proj, h0r, h0i, wb, wcr, wci, tab, d_skip, wglu)


def _gmlp_p_body(u_ref, v_ref, gv_ref, wcat_ref, bias_ref, o_ref, *, tb):
    hd = D_B // H_B
    lane_head = lax.broadcasted_iota(jnp.int32, (GM_CHUNK, D_B), 1) // hd
    wi = lax.broadcasted_iota(jnp.int32, (GM_CHUNK, H_B * GM_CHUNK), 0)
    wj = lax.broadcasted_iota(jnp.int32, (GM_CHUNK, H_B * GM_CHUNK), 1) % GM_CHUNK
    wcat = jnp.where(wi >= wj, wcat_ref[...], 0.0).astype(BF16)
    for c in range(tb // GM_CHUNK):
        rows = slice(c * GM_CHUNK, (c + 1) * GM_CHUNK)
        vn = _rms(jax.nn.gelu(v_ref[rows, :])) * gv_ref[...]
        vb = vn.astype(BF16)
        stack = jnp.concatenate(
            [jnp.where(lane_head == h, vb, jnp.zeros_like(vb)) for h in range(H_B)], axis=0)
        s = _dot(wcat, stack) + bias_ref[...]
        o_ref[rows, :] = (jax.nn.gelu(u_ref[rows, :]) * s).astype(o_ref.dtype)


def gmlp_prompt(proj3, g_v, w_s, b_s, *, tb=512):
    B, L = proj3.shape[:2]
    hd = D_B // H_B
    wcat = jnp.transpose(w_s, (1, 0, 2)).reshape(GM_CHUNK, H_B * GM_CHUNK)
    bias = jnp.repeat(b_s.T, hd, axis=1)
    const2 = lambda b, t: (0, 0)
    ub = D_A // D_B
    return pl.pallas_call(
        functools.partial(_gmlp_p_body, tb=tb),
        grid=(B, L // tb),
        in_specs=[pl.BlockSpec((None, tb, D_B), lambda b, t: (b, t, ub)),
                  pl.BlockSpec((None, tb, D_B), lambda b, t: (b, t, ub + 1)),
                  pl.BlockSpec((1, D_B), const2),
                  pl.BlockSpec((GM_CHUNK, H_B * GM_CHUNK), const2),
                  pl.BlockSpec((GM_CHUNK, D_B), const2)],
        out_specs=pl.BlockSpec((None, tb, D_B), lambda b, t: (b, t, 0)),
        out_shape=jax.ShapeDtypeStruct((B, L, D_B), BF16),
        compiler_params=_cp(("arbitrary", "arbitrary")),
        name="gmlp_prompt",
    )(proj3, proj3, g_v, wcat, bias)


def _gmlp_s_body(u_ref, v_ref, gv_ref, w0_ref, b0_ref, o_ref, vn_ref):
    vn = _rms(jax.nn.gelu(v_ref[...])) * gv_ref[...]
    vn_ref[...] = vn
    s = w0_ref[...] * vn + b0_ref[...]
    o_ref[...] = (jax.nn.gelu(u_ref[...]) * s).astype(o_ref.dtype)


def gmlp_sample(proj, g_v, w_s, b_s):
    R = proj.shape[0]
    hd = D_B // H_B
    w0 = jnp.repeat(w_s[:, 0, 0], hd).reshape(1, D_B)
    b0 = jnp.repeat(b_s[:, 0], hd).reshape(1, D_B)
    z2 = lambda i: (0, 0)
    ub = D_A // D_B
    return pl.pallas_call(
        _gmlp_s_body,
        grid=(1,),
        in_specs=[pl.BlockSpec((R, D_B), lambda i: (0, ub)),
                  pl.BlockSpec((R, D_B), lambda i: (0, ub + 1)),
                  pl.BlockSpec((1, D_B), z2), pl.BlockSpec((1, D_B), z2), pl.BlockSpec((1, D_B), z2)],
        out_specs=[pl.BlockSpec((R, D_B), z2), pl.BlockSpec((R, D_B), z2)],
        out_shape=[jax.ShapeDtypeStruct((R, D_B), BF16), jax.ShapeDtypeStruct((R, D_B), F32)],
        compiler_params=_cp(("arbitrary",)),
        name="gmlp_sample",
    )(proj, proj, g_v, w0, b0)


def _head_expand_matrix():
    r = lax.broadcasted_iota(jnp.int32, (LANE, D_C), 0)
    c = lax.broadcasted_iota(jnp.int32, (LANE, D_C), 1) // HD_C
    return jnp.where(r == c, 1.0, 0.0).astype(BF16)


def _expand_heads(v, e):
    hi, lo = _split_bf16(v)
    return _dot(hi, e) + _dot(lo, e)


def _ssd_p_body(z_ref, xbc_ref, dt_ref, cw_ref, cb_ref, dtb_ref, alog_ref, dsk_ref, gn_ref,
                o_ref, st_ref, h_ref, h_sc, cv_sc, *, nt):
    Q = SSD_CHUNK
    t = pl.program_id(1)

    @pl.when(t == 0)
    def _():
        h_sc[...] = jnp.zeros_like(h_sc)
        cv_sc[0:SUBLANE, :] = jnp.zeros((SUBLANE, D_XBC), F32)

    xbc = xbc_ref[...]
    cv_sc[SUBLANE:SUBLANE + Q, :] = xbc
    acc = cb_ref[...]
    for k in range(K_C):
        off = SUBLANE - (K_C - 1) + k
        acc = acc + cw_ref[k:k + 1, :] * cv_sc[off:off + Q, :]
    cv_sc[0:SUBLANE, :] = xbc[Q - SUBLANE:, :]
    st_ref[...] = xbc[Q - (K_C - 1):, :]
    xc = _silu(acc)
    xs = xc[:, :D_C]

    dt = _softplus(dt_ref[...] + dtb_ref[...])
    a = -jnp.exp(alog_ref[...])
    da = dt * a
    ii = lax.broadcasted_iota(jnp.int32, (Q, Q), 0)
    jj = lax.broadcasted_iota(jnp.int32, (Q, Q), 1)
    causal = ii >= jj
    tril = jnp.where(causal, 1.0, 0.0).astype(BF16)
    d0, d1 = _split_bf16(da)
    d2 = (da - d0.astype(F32) - d1.astype(F32)).astype(BF16)
    cs = _dot(tril, d0) + _dot(tril, d1) + _dot(tril, d2)
    cst = cs.T
    cs_end = cs[Q - 1:Q, :]
    e = _head_expand_matrix()
    dt_f = _expand_heads(dt, e)
    ws_f = _expand_heads(dt * jnp.exp(cs_end - cs), e)
    ecs_f = _expand_heads(jnp.exp(cs), e)
    xdt = xs * dt_f
    xw = xs * ws_f

    lane = lax.broadcasted_iota(jnp.int32, (Q, LANE), 1)
    hpg = H_C // G_C
    gw = hpg * HD_C
    ys = []
    for g in range(G_C):
        bg = xc[:, D_C + g * N_C:D_C + (g + 1) * N_C].astype(BF16)
        cg = xc[:, D_C + G_C * N_C + g * N_C:D_C + G_C * N_C + (g + 1) * N_C].astype(BF16)
        gmat = _dot_nt(cg, bg)
        hprev = h_sc[g * gw:(g + 1) * gw, :]
        yoff = _dot_nt(cg, hprev.astype(BF16)) * ecs_f[:, g * gw:(g + 1) * gw]
        snew = _dot_tn(xw[:, g * gw:(g + 1) * gw].astype(BF16), bg)
        for hp in range(hpg // 2):
            h0 = g * hpg + 2 * hp
            xpair = xdt[:, h0 * HD_C:(h0 + 2) * HD_C]
            x_lo = jnp.where(lane < HD_C, xpair, 0.0).astype(BF16)
            x_hi = jnp.where(lane >= HD_C, xpair, 0.0).astype(BF16)
            yd = None
            for hh, xh in ((h0, x_lo), (h0 + 1, x_hi)):
                seg = cs[:, hh:hh + 1] - cst[hh:hh + 1, :]
                sc = (gmat * jnp.exp(jnp.where(causal, seg, NEG_BIG))).astype(BF16)
                part = _dot(sc, xh)
                yd = part if yd is None else yd + part
            ys.append(yd + yoff[:, 2 * hp * HD_C:(2 * hp + 2) * HD_C])
        for hh in range(hpg):
            h = g * hpg + hh
            cd = jnp.exp(cst[h:h + 1, Q - 1:Q])
            rows = slice(h * HD_C, (h + 1) * HD_C)
            h_sc[rows, :] = h_sc[rows, :] * cd + snew[hh * HD_C:(hh + 1) * HD_C, :]

    y = jnp.concatenate(ys, axis=1) + dsk_ref[...] * xs
    y = y * _silu(z_ref[...])
    o_ref[...] = (_rms(y) * gn_ref[...]).astype(o_ref.dtype)

    @pl.when(t == nt - 1)
    def _():
        h_ref[...] = h_sc[...]


def _ssd_params(dt_bias, a_log, d_skip):
    pad = LANE - H_C
    dtb = jnp.pad(dt_bias, (0, pad)).reshape(1, LANE)
    alog = jnp.pad(a_log, (0, pad)).reshape(1, LANE)
    dsk = jnp.repeat(d_skip, HD_C).reshape(1, D_C)
    return dtb, alog, dsk


def ssd_prompt(z3, xd3, conv_w, conv_b, dt_bias, a_log, d_skip, g_norm):
    B, L = z3.shape[:2]
    Q = SSD_CHUNK
    nt = L // Q
    dtb, alog, dsk = _ssd_params(dt_bias, a_log, d_skip)
    const2 = lambda b, t: (0, 0)
    blk = lambda w: pl.BlockSpec((None, Q, w), lambda b, t: (b, t, 0))
    return pl.pallas_call(
        functools.partial(_ssd_p_body, nt=nt),
        grid=(B, nt),
        in_specs=[blk(D_C), blk(D_XBC),
                  pl.BlockSpec((None, Q, LANE), lambda b, t: (b, t, D_XBC // LANE)),
                  pl.BlockSpec((K_C, D_XBC), const2), pl.BlockSpec((1, D_XBC), const2),
                  pl.BlockSpec((1, LANE), const2), pl.BlockSpec((1, LANE), const2),
                  pl.BlockSpec((1, D_C), const2), pl.BlockSpec((1, D_C), const2)],
        out_specs=[blk(D_C),
                   pl.BlockSpec((None, K_C - 1, D_XBC), lambda b, t: (b, 0, 0)),
                   pl.BlockSpec((None, H_C * HD_C, N_C), lambda b, t: (b, 0, 0))],
        out_shape=[jax.ShapeDtypeStruct((B, L, D_C), BF16),
                   jax.ShapeDtypeStruct((B, K_C - 1, D_XBC), F32),
                   jax.ShapeDtypeStruct((B, H_C * HD_C, N_C), F32)],
        scratch_shapes=[pltpu.VMEM((H_C * HD_C, N_C), F32), pltpu.VMEM((SUBLANE + Q, D_XBC), F32)],
        compiler_params=_cp(("arbitrary", "arbitrary")),
        name="ssd_prompt",
    )(z3, xd3, xd3, conv_w, conv_b.reshape(1, D_XBC), dtb, alog, dsk, g_norm.reshape(1, D_C))


def _ssd_s_body(z_ref, xbc_ref, dt_ref, buf_ref, h0_ref, cw_ref, cb_ref, dtb_ref, alog_ref,
                dsk_ref, gn_ref, *rest, tbatch, nsteps, has_prev):
    o_ref, hn_ref, xs_sc, bc_sc, xt_sc, at_sc, y_sc = rest[1:] if has_prev else rest
    s = pl.program_id(0)
    R = xs_sc.shape[0]

    @pl.when(s == 0)
    def _():
        acc = cb_ref[...] + cw_ref[K_C - 1:K_C, :] * xbc_ref[...]
        for k in range(K_C - 1):
            acc = acc + cw_ref[k:k + 1, :] * buf_ref[k]
        xc = _silu(acc)
        xs = xc[:, :D_C]
        xs_sc[...] = xs
        bc_sc[...] = xc[:, D_C:]
        dt = _softplus(dt_ref[...] + dtb_ref[...])
        e = _head_expand_matrix()
        dt_f = _expand_heads(dt, e)
        da_f = jnp.exp(_expand_heads(dt * (-jnp.exp(alog_ref[...])), e))
        for t_sc, v in ((xt_sc, xs * dt_f), (at_sc, da_f)):
            hi, lo = _split_bf16(v.T)
            t_sc[:, :R] = hi
            t_sc[:, R:] = lo

    gw = (H_C // G_C) * HD_C
    kk = lax.broadcasted_iota(jnp.int32, (2 * R, LANE), 0) % R
    r0 = pl.multiple_of(s * tbatch, tbatch)
    bc8 = bc_sc[pl.ds(r0, tbatch), :]
    ti = lax.broadcasted_iota(jnp.int32, (tbatch, gw), 0)
    ytile = [jnp.zeros((tbatch, gw), F32) for _ in range(G_C)]
    for i in range(tbatch):
        b = s * tbatch + i
        onehot = jnp.where(kk == b, 1.0, 0.0).astype(BF16)
        xb = _dot(xt_sc[...], onehot)
        ab = _dot(at_sc[...], onehot)
        for g in range(G_C):
            rows = slice(g * gw, (g + 1) * gw)
            brow = bc8[i:i + 1, g * N_C:(g + 1) * N_C]
            hn = h0_ref[i, rows, :] * ab[rows, :] + xb[rows, :] * brow
            hn_ref[i, rows, :] = hn
            c8 = bc8[:, G_C * N_C + g * N_C:G_C * N_C + (g + 1) * N_C]
            yg = _dot_nt(c8.astype(BF16), hn.astype(BF16))
            ytile[g] = jnp.where(ti == i, yg, ytile[g])
    for g in range(G_C):
        y_sc[pl.ds(r0, tbatch), g * gw:(g + 1) * gw] = ytile[g]

    @pl.when(s == nsteps - 1)
    def _():
        y = y_sc[...] + dsk_ref[...] * xs_sc[...]
        y = y * _silu(z_ref[...])
        o_ref[...] = (_rms(y) * gn_ref[...]).astype(o_ref.dtype)


def ssd_sample(z, xd, buf, h0_all, l, hn_all, conv_w, conv_b, dt_bias, a_log, d_skip, g_norm,
               *, tbatch=16):
    R = z.shape[0]
    nsteps = R // tbatch
    dtb, alog, dsk = _ssd_params(dt_bias, a_log, d_skip)
    z2 = lambda s: (0, 0)
    hw = H_C * HD_C
    st_spec = pl.BlockSpec((None, tbatch, hw, N_C), lambda s: (l, s, 0, 0))
    in_specs = [pl.BlockSpec((R, D_C), z2), pl.BlockSpec((R, D_XBC), z2),
                pl.BlockSpec((R, LANE), lambda s: (0, D_XBC // LANE)),
                pl.BlockSpec((K_C - 1, R, D_XBC), lambda s: (0, 0, 0)),
                st_spec,
                pl.BlockSpec((K_C, D_XBC), z2), pl.BlockSpec((1, D_XBC), z2),
                pl.BlockSpec((1, LANE), z2), pl.BlockSpec((1, LANE), z2),
                pl.BlockSpec((1, D_C), z2), pl.BlockSpec((1, D_C), z2)]
    args = [z, xd, xd, buf, h0_all, conv_w, conv_b.reshape(1, D_XBC), dtb, alog, dsk,
            g_norm.reshape(1, D_C)]
    aliases = {}
    if hn_all is not None:
        aliases = {len(args): 1}
        in_specs.append(pl.BlockSpec(memory_space=pl.ANY))
        args.append(hn_all)
    return pl.pallas_call(
        functools.partial(_ssd_s_body, tbatch=tbatch, nsteps=nsteps, has_prev=hn_all is not None),
        grid=(nsteps,),
        in_specs=in_specs,
        out_specs=[pl.BlockSpec((R, D_C), z2), st_spec],
        out_shape=[jax.ShapeDtypeStruct((R, D_C), BF16),
                   jax.ShapeDtypeStruct(h0_all.shape, F32)],
        scratch_shapes=[pltpu.VMEM((R, D_C), F32), pltpu.VMEM((R, 2 * G_C * N_C), F32),
                        pltpu.VMEM((D_C, 2 * R), BF16), pltpu.VMEM((D_C, 2 * R), BF16),
                        pltpu.VMEM((R, D_C), F32)],
        input_output_aliases=aliases,
        compiler_params=_cp(("arbitrary",)),
        name="ssd_sample",
    )(*args)


def _up_p_body(x_ref, wa_ref, wb_ref, cwa_ref, cwb_ref, cba_ref, cbb_ref,
               o_ref, sa_ref, sb_ref, wa_sc, wb_sc, ca_sc, cb_sc, *, tm, tiles_per_seq):
    i = pl.program_id(1)

    @pl.when(i == 0)
    def _():
        wa_sc[...] = wa_ref[...].astype(BF16)
        wb_sc[...] = wb_ref[...].astype(BF16)

    @pl.when(i % tiles_per_seq == 0)
    def _():
        ca_sc[0:SUBLANE, :] = jnp.zeros((SUBLANE, ca_sc.shape[1]), F32)
        cb_sc[0:SUBLANE, :] = jnp.zeros((SUBLANE, cb_sc.shape[1]), F32)

    x = x_ref[...]

    def half(w_sc, c_sc, cw_ref, cb_ref, st_ref):
        up = _dot(x, w_sc[...])
        c_sc[SUBLANE:SUBLANE + tm, :] = up
        acc = cb_ref[...]
        for k in range(K_F):
            off = SUBLANE - (K_F - 1) + k
            acc = acc + cw_ref[k:k + 1, :] * c_sc[off:off + tm, :]
        c_sc[0:SUBLANE, :] = up[tm - SUBLANE:, :]
        st_ref[...] = up[tm - (K_F - 1):, :]
        return acc

    a = half(wa_sc, ca_sc, cwa_ref, cba_ref, sa_ref)
    b = half(wb_sc, cb_sc, cwb_ref, cbb_ref, sb_ref)
    o_ref[...] = (_silu(a) * b).astype(o_ref.dtype)


def up_prompt(h2, w_up, l, conv_w, conv_b, *, seq_len, tm=1024, tn=512):
    M, D = h2.shape
    B = M // seq_len
    tiles_per_seq = seq_len // tm
    nj = D_FF // tn
    cb = conv_b.reshape(1, 2 * D_FF)
    body = functools.partial(_up_p_body, tm=tm, tiles_per_seq=tiles_per_seq)
    st_spec = pl.BlockSpec((None, K_F - 1, tn), lambda j, i: (i // tiles_per_seq, 0, j))
    return pl.pallas_call(
        body,
        grid=(nj, M // tm),
        in_specs=[pl.BlockSpec((tm, D), lambda j, i: (i, 0)),
                  pl.BlockSpec((None, D, tn), lambda j, i: (l, 0, j)),
                  pl.BlockSpec((None, D, tn), lambda j, i: (l, 0, nj + j)),
                  pl.BlockSpec((K_F, tn), lambda j, i: (0, j)),
                  pl.BlockSpec((K_F, tn), lambda j, i: (0, nj + j)),
                  pl.BlockSpec((1, tn), lambda j, i: (0, j)),
                  pl.BlockSpec((1, tn), lambda j, i: (0, nj + j))],
        out_specs=[pl.BlockSpec((tm, tn), lambda j, i: (i, j)), st_spec, st_spec],
        out_shape=[jax.ShapeDtypeStruct((M, D_FF), BF16),
                   jax.ShapeDtypeStruct((B, K_F - 1, D_FF), F32),
                   jax.ShapeDtypeStruct((B, K_F - 1, D_FF), F32)],
        scratch_shapes=[pltpu.VMEM((D, tn), BF16), pltpu.VMEM((D, tn), BF16),
                        pltpu.VMEM((SUBLANE + tm, tn), F32), pltpu.VMEM((SUBLANE + tm, tn), F32)],
        compiler_params=_cp(("arbitrary", "arbitrary")),
        name="up_prompt",
    )(h2, w_up, w_up, conv_w, conv_w, cb, cb)


def _up_s_body(x_ref, w_ref, buf_ref, cw_ref, cb_ref, *rest, nj, has_prev):
    o_ref, new_ref, a_sc = rest[1:] if has_prev else rest
    s = pl.program_id(0)
    up = _dot(x_ref[...], w_ref[...].astype(BF16))
    acc = cb_ref[...] + cw_ref[K_F - 1:K_F, :] * up
    for k in range(K_F - 1):
        acc = acc + cw_ref[k:k + 1, :] * buf_ref[:, k, :]
    for k in range(1, K_F - 1):
        new_ref[:, k - 1, :] = buf_ref[:, k, :]
    new_ref[:, K_F - 2, :] = up

    @pl.when(s < nj)
    def _():
        a_sc[s] = acc

    @pl.when(s >= nj)
    def _():
        o_ref[...] = (_silu(a_sc[s - nj]) * acc).astype(o_ref.dtype)


def up_sample(h2, w_up, l, buf_all, new_all, conv_w, conv_b, *, tn=512):
    R, D = h2.shape
    nj = D_FF // tn
    st = pl.BlockSpec((None, R, K_F - 1, tn), lambda s: (l, 0, 0, s))
    in_specs = [pl.BlockSpec((R, D), lambda s: (0, 0)),
                pl.BlockSpec((None, D, tn), lambda s: (l, 0, s)),
                st,
                pl.BlockSpec((K_F, tn), lambda s: (0, s)),
                pl.BlockSpec((1, tn), lambda s: (0, s))]
    args = [h2, w_up, buf_all, conv_w, conv_b.reshape(1, 2 * D_FF)]
    aliases = {}
    if new_all is not None:
        aliases = {len(args): 1}
        in_specs.append(pl.BlockSpec(memory_space=pl.ANY))
        args.append(new_all)
    act, new = pl.pallas_call(
        functools.partial(_up_s_body, nj=nj, has_prev=new_all is not None),
        grid=(2 * nj,),
        in_specs=in_specs,
        out_specs=[pl.BlockSpec((R, tn), lambda s: (0, jnp.maximum(s - nj, 0))), st],
        out_shape=[jax.ShapeDtypeStruct((R, D_FF), BF16),
                   jax.ShapeDtypeStruct(buf_all.shape, F32)],
        scratch_shapes=[pltpu.VMEM((nj, R, tn), F32)],
        input_output_aliases=aliases,
        compiler_params=_cp(("arbitrary",)),
        name="up_sample",
    )(*args)
    return act, new


def _layer(x3, h, mods, l, p, state, next_norm, *, seq_len, tm, tn_merge):
    B, L, D = x3.shape
    M = B * L
    sh_m, sc_m, gt_m, sh_f, sc_f, gt_f = mods
    prompt = state is None
    w_buffers = 1 if M > tm else 2

    def proj(col0, n, name):
        return matmul([h], [(p['w_in_t'], l, col0, True)], [], [(n, F32)], _epi_store,
                      x_of_w=(0,), tm=tm, tn=n, name=name, vmem_mb=56, w_buffers=1)[0]

    uav = proj(OFF_UA, D_A + 2 * D_B, "proj_uav")
    zc = proj(OFF_Z, D_C, "proj_z")
    xd = proj(OFF_XBC, D_XBC + LANE, "proj_xd")

    tab, wb, bbr, bbi = s5_prep(p['lam_re'][l], p['lam_im'][l], p['log_dt'][l],
                                p['b_re'][l], p['b_im'][l])
    d_a = p['s5_d'][l].reshape(1, D_A)
    wglu = p['w_glu'][l].astype(BF16)
    g_v = p['g_v'][l].reshape(1, D_B)

    if prompt:
        s5_ops = s5_chunk_prep(p['lam_re'][l], p['lam_im'][l], p['log_dt'][l], bbr, bbi,
                               p['c_re'][l], p['c_im'][l])
        o_a, s5r, s5i = s5_prompt(uav.reshape(B, L, -1), s5_ops, d_a, wglu)
        o_b = gmlp_prompt(uav.reshape(B, L, -1), g_v, p['w_s'][l], p['b_s'][l])
        v_rows = None
        o_c, convc, ssm = ssd_prompt(zc.reshape(B, L, -1), xd.reshape(B, L, -1),
                                     p['ssd_conv_w'][l], p['ssd_conv_b'][l], p['dt_bias'][l],
                                     p['a_log'][l], p['ssd_d'][l], p['ssd_g'][l])
        s5r = s5r.reshape(B, G_A, P_A)
        s5i = s5i.reshape(B, G_A, P_A)
    else:
        s5_re0, s5_im0, ssm_all, ssm_new_all, convc0, ffn_all, ffn_new_all = state
        wcr, wci = s5_out_weights(p['c_re'][l], p['c_im'][l])
        o_a, s5r, s5i = s5_sample(uav, s5_re0.reshape(M, N_S5), s5_im0.reshape(M, N_S5),
                                  wb, wcr, wci, tab, d_a, wglu)
        o_b, v_rows = gmlp_sample(uav, g_v, p['w_s'][l], p['b_s'][l])
        o_c, ssm = ssd_sample(zc, xd, jnp.transpose(convc0, (1, 0, 2)),
                              ssm_all, l, ssm_new_all,
                              p['ssd_conv_w'][l], p['ssd_conv_b'][l], p['dt_bias'][l],
                              p['a_log'][l], p['ssd_d'][l], p['ssd_g'][l])
        convc = jnp.concatenate([convc0[:, 1:], xd[:, None, :D_XBC]], axis=1)
        s5r = s5r.reshape(M, G_A, P_A)
        s5i = s5i.reshape(M, G_A, P_A)
    o_a = o_a.reshape(M, D_A)
    o_b = o_b.reshape(M, D_B)
    o_c = o_c.reshape(M, D_C)
    if prompt:
        ssm = ssm.reshape(B, H_C, HD_C, N_C)

    w_t = p['w_in_t']
    merged = matmul(
        [h, o_a, o_b, o_c],
        [(p['w_pa'], l, 0, False), (p['w_pb'], l, 0, False), (p['w_pc'], l, 0, False),
         (w_t, l, OFF_GATES, True), (w_t, l, OFF_GATES + D, True), (w_t, l, OFF_GATES + 2 * D, True)],
        [], [(D, BF16)], _epi_merge, x_of_w=(1, 2, 3, 0, 0, 0), tm=tm, tn=tn_merge, name="merge",
        vmem_mb=56, w_buffers=w_buffers)[0]

    tm_r = min(tm, 256)
    x2, h2 = residual_norm(merged, p['w_out'], l, x3.reshape(M, D), gt_m,
                           p['g_ffn'][l].reshape(1, D), (sc_f, sh_f), tm=min(tm, 512),
                           seq_len=seq_len, emit_x=True, name="out_norm")
    if prompt:
        act, st_a, st_b = up_prompt(h2, p['w_up'], l, p['ffn_conv_w'][l], p['ffn_conv_b'][l],
                                    seq_len=seq_len, tm=tm)
        convf = jnp.concatenate([st_a, st_b], axis=-1)
    else:
        act, convf = up_sample(h2, p['w_up'], l, ffn_all, ffn_new_all,
                               p['ffn_conv_w'][l], p['ffn_conv_b'][l])

    g_next, mod_next = next_norm
    x2, h_next = residual_norm(act, p['w_down_bf16'], l, x2, gt_f, g_next, mod_next, tm=tm_r,
                               seq_len=seq_len, emit_x=mod_next is not None, name="down_norm")
    x_out = None if x2 is None else x2.reshape(B, L, D)
    return x_out, h_next, s5r, s5i, ssm, convc, convf, v_rows


def kernel(x_prompt, x_sample, c_prompt, c_sample, state_s5_re, state_s5_im, state_ssm, state_ssd_conv, state_ffn_conv, w_mod, b_mod, g_mix, w_in, s5_lam_re, s5_lam_im, s5_log_dt, s5_b_re, s5_b_im, s5_c_re, s5_c_im, s5_d, s5_w_glu, gm_g_v, gm_w_s, gm_b_s, ssd_conv_w, ssd_conv_b, ssd_dt_bias, ssd_a_log, ssd_d, ssd_g_norm, w_pa, w_pb, w_pc, w_out, g_ffn, ffn_w_up, ffn_conv_w, ffn_conv_b, ffn_w_down, g_final):
    p = {
        'g_mix': g_mix, 'w_in_t': jnp.swapaxes(w_in, 1, 2),
        'lam_re': s5_lam_re, 'lam_im': s5_lam_im, 'log_dt': s5_log_dt,
        'b_re': s5_b_re, 'b_im': s5_b_im, 'c_re': s5_c_re, 'c_im': s5_c_im,
        's5_d': s5_d, 'w_glu': s5_w_glu,
        'g_v': gm_g_v, 'w_s': gm_w_s, 'b_s': gm_b_s,
        'ssd_conv_w': ssd_conv_w, 'ssd_conv_b': ssd_conv_b, 'dt_bias': ssd_dt_bias,
        'a_log': ssd_a_log, 'ssd_d': ssd_d, 'ssd_g': ssd_g_norm,
        'w_pa': w_pa, 'w_pb': w_pb, 'w_pc': w_pc, 'w_out': w_out,
        'g_ffn': g_ffn, 'w_up': ffn_w_up, 'ffn_conv_w': ffn_conv_w,
        'ffn_conv_b': ffn_conv_b, 'w_down_bf16': ffn_w_down.astype(BF16),
    }
    bp, seq, D = x_prompt.shape
    bs = x_sample.shape[0]

    n_c = bs + bp
    pad = (-n_c) % SUBLANE
    c_all = jnp.concatenate([c_sample, c_prompt, jnp.zeros((pad, D), F32)], axis=0)
    mod = mod_all(c_all, w_mod, b_mod)

    xp = x_prompt
    xs = x_sample.reshape(1, bs, D)
    outs_p = [[] for _ in range(5)]
    outs_s = [[] for _ in range(6)]
    ssm_all = state_ssm.reshape(DEPTH, bs, H_C * HD_C, N_C)
    ssm_new_all = None
    ffn_new_all = None
    mods_s = [[TokenMod(mod, l, k) for k in range(6)] for l in range(DEPTH)]
    mods_p = [[m[:, None, :] for m in jnp.split(mod[l, bs:bs + bp], 6, axis=-1)] for l in range(DEPTH)]

    def norm_after(l, mods):
        if l + 1 == DEPTH:
            return g_final.reshape(1, D), None
        return g_mix[l + 1].reshape(1, D), (mods[l + 1][1], mods[l + 1][0])

    g0 = g_mix[0].reshape(1, D)
    hp = norm_mod(xp, g0, mods_p[0][1], mods_p[0][0], tm=512).reshape(bp * seq, D)
    hs = norm_mod(xs, g0, mods_s[0][1], mods_s[0][0], tm=bs).reshape(bs, D)
    for l in range(DEPTH):
        xp, hp, *st_p = _layer(xp, hp, mods_p[l], l, p, None, norm_after(l, mods_p),
                               seq_len=seq, tm=1024, tn_merge=512)
        for acc, v in zip(outs_p, st_p[:5]):
            acc.append(v)
        state = (state_s5_re[l], state_s5_im[l], ssm_all, ssm_new_all, state_ssd_conv[l],
                 state_ffn_conv, ffn_new_all)
        xs, hs, *st_s = _layer(xs, hs, mods_s[l], l, p, state, norm_after(l, mods_s),
                               seq_len=1, tm=bs, tn_merge=256)
        ssm_new_all = st_s[2]
        ffn_new_all = st_s[4]
        st_s[5] = st_s[5].reshape(bs, 1, D_B)
        for acc, v in zip(outs_s, st_s):
            acc.append(v)

    y_prompt = hp.reshape(bp, seq, D)
    y_sample = hs.reshape(bs, 1, D)
    whole = {2: ssm_new_all.reshape(DEPTH, bs, H_C, HD_C, N_C),
             4: ffn_new_all}
    outs_s = [whole[k] if k in whole else jnp.stack(v) for k, v in enumerate(outs_s)]
    return (y_prompt, y_sample, *[jnp.stack(v) for v in outs_p], *outs_s)
```

```python
import functools
from typing import NamedTuple

import jax
import jax.numpy as jnp
from jax import lax
from jax.experimental import pallas as pl
from jax.experimental.pallas import tpu as pltpu

F32 = jnp.float32
BF16 = jnp.bfloat16

LANE = 128
SUBLANE = 8

D_MODEL = 2048
DEPTH = 2
D_A = 512
S5_GROUP = 16
G_A = D_A // S5_GROUP
P_A = 64
N_S5 = G_A * P_A
S5_GPB = LANE // S5_GROUP
S5_NB = D_A // LANE
S5_SPB = S5_GPB * P_A
S5_T = 8
D_B = 512
H_B = 8
GM_CHUNK = 128
D_C = 1024
HD_C = 64
H_C = D_C // HD_C
N_C = 128
G_C = 2
K_C = 4
SSD_CHUNK = 128
D_XBC = D_C + 2 * G_C * N_C
D_FF = 5632
K_F = 3
EPS = 1e-6

OFF_UA = 0
OFF_Z = D_A + 2 * D_B
OFF_XBC = OFF_Z + D_C
OFF_DT = OFF_XBC + D_XBC
OFF_GATES = OFF_DT + H_C

NEG_BIG = -1e30


def _cp(sem, vmem_mb=48):
    return pltpu.CompilerParams(dimension_semantics=sem, vmem_limit_bytes=vmem_mb * 1024 * 1024)


def _sigmoid(x):
    return 0.5 * jnp.tanh(0.5 * x) + 0.5


def _silu(x):
    return x * _sigmoid(x)


def _softplus(x):
    return jnp.maximum(x, 0.0) + jnp.log1p(jnp.exp(-jnp.abs(x)))


def _rms(x):
    return x * lax.rsqrt(jnp.mean(x * x, axis=-1, keepdims=True) + EPS)


def _split_bf16(x):
    hi = x.astype(BF16)
    lo = (x - hi.astype(F32)).astype(BF16)
    return hi, lo


def _dot(a, b):
    return jnp.dot(a, b, preferred_element_type=F32)


def _dot_nt(a, b):
    return lax.dot_general(a, b, (((1,), (1,)), ((), ())), preferred_element_type=F32)


def _dot_tn(a, b):
    return lax.dot_general(a, b, (((0,), (0,)), ((), ())), preferred_element_type=F32)


def _mod_body(c_ref, w_ref, b_ref, o_ref):
    a = _silu(c_ref[...]).astype(BF16)
    o_ref[...] = _dot(a, w_ref[...].astype(BF16)) + b_ref[...]


def mod_all(c_all, w_mod, b_mod, *, tn=1024):
    R, D = c_all.shape
    N = w_mod.shape[-1]
    return pl.pallas_call(
        _mod_body,
        grid=(DEPTH, N // tn),
        in_specs=[pl.BlockSpec((R, D), lambda l, j: (0, 0)),
                  pl.BlockSpec((None, D, tn), lambda l, j: (l, 0, j)),
                  pl.BlockSpec((None, 1, tn), lambda l, j: (l, 0, j))],
        out_specs=pl.BlockSpec((None, R, tn), lambda l, j: (l, 0, j)),
        out_shape=jax.ShapeDtypeStruct((DEPTH, R, N), F32),
        compiler_params=_cp(("arbitrary", "arbitrary")),
        name="mod_all",
    )(c_all, w_mod, b_mod.reshape(DEPTH, 1, N))


def _norm_mod_body(x_ref, g_ref, sc_ref, sh_ref, o_ref):
    y = _rms(x_ref[...]) * g_ref[...]
    o_ref[...] = (y * (1.0 + sc_ref[...]) + sh_ref[...]).astype(o_ref.dtype)


class TokenMod(NamedTuple):
    arr: jax.Array
    l: int
    k: int


def _mod_operand(m, tm, D, tile_of, seq_of):
    if isinstance(m, TokenMod):
        return m.arr, pl.BlockSpec((None, tm, D), lambda *g: (m.l, tile_of(*g), m.k))
    return m, pl.BlockSpec((None, 1, D), lambda *g: (seq_of(*g), 0, 0))


def norm_mod(x3, g, sc, sh, *, tm):
    B, L, D = x3.shape
    ops = [_mod_operand(m, tm, D, lambda b, i: i, lambda b, i: b) for m in (sc, sh)]
    return pl.pallas_call(
        _norm_mod_body,
        grid=(B, L // tm),
        in_specs=[pl.BlockSpec((None, tm, D), lambda b, i: (b, i, 0)),
                  pl.BlockSpec((1, D), lambda b, i: (0, 0)),
                  ops[0][1], ops[1][1]],
        out_specs=pl.BlockSpec((None, tm, D), lambda b, i: (b, i, 0)),
        out_shape=jax.ShapeDtypeStruct((B, L, D), BF16),
        compiler_params=_cp(("arbitrary", "arbitrary")),
        name="norm_mod",
    )(x3, g, ops[0][0], ops[1][0])


def _mm_body(*refs, x_of_w, w_is_t, n_x, epilogue):
    n_w = len(x_of_w)
    xs = refs[:n_x]
    ws = refs[n_x:n_x + n_w]
    o_ref = refs[n_x + n_w]
    wsc = refs[n_x + n_w + 1:]

    @pl.when(pl.program_id(1) == 0)
    def _():
        for w, s, is_t in zip(ws, wsc, w_is_t):
            s[...] = (w[...].T if is_t else w[...]).astype(BF16)

    accs = [_dot(xs[xi][...], s[...]) for xi, s in zip(x_of_w, wsc)]
    o_ref[...] = epilogue(accs).astype(o_ref.dtype)


def matmul(xs, ws, n_out, out_dtype, epilogue, *, x_of_w, tm, tn, name, w_buffers):
    M = xs[0].shape[0]
    N = n_out
    in_specs = [pl.BlockSpec((tm, x.shape[1]), lambda j, i: (i, 0)) for x in xs]
    args = list(xs)
    scratch = []
    for w3, l, col0, is_t in ws:
        if is_t:
            K = w3.shape[2]
            assert col0 % SUBLANE == 0
            in_specs.append(pl.BlockSpec((None, pl.Element(tn), pl.Element(K)), functools.partial(
                lambda j, i, l, c: (l, (c + j * (tn // SUBLANE)) * SUBLANE, 0),
                l=l, c=col0 // SUBLANE), pipeline_mode=pl.Buffered(w_buffers)))
        else:
            K = w3.shape[1]
            assert col0 % tn == 0
            in_specs.append(pl.BlockSpec((None, K, tn), functools.partial(
                lambda j, i, l, c: (l, 0, c + j), l=l, c=col0 // tn),
                pipeline_mode=pl.Buffered(w_buffers)))
        args.append(w3)
        scratch.append(pltpu.VMEM((K, tn), BF16))
    body = functools.partial(_mm_body, x_of_w=tuple(x_of_w), w_is_t=tuple(w[3] for w in ws),
                             n_x=len(xs), epilogue=epilogue)
    return pl.pallas_call(
        body,
        grid=(N // tn, M // tm),
        in_specs=in_specs,
        out_specs=pl.BlockSpec((tm, tn), lambda j, i: (i, j)),
        out_shape=jax.ShapeDtypeStruct((M, N), out_dtype),
        scratch_shapes=scratch,
        compiler_params=_cp(("arbitrary", "arbitrary"), 56),
        name=name,
    )(*args)


def _epi_first(accs):
    return accs[0]


def _epi_merge(accs):
    pa, pb, pc, ga, gb, gc = accs
    return _sigmoid(ga) * pa + _sigmoid(gb) * pb + _sigmoid(gc) * pc


def _res_norm_body(*refs, cast_w, modulated, emit_x):
    a_ref, w_ref, res_ref, gt_ref, g_ref = refs[:5]
    refs = refs[5:]
    if modulated:
        sc_ref, sh_ref = refs[:2]
        refs = refs[2:]
    if emit_x:
        x_ref = refs[0]
        refs = refs[1:]
    h_ref = refs[0]
    if cast_w:
        w_sc = refs[1]

        @pl.when(pl.program_id(0) == 0)
        def _():
            w_sc[...] = w_ref[...].astype(BF16)

        w = w_sc[...]
    else:
        w = w_ref[...]
    x = res_ref[...] + gt_ref[...] * _dot(a_ref[...], w)
    if emit_x:
        x_ref[...] = x
    y = _rms(x) * g_ref[...]
    if modulated:
        y = y * (1.0 + sc_ref[...]) + sh_ref[...]
    h_ref[...] = y.astype(h_ref.dtype)


def residual_norm(act, w3, l, res, gt, g, mod, *, tm, seq_len, emit_x, name):
    M, K = act.shape
    D = res.shape[1]
    tiles_per_seq = max(seq_len // tm, 1)
    cast_w = w3.dtype != BF16
    operand = lambda m: _mod_operand(m, tm, D, lambda i: i, lambda i: i // tiles_per_seq)

    row = lambda w: pl.BlockSpec((tm, w), lambda i: (i, 0))
    gt_arr, gt_spec = operand(gt)
    in_specs = [row(K),
                pl.BlockSpec((None, K, D), lambda i: (l, 0, 0), pipeline_mode=pl.Buffered(1)),
                row(D), gt_spec, pl.BlockSpec((1, D), lambda i: (0, 0))]
    args = [act, w3, res, gt_arr, g]
    if mod is not None:
        for m_arr, m_spec in map(operand, mod):
            in_specs.append(m_spec)
            args.append(m_arr)
    out_specs = [row(D)] * (2 if emit_x else 1)
    out_shape = ([jax.ShapeDtypeStruct((M, D), F32)] if emit_x else []) + [
        jax.ShapeDtypeStruct((M, D), BF16 if mod is not None else F32)]
    outs = pl.pallas_call(
        functools.partial(_res_norm_body, cast_w=cast_w, modulated=mod is not None, emit_x=emit_x),
        grid=(M // tm,),
        in_specs=in_specs,
        out_specs=out_specs,
        out_shape=out_shape,
        scratch_shapes=[pltpu.VMEM((K, D), BF16)] if cast_w else [],
        compiler_params=_cp(("arbitrary",), 56),
        name=name,
    )(*args)
    return (outs[0], outs[1]) if emit_x else (None, outs[0])


def _s5_prep_body(lrf_ref, lif_ref, ldf_ref, lrr_ref, lir_ref, ldr_ref, bre_ref, bim_ref,
                  tab_ref, bbr_ref, bbi_ref):
    dtf = jnp.exp(ldf_ref[...])
    _power_tables(tab_ref, lrf_ref[...] * dtf, lif_ref[...] * dtf, 1)
    dtr = jnp.exp(ldr_ref[...])
    lr = lrr_ref[...]
    li = lir_ref[...]
    m1 = jnp.exp(lr * dtr)
    ar = m1 * jnp.cos(li * dtr)
    ai = m1 * jnp.sin(li * dtr)
    den = lr * lr + li * li
    nr = ar - 1.0
    kr = (nr * lr + ai * li) / den
    ki = (ai * lr - nr * li) / den
    bre = bre_ref[...]
    bim = bim_ref[...]
    bbr_ref[...] = kr * bre - ki * bim
    bbi_ref[...] = kr * bim + ki * bre


def s5_prep(lam_re, lam_im, log_dt, b_re, b_im):
    lrf = lam_re.reshape(1, N_S5)
    lif = lam_im.reshape(1, N_S5)
    ldf = jnp.repeat(log_dt, P_A).reshape(1, N_S5)
    lrr = jnp.repeat(lam_re, S5_GROUP, axis=0)
    lir = jnp.repeat(lam_im, S5_GROUP, axis=0)
    ldr = jnp.repeat(log_dt, S5_GROUP).reshape(D_A, 1)
    bre = jnp.transpose(b_re, (0, 2, 1)).reshape(D_A, P_A)
    bim = jnp.transpose(b_im, (0, 2, 1)).reshape(D_A, P_A)
    tab, bbr, bbi = pl.pallas_call(
        _s5_prep_body,
        out_shape=[jax.ShapeDtypeStruct((8, SUBLANE, N_S5), F32),
                   jax.ShapeDtypeStruct((D_A, P_A), F32),
                   jax.ShapeDtypeStruct((D_A, P_A), F32)],
        name="s5_prep",
    )(lrf, lif, ldf, lrr, lir, ldr, bre, bim)
    eye = jnp.eye(S5_GPB, dtype=F32)

    def blockdiag(m):
        m = m.reshape(S5_NB, S5_GPB, S5_GROUP, P_A)
        return jnp.einsum('kghp,gj->kghjp', m, eye).reshape(S5_NB, LANE, S5_SPB)

    wb = jnp.concatenate([blockdiag(bbr), blockdiag(bbi)], axis=2).astype(BF16)
    return tab, wb, bbr, bbi


def _power_tables(tab_ref, lr_dt, li_dt, stride):
    n_lanes = lr_dt.shape[1]
    row = lax.broadcasted_iota(jnp.int32, (SUBLANE, n_lanes), 0)
    n = ((row + 1) * stride).astype(F32)
    mag = jnp.exp(n * lr_dt)
    ang = n * li_dt
    pr = mag * jnp.cos(ang)
    pi = mag * jnp.sin(ang)
    for k, d in enumerate((1, 2, 4)):
        keep = row >= d
        tab_ref[2 * k] = jnp.where(keep, pr[d - 1:d, :], 0.0)
        tab_ref[2 * k + 1] = jnp.where(keep, pi[d - 1:d, :], 0.0)
    tab_ref[6] = pr
    tab_ref[7] = pi


def _s5_chunk_prep_body(lrf_ref, lif_ref, ldf_ref, lrr_ref, lir_ref, ldr_ref, bbr_ref, bbi_ref,
                        cr_ref, ci_ref, tab_ref, wst_ref, kt_ref, wor_ref, woi_ref):
    T = S5_T
    dtf = jnp.exp(ldf_ref[...])
    _power_tables(tab_ref, lrf_ref[...] * dtf, lif_ref[...] * dtf, T)
    dtr = jnp.exp(ldr_ref[...])
    lr = lrr_ref[...] * dtr
    li = lir_ref[...] * dtr
    bbr = bbr_ref[...]
    bbi = bbi_ref[...]
    cr = cr_ref[...]
    ci = ci_ref[...]
    spread = jnp.where(lax.broadcasted_iota(jnp.int32, (P_A, S5_SPB), 0)
                       == lax.broadcasted_iota(jnp.int32, (P_A, S5_SPB), 1) % P_A, 1.0, 0.0).astype(BF16)
    spread_t = jnp.where(lax.broadcasted_iota(jnp.int32, (S5_SPB, P_A), 0) % P_A
                         == lax.broadcasted_iota(jnp.int32, (S5_SPB, P_A), 1), 1.0, 0.0).astype(BF16)
    in_mask = (lax.broadcasted_iota(jnp.int32, (LANE, S5_SPB), 0) // S5_GROUP
               == lax.broadcasted_iota(jnp.int32, (LANE, S5_SPB), 1) // P_A)
    out_mask = (lax.broadcasted_iota(jnp.int32, (S5_SPB, LANE), 0) // P_A
                == lax.broadcasted_iota(jnp.int32, (S5_SPB, LANE), 1) // S5_GROUP)
    k_mask = (lax.broadcasted_iota(jnp.int32, (LANE, LANE), 0) // S5_GROUP
              == lax.broadcasted_iota(jnp.int32, (LANE, LANE), 1) // S5_GROUP)
    zero_tile = jnp.zeros((LANE, LANE), BF16)
    mag = jnp.exp(lr)
    a_r = mag * jnp.cos(li)
    a_i = mag * jnp.sin(li)
    pr = jnp.ones_like(lr)
    pi = jnp.zeros_like(lr)
    for n in range(T + 1):
        if n >= 1:
            pr, pi = pr * a_r - pi * a_i, pr * a_i + pi * a_r
        car = (cr * pr - ci * pi).astype(BF16)
        cai = (cr * pi + ci * pr).astype(BF16)
        for k in range(S5_NB):
            blk = slice(k * LANE, (k + 1) * LANE)
            if n >= 1:
                cols = slice((n - 1) * LANE, n * LANE)
                wor_ref[k, :, cols] = jnp.where(out_mask, _dot_nt(spread_t, car[blk]), 0.0).astype(BF16)
                woi_ref[k, :, cols] = jnp.where(out_mask, _dot_nt(spread_t, cai[blk]), 0.0).astype(BF16)
            if n < T:
                rows = slice((T - 1 - n) * LANE, (T - n) * LANE)
                wr = (pr * bbr - pi * bbi)[blk].astype(BF16)
                wi = (pr * bbi + pi * bbr)[blk].astype(BF16)
                wst_ref[k, rows, 0:S5_SPB] = jnp.where(in_mask, _dot(wr, spread), 0.0).astype(BF16)
                wst_ref[k, rows, S5_SPB:2 * S5_SPB] = jnp.where(in_mask, _dot(wi, spread), 0.0).astype(BF16)
                kt = (_dot_nt(bbr[blk].astype(BF16), car[blk])
                      - _dot_nt(bbi[blk].astype(BF16), cai[blk]))
                kt = jnp.where(k_mask, kt, 0.0).astype(BF16)
                for ti in range(T - n):
                    to = ti + n
                    kt_ref[k, ti * LANE:(ti + 1) * LANE, to * LANE:(to + 1) * LANE] = kt
                    if n >= 1:
                        kt_ref[k, to * LANE:(to + 1) * LANE, ti * LANE:(ti + 1) * LANE] = zero_tile


def s5_chunk_prep(lam_re, lam_im, log_dt, bbr, bbi, c_re, c_im):
    T = S5_T
    lrf = lam_re.reshape(1, N_S5)
    lif = lam_im.reshape(1, N_S5)
    ldf = jnp.repeat(log_dt, P_A).reshape(1, N_S5)
    lrr = jnp.repeat(lam_re, S5_GROUP, axis=0)
    lir = jnp.repeat(lam_im, S5_GROUP, axis=0)
    ldr = jnp.repeat(log_dt, S5_GROUP).reshape(D_A, 1)
    op = lambda r, c: jax.ShapeDtypeStruct((S5_NB, r, c), BF16)
    return pl.pallas_call(
        _s5_chunk_prep_body,
        out_shape=[jax.ShapeDtypeStruct((8, SUBLANE, N_S5), F32), op(T * LANE, 2 * S5_SPB),
                   op(T * LANE, T * LANE), op(S5_SPB, T * LANE), op(S5_SPB, T * LANE)],
        compiler_params=pltpu.CompilerParams(vmem_limit_bytes=48 * 1024 * 1024),
        name="s5_chunk_prep",
    )(lrf, lif, ldf, lrr, lir, ldr, bbr, bbi, c_re.reshape(D_A, P_A), c_im.reshape(D_A, P_A))


def _s5_c_body(*refs, tb):
    T = S5_T
    u_refs = refs[:S5_NB]
    (wst_ref, kt_ref, wor_ref, woi_ref, tab_ref, d_ref, wglu_ref,
     o_ref, hr_ref, hi_ref, s_sc, car_sc) = refs[S5_NB:-S5_NB]
    o_scs = refs[-S5_NB:]

    def step_rows(kb, t):
        return u_refs[kb][pl.ds(t, tb, stride=T), :]

    @pl.when(pl.program_id(1) == 0)
    def _():
        car_sc[...] = jnp.zeros_like(car_sc)

    s_sc[0:SUBLANE, :] = car_sc[...]
    xks = [jnp.concatenate([step_rows(kb, t).astype(BF16) for t in range(T)], axis=1)
           for kb in range(S5_NB)]
    for kb in range(S5_NB):
        s = _dot(xks[kb], wst_ref[kb])
        s_sc[SUBLANE:SUBLANE + tb, kb * S5_SPB:(kb + 1) * S5_SPB] = s[:, :S5_SPB]
        s_sc[SUBLANE:SUBLANE + tb, N_S5 + kb * S5_SPB:N_S5 + (kb + 1) * S5_SPB] = s[:, S5_SPB:]

    def tile(rt, _):
        r0 = pl.multiple_of(SUBLANE + rt * SUBLANE, SUBLANE)
        for lg in range(N_S5 // LANE):
            cre = slice(lg * LANE, (lg + 1) * LANE)
            cim = slice(N_S5 + lg * LANE, N_S5 + (lg + 1) * LANE)
            xr = s_sc[pl.ds(r0, SUBLANE), cre]
            xi = s_sc[pl.ds(r0, SUBLANE), cim]
            for k, d in enumerate((1, 2, 4)):
                a_r = tab_ref[2 * k, :, cre]
                a_i = tab_ref[2 * k + 1, :, cre]
                sr = pltpu.roll(xr, d, 0)
                si = pltpu.roll(xi, d, 0)
                xr, xi = xr + a_r * sr - a_i * si, xi + a_r * si + a_i * sr
            p_r = tab_ref[6, :, cre]
            p_i = tab_ref[7, :, cre]
            cr = car_sc[:, cre]
            ci = car_sc[:, cim]
            xr, xi = xr + p_r * cr - p_i * ci, xi + p_r * ci + p_i * cr
            s_sc[pl.ds(r0, SUBLANE), cre] = xr
            s_sc[pl.ds(r0, SUBLANE), cim] = xi
            last = SUBLANE - 1
            car_sc[:, cre] = jnp.broadcast_to(xr[last:last + 1, :], (SUBLANE, LANE))
            car_sc[:, cim] = jnp.broadcast_to(xi[last:last + 1, :], (SUBLANE, LANE))
        return 0

    lax.fori_loop(0, tb // SUBLANE, tile, 0)

    ys = []
    for kb in range(S5_NB):
        h_r = s_sc[SUBLANE - 1:SUBLANE - 1 + tb, kb * S5_SPB:(kb + 1) * S5_SPB].astype(BF16)
        h_i = s_sc[SUBLANE - 1:SUBLANE - 1 + tb,
                   N_S5 + kb * S5_SPB:N_S5 + (kb + 1) * S5_SPB].astype(BF16)
        ys.append(_dot(xks[kb], kt_ref[kb])
                  + _dot(h_r, wor_ref[kb]) - _dot(h_i, woi_ref[kb]))
    for t in range(T):
        y = jnp.concatenate([ys[kb][:, t * LANE:(t + 1) * LANE] for kb in range(S5_NB)], axis=1)
        u = jnp.concatenate([step_rows(kb, t) for kb in range(S5_NB)], axis=1)
        y = jax.nn.gelu(y + d_ref[...] * u)
        y = y * _sigmoid(_dot(y.astype(BF16), wglu_ref[...]))
        for kb in range(S5_NB):
            o_scs[kb][pl.ds(t, tb, stride=T), :] = y[:, kb * LANE:(kb + 1) * LANE]
    for kb in range(S5_NB):
        o_ref[:, kb * LANE:(kb + 1) * LANE] = o_scs[kb][...].astype(o_ref.dtype)
    hr_ref[...] = car_sc[0:1, :N_S5]
    hi_ref[...] = car_sc[0:1, N_S5:]


def s5_prompt(proj3, ops, d_skip, wglu, *, tb=256):
    tab, wst, ktoe, wor, woi = ops
    B, L, _ = proj3.shape
    T = S5_T
    nc = L // T
    tb = min(tb, nc)
    bw = T * LANE
    const3 = lambda b, t: (0, 0, 0)
    once = dict(pipeline_mode=pl.Buffered(1))
    u_specs = [pl.BlockSpec((None, tb * T, LANE), functools.partial(lambda b, t, kb: (b, t, kb), kb=kb))
               for kb in range(S5_NB)]
    return pl.pallas_call(
        functools.partial(_s5_c_body, tb=tb),
        grid=(B, nc // tb),
        in_specs=[*u_specs,
                  pl.BlockSpec((S5_NB, bw, 2 * S5_SPB), const3, **once),
                  pl.BlockSpec((S5_NB, bw, bw), const3, **once),
                  pl.BlockSpec((S5_NB, S5_SPB, bw), const3, **once),
                  pl.BlockSpec((S5_NB, S5_SPB, bw), const3, **once),
                  pl.BlockSpec((8, SUBLANE, N_S5), const3, **once),
                  pl.BlockSpec((1, D_A), lambda b, t: (0, 0)),
                  pl.BlockSpec((D_A, D_A), lambda b, t: (0, 0))],
        out_specs=[pl.BlockSpec((None, tb * T, D_A), lambda b, t: (b, t, 0)),
                   pl.BlockSpec((None, 1, N_S5), lambda b, t: (b, 0, 0)),
                   pl.BlockSpec((None, 1, N_S5), lambda b, t: (b, 0, 0))],
        out_shape=[jax.ShapeDtypeStruct((B, L, D_A), BF16),
                   jax.ShapeDtypeStruct((B, 1, N_S5), F32),
                   jax.ShapeDtypeStruct((B, 1, N_S5), F32)],
        scratch_shapes=[pltpu.VMEM((SUBLANE + tb, 2 * N_S5), F32),
                        pltpu.VMEM((SUBLANE, 2 * N_S5), F32)]
        + [pltpu.VMEM((tb * T, LANE), F32)] * S5_NB,
        compiler_params=_cp(("arbitrary", "arbitrary"), 58),
        name="s5_prompt",
    )(*[proj3] * S5_NB, wst, ktoe, wor, woi, tab, d_skip, wglu)


def s5_out_weights(c_re, c_im):
    eye = jnp.eye(S5_GPB, dtype=F32)

    def blockdiag(c):
        c = c.reshape(S5_NB, S5_GPB, S5_GROUP, P_A)
        return jnp.einsum('kghp,gj->kgpjh', c, eye).reshape(S5_NB, S5_SPB, LANE).astype(BF16)

    return blockdiag(c_re), blockdiag(c_im)


def _s5_in(ub, wb_ref, kb):
    return _dot(ub[:, kb * LANE:(kb + 1) * LANE], wb_ref[kb])


def _s5_tail(h_blocks, u, wcr_ref, wci_ref, d_ref, wglu_ref):
    ys = []
    for kb in range(S5_NB):
        hr, hi = h_blocks(kb)
        ys.append(_dot(hr.astype(BF16), wcr_ref[kb]) - _dot(hi.astype(BF16), wci_ref[kb]))
    y = jnp.concatenate(ys, axis=1) + d_ref[...] * u
    y = jax.nn.gelu(y)
    return y * _sigmoid(_dot(y.astype(BF16), wglu_ref[...]))


def _s5_s_body(u_ref, h0r_ref, h0i_ref, wb_ref, wcr_ref, wci_ref, tab_ref, d_ref, wglu_ref,
               o_ref, hr_ref, hi_ref):
    u = u_ref[...]
    ub = u.astype(BF16)
    for kb in range(S5_NB):
        cols = slice(kb * S5_SPB, (kb + 1) * S5_SPB)
        bu = _s5_in(ub, wb_ref, kb)
        a_r = tab_ref[6, 0:1, cols]
        a_i = tab_ref[7, 0:1, cols]
        h0r = h0r_ref[:, cols]
        h0i = h0i_ref[:, cols]
        hr_ref[:, cols] = a_r * h0r - a_i * h0i + bu[:, :S5_SPB]
        hi_ref[:, cols] = a_r * h0i + a_i * h0r + bu[:, S5_SPB:]

    def h_blocks(kb):
        cols = slice(kb * S5_SPB, (kb + 1) * S5_SPB)
        return hr_ref[:, cols], hi_ref[:, cols]

    o_ref[...] = _s5_tail(h_blocks, u, wcr_ref, wci_ref, d_ref, wglu_ref).astype(o_ref.dtype)


def s5_sample(proj, h0r, h0i, wb, wcr, wci, tab, d_skip, wglu):
    R = proj.shape[0]
    z2 = lambda i: (0, 0)
    return pl.pallas_call(
        _s5_s_body,
        grid=(1,),
        in_specs=[pl.BlockSpec((R, D_A), z2),
                  pl.BlockSpec((R, N_S5), z2), pl.BlockSpec((R, N_S5), z2),
                  pl.BlockSpec((S5_NB, LANE, 2 * S5_SPB), lambda i: (0, 0, 0)),
                  pl.BlockSpec((S5_NB, S5_SPB, LANE), lambda i: (0, 0, 0)),
                  pl.BlockSpec((S5_NB, S5_SPB, LANE), lambda i: (0, 0, 0)),
                  pl.BlockSpec((8, SUBLANE, N_S5), lambda i: (0, 0, 0)),
                  pl.BlockSpec((1, D_A), z2), pl.BlockSpec((D_A, D_A), z2)],
        out_specs=[pl.BlockSpec((R, D_A), z2), pl.BlockSpec((R, N_S5), z2), pl.BlockSpec((R, N_S5), z2)],
        out_shape=[jax.ShapeDtypeStruct((R, D_A), BF16),
                   jax.ShapeDtypeStruct((R, N_S5), F32),
                   jax.ShapeDtypeStruct((R, N_S5), F32)],
        compiler_params=_cp(("arbitrary",)),
        name="s5_sample",
    )(proj, h0r, h0i, wb, wcr, wci, tab, d_skip, wglu)


def _gmlp_p_body(u_ref, v_ref, gv_ref, wcat_ref, bias_ref, o_ref, *, tb):
    hd = D_B // H_B
    lane_head = lax.broadcasted_iota(jnp.int32, (GM_CHUNK, D_B), 1) // hd
    wi = lax.broadcasted_iota(jnp.int32, (GM_CHUNK, H_B * GM_CHUNK), 0)
    wj = lax.broadcasted_iota(jnp.int32, (GM_CHUNK, H_B * GM_CHUNK), 1) % GM_CHUNK
    wcat = jnp.where(wi >= wj, wcat_ref[...], 0.0).astype(BF16)
    for c in range(tb // GM_CHUNK):
        rows = slice(c * GM_CHUNK, (c + 1) * GM_CHUNK)
        vn = _rms(jax.nn.gelu(v_ref[rows, :])) * gv_ref[...]
        vb = vn.astype(BF16)
        stack = jnp.concatenate(
            [jnp.where(lane_head == h, vb, jnp.zeros_like(vb)) for h in range(H_B)], axis=0)
        s = _dot(wcat, stack) + bias_ref[...]
        o_ref[rows, :] = (jax.nn.gelu(u_ref[rows, :]) * s).astype(o_ref.dtype)


def gmlp_prompt(proj3, g_v, w_s, b_s, *, tb=1024):
    B, L = proj3.shape[:2]
    tb = min(tb, L)
    hd = D_B // H_B
    wcat = jnp.transpose(w_s, (1, 0, 2)).reshape(GM_CHUNK, H_B * GM_CHUNK)
    bias = jnp.repeat(b_s.T, hd, axis=1)
    const2 = lambda b, t: (0, 0)
    ub = D_A // D_B
    return pl.pallas_call(
        functools.partial(_gmlp_p_body, tb=tb),
        grid=(B, L // tb),
        in_specs=[pl.BlockSpec((None, tb, D_B), lambda b, t: (b, t, ub)),
                  pl.BlockSpec((None, tb, D_B), lambda b, t: (b, t, ub + 1)),
                  pl.BlockSpec((1, D_B), const2),
                  pl.BlockSpec((GM_CHUNK, H_B * GM_CHUNK), const2),
                  pl.BlockSpec((GM_CHUNK, D_B), const2)],
        out_specs=pl.BlockSpec((None, tb, D_B), lambda b, t: (b, t, 0)),
        out_shape=jax.ShapeDtypeStruct((B, L, D_B), BF16),
        compiler_params=_cp(("arbitrary", "arbitrary")),
        name="gmlp_prompt",
    )(proj3, proj3, g_v, wcat, bias)


def _gmlp_s_body(u_ref, v_ref, gv_ref, w0_ref, b0_ref, o_ref, vn_ref):
    vn = _rms(jax.nn.gelu(v_ref[...])) * gv_ref[...]
    vn_ref[...] = vn
    s = w0_ref[...] * vn + b0_ref[...]
    o_ref[...] = (jax.nn.gelu(u_ref[...]) * s).astype(o_ref.dtype)


def gmlp_sample(proj, g_v, w_s, b_s):
    R = proj.shape[0]
    hd = D_B // H_B
    w0 = jnp.repeat(w_s[:, 0, 0], hd).reshape(1, D_B)
    b0 = jnp.repeat(b_s[:, 0], hd).reshape(1, D_B)
    z2 = lambda i: (0, 0)
    ub = D_A // D_B
    return pl.pallas_call(
        _gmlp_s_body,
        grid=(1,),
        in_specs=[pl.BlockSpec((R, D_B), lambda i: (0, ub)),
                  pl.BlockSpec((R, D_B), lambda i: (0, ub + 1)),
                  pl.BlockSpec((1, D_B), z2), pl.BlockSpec((1, D_B), z2), pl.BlockSpec((1, D_B), z2)],
        out_specs=[pl.BlockSpec((R, D_B), z2), pl.BlockSpec((R, D_B), z2)],
        out_shape=[jax.ShapeDtypeStruct((R, D_B), BF16), jax.ShapeDtypeStruct((R, D_B), F32)],
        compiler_params=_cp(("arbitrary",)),
        name="gmlp_sample",
    )(proj, proj, g_v, w0, b0)


def _head_expand_matrix():
    r = lax.broadcasted_iota(jnp.int32, (LANE, D_C), 0)
    c = lax.broadcasted_iota(jnp.int32, (LANE, D_C), 1) // HD_C
    return jnp.where(r == c, 1.0, 0.0).astype(BF16)


def _expand_heads(v, e):
    hi, lo = _split_bf16(v)
    return _dot(hi, e) + _dot(lo, e)


def _ssd_p_body(z_ref, xbc_ref, dt_ref, cw_ref, cb_ref, dtb_ref, alog_ref, dsk_ref, gn_ref,
                o_ref, st_ref, h_ref, h_sc, cv_sc, *, nt):
    Q = SSD_CHUNK
    t = pl.program_id(1)

    @pl.when(t == 0)
    def _():
        h_sc[...] = jnp.zeros_like(h_sc)
        cv_sc[0:SUBLANE, :] = jnp.zeros((SUBLANE, D_XBC), F32)

    xbc = xbc_ref[...]
    cv_sc[SUBLANE:SUBLANE + Q, :] = xbc
    acc = cb_ref[...]
    for k in range(K_C):
        off = SUBLANE - (K_C - 1) + k
        acc = acc + cw_ref[k:k + 1, :] * cv_sc[off:off + Q, :]
    cv_sc[0:SUBLANE, :] = xbc[Q - SUBLANE:, :]
    st_ref[...] = xbc[Q - (K_C - 1):, :]
    xc = _silu(acc)
    xs = xc[:, :D_C]

    dt = _softplus(dt_ref[...] + dtb_ref[...])
    a = -jnp.exp(alog_ref[...])
    da = dt * a
    ii = lax.broadcasted_iota(jnp.int32, (Q, Q), 0)
    jj = lax.broadcasted_iota(jnp.int32, (Q, Q), 1)
    causal = ii >= jj
    tril = jnp.where(causal, 1.0, 0.0).astype(BF16)
    d0, d1 = _split_bf16(da)
    d2 = (da - d0.astype(F32) - d1.astype(F32)).astype(BF16)
    cs = _dot(tril, d0) + _dot(tril, d1) + _dot(tril, d2)
    cst = cs.T
    cs_end = cs[Q - 1:Q, :]
    e = _head_expand_matrix()
    dt_f = _expand_heads(dt, e)
    ws_f = _expand_heads(dt * jnp.exp(cs_end - cs), e)
    ecs_f = _expand_heads(jnp.exp(cs), e)
    xdt = xs * dt_f
    xw = xs * ws_f

    lane = lax.broadcasted_iota(jnp.int32, (Q, LANE), 1)
    hpg = H_C // G_C
    gw = hpg * HD_C
    ys = []
    for g in range(G_C):
        bg = xc[:, D_C + g * N_C:D_C + (g + 1) * N_C].astype(BF16)
        cg = xc[:, D_C + G_C * N_C + g * N_C:D_C + G_C * N_C + (g + 1) * N_C].astype(BF16)
        gmat = _dot_nt(cg, bg)
        hprev = h_sc[g * gw:(g + 1) * gw, :]
        yoff = _dot_nt(cg, hprev.astype(BF16)) * ecs_f[:, g * gw:(g + 1) * gw]
        snew = _dot_tn(xw[:, g * gw:(g + 1) * gw].astype(BF16), bg)
        for hp in range(hpg // 2):
            h0 = g * hpg + 2 * hp
            xpair = xdt[:, h0 * HD_C:(h0 + 2) * HD_C]
            x_lo = jnp.where(lane < HD_C, xpair, 0.0).astype(BF16)
            x_hi = jnp.where(lane >= HD_C, xpair, 0.0).astype(BF16)
            yd = None
            for hh, xh in ((h0, x_lo), (h0 + 1, x_hi)):
                seg = cs[:, hh:hh + 1] - cst[hh:hh + 1, :]
                sc = (gmat * jnp.exp(jnp.where(causal, seg, NEG_BIG))).astype(BF16)
                part = _dot(sc, xh)
                yd = part if yd is None else yd + part
            ys.append(yd + yoff[:, 2 * hp * HD_C:(2 * hp + 2) * HD_C])
        for hh in range(hpg):
            h = g * hpg + hh
            cd = jnp.exp(cst[h:h + 1, Q - 1:Q])
            rows = slice(h * HD_C, (h + 1) * HD_C)
            h_sc[rows, :] = h_sc[rows, :] * cd + snew[hh * HD_C:(hh + 1) * HD_C, :]

    y = jnp.concatenate(ys, axis=1) + dsk_ref[...] * xs
    y = y * _silu(z_ref[...])
    o_ref[...] = (_rms(y) * gn_ref[...]).astype(o_ref.dtype)

    @pl.when(t == nt - 1)
    def _():
        h_ref[...] = h_sc[...]


def _ssd_params(dt_bias, a_log, d_skip):
    pad = LANE - H_C
    dtb = jnp.pad(dt_bias, (0, pad)).reshape(1, LANE)
    alog = jnp.pad(a_log, (0, pad)).reshape(1, LANE)
    dsk = jnp.repeat(d_skip, HD_C).reshape(1, D_C)
    return dtb, alog, dsk


def ssd_prompt(z3, xd3, conv_w, conv_b, dt_bias, a_log, d_skip, g_norm):
    B, L = z3.shape[:2]
    Q = SSD_CHUNK
    nt = L // Q
    dtb, alog, dsk = _ssd_params(dt_bias, a_log, d_skip)
    const2 = lambda b, t: (0, 0)
    blk = lambda w: pl.BlockSpec((None, Q, w), lambda b, t: (b, t, 0))
    return pl.pallas_call(
        functools.partial(_ssd_p_body, nt=nt),
        grid=(B, nt),
        in_specs=[blk(D_C), blk(D_XBC),
                  pl.BlockSpec((None, Q, LANE), lambda b, t: (b, t, D_XBC // LANE)),
                  pl.BlockSpec((K_C, D_XBC), const2), pl.BlockSpec((1, D_XBC), const2),
                  pl.BlockSpec((1, LANE), const2), pl.BlockSpec((1, LANE), const2),
                  pl.BlockSpec((1, D_C), const2), pl.BlockSpec((1, D_C), const2)],
        out_specs=[blk(D_C),
                   pl.BlockSpec((None, K_C - 1, D_XBC), lambda b, t: (b, 0, 0)),
                   pl.BlockSpec((None, H_C * HD_C, N_C), lambda b, t: (b, 0, 0))],
        out_shape=[jax.ShapeDtypeStruct((B, L, D_C), BF16),
                   jax.ShapeDtypeStruct((B, K_C - 1, D_XBC), F32),
                   jax.ShapeDtypeStruct((B, H_C * HD_C, N_C), F32)],
        scratch_shapes=[pltpu.VMEM((H_C * HD_C, N_C), F32), pltpu.VMEM((SUBLANE + Q, D_XBC), F32)],
        compiler_params=_cp(("arbitrary", "arbitrary")),
        name="ssd_prompt",
    )(z3, xd3, xd3, conv_w, conv_b.reshape(1, D_XBC), dtb, alog, dsk, g_norm.reshape(1, D_C))


def _ssd_s_body(z_ref, xbc_ref, dt_ref, buf_ref, h0_ref, cw_ref, cb_ref, dtb_ref, alog_ref,
                dsk_ref, gn_ref, *rest, tbatch, nsteps, has_prev):
    o_ref, hn_ref, xs_sc, bc_sc, xt_sc, at_sc, y_sc = rest[1:] if has_prev else rest
    s = pl.program_id(0)
    R = xs_sc.shape[0]

    @pl.when(s == 0)
    def _():
        acc = cb_ref[...] + cw_ref[K_C - 1:K_C, :] * xbc_ref[...]
        for k in range(K_C - 1):
            acc = acc + cw_ref[k:k + 1, :] * buf_ref[k]
        xc = _silu(acc)
        xs = xc[:, :D_C]
        xs_sc[...] = xs
        bc_sc[...] = xc[:, D_C:]
        dt = _softplus(dt_ref[...] + dtb_ref[...])
        e = _head_expand_matrix()
        dt_f = _expand_heads(dt, e)
        da_f = jnp.exp(_expand_heads(dt * (-jnp.exp(alog_ref[...])), e))
        for t_sc, v in ((xt_sc, xs * dt_f), (at_sc, da_f)):
            hi, lo = _split_bf16(v.T)
            t_sc[:, :R] = hi
            t_sc[:, R:] = lo

    gw = (H_C // G_C) * HD_C
    kk = lax.broadcasted_iota(jnp.int32, (2 * R, LANE), 0) % R
    r0 = pl.multiple_of(s * tbatch, tbatch)
    bc8 = bc_sc[pl.ds(r0, tbatch), :]
    ti = lax.broadcasted_iota(jnp.int32, (tbatch, gw), 0)
    ytile = [jnp.zeros((tbatch, gw), F32) for _ in range(G_C)]
    for i in range(tbatch):
        b = s * tbatch + i
        onehot = jnp.where(kk == b, 1.0, 0.0).astype(BF16)
        xb = _dot(xt_sc[...], onehot)
        ab = _dot(at_sc[...], onehot)
        for g in range(G_C):
            rows = slice(g * gw, (g + 1) * gw)
            brow = bc8[i:i + 1, g * N_C:(g + 1) * N_C]
            hn = h0_ref[i, rows, :] * ab[rows, :] + xb[rows, :] * brow
            hn_ref[i, rows, :] = hn
            c8 = bc8[:, G_C * N_C + g * N_C:G_C * N_C + (g + 1) * N_C]
            yg = _dot_nt(c8.astype(BF16), hn.astype(BF16))
            ytile[g] = jnp.where(ti == i, yg, ytile[g])
    for g in range(G_C):
        y_sc[pl.ds(r0, tbatch), g * gw:(g + 1) * gw] = ytile[g]

    @pl.when(s == nsteps - 1)
    def _():
        y = y_sc[...] + dsk_ref[...] * xs_sc[...]
        y = y * _silu(z_ref[...])
        o_ref[...] = (_rms(y) * gn_ref[...]).astype(o_ref.dtype)


def ssd_sample(z, xd, buf, h0_all, l, hn_all, conv_w, conv_b, dt_bias, a_log, d_skip, g_norm,
               *, tbatch=16):
    R = z.shape[0]
    nsteps = R // tbatch
    dtb, alog, dsk = _ssd_params(dt_bias, a_log, d_skip)
    z2 = lambda s: (0, 0)
    hw = H_C * HD_C
    st_spec = pl.BlockSpec((None, tbatch, hw, N_C), lambda s: (l, s, 0, 0))
    in_specs = [pl.BlockSpec((R, D_C), z2), pl.BlockSpec((R, D_XBC), z2),
                pl.BlockSpec((R, LANE), lambda s: (0, D_XBC // LANE)),
                pl.BlockSpec((K_C - 1, R, D_XBC), lambda s: (0, 0, 0)),
                st_spec,
                pl.BlockSpec((K_C, D_XBC), z2), pl.BlockSpec((1, D_XBC), z2),
                pl.BlockSpec((1, LANE), z2), pl.BlockSpec((1, LANE), z2),
                pl.BlockSpec((1, D_C), z2), pl.BlockSpec((1, D_C), z2)]
    args = [z, xd, xd, buf, h0_all, conv_w, conv_b.reshape(1, D_XBC), dtb, alog, dsk,
            g_norm.reshape(1, D_C)]
    aliases = {}
    if hn_all is not None:
        aliases = {len(args): 1}
        in_specs.append(pl.BlockSpec(memory_space=pl.ANY))
        args.append(hn_all)
    return pl.pallas_call(
        functools.partial(_ssd_s_body, tbatch=tbatch, nsteps=nsteps, has_prev=hn_all is not None),
        grid=(nsteps,),
        in_specs=in_specs,
        out_specs=[pl.BlockSpec((R, D_C), z2), st_spec],
        out_shape=[jax.ShapeDtypeStruct((R, D_C), BF16),
                   jax.ShapeDtypeStruct(h0_all.shape, F32)],
        scratch_shapes=[pltpu.VMEM((R, D_C), F32), pltpu.VMEM((R, 2 * G_C * N_C), F32),
                        pltpu.VMEM((D_C, 2 * R), BF16), pltpu.VMEM((D_C, 2 * R), BF16),
                        pltpu.VMEM((R, D_C), F32)],
        input_output_aliases=aliases,
        compiler_params=_cp(("arbitrary",)),
        name="ssd_sample",
    )(*args)


def _up_p_body(x_ref, wa_ref, wb_ref, cwa_ref, cwb_ref, cba_ref, cbb_ref,
               o_ref, sa_ref, sb_ref, wa_sc, wb_sc, ca_sc, cb_sc, *, tm, tiles_per_seq):
    i = pl.program_id(1)

    @pl.when(i == 0)
    def _():
        wa_sc[...] = wa_ref[...].astype(BF16)
        wb_sc[...] = wb_ref[...].astype(BF16)

    @pl.when(i % tiles_per_seq == 0)
    def _():
        ca_sc[0:SUBLANE, :] = jnp.zeros((SUBLANE, ca_sc.shape[1]), F32)
        cb_sc[0:SUBLANE, :] = jnp.zeros((SUBLANE, cb_sc.shape[1]), F32)

    x = x_ref[...]

    def half(w_sc, c_sc, cw_ref, cb_ref, st_ref):
        up = _dot(x, w_sc[...])
        c_sc[SUBLANE:SUBLANE + tm, :] = up
        acc = cb_ref[...]
        for k in range(K_F):
            off = SUBLANE - (K_F - 1) + k
            acc = acc + cw_ref[k:k + 1, :] * c_sc[off:off + tm, :]
        c_sc[0:SUBLANE, :] = up[tm - SUBLANE:, :]
        st_ref[...] = up[tm - (K_F - 1):, :]
        return acc

    a = half(wa_sc, ca_sc, cwa_ref, cba_ref, sa_ref)
    b = half(wb_sc, cb_sc, cwb_ref, cbb_ref, sb_ref)
    o_ref[...] = (_silu(a) * b).astype(o_ref.dtype)


def up_prompt(h2, w_up, l, conv_w, conv_b, *, seq_len, tm=1024, tn=512):
    M, D = h2.shape
    B = M // seq_len
    tiles_per_seq = seq_len // tm
    nj = D_FF // tn
    cb = conv_b.reshape(1, 2 * D_FF)
    body = functools.partial(_up_p_body, tm=tm, tiles_per_seq=tiles_per_seq)
    st_spec = pl.BlockSpec((None, K_F - 1, tn), lambda j, i: (i // tiles_per_seq, 0, j))
    return pl.pallas_call(
        body,
        grid=(nj, M // tm),
        in_specs=[pl.BlockSpec((tm, D), lambda j, i: (i, 0)),
                  pl.BlockSpec((None, D, tn), lambda j, i: (l, 0, j)),
                  pl.BlockSpec((None, D, tn), lambda j, i: (l, 0, nj + j)),
                  pl.BlockSpec((K_F, tn), lambda j, i: (0, j)),
                  pl.BlockSpec((K_F, tn), lambda j, i: (0, nj + j)),
                  pl.BlockSpec((1, tn), lambda j, i: (0, j)),
                  pl.BlockSpec((1, tn), lambda j, i: (0, nj + j))],
        out_specs=[pl.BlockSpec((tm, tn), lambda j, i: (i, j)), st_spec, st_spec],
        out_shape=[jax.ShapeDtypeStruct((M, D_FF), BF16),
                   jax.ShapeDtypeStruct((B, K_F - 1, D_FF), F32),
                   jax.ShapeDtypeStruct((B, K_F - 1, D_FF), F32)],
        scratch_shapes=[pltpu.VMEM((D, tn), BF16), pltpu.VMEM((D, tn), BF16),
                        pltpu.VMEM((SUBLANE + tm, tn), F32), pltpu.VMEM((SUBLANE + tm, tn), F32)],
        compiler_params=_cp(("arbitrary", "arbitrary")),
        name="up_prompt",
    )(h2, w_up, w_up, conv_w, conv_w, cb, cb)


def _up_s_body(x_ref, w_ref, buf_ref, cw_ref, cb_ref, *rest, nj, has_prev):
    o_ref, new_ref, a_sc = rest[1:] if has_prev else rest
    s = pl.program_id(0)
    up = _dot(x_ref[...], w_ref[...].astype(BF16))
    acc = cb_ref[...] + cw_ref[K_F - 1:K_F, :] * up
    for k in range(K_F - 1):
        acc = acc + cw_ref[k:k + 1, :] * buf_ref[:, k, :]
    for k in range(1, K_F - 1):
        new_ref[:, k - 1, :] = buf_ref[:, k, :]
    new_ref[:, K_F - 2, :] = up

    @pl.when(s < nj)
    def _():
        a_sc[s] = acc

    @pl.when(s >= nj)
    def _():
        o_ref[...] = (_silu(a_sc[s - nj]) * acc).astype(o_ref.dtype)


def up_sample(h2, w_up, l, buf_all, new_all, conv_w, conv_b, *, tn=512):
    R, D = h2.shape
    nj = D_FF // tn
    st = pl.BlockSpec((None, R, K_F - 1, tn), lambda s: (l, 0, 0, s))
    in_specs = [pl.BlockSpec((R, D), lambda s: (0, 0)),
                pl.BlockSpec((None, D, tn), lambda s: (l, 0, s)),
                st,
                pl.BlockSpec((K_F, tn), lambda s: (0, s)),
                pl.BlockSpec((1, tn), lambda s: (0, s))]
    args = [h2, w_up, buf_all, conv_w, conv_b.reshape(1, 2 * D_FF)]
    aliases = {}
    if new_all is not None:
        aliases = {len(args): 1}
        in_specs.append(pl.BlockSpec(memory_space=pl.ANY))
        args.append(new_all)
    act, new = pl.pallas_call(
        functools.partial(_up_s_body, nj=nj, has_prev=new_all is not None),
        grid=(2 * nj,),
        in_specs=in_specs,
        out_specs=[pl.BlockSpec((R, tn), lambda s: (0, jnp.maximum(s - nj, 0))), st],
        out_shape=[jax.ShapeDtypeStruct((R, D_FF), BF16),
                   jax.ShapeDtypeStruct(buf_all.shape, F32)],
        scratch_shapes=[pltpu.VMEM((nj, R, tn), F32)],
        input_output_aliases=aliases,
        compiler_params=_cp(("arbitrary",)),
        name="up_sample",
    )(*args)
    return act, new


def _layer(x3, h, mods, l, p, state, next_norm, *, seq_len, tm, tn_merge):
    B, L, D = x3.shape
    M = B * L
    sh_m, sc_m, gt_m, sh_f, sc_f, gt_f = mods
    prompt = state is None
    w_buffers = 1 if M > tm else 2

    def proj(col0, n, name):
        return matmul([h], [(p['w_in_t'], l, col0, True)], n, F32, _epi_first,
                      x_of_w=(0,), tm=tm, tn=n, name=name, w_buffers=1)

    uav = proj(OFF_UA, D_A + 2 * D_B, "proj_uav")
    zc = proj(OFF_Z, D_C, "proj_z")
    xd = proj(OFF_XBC, D_XBC + LANE, "proj_xd")

    tab, wb, bbr, bbi = s5_prep(p['lam_re'][l], p['lam_im'][l], p['log_dt'][l],
                                p['b_re'][l], p['b_im'][l])
    d_a = p['s5_d'][l].reshape(1, D_A)
    wglu = p['w_glu'][l].astype(BF16)
    g_v = p['g_v'][l].reshape(1, D_B)

    if prompt:
        s5_ops = s5_chunk_prep(p['lam_re'][l], p['lam_im'][l], p['log_dt'][l], bbr, bbi,
                               p['c_re'][l], p['c_im'][l])
        o_a, s5r, s5i = s5_prompt(uav.reshape(B, L, -1), s5_ops, d_a, wglu)
        o_b = gmlp_prompt(uav.reshape(B, L, -1), g_v, p['w_s'][l], p['b_s'][l])
        v_rows = None
        o_c, convc, ssm = ssd_prompt(zc.reshape(B, L, -1), xd.reshape(B, L, -1),
                                     p['ssd_conv_w'][l], p['ssd_conv_b'][l], p['dt_bias'][l],
                                     p['a_log'][l], p['ssd_d'][l], p['ssd_g'][l])
        s5r = s5r.reshape(B, G_A, P_A)
        s5i = s5i.reshape(B, G_A, P_A)
    else:
        s5_re0, s5_im0, ssm_all, ssm_new_all, convc0, ffn_all, ffn_new_all = state
        wcr, wci = s5_out_weights(p['c_re'][l], p['c_im'][l])
        o_a, s5r, s5i = s5_sample(uav, s5_re0.reshape(M, N_S5), s5_im0.reshape(M, N_S5),
                                  wb, wcr, wci, tab, d_a, wglu)
        o_b, v_rows = gmlp_sample(uav, g_v, p['w_s'][l], p['b_s'][l])
        o_c, ssm = ssd_sample(zc, xd, jnp.transpose(convc0, (1, 0, 2)),
                              ssm_all, l, ssm_new_all,
                              p['ssd_conv_w'][l], p['ssd_conv_b'][l], p['dt_bias'][l],
                              p['a_log'][l], p['ssd_d'][l], p['ssd_g'][l])
        convc = jnp.concatenate([convc0[:, 1:], xd[:, None, :D_XBC]], axis=1)
        s5r = s5r.reshape(M, G_A, P_A)
        s5i = s5i.reshape(M, G_A, P_A)
    o_a = o_a.reshape(M, D_A)
    o_b = o_b.reshape(M, D_B)
    o_c = o_c.reshape(M, D_C)
    if prompt:
        ssm = ssm.reshape(B, H_C, HD_C, N_C)

    w_t = p['w_in_t']
    merged = matmul(
        [h, o_a, o_b, o_c],
        [(p['w_pa'], l, 0, False), (p['w_pb'], l, 0, False), (p['w_pc'], l, 0, False),
         (w_t, l, OFF_GATES, True), (w_t, l, OFF_GATES + D, True), (w_t, l, OFF_GATES + 2 * D, True)],
        D, BF16, _epi_merge, x_of_w=(1, 2, 3, 0, 0, 0), tm=tm, tn=tn_merge, name="merge",
        w_buffers=w_buffers)

    tm_r = min(tm, 256)
    x2, h2 = residual_norm(merged, p['w_out'], l, x3.reshape(M, D), gt_m,
                           p['g_ffn'][l].reshape(1, D), (sc_f, sh_f), tm=min(tm, 512),
                           seq_len=seq_len, emit_x=True, name="out_norm")
    if prompt:
        act, st_a, st_b = up_prompt(h2, p['w_up'], l, p['ffn_conv_w'][l], p['ffn_conv_b'][l],
                                    seq_len=seq_len, tm=tm)
        convf = jnp.concatenate([st_a, st_b], axis=-1)
    else:
        act, convf = up_sample(h2, p['w_up'], l, ffn_all, ffn_new_all,
                               p['ffn_conv_w'][l], p['ffn_conv_b'][l])

    g_next, mod_next = next_norm
    x2, h_next = residual_norm(act, p['w_down_bf16'], l, x2, gt_f, g_next, mod_next, tm=tm_r,
                               seq_len=seq_len, emit_x=mod_next is not None, name="down_norm")
    x_out = None if x2 is None else x2.reshape(B, L, D)
    return x_out, h_next, s5r, s5i, ssm, convc, convf, v_rows


def kernel(x_prompt, x_sample, c_prompt, c_sample, state_s5_re, state_s5_im, state_ssm, state_ssd_conv, state_ffn_conv, w_mod, b_mod, g_mix, w_in, s5_lam_re, s5_lam_im, s5_log_dt, s5_b_re, s5_b_im, s5_c_re, s5_c_im, s5_d, s5_w_glu, gm_g_v, gm_w_s, gm_b_s, ssd_conv_w, ssd_conv_b, ssd_dt_bias, ssd_a_log, ssd_d, ssd_g_norm, w_pa, w_pb, w_pc, w_out, g_ffn, ffn_w_up, ffn_conv_w, ffn_conv_b, ffn_w_down, g_final):
    p = {
        'g_mix': g_mix, 'w_in_t': jnp.swapaxes(w_in, 1, 2),
        'lam_re': s5_lam_re, 'lam_im': s5_lam_im, 'log_dt': s5_log_dt,
        'b_re': s5_b_re, 'b_im': s5_b_im, 'c_re': s5_c_re, 'c_im': s5_c_im,
        's5_d': s5_d, 'w_glu': s5_w_glu,
        'g_v': gm_g_v, 'w_s': gm_w_s, 'b_s': gm_b_s,
        'ssd_conv_w': ssd_conv_w, 'ssd_conv_b': ssd_conv_b, 'dt_bias': ssd_dt_bias,
        'a_log': ssd_a_log, 'ssd_d': ssd_d, 'ssd_g': ssd_g_norm,
        'w_pa': w_pa, 'w_pb': w_pb, 'w_pc': w_pc, 'w_out': w_out,
        'g_ffn': g_ffn, 'w_up': ffn_w_up, 'ffn_conv_w': ffn_conv_w,
        'ffn_conv_b': ffn_conv_b, 'w_down_bf16': ffn_w_down.astype(BF16),
    }
    bp, seq, D = x_prompt.shape
    bs = x_sample.shape[0]

    n_c = bs + bp
    pad = (-n_c) % SUBLANE
    c_all = jnp.concatenate([c_sample, c_prompt, jnp.zeros((pad, D), F32)], axis=0)
    mod = mod_all(c_all, w_mod, b_mod)

    xp = x_prompt
    xs = x_sample.reshape(1, bs, D)
    outs_p = [[] for _ in range(5)]
    outs_s = [[] for _ in range(6)]
    ssm_all = state_ssm.reshape(DEPTH, bs, H_C * HD_C, N_C)
    ssm_new_all = None
    ffn_new_all = None
    mods_s = [[TokenMod(mod, l, k) for k in range(6)] for l in range(DEPTH)]
    mods_p = [[m[:, None, :] for m in jnp.split(mod[l, bs:bs + bp], 6, axis=-1)] for l in range(DEPTH)]

    def norm_after(l, mods):
        if l + 1 == DEPTH:
            return g_final.reshape(1, D), None
        return g_mix[l + 1].reshape(1, D), (mods[l + 1][1], mods[l + 1][0])

    g0 = g_mix[0].reshape(1, D)
    hp = norm_mod(xp, g0, mods_p[0][1], mods_p[0][0], tm=512).reshape(bp * seq, D)
    hs = norm_mod(xs, g0, mods_s[0][1], mods_s[0][0], tm=bs).reshape(bs, D)
    for l in range(DEPTH):
        xp, hp, *st_p = _layer(xp, hp, mods_p[l], l, p, None, norm_after(l, mods_p),
                               seq_len=seq, tm=1024, tn_merge=512)
        for acc, v in zip(outs_p, st_p[:5]):
            acc.append(v)
        state = (state_s5_re[l], state_s5_im[l], ssm_all, ssm_new_all, state_ssd_conv[l],
                 state_ffn_conv, ffn_new_all)
        xs, hs, *st_s = _layer(xs, hs, mods_s[l], l, p, state, norm_after(l, mods_s),
                               seq_len=1, tm=bs, tn_merge=256)
        ssm_new_all = st_s[2]
        ffn_new_all = st_s[4]
        st_s[5] = st_s[5].reshape(bs, 1, D_B)
        for acc, v in zip(outs_s, st_s):
            acc.append(v)

    y_prompt = hp.reshape(bp, seq, D)
    y_sample = hs.reshape(bs, 1, D)
    whole = {2: ssm_new_all.reshape(DEPTH, bs, H_C, HD_C, N_C),
             4: ffn_new_all}
    outs_s = [whole[k] if k in whole else jnp.stack(v) for k, v in enumerate(outs_s)]
    return (y_prompt, y_sample, *[jnp.stack(v) for v in outs_p], *outs_s)
```
